```python
import math
import jax, jax.numpy as jnp
from jax import lax
import numpy as np


D_MODEL = 2048
BATCH = 1
SEQ = 8192
DEPTH = 1

MIX_WIDTH = D_MODEL
SSM_WIDTH = MIX_WIDTH // 2
NA_WIDTH = MIX_WIDTH - SSM_WIDTH
SSM_GROUP = 16
SSM_GROUPS = SSM_WIDTH // SSM_GROUP
SSM_STATE = 64
DT_MIN = 1e-3
DT_MAX = 1e-1
NA_HEAD_DIM = 64
NA_HEADS = NA_WIDTH // NA_HEAD_DIM
GRID_W = 64
NA_KH_MAX = 8
NA_KW = 16
N_EXPERTS = 32
TOP_K = 4
D_FF = D_MODEL
SWIGLU_LIMIT = 7.0
SWIGLU_ALPHA = 1.702
MOE_BLOCK = 128
RMS_EPS = 1e-5

kernel_name = 'hymba_s5_natten_moe_encoder'


def rmsnorm(x, g):
    xf = x.astype(jnp.float32)
    y = xf * lax.rsqrt(jnp.mean(xf * xf, axis=-1, keepdims=True) + RMS_EPS)
    return (y * g.astype(jnp.float32)).astype(x.dtype)


def _s5_scan(ug, lam_re, lam_im, log_dt, b_re, b_im, c_re, c_im, reverse):
    f32 = jnp.float32
    lam = lax.complex(jnp.minimum(lam_re.astype(f32), -1e-4), lam_im.astype(f32))
    dt = jnp.exp(log_dt.astype(f32))[:, None]
    lam_bar = jnp.exp(lam * dt)
    b = lax.complex(b_re.astype(f32), b_im.astype(f32))
    b_bar = ((lam_bar - 1.0) / lam)[..., None] * b
    bu = jnp.einsum('blgc,gpc->blgp', ug.astype(jnp.complex64), b_bar)
    a = jnp.broadcast_to(lam_bar, bu.shape)

    def combine(left, right):
        a_l, s_l = left
        a_r, s_r = right
        return a_r * a_l, a_r * s_l + s_r

    _, states = lax.associative_scan(combine, (a, bu), reverse=reverse, axis=1)
    c = lax.complex(c_re.astype(f32), c_im.astype(f32))
    return jnp.einsum('blgp,gcp->blgc', states, c).real


def _s5_bidirectional(u, lam_re_f, lam_im_f, log_dt_f, b_re_f, b_im_f, c_re_f, c_im_f,
                      lam_re_b, lam_im_b, log_dt_b, b_re_b, b_im_b, c_re_b, c_im_b,
                      d, w_glu, b_glu):
    f32 = jnp.float32
    bsz, l, _ = u.shape
    uf = u.astype(f32)
    ug = uf.reshape(bsz, l, SSM_GROUPS, SSM_GROUP)
    y = (_s5_scan(ug, lam_re_f, lam_im_f, log_dt_f, b_re_f, b_im_f, c_re_f, c_im_f, False)
         + _s5_scan(ug, lam_re_b, lam_im_b, log_dt_b, b_re_b, b_im_b, c_re_b, c_im_b, True))
    y = y.reshape(bsz, l, SSM_WIDTH) + d.astype(f32) * uf
    y = jax.nn.gelu(y)
    y = y * jax.nn.sigmoid(y @ w_glu.astype(f32) + b_glu.astype(f32))
    return y.astype(u.dtype)


def _neighbourhood_attention(q, k, v, rpb):
    bsz, l, h, dh = q.shape
    rows = l // GRID_W
    kh = min(NA_KH_MAX, rows)
    kw = NA_KW
    q = q.reshape(bsz, rows, GRID_W, h, dh)
    k = k.reshape(bsz, rows, GRID_W, h, dh)
    v = v.reshape(bsz, rows, GRID_W, h, dh)
    r = jnp.arange(rows)
    row_start = jnp.clip(r - kh // 2, 0, rows - kh)
    row_idx = row_start[:, None] + jnp.arange(kh)[None, :]
    k_blk = jnp.take(k, row_idx, axis=1)
    v_blk = jnp.take(v, row_idx, axis=1)
    c = jnp.arange(GRID_W)
    col_start = jnp.clip(c - kw // 2, 0, GRID_W - kw)
    col_valid = (c[None, :] >= col_start[:, None]) & (c[None, :] < col_start[:, None] + kw)
    dr = row_idx - r[:, None] + (NA_KH_MAX - 1)
    dc = jnp.clip(c[None, :] - c[:, None], -(kw - 1), kw - 1) + (kw - 1)
    bias = rpb.astype(jnp.float32)[:, dr[:, None, :, None], dc[None, :, None, :]]
    bias = jnp.transpose(bias, (1, 0, 2, 3, 4))
    scale = dh ** -0.5
    s = jnp.einsum('brqhd,brkwhd->brhqkw', q, k_blk).astype(jnp.float32) * scale + bias[None]
    s = jnp.where(col_valid[:, None, :], s, -jnp.inf)
    p = jax.nn.softmax(s.reshape(bsz, rows, h, GRID_W, kh * GRID_W), axis=-1)
    p = p.reshape(bsz, rows, h, GRID_W, kh, GRID_W).astype(v.dtype)
    out = jnp.einsum('brhqkw,brkwhd->brqhd', p, v_blk)
    return out.reshape(bsz, l, h * dh)


def _moe(hn, w_router, b_router, w_gate, b_gate, w_up, b_up, w_down, b_down):
    bsz, l, d = hn.shape
    n = bsz * l
    xf = hn.reshape(n, d)
    logits = (xf @ w_router).astype(jnp.float32) + b_router.astype(jnp.float32)
    top_val, top_idx = lax.top_k(logits, TOP_K)
    gates = jax.nn.softmax(top_val, axis=-1)
    flat_e = top_idx.reshape(-1)
    flat_tok = jnp.repeat(jnp.arange(n, dtype=jnp.int32), TOP_K)
    flat_g = gates.reshape(-1)
    order = jnp.argsort(flat_e)
    sorted_e = flat_e[order]
    counts = jnp.bincount(flat_e, length=N_EXPERTS)
    starts = jnp.cumsum(counts) - counts
    padded = (counts + MOE_BLOCK - 1) // MOE_BLOCK * MOE_BLOCK
    padded_end = jnp.cumsum(padded)
    padded_start = padded_end - padded
    rank = jnp.arange(n * TOP_K) - starts[sorted_e]
    dest = padded_start[sorted_e] + rank
    n_blocks = -(-(n * TOP_K) // MOE_BLOCK) + N_EXPERTS
    n_rows = n_blocks * MOE_BLOCK
    row_tok = jnp.zeros((n_rows,), jnp.int32).at[dest].set(flat_tok[order])
    row_gate = jnp.zeros((n_rows,), jnp.float32).at[dest].set(flat_g[order])
    block_start = jnp.arange(n_blocks) * MOE_BLOCK
    block_e = jnp.minimum(jnp.searchsorted(padded_end, block_start, side='right'), N_EXPERTS - 1)
    xs = xf[row_tok].reshape(n_blocks, MOE_BLOCK, d)

    def expert_block(args):
        xb, e = args
        g = jnp.minimum(xb @ w_gate[e] + b_gate[e], SWIGLU_LIMIT)
        u = jnp.clip(xb @ w_up[e] + b_up[e], -SWIGLU_LIMIT, SWIGLU_LIMIT)
        act = (u + 1.0) * (g * jax.nn.sigmoid(SWIGLU_ALPHA * g))
        return act @ w_down[e] + b_down[e]

    ys = lax.map(expert_block, (xs, block_e)).reshape(n_rows, d)
    ys = ys * row_gate[:, None].astype(ys.dtype)
    out = jax.ops.segment_sum(ys, row_tok, num_segments=n)
    return out.reshape(bsz, l, d)


def setup_inputs(seed: int = 0) -> dict:
    key = jax.random.key(seed)
    keys = iter(jax.random.split(key, 48))
    f32 = jnp.float32
    L = DEPTH

    def normal(shape, scale):
        return jax.random.normal(next(keys), shape, f32) * scale

    def gain(shape):
        return 1.0 + normal(shape, 0.02)

    n_idx = jnp.arange(SSM_STATE, dtype=f32)

    def lam_re():
        return -0.5 + normal((L, SSM_GROUPS, SSM_STATE), 0.01)

    def lam_im():
        return math.pi * n_idx + normal((L, SSM_GROUPS, SSM_STATE), 0.01)

    def log_dt():
        return jax.random.uniform(next(keys), (L, SSM_GROUPS), f32, math.log(DT_MIN), math.log(DT_MAX))

    b_scale = (2.0 * SSM_GROUP) ** -0.5
    c_scale = 0.5
    inp = {}
    inp['x'] = normal((BATCH, SEQ, D_MODEL), 1.0)
    inp['g_mix'] = gain((L, D_MODEL))
    inp['w_in'] = normal((L, D_MODEL, SSM_WIDTH + 3 * NA_WIDTH), D_MODEL ** -0.5)
    for tag in ('fwd', 'bwd'):
        inp['lam_re_' + tag] = lam_re()
        inp['lam_im_' + tag] = lam_im()
        inp['log_dt_' + tag] = log_dt()
        inp['b_re_' + tag] = normal((L, SSM_GROUPS, SSM_STATE, SSM_GROUP), b_scale)
        inp['b_im_' + tag] = normal((L, SSM_GROUPS, SSM_STATE, SSM_GROUP), b_scale)
        inp['c_re_' + tag] = normal((L, SSM_GROUPS, SSM_GROUP, SSM_STATE), c_scale)
        inp['c_im_' + tag] = normal((L, SSM_GROUPS, SSM_GROUP, SSM_STATE), c_scale)
    inp['ssm_d'] = normal((L, SSM_WIDTH), 1.0)
    inp['w_glu'] = normal((L, SSM_WIDTH, SSM_WIDTH), SSM_WIDTH ** -0.5)
    inp['b_glu'] = normal((L, SSM_WIDTH), 0.01)
    inp['na_rpb'] = normal((L, NA_HEADS, 2 * NA_KH_MAX - 1, 2 * NA_KW - 1), 0.1)
    inp['g_ssm_out'] = gain((L, SSM_WIDTH))
    inp['g_na_out'] = gain((L, NA_WIDTH))
    inp['w_out'] = normal((L, MIX_WIDTH, D_MODEL), MIX_WIDTH ** -0.5)
    inp['g_moe'] = gain((L, D_MODEL))
    inp['w_router'] = normal((L, D_MODEL, N_EXPERTS), D_MODEL ** -0.5)
    inp['b_router'] = normal((L, N_EXPERTS), 0.01)
    inp['w_gate'] = normal((L, N_EXPERTS, D_MODEL, D_FF), D_MODEL ** -0.5)
    inp['b_gate'] = normal((L, N_EXPERTS, D_FF), 0.01)
    inp['w_up'] = normal((L, N_EXPERTS, D_MODEL, D_FF), D_MODEL ** -0.5)
    inp['b_up'] = normal((L, N_EXPERTS, D_FF), 0.01)
    inp['w_down'] = normal((L, N_EXPERTS, D_FF, D_MODEL), D_FF ** -0.5)
    inp['b_down'] = normal((L, N_EXPERTS, D_MODEL), 0.01)
    inp['g_final'] = gain((D_MODEL,))
    return inp


def reference(x, g_mix, w_in,
              lam_re_fwd, lam_im_fwd, log_dt_fwd, b_re_fwd, b_im_fwd, c_re_fwd, c_im_fwd,
              lam_re_bwd, lam_im_bwd, log_dt_bwd, b_re_bwd, b_im_bwd, c_re_bwd, c_im_bwd,
              ssm_d, w_glu, b_glu, na_rpb, g_ssm_out, g_na_out, w_out,
              g_moe, w_router, b_router, w_gate, b_gate, w_up, b_up, w_down, b_down,
              g_final):
    bsz, l, _ = x.shape
    for i in range(DEPTH):
        h = rmsnorm(x, g_mix[i])
        proj = jnp.einsum('bld,de->ble', h, w_in[i])
        u, q, k, v = jnp.split(proj, [SSM_WIDTH, SSM_WIDTH + NA_WIDTH, SSM_WIDTH + 2 * NA_WIDTH], axis=-1)
        y_ssm = _s5_bidirectional(
            u,
            lam_re_fwd[i], lam_im_fwd[i], log_dt_fwd[i], b_re_fwd[i], b_im_fwd[i], c_re_fwd[i], c_im_fwd[i],
            lam_re_bwd[i], lam_im_bwd[i], log_dt_bwd[i], b_re_bwd[i], b_im_bwd[i], c_re_bwd[i], c_im_bwd[i],
            ssm_d[i], w_glu[i], b_glu[i])
        hd = (bsz, l, NA_HEADS, NA_HEAD_DIM)
        y_na = _neighbourhood_attention(q.reshape(hd), k.reshape(hd), v.reshape(hd), na_rpb[i])
        mixed = jnp.concatenate([rmsnorm(y_ssm, g_ssm_out[i]), rmsnorm(y_na, g_na_out[i])], axis=-1)
        x = x + jnp.einsum('ble,ed->bld', mixed, w_out[i])
        x = x + _moe(rmsnorm(x, g_moe[i]), w_router[i], b_router[i], w_gate[i], b_gate[i],
                     w_up[i], b_up[i], w_down[i], b_down[i])
    return rmsnorm(x, g_final)
```

```python
import functools
import math

import jax
import jax.numpy as jnp
from jax import lax
from jax.experimental import pallas as pl
from jax.experimental.pallas import tpu as pltpu

f32 = jnp.float32
bf16 = jnp.bfloat16

D_MODEL = 2048
SEQ = 8192
SSM_WIDTH = 1024
NA_WIDTH = 1024
SSM_GROUP = 16
SSM_GROUPS = 64
SSM_STATE = 64
NA_HEAD_DIM = 64
NA_HEADS = 16
GRID_W = 64
GRID_ROWS = SEQ // GRID_W
NA_KH = 8
NA_KW = 16
N_EXPERTS = 32
TOP_K = 4
D_FF = 2048
SWIGLU_LIMIT = 7.0
SWIGLU_ALPHA = 1.702
RMS_EPS = 1e-5

LANES = 128
NEG_BIG = -1e30

CHUNK_T = 16
N_CHUNKS = SEQ // CHUNK_T
GROUPS_PER_BLOCK = LANES // SSM_GROUP
N_LANE_BLOCKS = SSM_WIDTH // LANES
CAT_W = CHUNK_T * LANES
STATE_W = GROUPS_PER_BLOCK * SSM_STATE

ROW_CHUNK = 128
CHUNKS_PER_SB = 10
SB_ROWS = ROW_CHUNK * CHUNKS_PER_SB
MAX_CHUNKS = SEQ * TOP_K // ROW_CHUNK + N_EXPERTS
MAX_ROWS = MAX_CHUNKS * ROW_CHUNK
MAX_SB = MAX_CHUNKS // CHUNKS_PER_SB + N_EXPERTS
FF_TILE = 256
N_FF_TILES = D_FF // FF_TILE
N_OUT_TILES = D_MODEL // FF_TILE

VMEM_LIMIT = 56 * 1024 * 1024


def _cparams(semantics, vmem=VMEM_LIMIT):
    return pltpu.CompilerParams(dimension_semantics=semantics, vmem_limit_bytes=vmem)


def _rms(x, g):
    return x * lax.rsqrt(jnp.mean(x * x, axis=-1, keepdims=True) + RMS_EPS) * g


IN_TM = 512
IN_TN = 1024


def _in_proj_kernel(x_ref, g_ref, w_ref, u_ref, qkv_ref, h_ref):
    j = pl.program_id(1)

    @pl.when(j == 0)
    def _():
        h_ref[...] = _rms(x_ref[...], g_ref[...]).astype(bf16)

    acc = jnp.dot(h_ref[...], w_ref[...], preferred_element_type=f32)

    @pl.when(j == 0)
    def _():
        u_ref[...] = acc

    @pl.when(j > 0)
    def _():
        qkv_ref[...] = acc.astype(bf16)


def _in_proj(x, g_mix, w_in_bf):
    n_out = w_in_bf.shape[1]
    return pl.pallas_call(
        _in_proj_kernel,
        name="in_proj",
        grid=(SEQ // IN_TM, n_out // IN_TN),
        in_specs=[
            pl.BlockSpec((IN_TM, D_MODEL), lambda i, j: (i, 0)),
            pl.BlockSpec((1, D_MODEL), lambda i, j: (0, 0)),
            pl.BlockSpec((D_MODEL, IN_TN), lambda i, j: (0, j)),
        ],
        out_specs=[
            pl.BlockSpec((IN_TM, IN_TN), lambda i, j: (i, 0)),
            pl.BlockSpec((IN_TM, IN_TN), lambda i, j: (i, jnp.maximum(j - 1, 0))),
        ],
        out_shape=[
            jax.ShapeDtypeStruct((SEQ, SSM_WIDTH), f32),
            jax.ShapeDtypeStruct((SEQ, 3 * NA_WIDTH), bf16),
        ],
        scratch_shapes=[pltpu.VMEM((IN_TM, D_MODEL), bf16)],
        compiler_params=_cparams(("arbitrary", "arbitrary")),
    )(x, g_mix.reshape(1, D_MODEL), w_in_bf)


def _s5_discretise(lam_re, lam_im, log_dt, b_re, b_im, c_re, c_im):
    lam = lax.complex(jnp.minimum(lam_re.astype(f32), -1e-4), lam_im.astype(f32))
    dt = jnp.exp(log_dt.astype(f32))[:, None]
    lam_bar = jnp.exp(lam * dt)
    b_bar = ((lam_bar - 1.0) / lam)[..., None] * lax.complex(b_re.astype(f32), b_im.astype(f32))
    c = lax.complex(c_re.astype(f32), c_im.astype(f32))
    steps = jnp.arange(CHUNK_T + 1, dtype=f32)[:, None, None]
    powers = jnp.exp((lam * dt)[None] * steps)
    return powers, b_bar, c


def _s5_weights(fwd, bwd):
    pw_f, bb_f, c_f = _s5_discretise(*fwd)
    pw_b, bb_b, c_b = _s5_discretise(*bwd)
    nb, gb, t = N_LANE_BLOCKS, GROUPS_PER_BLOCK, CHUNK_T
    eye = jnp.eye(gb, dtype=f32)

    kf = jnp.einsum('gcp,jgp,gpd->jgcd', c_f, pw_f[:t], bb_f).real
    kb = jnp.einsum('gcp,jgp,gpd->jgcd', c_b, pw_b[:t], bb_b).real
    lag = jnp.arange(t)[:, None] - jnp.arange(t)[None, :]
    k_all = (jnp.where((lag >= 0)[:, :, None, None, None], kf[jnp.clip(lag, 0, t - 1)], 0.0)
             + jnp.where((lag <= 0)[:, :, None, None, None], kb[jnp.clip(-lag, 0, t - 1)], 0.0))
    k_all = k_all.reshape(t, t, nb, gb, SSM_GROUP, SSM_GROUP)
    w_intra = jnp.einsum('oiJgcd,gh->Jigdohc', k_all, eye).reshape(nb, CAT_W, CAT_W)

    bin_f = pw_f[:t][::-1][:, :, :, None] * bb_f[None]
    bin_b = pw_b[:t][:, :, :, None] * bb_b[None]

    def state_in(m):
        m = m.reshape(t, nb, gb, SSM_STATE, SSM_GROUP)
        return jnp.einsum('tJgpd,gh->Jtgdhp', m, eye).reshape(nb, CAT_W, STATE_W)

    w_in = jnp.concatenate([state_in(bin_f.real), state_in(bin_f.imag),
                            state_in(bin_b.real), state_in(bin_b.imag)], axis=-1)

    out_f = c_f[None] * pw_f[1:t + 1][:, :, None, :]
    out_b = c_b[None] * pw_b[1:t + 1][::-1][:, :, None, :]

    def state_out(m):
        m = m.reshape(t, nb, gb, SSM_GROUP, SSM_STATE)
        return jnp.einsum('tJgcp,gh->Jgpthc', m, eye).reshape(nb, STATE_W, CAT_W)

    w_out = jnp.concatenate([state_out(out_f.real), state_out(-out_f.imag),
                             state_out(out_b.real), state_out(-out_b.imag)], axis=1)

    def decay(a):
        return a.reshape(nb, 1, STATE_W)

    a_f, a_b = pw_f[t], pw_b[t]
    dec = jnp.concatenate([decay(a_f.real), decay(a_f.imag), decay(a_b.real), decay(a_b.imag)], axis=-1)
    w_all = jnp.stack([w_in, w_intra, w_out], axis=1).astype(bf16)
    return w_all, dec.astype(f32)


def _gelu_tanh(x):
    return 0.5 * x * (1.0 + jnp.tanh(math.sqrt(2.0 / math.pi) * (x + 0.044715 * (x * x * x))))


def _s5_kernel(u_ref, w_ref, dec_ref, d_ref, y_ref, ucat_ref, st_ref, acc_ref):
    k = pl.program_id(1)
    sw = STATE_W

    @pl.when(k == 0)
    def _():
        for t in range(CHUNK_T):
            ucat_ref[:, t * LANES:(t + 1) * LANES] = u_ref[pl.ds(t, N_CHUNKS, stride=CHUNK_T), :].astype(bf16)
        st_ref[...] = jnp.dot(ucat_ref[...], w_ref[0, 0], preferred_element_type=f32)
        afr = dec_ref[0, :, 0 * sw:1 * sw]
        afi = dec_ref[0, :, 1 * sw:2 * sw]
        abr = dec_ref[0, :, 2 * sw:3 * sw]
        abi = dec_ref[0, :, 3 * sw:4 * sw]

        def step(i, carry):
            sfr, sfi, sbr, sbi = carry
            r = N_CHUNKS - 1 - i
            zf = st_ref[pl.ds(i, 1), 0:2 * sw]
            zb = st_ref[pl.ds(r, 1), 2 * sw:4 * sw]
            st_ref[pl.ds(i, 1), 0:2 * sw] = jnp.concatenate([sfr, sfi], axis=-1)
            st_ref[pl.ds(r, 1), 2 * sw:4 * sw] = jnp.concatenate([sbr, sbi], axis=-1)
            nfr = afr * sfr - afi * sfi + zf[:, :sw]
            nfi = afi * sfr + afr * sfi + zf[:, sw:]
            nbr = abr * sbr - abi * sbi + zb[:, :sw]
            nbi = abi * sbr + abr * sbi + zb[:, sw:]
            return nfr, nfi, nbr, nbi

        zero = jnp.zeros((1, sw), f32)
        lax.fori_loop(0, N_CHUNKS, step, (zero, zero, zero, zero))

    @pl.when(k == 1)
    def _():
        acc_ref[...] = jnp.dot(ucat_ref[...], w_ref[0, 0], preferred_element_type=f32)

    @pl.when(k == 2)
    def _():
        acc_ref[...] += jnp.dot(st_ref[...].astype(bf16), w_ref[0, 0], preferred_element_type=f32)
        d = d_ref[...]
        for t in range(CHUNK_T):
            rows = pl.ds(t, N_CHUNKS, stride=CHUNK_T)
            v = acc_ref[:, t * LANES:(t + 1) * LANES] + d * u_ref[rows, :]
            y_ref[rows, :] = _gelu_tanh(v)


def _s5(u, w_all, dec, ssm_d):
    return pl.pallas_call(
        _s5_kernel,
        name="s5_scan",
        grid=(N_LANE_BLOCKS, 3),
        in_specs=[
            pl.BlockSpec((SEQ, LANES), lambda j, k: (0, j)),
            pl.BlockSpec((1, 1, CAT_W, CAT_W), lambda j, k: (j, k, 0, 0)),
            pl.BlockSpec((1, 1, 4 * STATE_W), lambda j, k: (j, 0, 0)),
            pl.BlockSpec((1, LANES), lambda j, k: (0, j)),
        ],
        out_specs=pl.BlockSpec((SEQ, LANES), lambda j, k: (0, j)),
        out_shape=jax.ShapeDtypeStruct((SEQ, SSM_WIDTH), f32),
        scratch_shapes=[
            pltpu.VMEM((N_CHUNKS, CAT_W), bf16),
            pltpu.VMEM((N_CHUNKS, 4 * STATE_W), f32),
            pltpu.VMEM((N_CHUNKS, CAT_W), f32),
        ],
        compiler_params=_cparams(("arbitrary", "arbitrary")),
    )(u, w_all, dec, ssm_d.reshape(1, SSM_WIDTH))


GLU_TM = 512


def _glu_kernel(y_ref, w_ref, b_ref, g_ref, o_ref):
    y = y_ref[...]
    z = jnp.dot(y.astype(bf16), w_ref[...], preferred_element_type=f32) + b_ref[...]
    o = y * (1.0 / (1.0 + jnp.exp(-z)))
    o_ref[...] = _rms(o, g_ref[...]).astype(bf16)


def _glu(y, w_glu_bf, b_glu, g_ssm_out):
    return pl.pallas_call(
        _glu_kernel,
        name="glu_norm",
        grid=(SEQ // GLU_TM,),
        in_specs=[
            pl.BlockSpec((GLU_TM, SSM_WIDTH), lambda i: (i, 0)),
            pl.BlockSpec((SSM_WIDTH, SSM_WIDTH), lambda i: (0, 0)),
            pl.BlockSpec((1, SSM_WIDTH), lambda i: (0, 0)),
            pl.BlockSpec((1, SSM_WIDTH), lambda i: (0, 0)),
        ],
        out_specs=pl.BlockSpec((GLU_TM, SSM_WIDTH), lambda i: (i, 0)),
        out_shape=jax.ShapeDtypeStruct((SEQ, SSM_WIDTH), bf16),
        compiler_params=_cparams(("arbitrary",)),
    )(y, w_glu_bf, b_glu.reshape(1, SSM_WIDTH), g_ssm_out.reshape(1, SSM_WIDTH))


NA_ROWS_PER_STEP = 8
NA_WIN = NA_KH * GRID_W
HEADS_PER_BLOCK = LANES // NA_HEAD_DIM


def _na_bias(rpb):
    v = jnp.arange(NA_KH)
    i = jnp.arange(NA_KH)
    dr = i[None, :] + (NA_KH - 1) - v[:, None]
    c = jnp.arange(GRID_W)
    col_start = jnp.clip(c - NA_KW // 2, 0, GRID_W - NA_KW)
    valid = (c[None, :] >= col_start[:, None]) & (c[None, :] < col_start[:, None] + NA_KW)
    dc = jnp.clip(c[None, :] - c[:, None], -(NA_KW - 1), NA_KW - 1) + (NA_KW - 1)
    b = rpb.astype(f32)[:, dr[:, None, :, None], dc[None, :, None, :]]
    b = jnp.where(valid[None, None, :, None, :], b, NEG_BIG)
    return b.reshape(NA_HEADS, NA_KH, GRID_W, NA_WIN)


def _natten_kernel(q_ref, k_ref, v_ref, b_ref, o_ref):
    rb = pl.program_id(1)
    lane = lax.broadcasted_iota(jnp.int32, (GRID_W, LANES), 1)
    head0 = lane < NA_HEAD_DIM
    scale = NA_HEAD_DIM ** -0.5

    def row(i, _):
        r = rb * NA_ROWS_PER_STEP + i
        rs = jnp.clip(r - NA_KH // 2, 0, GRID_ROWS - NA_KH)
        var = r - rs
        q = q_ref[pl.ds(pl.multiple_of(i * GRID_W, GRID_W), GRID_W), :] * scale
        start = pl.multiple_of(rs * GRID_W, GRID_W)
        kw = k_ref[pl.ds(start, NA_WIN), :]
        vw = v_ref[pl.ds(start, NA_WIN), :]
        outs = []
        for h in range(HEADS_PER_BLOCK):
            qh = jnp.where(head0 if h == 0 else ~head0, q, jnp.zeros_like(q))
            s = lax.dot_general(qh, kw, (((1,), (1,)), ((), ())), preferred_element_type=f32)
            s = s + b_ref[h, var]
            m = jnp.max(s, axis=-1, keepdims=True)
            p = jnp.exp(s - m)
            l = jnp.sum(p, axis=-1, keepdims=True)
            o = jnp.dot(p.astype(bf16), vw, preferred_element_type=f32)
            outs.append(o / l)
        o_ref[pl.ds(pl.multiple_of(i * GRID_W, GRID_W), GRID_W), :] = (
            jnp.where(head0, outs[0], outs[1]).astype(bf16))
        return 0

    lax.fori_loop(0, NA_ROWS_PER_STEP, row, 0)


def _natten(qkv, bias):
    tm = NA_ROWS_PER_STEP * GRID_W
    n_hb = NA_WIDTH // LANES
    return pl.pallas_call(
        _natten_kernel,
        name="natten",
        grid=(n_hb, GRID_ROWS // NA_ROWS_PER_STEP),
        in_specs=[
            pl.BlockSpec((tm, LANES), lambda h, r: (r, h)),
            pl.BlockSpec((SEQ, LANES), lambda h, r: (0, n_hb + h)),
            pl.BlockSpec((SEQ, LANES), lambda h, r: (0, 2 * n_hb + h)),
            pl.BlockSpec((HEADS_PER_BLOCK, NA_KH, GRID_W, NA_WIN), lambda h, r: (h, 0, 0, 0)),
        ],
        out_specs=pl.BlockSpec((tm, LANES), lambda h, r: (r, h)),
        out_shape=jax.ShapeDtypeStruct((SEQ, NA_WIDTH), bf16),
        compiler_params=_cparams(("arbitrary", "arbitrary")),
    )(qkv, qkv, qkv, bias)


OUT_TM = 512


def _out_proj_kernel(ssm_ref, na_ref, x_ref, gna_ref, w_ref, gmoe_ref, wr_ref, br_ref,
                     x1_ref, hn_ref, lg_ref):
    na = _rms(na_ref[...].astype(f32), gna_ref[...]).astype(bf16)
    y = jnp.dot(ssm_ref[...], w_ref[0:SSM_WIDTH, :], preferred_element_type=f32)
    y = y + jnp.dot(na, w_ref[SSM_WIDTH:, :], preferred_element_type=f32)
    x1 = x_ref[...] + y
    x1_ref[...] = x1
    hn = _rms(x1, gmoe_ref[...])
    hn_ref[...] = hn
    lg_ref[...] = jnp.dot(hn.astype(bf16), wr_ref[...], preferred_element_type=f32) + br_ref[...]


def _out_proj(ssm_n, y_na, x, g_na_out, w_out_bf, g_moe, w_router_pad, b_router_pad):
    row = lambda i: (i, 0)
    fixed = lambda i: (0, 0)
    return pl.pallas_call(
        _out_proj_kernel,
        name="out_proj",
        grid=(SEQ // OUT_TM,),
        in_specs=[
            pl.BlockSpec((OUT_TM, SSM_WIDTH), row),
            pl.BlockSpec((OUT_TM, NA_WIDTH), row),
            pl.BlockSpec((OUT_TM, D_MODEL), row),
            pl.BlockSpec((1, NA_WIDTH), fixed),
            pl.BlockSpec((D_MODEL, D_MODEL), fixed),
            pl.BlockSpec((1, D_MODEL), fixed),
            pl.BlockSpec((D_MODEL, LANES), fixed),
            pl.BlockSpec((1, LANES), fixed),
        ],
        out_specs=[
            pl.BlockSpec((OUT_TM, D_MODEL), row),
            pl.BlockSpec((OUT_TM, D_MODEL), row),
            pl.BlockSpec((OUT_TM, LANES), row),
        ],
        out_shape=[
            jax.ShapeDtypeStruct((SEQ, D_MODEL), f32),
            jax.ShapeDtypeStruct((SEQ, D_MODEL), f32),
            jax.ShapeDtypeStruct((SEQ, LANES), f32),
        ],
        compiler_params=_cparams(("arbitrary",)),
    )(ssm_n, y_na, x, g_na_out.reshape(1, NA_WIDTH), w_out_bf, g_moe.reshape(1, D_MODEL),
      w_router_pad, b_router_pad)


ROUTE_TM = 512


def _route_kernel(lg_ref, tri_ref, meta_ref, gate_ref, cnt_ref, carry_ref):
    i = pl.program_id(0)

    @pl.when(i == 0)
    def _():
        carry_ref[...] = jnp.zeros_like(carry_ref)

    lane = lax.broadcasted_iota(jnp.int32, (ROUTE_TM, LANES), 1)
    lane_f = lane.astype(f32)
    work = lg_ref[...]
    vals, hits = [], []
    for _ in range(TOP_K):
        m = jnp.max(work, axis=-1, keepdims=True)
        idx = jnp.min(jnp.where(work == m, lane_f, float(LANES)), axis=-1, keepdims=True)
        hit = lane_f == idx
        vals.append(m)
        hits.append((idx, hit))
        work = jnp.where(hit, -jnp.inf, work)

    exps = [jnp.exp(v - vals[0]) for v in vals]
    denom = exps[0] + exps[1] + exps[2] + exps[3]

    onehot = jnp.zeros((ROUTE_TM, LANES), f32)
    for _, hit in hits:
        onehot = onehot + hit.astype(f32)
    before = jnp.dot(tri_ref[...], onehot.astype(bf16), preferred_element_type=f32) + carry_ref[...]

    meta = jnp.zeros((ROUTE_TM, LANES), jnp.int32)
    gate = jnp.zeros((ROUTE_TM, LANES), f32)
    for k, (idx, hit) in enumerate(hits):
        rank = jnp.sum(jnp.where(hit, before, 0.0), axis=-1, keepdims=True).astype(jnp.int32)
        meta = jnp.where(lane == k, idx.astype(jnp.int32), meta)
        meta = jnp.where(lane == TOP_K + k, rank, meta)
        gate = jnp.where(lane == k, exps[k] / denom, gate)
    meta_ref[...] = meta
    gate_ref[...] = gate
    carry_ref[...] += jnp.sum(onehot, axis=0, keepdims=True)
    cnt_ref[...] = carry_ref[...]


def _route(logits):
    tri = (jnp.arange(ROUTE_TM)[:, None] > jnp.arange(ROUTE_TM)[None, :]).astype(bf16)
    row = lambda i: (i, 0)
    return pl.pallas_call(
        _route_kernel,
        name="route",
        grid=(SEQ // ROUTE_TM,),
        in_specs=[
            pl.BlockSpec((ROUTE_TM, LANES), row),
            pl.BlockSpec((ROUTE_TM, ROUTE_TM), lambda i: (0, 0)),
        ],
        out_specs=[
            pl.BlockSpec((ROUTE_TM, LANES), row),
            pl.BlockSpec((ROUTE_TM, LANES), row),
            pl.BlockSpec((1, LANES), lambda i: (0, 0)),
        ],
        out_shape=[
            jax.ShapeDtypeStruct((SEQ, LANES), jnp.int32),
            jax.ShapeDtypeStruct((SEQ, LANES), f32),
            jax.ShapeDtypeStruct((1, LANES), f32),
        ],
        scratch_shapes=[pltpu.VMEM((1, LANES), f32)],
        compiler_params=_cparams(("arbitrary",)),
    )(logits, tri)


def _routing_tables(meta, counts):
    idx = meta[:, :TOP_K]
    rank = meta[:, TOP_K:2 * TOP_K]
    cnt = counts[0, :N_EXPERTS].astype(jnp.int32)
    n_chunks = (cnt + ROW_CHUNK - 1) // ROW_CHUNK
    chunk_base = jnp.cumsum(n_chunks) - n_chunks
    dest = (chunk_base[idx] * ROW_CHUNK + rank).astype(jnp.int32).reshape(-1)
    total_chunks = jnp.sum(n_chunks)
    last_chunk = jnp.concatenate([jnp.where(cnt > 0, chunk_base + n_chunks - 1, -1),
                                  total_chunks[None]]).astype(jnp.int32)

    n_sb = (n_chunks + CHUNKS_PER_SB - 1) // CHUNKS_PER_SB
    sb_end = jnp.cumsum(n_sb)
    sb_start = sb_end - n_sb
    n_used = sb_end[-1]
    s = jnp.arange(MAX_SB)
    s_eff = jnp.minimum(s, n_used - 1)
    e = jnp.minimum(jnp.searchsorted(sb_end, s_eff, side='right'), N_EXPERTS - 1)
    kk = s_eff - sb_start[e]
    sb_chunk0 = chunk_base[e] + kk * CHUNKS_PER_SB
    sb_n = jnp.where(s < n_used, jnp.clip(n_chunks[e] - kk * CHUNKS_PER_SB, 0, CHUNKS_PER_SB), 0)
    used = jnp.stack([n_used, total_chunks]).astype(jnp.int32)
    return (dest, last_chunk, e.astype(jnp.int32), sb_chunk0.astype(jnp.int32),
            sb_n.astype(jnp.int32), used)


DISP_TM = 256


def _row_copy(src, src_row, dst, dst_row, sem):
    return pltpu.make_async_copy(src.at[pl.ds(src_row, 1), :], dst.at[pl.ds(dst_row, 1), :], sem)


def _dispatch_kernel(last_ref, dest_ref, hn_ref, xs_ref, zero_ref, sem_ref):
    i = pl.program_id(0)

    @pl.when(i == 0)
    def _():
        zero_ref[...] = jnp.zeros_like(zero_ref)

        def chunk_copy(c):
            row0 = pl.multiple_of(c * ROW_CHUNK, ROW_CHUNK)
            return pltpu.make_async_copy(zero_ref, xs_ref.at[pl.ds(row0, ROW_CHUNK), :], sem_ref.at[1])

        def start(e, _):
            @pl.when(last_ref[e] >= 0)
            def _():
                chunk_copy(last_ref[e]).start()
            return 0

        def wait(e, _):
            @pl.when(last_ref[e] >= 0)
            def _():
                chunk_copy(last_ref[e]).wait()
            return 0

        def start_tail(c, _):
            chunk_copy(c).start()
            return 0

        def wait_tail(c, _):
            chunk_copy(c).wait()
            return 0

        lax.fori_loop(0, N_EXPERTS, start, 0)
        lax.fori_loop(last_ref[N_EXPERTS], MAX_CHUNKS, start_tail, 0)
        lax.fori_loop(0, N_EXPERTS, wait, 0)
        lax.fori_loop(last_ref[N_EXPERTS], MAX_CHUNKS, wait_tail, 0)

    def issue(t, _):
        for k in range(TOP_K):
            _row_copy(hn_ref, t, xs_ref, dest_ref[t * TOP_K + k], sem_ref.at[0]).start()
        return 0

    lax.fori_loop(0, DISP_TM, issue, 0)
    for k in range(TOP_K):
        pltpu.make_async_copy(hn_ref, xs_ref.at[pl.ds(0, DISP_TM), :], sem_ref.at[0]).wait()


def _dispatch(last_chunk, dest, hn):
    return pl.pallas_call(
        _dispatch_kernel,
        name="dispatch",
        grid_spec=pltpu.PrefetchScalarGridSpec(
            num_scalar_prefetch=1,
            grid=(SEQ // DISP_TM,),
            in_specs=[
                pl.BlockSpec((DISP_TM * TOP_K,), lambda i, last: (i,), memory_space=pltpu.SMEM),
                pl.BlockSpec((DISP_TM, D_MODEL), lambda i, last: (i, 0)),
            ],
            out_specs=pl.BlockSpec(memory_space=pl.ANY),
            scratch_shapes=[
                pltpu.VMEM((ROW_CHUNK, D_MODEL), f32),
                pltpu.SemaphoreType.DMA((2,)),
            ],
        ),
        out_shape=jax.ShapeDtypeStruct((MAX_ROWS, D_MODEL), f32),
        compiler_params=_cparams(("arbitrary",)),
    )(last_chunk, dest, hn)


def _experts_kernel(e_ref, c0_ref, n_ref, used_ref,
                    xs_ref, wg_ref, wu_ref, wd_ref, bg_ref, bu_ref, bd_ref, ys_ref,
                    xin_ref, xbf_ref, act_ref, yt_ref, wa_ref, wb_ref, sem_ref):
    s = pl.program_id(0)
    j = pl.program_id(1)
    n = n_ref[s]
    c0 = c0_ref[s]

    def rows(c):
        return pl.ds(pl.multiple_of(c * ROW_CHUNK, ROW_CHUNK), ROW_CHUNK)

    def in_copy(c):
        return pltpu.make_async_copy(xs_ref.at[rows(c0 + c), :], xin_ref.at[rows(c), :], sem_ref.at[0])

    @pl.when(jnp.logical_and(n > 0, j == 0))
    def _():
        def start(c, _):
            in_copy(c).start()
            return 0

        def finish(c, _):
            in_copy(c).wait()
            return 0

        def cast(c, _):
            xbf_ref[rows(c), :] = xin_ref[rows(c), :].astype(bf16)
            return 0

        lax.fori_loop(0, n, start, 0)
        lax.fori_loop(0, n, finish, 0)
        lax.fori_loop(0, n, cast, 0)

    @pl.when(jnp.logical_and(n > 0, j < N_FF_TILES))
    def _():
        wa_ref[...] = wg_ref[0].astype(bf16)
        wb_ref[...] = wu_ref[0].astype(bf16)
        bg = bg_ref[0]
        bu = bu_ref[0]

        def chunk(c, _):
            x = xbf_ref[rows(c), :]
            g = jnp.dot(x, wa_ref[...], preferred_element_type=f32) + bg
            u = jnp.dot(x, wb_ref[...], preferred_element_type=f32) + bu
            g = jnp.minimum(g, SWIGLU_LIMIT)
            u = jnp.clip(u, -SWIGLU_LIMIT, SWIGLU_LIMIT)
            a = (u + 1.0) * (g * (1.0 / (1.0 + jnp.exp(-SWIGLU_ALPHA * g))))
            act_ref[j, rows(c), :] = a.astype(bf16)
            return 0

        lax.fori_loop(0, n, chunk, 0)

    @pl.when(jnp.logical_and(n > 0, j >= N_FF_TILES))
    def _():
        d = j - N_FF_TILES
        wa_ref[...] = wd_ref[0].astype(bf16)
        bd = bd_ref[0]
        cols = pl.ds(pl.multiple_of(d * FF_TILE, FF_TILE), FF_TILE)

        def out_copy(c):
            return pltpu.make_async_copy(yt_ref.at[rows(c), :], ys_ref.at[rows(c0 + c), cols], sem_ref.at[1])

        def chunk(c, _):
            a = jnp.concatenate([act_ref[f, rows(c), :] for f in range(N_FF_TILES)], axis=-1)
            yt_ref[rows(c), :] = jnp.dot(a, wa_ref[...], preferred_element_type=f32) + bd
            out_copy(c).start()
            return 0

        def finish(c, _):
            out_copy(c).wait()
            return 0

        lax.fori_loop(0, n, chunk, 0)
        lax.fori_loop(0, n, finish, 0)

    @pl.when(jnp.logical_and(s == MAX_SB - 1, j == N_FF_TILES + N_OUT_TILES - 1))
    def _():
        xin_ref[0:ROW_CHUNK, :] = jnp.zeros((ROW_CHUNK, D_MODEL), f32)

        def tail_copy(c):
            return pltpu.make_async_copy(xin_ref.at[0:ROW_CHUNK, :], ys_ref.at[rows(c), :], sem_ref.at[0])

        def start(c, _):
            tail_copy(c).start()
            return 0

        def finish(c, _):
            tail_copy(c).wait()
            return 0

        lax.fori_loop(used_ref[1], MAX_CHUNKS, start, 0)
        lax.fori_loop(used_ref[1], MAX_CHUNKS, finish, 0)


def _experts(sb_e, sb_c0, sb_n, n_used, xs, w_gate, b_gate, w_up, b_up, w_down, b_down):
    last_up = N_FF_TILES - 1

    def up_map(s, j, e, c0, n, used):
        live = s < used[0]
        return (e[s], 0, jnp.where(live, jnp.minimum(j, last_up), last_up))

    def down_map(s, j, e, c0, n, used):
        live = s < used[0]
        return (e[s], 0, jnp.where(live, jnp.maximum(j - N_FF_TILES, 0), N_OUT_TILES - 1))

    return pl.pallas_call(
        _experts_kernel,
        name="experts",
        grid_spec=pltpu.PrefetchScalarGridSpec(
            num_scalar_prefetch=4,
            grid=(MAX_SB, N_FF_TILES + N_OUT_TILES),
            in_specs=[
                pl.BlockSpec(memory_space=pl.ANY),
                pl.BlockSpec((1, D_MODEL, FF_TILE), up_map),
                pl.BlockSpec((1, D_MODEL, FF_TILE), up_map),
                pl.BlockSpec((1, D_FF, FF_TILE), down_map),
                pl.BlockSpec((1, 1, FF_TILE), up_map),
                pl.BlockSpec((1, 1, FF_TILE), up_map),
                pl.BlockSpec((1, 1, FF_TILE), down_map),
            ],
            out_specs=pl.BlockSpec(memory_space=pl.ANY),
            scratch_shapes=[
                pltpu.VMEM((SB_ROWS, D_MODEL), f32),
                pltpu.VMEM((SB_ROWS, D_MODEL), bf16),
                pltpu.VMEM((N_FF_TILES, SB_ROWS, FF_TILE), bf16),
                pltpu.VMEM((SB_ROWS, FF_TILE), f32),
                pltpu.VMEM((D_MODEL, FF_TILE), bf16),
                pltpu.VMEM((D_MODEL, FF_TILE), bf16),
                pltpu.SemaphoreType.DMA((2,)),
            ],
        ),
        out_shape=jax.ShapeDtypeStruct((MAX_ROWS, D_MODEL), f32),
        compiler_params=_cparams(("arbitrary", "arbitrary")),
    )(sb_e, sb_c0, sb_n, n_used, xs, w_gate, w_up, w_down,
      b_gate.reshape(N_EXPERTS, 1, D_FF), b_up.reshape(N_EXPERTS, 1, D_FF),
      b_down.reshape(N_EXPERTS, 1, D_MODEL))


COMB_TM = 256


def _combine_kernel(dest_ref, ys_ref, x1_ref, gate_ref, gf_ref, o_ref, buf_ref, sem_ref):
    def issue(t, _):
        for k in range(TOP_K):
            pltpu.make_async_copy(ys_ref.at[pl.ds(dest_ref[t * TOP_K + k], 1), :],
                                  buf_ref.at[k, pl.ds(t, 1), :], sem_ref.at[0]).start()
        return 0

    lax.fori_loop(0, COMB_TM, issue, 0)
    for k in range(TOP_K):
        pltpu.make_async_copy(ys_ref.at[pl.ds(0, COMB_TM), :], buf_ref.at[k], sem_ref.at[0]).wait()

    gate = gate_ref[...]
    acc = x1_ref[...]
    for k in range(TOP_K):
        acc = acc + gate[:, k:k + 1] * buf_ref[k]
    o_ref[...] = _rms(acc, gf_ref[...])


def _combine(dest, ys, x1, gates, g_final):
    return pl.pallas_call(
        _combine_kernel,
        name="combine",
        grid=(SEQ // COMB_TM,),
        in_specs=[
            pl.BlockSpec((COMB_TM * TOP_K,), lambda i: (i,), memory_space=pltpu.SMEM),
            pl.BlockSpec(memory_space=pl.ANY),
            pl.BlockSpec((COMB_TM, D_MODEL), lambda i: (i, 0)),
            pl.BlockSpec((COMB_TM, LANES), lambda i: (i, 0)),
            pl.BlockSpec((1, D_MODEL), lambda i: (0, 0)),
        ],
        out_specs=pl.BlockSpec((COMB_TM, D_MODEL), lambda i: (i, 0)),
        out_shape=jax.ShapeDtypeStruct((SEQ, D_MODEL), f32),
        scratch_shapes=[
            pltpu.VMEM((TOP_K, COMB_TM, D_MODEL), f32),
            pltpu.SemaphoreType.DMA((1,)),
        ],
        compiler_params=_cparams(("arbitrary",)),
    )(dest, ys, x1, gates, g_final.reshape(1, D_MODEL))


def kernel(x, g_mix, w_in, lam_re_fwd, lam_im_fwd, log_dt_fwd, b_re_fwd, b_im_fwd, c_re_fwd, c_im_fwd, lam_re_bwd, lam_im_bwd, log_dt_bwd, b_re_bwd, b_im_bwd, c_re_bwd, c_im_bwd, ssm_d, w_glu, b_glu, na_rpb, g_ssm_out, g_na_out, w_out, g_moe, w_router, b_router, w_gate, b_gate, w_up, b_up, w_down, b_down, g_final):
    x2 = x.reshape(SEQ, D_MODEL)

    u, qkv = _in_proj(x2, g_mix[0], w_in[0].astype(bf16))

    s5_w, s5_dec = _s5_weights(
        (lam_re_fwd[0], lam_im_fwd[0], log_dt_fwd[0], b_re_fwd[0], b_im_fwd[0], c_re_fwd[0], c_im_fwd[0]),
        (lam_re_bwd[0], lam_im_bwd[0], log_dt_bwd[0], b_re_bwd[0], b_im_bwd[0], c_re_bwd[0], c_im_bwd[0]))
    y = _s5(u, s5_w, s5_dec, ssm_d[0])
    ssm_n = _glu(y, w_glu[0].astype(bf16), b_glu[0], g_ssm_out[0])

    y_na = _natten(qkv, _na_bias(na_rpb[0]))

    w_router_pad = jnp.zeros((D_MODEL, LANES), bf16).at[:, :N_EXPERTS].set(w_router[0].astype(bf16))
    b_router_pad = jnp.full((1, LANES), NEG_BIG, f32).at[0, :N_EXPERTS].set(b_router[0].astype(f32))
    x1, hn, logits = _out_proj(ssm_n, y_na, x2, g_na_out[0], w_out[0].astype(bf16), g_moe[0],
                               w_router_pad, b_router_pad)

    meta, gates, counts = _route(logits)
    dest, last_chunk, sb_e, sb_c0, sb_n, n_used = _routing_tables(meta, counts)

    xs = _dispatch(last_chunk, dest, hn)
    ys = _experts(sb_e, sb_c0, sb_n, n_used, xs, w_gate[0], b_gate[0], w_up[0], b_up[0],
                  w_down[0], b_down[0])
    out = _combine(dest, ys, x1, gates, g_final)
    return out.reshape(x.shape)
```

```python
import functools
import math

import jax
import jax.numpy as jnp
from jax import lax
from jax.experimental import pallas as pl
from jax.experimental.pallas import tpu as pltpu

f32 = jnp.float32
bf16 = jnp.bfloat16

D_MODEL = 2048
SEQ = 8192
SSM_WIDTH = 1024
NA_WIDTH = 1024
SSM_GROUP = 16
SSM_GROUPS = 64
SSM_STATE = 64
NA_HEAD_DIM = 64
NA_HEADS = 16
GRID_W = 64
GRID_ROWS = SEQ // GRID_W
NA_KH = 8
NA_KW = 16
N_EXPERTS = 32
TOP_K = 4
D_FF = 2048
SWIGLU_LIMIT = 7.0
SWIGLU_ALPHA = 1.702
RMS_EPS = 1e-5

LANES = 128
NEG_BIG = -1e30

CHUNK_T = 16
N_CHUNKS = SEQ // CHUNK_T
GROUPS_PER_BLOCK = LANES // SSM_GROUP
N_LANE_BLOCKS = SSM_WIDTH // LANES
CAT_W = CHUNK_T * LANES
STATE_W = GROUPS_PER_BLOCK * SSM_STATE

ROW_CHUNK = 128
CHUNKS_PER_SB = 10
SB_ROWS = ROW_CHUNK * CHUNKS_PER_SB
MAX_CHUNKS = SEQ * TOP_K // ROW_CHUNK + N_EXPERTS
MAX_ROWS = MAX_CHUNKS * ROW_CHUNK
MAX_SB = MAX_CHUNKS // CHUNKS_PER_SB + N_EXPERTS
FF_TILE = 256
N_FF_TILES = D_FF // FF_TILE
N_OUT_TILES = D_MODEL // FF_TILE

VMEM_LIMIT = 56 * 1024 * 1024


def _cparams(semantics, vmem=VMEM_LIMIT):
    return pltpu.CompilerParams(dimension_semantics=semantics, vmem_limit_bytes=vmem)


def _rms(x, g):
    return x * lax.rsqrt(jnp.mean(x * x, axis=-1, keepdims=True) + RMS_EPS) * g


IN_TM = 512
IN_TN = 1024


def _in_proj_kernel(x_ref, g_ref, w_ref, u_ref, qkv_ref, h_ref):
    j = pl.program_id(1)

    @pl.when(j == 0)
    def _():
        h_ref[...] = _rms(x_ref[...], g_ref[...]).astype(bf16)

    acc = jnp.dot(h_ref[...], w_ref[...], preferred_element_type=f32)

    @pl.when(j == 0)
    def _():
        u_ref[...] = acc

    @pl.when(j > 0)
    def _():
        qkv_ref[...] = acc.astype(bf16)


def _in_proj(x, g_mix, w_in_bf):
    n_out = w_in_bf.shape[1]
    return pl.pallas_call(
        _in_proj_kernel,
        name="in_proj",
        grid=(SEQ // IN_TM, n_out // IN_TN),
        in_specs=[
            pl.BlockSpec((IN_TM, D_MODEL), lambda i, j: (i, 0)),
            pl.BlockSpec((1, D_MODEL), lambda i, j: (0, 0)),
            pl.BlockSpec((D_MODEL, IN_TN), lambda i, j: (0, j)),
        ],
        out_specs=[
            pl.BlockSpec((IN_TM, IN_TN), lambda i, j: (i, 0)),
            pl.BlockSpec((IN_TM, IN_TN), lambda i, j: (i, jnp.maximum(j - 1, 0))),
        ],
        out_shape=[
            jax.ShapeDtypeStruct((SEQ, SSM_WIDTH), f32),
            jax.ShapeDtypeStruct((SEQ, 3 * NA_WIDTH), bf16),
        ],
        scratch_shapes=[pltpu.VMEM((IN_TM, D_MODEL), bf16)],
        compiler_params=_cparams(("arbitrary", "arbitrary")),
    )(x, g_mix.reshape(1, D_MODEL), w_in_bf)


def _cmul(ar, ai, br, bi):
    return ar * br - ai * bi, ar * bi + ai * br


def _s5_discretise(lam_re, lam_im, log_dt, b_re, b_im, c_re, c_im):
    a = jnp.minimum(lam_re.astype(f32), -1e-4)
    w = lam_im.astype(f32)
    dt = jnp.exp(log_dt.astype(f32))[:, None]
    steps = jnp.arange(CHUNK_T + 1, dtype=f32)[:, None, None]
    mag = jnp.exp((a * dt)[None] * steps)
    ang = (w * dt)[None] * steps
    pw = (mag * jnp.cos(ang), mag * jnp.sin(ang))
    xr, xi = pw[0][1] - 1.0, pw[1][1]
    den = a * a + w * w
    qr, qi = (xr * a + xi * w) / den, (xi * a - xr * w) / den
    bb = _cmul(qr[..., None], qi[..., None], b_re.astype(f32), b_im.astype(f32))
    return pw, bb, (c_re.astype(f32), c_im.astype(f32))


def _pair_blockdiag(m):
    z = jnp.zeros_like(m[..., 0, :, :])
    top = jnp.concatenate([m[..., 0, :, :], z], axis=-1)
    bot = jnp.concatenate([z, m[..., 1, :, :]], axis=-1)
    return jnp.concatenate([top, bot], axis=-2)


def _s5_weights(fwd, bwd):
    hi = lax.Precision.HIGHEST
    t = CHUNK_T
    pw_f, bb_f, c_f = _s5_discretise(*fwd)
    pw_b, bb_b, c_b = _s5_discretise(*bwd)
    pairs = (N_LANE_BLOCKS, GROUPS_PER_BLOCK // 2, 2)

    def lag_kernel(pw, bb, c):
        m_re, m_im = _cmul(c[0][None], c[1][None], pw[0][:t, :, None, :], pw[1][:t, :, None, :])
        return (jnp.einsum('jgcp,gpd->jgcd', m_re, bb[0], precision=hi)
                - jnp.einsum('jgcp,gpd->jgcd', m_im, bb[1], precision=hi))

    kf, kb = lag_kernel(pw_f, bb_f, c_f), lag_kernel(pw_b, bb_b, c_b)
    k_lag = jnp.concatenate([kb[:0:-1], (kf[0] + kb[0])[None], kf[1:]], axis=0)
    lag = jnp.arange(t)[:, None] - jnp.arange(t)[None, :] + (t - 1)
    sel = (lag[:, :, None] == jnp.arange(2 * t - 1)).astype(f32)
    w_intra = jnp.einsum('oil,lgcd->gidoc', sel, k_lag, precision=hi)
    w_intra = _pair_blockdiag(w_intra.reshape(*pairs, t * SSM_GROUP, t * SSM_GROUP))

    in_f = _cmul(pw_f[0][:t][::-1][..., None], pw_f[1][:t][::-1][..., None], bb_f[0][None], bb_f[1][None])
    in_b = _cmul(pw_b[0][:t][..., None], pw_b[1][:t][..., None], bb_b[0][None], bb_b[1][None])

    def state_in(m):
        m = jnp.transpose(m, (1, 0, 3, 2)).reshape(*pairs, t * SSM_GROUP, SSM_STATE)
        return _pair_blockdiag(m)

    w_in = jnp.concatenate([state_in(in_f[0]), state_in(in_f[1]),
                            state_in(in_b[0]), state_in(in_b[1])], axis=-1)

    out_f = _cmul(c_f[0][None], c_f[1][None], pw_f[0][1:t + 1][:, :, None, :], pw_f[1][1:t + 1][:, :, None, :])
    out_b = _cmul(c_b[0][None], c_b[1][None],
                  pw_b[0][1:t + 1][::-1][:, :, None, :], pw_b[1][1:t + 1][::-1][:, :, None, :])

    def state_out(m):
        m = jnp.transpose(m, (1, 3, 0, 2)).reshape(*pairs, SSM_STATE, t * SSM_GROUP)
        return _pair_blockdiag(m)

    w_out = jnp.concatenate([state_out(out_f[0]), state_out(-out_f[1]),
                             state_out(out_b[0]), state_out(-out_b[1])], axis=-2)

    def decay(a):
        return a.reshape(N_LANE_BLOCKS, 1, STATE_W)

    dec = jnp.concatenate([decay(pw_f[0][t]), decay(pw_f[1][t]), decay(pw_b[0][t]), decay(pw_b[1][t])],
                          axis=-1)
    w_all = jnp.stack([w_in, w_intra, w_out], axis=2).astype(bf16)
    return w_all, dec.astype(f32)


def _gelu_tanh(x):
    return 0.5 * x * (1.0 + jnp.tanh(math.sqrt(2.0 / math.pi) * (x + 0.044715 * (x * x * x))))


PAIRS_PER_BLOCK = GROUPS_PER_BLOCK // 2
PAIR_W = 2 * CHUNK_T * SSM_GROUP


def _s5_kernel(u_ref, w_ref, dec_ref, d_ref, y_ref, cat_ref, catp_ref, st_ref, perm_ref):
    sw = STATE_W

    @pl.when(pl.program_id(0) == 0)
    def _():
        def strip(t, _):
            row = lax.broadcasted_iota(jnp.int32, (LANES, CAT_W), 0)
            col = lax.broadcasted_iota(jnp.int32, (LANES, CAT_W), 1)
            target = (row // SSM_GROUP) * (CHUNK_T * SSM_GROUP) + t * SSM_GROUP + row % SSM_GROUP
            perm_ref[pl.ds(pl.multiple_of(t * LANES, LANES), LANES), :] = (
                jnp.where(col == target, 1.0, 0.0).astype(bf16))
            return 0

        lax.fori_loop(0, CHUNK_T, strip, 0)

    for t in range(CHUNK_T):
        cat_ref[:, t * LANES:(t + 1) * LANES] = u_ref[pl.ds(t, N_CHUNKS, stride=CHUNK_T), :].astype(bf16)
    catp_ref[...] = jnp.dot(cat_ref[...], perm_ref[...], preferred_element_type=f32).astype(bf16)

    for p in range(PAIRS_PER_BLOCK):
        z = jnp.dot(catp_ref[:, p * PAIR_W:(p + 1) * PAIR_W], w_ref[0, p, 0], preferred_element_type=f32)
        for part in range(4):
            st_ref[:, part * sw + p * LANES:part * sw + (p + 1) * LANES] = z[:, part * LANES:(part + 1) * LANES]

    afr = dec_ref[0, :, 0 * sw:1 * sw]
    afi = dec_ref[0, :, 1 * sw:2 * sw]
    abr = dec_ref[0, :, 2 * sw:3 * sw]
    abi = dec_ref[0, :, 3 * sw:4 * sw]

    def step(i, carry):
        sfr, sfi, sbr, sbi = carry
        r = N_CHUNKS - 1 - i
        zf = st_ref[pl.ds(i, 1), 0:2 * sw]
        zb = st_ref[pl.ds(r, 1), 2 * sw:4 * sw]
        st_ref[pl.ds(i, 1), 0:2 * sw] = jnp.concatenate([sfr, sfi], axis=-1)
        st_ref[pl.ds(r, 1), 2 * sw:4 * sw] = jnp.concatenate([sbr, sbi], axis=-1)
        nfr = afr * sfr - afi * sfi + zf[:, :sw]
        nfi = afi * sfr + afr * sfi + zf[:, sw:]
        nbr = abr * sbr - abi * sbi + zb[:, :sw]
        nbi = abi * sbr + abr * sbi + zb[:, sw:]
        return nfr, nfi, nbr, nbi

    zero = jnp.zeros((1, sw), f32)
    lax.fori_loop(0, N_CHUNKS, step, (zero, zero, zero, zero))

    for p in range(PAIRS_PER_BLOCK):
        s_in = jnp.concatenate(
            [st_ref[:, part * sw + p * LANES:part * sw + (p + 1) * LANES] for part in range(4)], axis=-1)
        yp = jnp.dot(catp_ref[:, p * PAIR_W:(p + 1) * PAIR_W], w_ref[0, p, 1], preferred_element_type=f32)
        yp = yp + jnp.dot(s_in.astype(bf16), w_ref[0, p, 2], preferred_element_type=f32)
        cat_ref[:, p * PAIR_W:(p + 1) * PAIR_W] = yp.astype(bf16)

    st_ref[...] = lax.dot_general(cat_ref[...], perm_ref[...], (((1,), (1,)), ((), ())),
                                  preferred_element_type=f32)
    d = d_ref[...]
    for t in range(CHUNK_T):
        rows = pl.ds(t, N_CHUNKS, stride=CHUNK_T)
        v = st_ref[:, t * LANES:(t + 1) * LANES] + d * u_ref[rows, :]
        y_ref[rows, :] = _gelu_tanh(v)


def _s5(u, w_all, dec, ssm_d):
    return pl.pallas_call(
        _s5_kernel,
        name="s5_scan",
        grid=(N_LANE_BLOCKS,),
        in_specs=[
            pl.BlockSpec((SEQ, LANES), lambda j: (0, j)),
            pl.BlockSpec((1, PAIRS_PER_BLOCK, 3, PAIR_W, PAIR_W), lambda j: (j, 0, 0, 0, 0)),
            pl.BlockSpec((1, 1, 4 * STATE_W), lambda j: (j, 0, 0)),
            pl.BlockSpec((1, LANES), lambda j: (0, j)),
        ],
        out_specs=pl.BlockSpec((SEQ, LANES), lambda j: (0, j)),
        out_shape=jax.ShapeDtypeStruct((SEQ, SSM_WIDTH), f32),
        scratch_shapes=[
            pltpu.VMEM((N_CHUNKS, CAT_W), bf16),
            pltpu.VMEM((N_CHUNKS, CAT_W), bf16),
            pltpu.VMEM((N_CHUNKS, 4 * STATE_W), f32),
            pltpu.VMEM((CAT_W, CAT_W), bf16),
        ],
        compiler_params=_cparams(("arbitrary",)),
    )(u, w_all, dec, ssm_d.reshape(1, SSM_WIDTH))


GLU_TM = 512


def _glu_kernel(y_ref, w_ref, b_ref, g_ref, o_ref):
    y = y_ref[...]
    z = jnp.dot(y.astype(bf16), w_ref[...], preferred_element_type=f32) + b_ref[...]
    o = y * (1.0 / (1.0 + jnp.exp(-z)))
    o_ref[...] = _rms(o, g_ref[...]).astype(bf16)


def _glu(y, w_glu_bf, b_glu, g_ssm_out):
    return pl.pallas_call(
        _glu_kernel,
        name="glu_norm",
        grid=(SEQ // GLU_TM,),
        in_specs=[
            pl.BlockSpec((GLU_TM, SSM_WIDTH), lambda i: (i, 0)),
            pl.BlockSpec((SSM_WIDTH, SSM_WIDTH), lambda i: (0, 0)),
            pl.BlockSpec((1, SSM_WIDTH), lambda i: (0, 0)),
            pl.BlockSpec((1, SSM_WIDTH), lambda i: (0, 0)),
        ],
        out_specs=pl.BlockSpec((GLU_TM, SSM_WIDTH), lambda i: (i, 0)),
        out_shape=jax.ShapeDtypeStruct((SEQ, SSM_WIDTH), bf16),
        compiler_params=_cparams(("arbitrary",)),
    )(y, w_glu_bf, b_glu.reshape(1, SSM_WIDTH), g_ssm_out.reshape(1, SSM_WIDTH))


NA_ROWS_PER_STEP = 8
NA_WIN = NA_KH * GRID_W
HEADS_PER_BLOCK = LANES // NA_HEAD_DIM


def _na_bias(rpb):
    c = jnp.arange(GRID_W)
    col_start = jnp.clip(c - NA_KW // 2, 0, GRID_W - NA_KW)
    valid = (c[None, :] >= col_start[:, None]) & (c[None, :] < col_start[:, None] + NA_KW)
    dc = jnp.clip(c[None, :] - c[:, None], -(NA_KW - 1), NA_KW - 1) + (NA_KW - 1)
    sel = (dc[None] == jnp.arange(2 * NA_KW - 1)[:, None, None]).astype(f32)
    tab = jnp.einsum('hrc,cqk->hrqk', rpb.astype(f32), sel, precision=lax.Precision.HIGHEST)
    tab = jnp.where(valid[None, None], tab, NEG_BIG)
    b = jnp.stack([tab[:, NA_KH - 1 - v:2 * NA_KH - 1 - v] for v in range(NA_KH)], axis=1)
    return jnp.transpose(b, (0, 1, 3, 2, 4)).reshape(NA_HEADS, NA_KH, GRID_W, NA_WIN)


def _natten_kernel(q_ref, k_ref, v_ref, b_ref, o_ref):
    rb = pl.program_id(1)
    lane = lax.broadcasted_iota(jnp.int32, (GRID_W, LANES), 1)
    head0 = lane < NA_HEAD_DIM
    scale = NA_HEAD_DIM ** -0.5

    def row(i, _):
        r = rb * NA_ROWS_PER_STEP + i
        rs = jnp.clip(r - NA_KH // 2, 0, GRID_ROWS - NA_KH)
        var = r - rs
        q = q_ref[pl.ds(pl.multiple_of(i * GRID_W, GRID_W), GRID_W), :] * scale
        start = pl.multiple_of(rs * GRID_W, GRID_W)
        kw = k_ref[pl.ds(start, NA_WIN), :]
        vw = v_ref[pl.ds(start, NA_WIN), :]
        outs = []
        for h in range(HEADS_PER_BLOCK):
            qh = jnp.where(head0 if h == 0 else ~head0, q, jnp.zeros_like(q))
            s = lax.dot_general(qh, kw, (((1,), (1,)), ((), ())), preferred_element_type=f32)
            s = s + b_ref[h, var]
            m = jnp.max(s, axis=-1, keepdims=True)
            p = jnp.exp(s - m)
            l = jnp.sum(p, axis=-1, keepdims=True)
            o = jnp.dot(p.astype(bf16), vw, preferred_element_type=f32)
            outs.append(o / l)
        o_ref[pl.ds(pl.multiple_of(i * GRID_W, GRID_W), GRID_W), :] = (
            jnp.where(head0, outs[0], outs[1]).astype(bf16))
        return 0

    lax.fori_loop(0, NA_ROWS_PER_STEP, row, 0)


def _natten(qkv, bias):
    tm = NA_ROWS_PER_STEP * GRID_W
    n_hb = NA_WIDTH // LANES
    return pl.pallas_call(
        _natten_kernel,
        name="natten",
        grid=(n_hb, GRID_ROWS // NA_ROWS_PER_STEP),
        in_specs=[
            pl.BlockSpec((tm, LANES), lambda h, r: (r, h)),
            pl.BlockSpec((SEQ, LANES), lambda h, r: (0, n_hb + h)),
            pl.BlockSpec((SEQ, LANES), lambda h, r: (0, 2 * n_hb + h)),
            pl.BlockSpec((HEADS_PER_BLOCK, NA_KH, GRID_W, NA_WIN), lambda h, r: (h, 0, 0, 0)),
        ],
        out_specs=pl.BlockSpec((tm, LANES), lambda h, r: (r, h)),
        out_shape=jax.ShapeDtypeStruct((SEQ, NA_WIDTH), bf16),
        compiler_params=_cparams(("arbitrary", "arbitrary")),
    )(qkv, qkv, qkv, bias)


OUT_TM = 512


def _out_proj_kernel(ssm_ref, na_ref, x_ref, gna_ref, w_ref, gmoe_ref, wr_ref, br_ref,
                     x1_ref, hn_ref, lg_ref):
    na = _rms(na_ref[...].astype(f32), gna_ref[...]).astype(bf16)
    y = jnp.dot(ssm_ref[...], w_ref[0:SSM_WIDTH, :], preferred_element_type=f32)
    y = y + jnp.dot(na, w_ref[SSM_WIDTH:, :], preferred_element_type=f32)
    x1 = x_ref[...] + y
    x1_ref[...] = x1
    hn = _rms(x1, gmoe_ref[...])
    hn_ref[...] = hn
    lg_ref[...] = jnp.dot(hn.astype(bf16), wr_ref[...], preferred_element_type=f32) + br_ref[...]


def _out_proj(ssm_n, y_na, x, g_na_out, w_out_bf, g_moe, w_router_pad, b_router_pad):
    row = lambda i: (i, 0)
    fixed = lambda i: (0, 0)
    return pl.pallas_call(
        _out_proj_kernel,
        name="out_proj",
        grid=(SEQ // OUT_TM,),
        in_specs=[
            pl.BlockSpec((OUT_TM, SSM_WIDTH), row),
            pl.BlockSpec((OUT_TM, NA_WIDTH), row),
            pl.BlockSpec((OUT_TM, D_MODEL), row),
            pl.BlockSpec((1, NA_WIDTH), fixed),
            pl.BlockSpec((D_MODEL, D_MODEL), fixed),
            pl.BlockSpec((1, D_MODEL), fixed),
            pl.BlockSpec((D_MODEL, LANES), fixed),
            pl.BlockSpec((1, LANES), fixed),
        ],
        out_specs=[
            pl.BlockSpec((OUT_TM, D_MODEL), row),
            pl.BlockSpec((OUT_TM, D_MODEL), row),
            pl.BlockSpec((OUT_TM, LANES), row),
        ],
        out_shape=[
            jax.ShapeDtypeStruct((SEQ, D_MODEL), f32),
            jax.ShapeDtypeStruct((SEQ, D_MODEL), f32),
            jax.ShapeDtypeStruct((SEQ, LANES), f32),
        ],
        compiler_params=_cparams(("arbitrary",)),
    )(ssm_n, y_na, x, g_na_out.reshape(1, NA_WIDTH), w_out_bf, g_moe.reshape(1, D_MODEL),
      w_router_pad, b_router_pad)


ROUTE_TM = 512


def _route_kernel(lg_ref, tri_ref, meta_ref, gate_ref, cnt_ref, carry_ref):
    i = pl.program_id(0)

    @pl.when(i == 0)
    def _():
        carry_ref[...] = jnp.zeros_like(carry_ref)

    lane = lax.broadcasted_iota(jnp.int32, (ROUTE_TM, LANES), 1)
    lane_f = lane.astype(f32)
    work = lg_ref[...]
    vals, hits = [], []
    for _ in range(TOP_K):
        m = jnp.max(work, axis=-1, keepdims=True)
        idx = jnp.min(jnp.where(work == m, lane_f, float(LANES)), axis=-1, keepdims=True)
        hit = lane_f == idx
        vals.append(m)
        hits.append((idx, hit))
        work = jnp.where(hit, -jnp.inf, work)

    exps = [jnp.exp(v - vals[0]) for v in vals]
    denom = exps[0] + exps[1] + exps[2] + exps[3]

    onehot = jnp.zeros((ROUTE_TM, LANES), f32)
    for _, hit in hits:
        onehot = onehot + hit.astype(f32)
    before = jnp.dot(tri_ref[...], onehot.astype(bf16), preferred_element_type=f32) + carry_ref[...]

    meta = jnp.zeros((ROUTE_TM, LANES), jnp.int32)
    gate = jnp.zeros((ROUTE_TM, LANES), f32)
    for k, (idx, hit) in enumerate(hits):
        rank = jnp.sum(jnp.where(hit, before, 0.0), axis=-1, keepdims=True).astype(jnp.int32)
        meta = jnp.where(lane == k, idx.astype(jnp.int32), meta)
        meta = jnp.where(lane == TOP_K + k, rank, meta)
        gate = jnp.where(lane == k, exps[k] / denom, gate)
    meta_ref[...] = meta
    gate_ref[...] = gate
    carry_ref[...] += jnp.sum(onehot, axis=0, keepdims=True)
    cnt_ref[...] = carry_ref[...]


def _route(logits):
    tri = (jnp.arange(ROUTE_TM)[:, None] > jnp.arange(ROUTE_TM)[None, :]).astype(bf16)
    row = lambda i: (i, 0)
    return pl.pallas_call(
        _route_kernel,
        name="route",
        grid=(SEQ // ROUTE_TM,),
        in_specs=[
            pl.BlockSpec((ROUTE_TM, LANES), row),
            pl.BlockSpec((ROUTE_TM, ROUTE_TM), lambda i: (0, 0)),
        ],
        out_specs=[
            pl.BlockSpec((ROUTE_TM, LANES), row),
            pl.BlockSpec((ROUTE_TM, LANES), row),
            pl.BlockSpec((1, LANES), lambda i: (0, 0)),
        ],
        out_shape=[
            jax.ShapeDtypeStruct((SEQ, LANES), jnp.int32),
            jax.ShapeDtypeStruct((SEQ, LANES), f32),
            jax.ShapeDtypeStruct((1, LANES), f32),
        ],
        scratch_shapes=[pltpu.VMEM((1, LANES), f32)],
        compiler_params=_cparams(("arbitrary",)),
    )(logits, tri)


def _routing_tables(meta, counts):
    idx = meta[:, :TOP_K]
    rank = meta[:, TOP_K:2 * TOP_K]
    cnt = counts[0, :N_EXPERTS].astype(jnp.int32)
    n_chunks = (cnt + ROW_CHUNK - 1) // ROW_CHUNK
    chunk_base = jnp.cumsum(n_chunks) - n_chunks
    dest = (chunk_base[idx] * ROW_CHUNK + rank).astype(jnp.int32).reshape(-1)
    total_chunks = jnp.sum(n_chunks)
    last_chunk = jnp.concatenate([jnp.where(cnt > 0, chunk_base + n_chunks - 1, -1),
                                  total_chunks[None]]).astype(jnp.int32)

    n_sb = (n_chunks + CHUNKS_PER_SB - 1) // CHUNKS_PER_SB
    sb_end = jnp.cumsum(n_sb)
    sb_start = sb_end - n_sb
    n_used = sb_end[-1]
    s = jnp.arange(MAX_SB)
    s_eff = jnp.minimum(s, n_used - 1)
    e = jnp.minimum(jnp.searchsorted(sb_end, s_eff, side='right'), N_EXPERTS - 1)
    kk = s_eff - sb_start[e]
    sb_chunk0 = chunk_base[e] + kk * CHUNKS_PER_SB
    sb_n = jnp.where(s < n_used, jnp.clip(n_chunks[e] - kk * CHUNKS_PER_SB, 0, CHUNKS_PER_SB), 0)
    used = jnp.stack([n_used, total_chunks]).astype(jnp.int32)
    return (dest, last_chunk, e.astype(jnp.int32), sb_chunk0.astype(jnp.int32),
            sb_n.astype(jnp.int32), used)


DISP_TM = 256


def _row_copy(src, src_row, dst, dst_row, sem):
    return pltpu.make_async_copy(src.at[pl.ds(src_row, 1), :], dst.at[pl.ds(dst_row, 1), :], sem)


def _dispatch_kernel(last_ref, dest_ref, hn_ref, xs_ref, zero_ref, sem_ref):
    i = pl.program_id(0)

    @pl.when(i == 0)
    def _():
        zero_ref[...] = jnp.zeros_like(zero_ref)

        def chunk_copy(c):
            row0 = pl.multiple_of(c * ROW_CHUNK, ROW_CHUNK)
            return pltpu.make_async_copy(zero_ref, xs_ref.at[pl.ds(row0, ROW_CHUNK), :], sem_ref.at[1])

        def start(e, _):
            @pl.when(last_ref[e] >= 0)
            def _():
                chunk_copy(last_ref[e]).start()
            return 0

        def wait(e, _):
            @pl.when(last_ref[e] >= 0)
            def _():
                chunk_copy(last_ref[e]).wait()
            return 0

        def start_tail(c, _):
            chunk_copy(c).start()
            return 0

        def wait_tail(c, _):
            chunk_copy(c).wait()
            return 0

        lax.fori_loop(0, N_EXPERTS, start, 0)
        lax.fori_loop(last_ref[N_EXPERTS], MAX_CHUNKS, start_tail, 0)
        lax.fori_loop(0, N_EXPERTS, wait, 0)
        lax.fori_loop(last_ref[N_EXPERTS], MAX_CHUNKS, wait_tail, 0)

    def issue(t, _):
        for k in range(TOP_K):
            _row_copy(hn_ref, t, xs_ref, dest_ref[t * TOP_K + k], sem_ref.at[0]).start()
        return 0

    lax.fori_loop(0, DISP_TM, issue, 0)
    for k in range(TOP_K):
        pltpu.make_async_copy(hn_ref, xs_ref.at[pl.ds(0, DISP_TM), :], sem_ref.at[0]).wait()


def _dispatch(last_chunk, dest, hn):
    return pl.pallas_call(
        _dispatch_kernel,
        name="dispatch",
        grid_spec=pltpu.PrefetchScalarGridSpec(
            num_scalar_prefetch=1,
            grid=(SEQ // DISP_TM,),
            in_specs=[
                pl.BlockSpec((DISP_TM * TOP_K,), lambda i, last: (i,), memory_space=pltpu.SMEM),
                pl.BlockSpec((DISP_TM, D_MODEL), lambda i, last: (i, 0)),
            ],
            out_specs=pl.BlockSpec(memory_space=pl.ANY),
            scratch_shapes=[
                pltpu.VMEM((ROW_CHUNK, D_MODEL), f32),
                pltpu.SemaphoreType.DMA((2,)),
            ],
        ),
        out_shape=jax.ShapeDtypeStruct((MAX_ROWS, D_MODEL), f32),
        compiler_params=_cparams(("arbitrary",)),
    )(last_chunk, dest, hn)


def _experts_kernel(e_ref, c0_ref, n_ref, used_ref,
                    xs_ref, wg_ref, wu_ref, wd_ref, bg_ref, bu_ref, bd_ref, ys_ref,
                    xin_ref, xbf_ref, act_ref, yt_ref, wa_ref, wb_ref, sem_ref):
    s = pl.program_id(0)
    j = pl.program_id(1)
    n = n_ref[s]
    c0 = c0_ref[s]

    def rows(c):
        return pl.ds(pl.multiple_of(c * ROW_CHUNK, ROW_CHUNK), ROW_CHUNK)

    def in_copy(c):
        return pltpu.make_async_copy(xs_ref.at[rows(c0 + c), :], xin_ref.at[rows(c), :], sem_ref.at[0])

    @pl.when(jnp.logical_and(n > 0, j == 0))
    def _():
        def start(c, _):
            in_copy(c).start()
            return 0

        def finish(c, _):
            in_copy(c).wait()
            return 0

        def cast(c, _):
            xbf_ref[rows(c), :] = xin_ref[rows(c), :].astype(bf16)
            return 0

        lax.fori_loop(0, n, start, 0)
        lax.fori_loop(0, n, finish, 0)
        lax.fori_loop(0, n, cast, 0)

    @pl.when(jnp.logical_and(n > 0, j < N_FF_TILES))
    def _():
        wa_ref[...] = wg_ref[0].astype(bf16)
        wb_ref[...] = wu_ref[0].astype(bf16)
        bg = bg_ref[0]
        bu = bu_ref[0]

        def chunk(c, _):
            x = xbf_ref[rows(c), :]
            g = jnp.dot(x, wa_ref[...], preferred_element_type=f32) + bg
            u = jnp.dot(x, wb_ref[...], preferred_element_type=f32) + bu
            g = jnp.minimum(g, SWIGLU_LIMIT)
            u = jnp.clip(u, -SWIGLU_LIMIT, SWIGLU_LIMIT)
            a = (u + 1.0) * (g * (1.0 / (1.0 + jnp.exp(-SWIGLU_ALPHA * g))))
            act_ref[j, rows(c), :] = a.astype(bf16)
            return 0

        lax.fori_loop(0, n, chunk, 0)

    @pl.when(jnp.logical_and(n > 0, j >= N_FF_TILES))
    def _():
        d = j - N_FF_TILES
        wa_ref[...] = wd_ref[0].astype(bf16)
        bd = bd_ref[0]
        cols = pl.ds(pl.multiple_of(d * FF_TILE, FF_TILE), FF_TILE)

        def out_copy(c):
            return pltpu.make_async_copy(yt_ref.at[rows(c), :], ys_ref.at[rows(c0 + c), cols], sem_ref.at[1])

        def chunk(c, _):
            a = jnp.concatenate([act_ref[f, rows(c), :] for f in range(N_FF_TILES)], axis=-1)
            yt_ref[rows(c), :] = jnp.dot(a, wa_ref[...], preferred_element_type=f32) + bd
            out_copy(c).start()
            return 0

        def finish(c, _):
            out_copy(c).wait()
            return 0

        lax.fori_loop(0, n, chunk, 0)
        lax.fori_loop(0, n, finish, 0)

    @pl.when(jnp.logical_and(s == MAX_SB - 1, j == N_FF_TILES + N_OUT_TILES - 1))
    def _():
        xin_ref[0:ROW_CHUNK, :] = jnp.zeros((ROW_CHUNK, D_MODEL), f32)

        def tail_copy(c):
            return pltpu.make_async_copy(xin_ref.at[0:ROW_CHUNK, :], ys_ref.at[rows(c), :], sem_ref.at[0])

        def start(c, _):
            tail_copy(c).start()
            return 0

        def finish(c, _):
            tail_copy(c).wait()
            return 0

        lax.fori_loop(used_ref[1], MAX_CHUNKS, start, 0)
        lax.fori_loop(used_ref[1], MAX_CHUNKS, finish, 0)


def _experts(sb_e, sb_c0, sb_n, n_used, xs, w_gate, b_gate, w_up, b_up, w_down, b_down):
    last_up = N_FF_TILES - 1

    def up_map(s, j, e, c0, n, used):
        live = s < used[0]
        return (e[s], 0, jnp.where(live, jnp.minimum(j, last_up), last_up))

    def down_map(s, j, e, c0, n, used):
        live = s < used[0]
        return (e[s], 0, jnp.where(live, jnp.maximum(j - N_FF_TILES, 0), N_OUT_TILES - 1))

    return pl.pallas_call(
        _experts_kernel,
        name="experts",
        grid_spec=pltpu.PrefetchScalarGridSpec(
            num_scalar_prefetch=4,
            grid=(MAX_SB, N_FF_TILES + N_OUT_TILES),
            in_specs=[
                pl.BlockSpec(memory_space=pl.ANY),
                pl.BlockSpec((1, D_MODEL, FF_TILE), up_map),
                pl.BlockSpec((1, D_MODEL, FF_TILE), up_map),
                pl.BlockSpec((1, D_FF, FF_TILE), down_map),
                pl.BlockSpec((1, 1, FF_TILE), up_map),
                pl.BlockSpec((1, 1, FF_TILE), up_map),
                pl.BlockSpec((1, 1, FF_TILE), down_map),
            ],
            out_specs=pl.BlockSpec(memory_space=pl.ANY),
            scratch_shapes=[
                pltpu.VMEM((SB_ROWS, D_MODEL), f32),
                pltpu.VMEM((SB_ROWS, D_MODEL), bf16),
                pltpu.VMEM((N_FF_TILES, SB_ROWS, FF_TILE), bf16),
                pltpu.VMEM((SB_ROWS, FF_TILE), f32),
                pltpu.VMEM((D_MODEL, FF_TILE), bf16),
                pltpu.VMEM((D_MODEL, FF_TILE), bf16),
                pltpu.SemaphoreType.DMA((2,)),
            ],
        ),
        out_shape=jax.ShapeDtypeStruct((MAX_ROWS, D_MODEL), f32),
        compiler_params=_cparams(("arbitrary", "arbitrary")),
    )(sb_e, sb_c0, sb_n, n_used, xs, w_gate, w_up, w_down,
      b_gate.reshape(N_EXPERTS, 1, D_FF), b_up.reshape(N_EXPERTS, 1, D_FF),
      b_down.reshape(N_EXPERTS, 1, D_MODEL))


COMB_TM = 256


def _combine_kernel(dest_ref, ys_ref, x1_ref, gate_ref, gf_ref, o_ref, buf_ref, sem_ref):
    def issue(t, _):
        for k in range(TOP_K):
            pltpu.make_async_copy(ys_ref.at[pl.ds(dest_ref[t * TOP_K + k], 1), :],
                                  buf_ref.at[k, pl.ds(t, 1), :], sem_ref.at[0]).start()
        return 0

    lax.fori_loop(0, COMB_TM, issue, 0)
    for k in range(TOP_K):
        pltpu.make_async_copy(ys_ref.at[pl.ds(0, COMB_TM), :], buf_ref.at[k], sem_ref.at[0]).wait()

    gate = gate_ref[...]
    acc = x1_ref[...]
    for k in range(TOP_K):
        acc = acc + gate[:, k:k + 1] * buf_ref[k]
    o_ref[...] = _rms(acc, gf_ref[...])


def _combine(dest, ys, x1, gates, g_final):
    return pl.pallas_call(
        _combine_kernel,
        name="combine",
        grid=(SEQ // COMB_TM,),
        in_specs=[
            pl.BlockSpec((COMB_TM * TOP_K,), lambda i: (i,), memory_space=pltpu.SMEM),
            pl.BlockSpec(memory_space=pl.ANY),
            pl.BlockSpec((COMB_TM, D_MODEL), lambda i: (i, 0)),
            pl.BlockSpec((COMB_TM, LANES), lambda i: (i, 0)),
            pl.BlockSpec((1, D_MODEL), lambda i: (0, 0)),
        ],
        out_specs=pl.BlockSpec((COMB_TM, D_MODEL), lambda i: (i, 0)),
        out_shape=jax.ShapeDtypeStruct((SEQ, D_MODEL), f32),
        scratch_shapes=[
            pltpu.VMEM((TOP_K, COMB_TM, D_MODEL), f32),
            pltpu.SemaphoreType.DMA((1,)),
        ],
        compiler_params=_cparams(("arbitrary",)),
    )(dest, ys, x1, gates, g_final.reshape(1, D_MODEL))


def kernel(x, g_mix, w_in, lam_re_fwd, lam_im_fwd, log_dt_fwd, b_re_fwd, b_im_fwd, c_re_fwd, c_im_fwd, lam_re_bwd, lam_im_bwd, log_dt_bwd, b_re_bwd, b_im_bwd, c_re_bwd, c_im_bwd, ssm_d, w_glu, b_glu, na_rpb, g_ssm_out, g_na_out, w_out, g_moe, w_router, b_router, w_gate, b_gate, w_up, b_up, w_down, b_down, g_final):
    x2 = x.reshape(SEQ, D_MODEL)

    u, qkv = _in_proj(x2, g_mix[0], w_in[0].astype(bf16))

    s5_w, s5_dec = _s5_weights(
        (lam_re_fwd[0], lam_im_fwd[0], log_dt_fwd[0], b_re_fwd[0], b_im_fwd[0], c_re_fwd[0], c_im_fwd[0]),
        (lam_re_bwd[0], lam_im_bwd[0], log_dt_bwd[0], b_re_bwd[0], b_im_bwd[0], c_re_bwd[0], c_im_bwd[0]))
    y = _s5(u, s5_w, s5_dec, ssm_d[0])
    ssm_n = _glu(y, w_glu[0].astype(bf16), b_glu[0], g_ssm_out[0])

    y_na = _natten(qkv, _na_bias(na_rpb[0]))

    w_router_pad = jnp.zeros((D_MODEL, LANES), bf16).at[:, :N_EXPERTS].set(w_router[0].astype(bf16))
    b_router_pad = jnp.full((1, LANES), NEG_BIG, f32).at[0, :N_EXPERTS].set(b_router[0].astype(f32))
    x1, hn, logits = _out_proj(ssm_n, y_na, x2, g_na_out[0], w_out[0].astype(bf16), g_moe[0],
                               w_router_pad, b_router_pad)

    meta, gates, counts = _route(logits)
    dest, last_chunk, sb_e, sb_c0, sb_n, n_used = _routing_tables(meta, counts)

    xs = _dispatch(last_chunk, dest, hn)
    ys = _experts(sb_e, sb_c0, sb_n, n_used, xs, w_gate[0], b_gate[0], w_up[0], b_up[0],
                  w_down[0], b_down[0])
    out = _combine(dest, ys, x1, gates, g_final)
    return out.reshape(x.shape)
```

```python
import functools
import math

import jax
import jax.numpy as jnp
from jax import lax
from jax.experimental import pallas as pl
from jax.experimental.pallas import tpu as pltpu

f32 = jnp.float32
bf16 = jnp.bfloat16

D_MODEL = 2048
SEQ = 8192
SSM_WIDTH = 1024
NA_WIDTH = 1024
SSM_GROUP = 16
SSM_GROUPS = 64
SSM_STATE = 64
NA_HEAD_DIM = 64
NA_HEADS = 16
GRID_W = 64
GRID_ROWS = SEQ // GRID_W
NA_KH = 8
NA_KW = 16
N_EXPERTS = 32
TOP_K = 4
D_FF = 2048
SWIGLU_LIMIT = 7.0
SWIGLU_ALPHA = 1.702
RMS_EPS = 1e-5

LANES = 128
NEG_BIG = -1e30

CHUNK_T = 16
N_CHUNKS = SEQ // CHUNK_T
GROUPS_PER_BLOCK = LANES // SSM_GROUP
N_LANE_BLOCKS = SSM_WIDTH // LANES
CAT_W = CHUNK_T * LANES
STATE_W = GROUPS_PER_BLOCK * SSM_STATE

ROW_CHUNK = 128
CHUNKS_PER_SB = 10
SB_ROWS = ROW_CHUNK * CHUNKS_PER_SB
MAX_CHUNKS = SEQ * TOP_K // ROW_CHUNK + N_EXPERTS
MAX_ROWS = MAX_CHUNKS * ROW_CHUNK
MAX_SB = MAX_CHUNKS // CHUNKS_PER_SB + N_EXPERTS
FF_TILE = 256
N_FF_TILES = D_FF // FF_TILE
N_OUT_TILES = D_MODEL // FF_TILE

VMEM_LIMIT = 56 * 1024 * 1024


def _cparams(semantics, vmem=VMEM_LIMIT):
    return pltpu.CompilerParams(dimension_semantics=semantics, vmem_limit_bytes=vmem)


def _rms(x, g):
    return x * lax.rsqrt(jnp.mean(x * x, axis=-1, keepdims=True) + RMS_EPS) * g


IN_TM = 512
IN_TN = 1024


def _in_proj_kernel(x_ref, g_ref, w_ref, u_ref, qkv_ref, h_ref):
    j = pl.program_id(1)

    @pl.when(j == 0)
    def _():
        h_ref[...] = _rms(x_ref[...], g_ref[...]).astype(bf16)

    acc = jnp.dot(h_ref[...], w_ref[...], preferred_element_type=f32)

    @pl.when(j == 0)
    def _():
        u_ref[...] = acc

    @pl.when(j > 0)
    def _():
        qkv_ref[...] = acc.astype(bf16)


def _in_proj(x, g_mix, w_in_bf):
    n_out = w_in_bf.shape[1]
    return pl.pallas_call(
        _in_proj_kernel,
        name="in_proj",
        grid=(SEQ // IN_TM, n_out // IN_TN),
        in_specs=[
            pl.BlockSpec((IN_TM, D_MODEL), lambda i, j: (i, 0)),
            pl.BlockSpec((1, D_MODEL), lambda i, j: (0, 0)),
            pl.BlockSpec((D_MODEL, IN_TN), lambda i, j: (0, j)),
        ],
        out_specs=[
            pl.BlockSpec((IN_TM, IN_TN), lambda i, j: (i, 0)),
            pl.BlockSpec((IN_TM, IN_TN), lambda i, j: (i, jnp.maximum(j - 1, 0))),
        ],
        out_shape=[
            jax.ShapeDtypeStruct((SEQ, SSM_WIDTH), f32),
            jax.ShapeDtypeStruct((SEQ, 3 * NA_WIDTH), bf16),
        ],
        scratch_shapes=[pltpu.VMEM((IN_TM, D_MODEL), bf16)],
        compiler_params=_cparams(("arbitrary", "arbitrary")),
    )(x, g_mix.reshape(1, D_MODEL), w_in_bf)


def _cmul(ar, ai, br, bi):
    return ar * br - ai * bi, ar * bi + ai * br


def _s5_discretise(lam_re, lam_im, log_dt, b_re, b_im, c_re, c_im):
    a = jnp.minimum(lam_re.astype(f32), -1e-4)
    w = lam_im.astype(f32)
    dt = jnp.exp(log_dt.astype(f32))[:, None]
    steps = jnp.arange(CHUNK_T + 1, dtype=f32)[:, None, None]
    mag = jnp.exp((a * dt)[None] * steps)
    ang = (w * dt)[None] * steps
    pw = (mag * jnp.cos(ang), mag * jnp.sin(ang))
    xr, xi = pw[0][1] - 1.0, pw[1][1]
    den = a * a + w * w
    qr, qi = (xr * a + xi * w) / den, (xi * a - xr * w) / den
    bb = _cmul(qr[..., None], qi[..., None], b_re.astype(f32), b_im.astype(f32))
    return pw, bb, (c_re.astype(f32), c_im.astype(f32))


def _pair_blockdiag(m):
    z = jnp.zeros_like(m[..., 0, :, :])
    top = jnp.concatenate([m[..., 0, :, :], z], axis=-1)
    bot = jnp.concatenate([z, m[..., 1, :, :]], axis=-1)
    return jnp.concatenate([top, bot], axis=-2)


def _s5_weights(fwd, bwd):
    hi = lax.Precision.HIGHEST
    t = CHUNK_T
    pw_f, bb_f, c_f = _s5_discretise(*fwd)
    pw_b, bb_b, c_b = _s5_discretise(*bwd)
    pairs = (N_LANE_BLOCKS, GROUPS_PER_BLOCK // 2, 2)

    def lag_kernel(pw, bb, c):
        m_re, m_im = _cmul(c[0][None], c[1][None], pw[0][:t, :, None, :], pw[1][:t, :, None, :])
        return (jnp.einsum('jgcp,gpd->jgcd', m_re, bb[0], precision=hi)
                - jnp.einsum('jgcp,gpd->jgcd', m_im, bb[1], precision=hi))

    kf, kb = lag_kernel(pw_f, bb_f, c_f), lag_kernel(pw_b, bb_b, c_b)
    k_lag = jnp.concatenate([kb[:0:-1], (kf[0] + kb[0])[None], kf[1:]], axis=0)
    lag = jnp.arange(t)[:, None] - jnp.arange(t)[None, :] + (t - 1)
    sel = (lag[:, :, None] == jnp.arange(2 * t - 1)).astype(f32)
    w_intra = jnp.einsum('oil,lgcd->gidoc', sel, k_lag, precision=hi)
    w_intra = _pair_blockdiag(w_intra.reshape(*pairs, t * SSM_GROUP, t * SSM_GROUP))

    in_f = _cmul(pw_f[0][:t][::-1][..., None], pw_f[1][:t][::-1][..., None], bb_f[0][None], bb_f[1][None])
    in_b = _cmul(pw_b[0][:t][..., None], pw_b[1][:t][..., None], bb_b[0][None], bb_b[1][None])

    def state_in(m):
        m = jnp.transpose(m, (1, 0, 3, 2)).reshape(*pairs, t * SSM_GROUP, SSM_STATE)
        return _pair_blockdiag(m)

    w_in = jnp.concatenate([state_in(in_f[0]), state_in(in_f[1]),
                            state_in(in_b[0]), state_in(in_b[1])], axis=-1)

    out_f = _cmul(c_f[0][None], c_f[1][None], pw_f[0][1:t + 1][:, :, None, :], pw_f[1][1:t + 1][:, :, None, :])
    out_b = _cmul(c_b[0][None], c_b[1][None],
                  pw_b[0][1:t + 1][::-1][:, :, None, :], pw_b[1][1:t + 1][::-1][:, :, None, :])

    def state_out(m):
        m = jnp.transpose(m, (1, 3, 0, 2)).reshape(*pairs, SSM_STATE, t * SSM_GROUP)
        return _pair_blockdiag(m)

    w_out = jnp.concatenate([state_out(out_f[0]), state_out(-out_f[1]),
                             state_out(out_b[0]), state_out(-out_b[1])], axis=-2)

    def decay(a):
        return a.reshape(N_LANE_BLOCKS, 1, STATE_W)

    dec = jnp.concatenate([decay(pw_f[0][t]), decay(pw_f[1][t]), decay(pw_b[0][t]), decay(pw_b[1][t])],
                          axis=-1)
    w_all = jnp.stack([w_in, w_intra, w_out], axis=2).astype(bf16)
    return w_all, dec.astype(f32)


def _gelu_tanh(x):
    return 0.5 * x * (1.0 + jnp.tanh(math.sqrt(2.0 / math.pi) * (x + 0.044715 * (x * x * x))))


PAIRS_PER_BLOCK = GROUPS_PER_BLOCK // 2
PAIR_W = 2 * CHUNK_T * SSM_GROUP


def _s5_kernel(u_ref, w_ref, dec_ref, d_ref, y_ref, cat_ref, catp_ref, st_ref, perm_ref):
    sw = STATE_W

    @pl.when(pl.program_id(0) == 0)
    def _():
        def strip(t, _):
            row = lax.broadcasted_iota(jnp.int32, (LANES, CAT_W), 0)
            col = lax.broadcasted_iota(jnp.int32, (LANES, CAT_W), 1)
            target = (row // SSM_GROUP) * (CHUNK_T * SSM_GROUP) + t * SSM_GROUP + row % SSM_GROUP
            perm_ref[pl.ds(pl.multiple_of(t * LANES, LANES), LANES), :] = (
                jnp.where(col == target, 1.0, 0.0).astype(bf16))
            return 0

        lax.fori_loop(0, CHUNK_T, strip, 0)

    for t in range(CHUNK_T):
        cat_ref[:, t * LANES:(t + 1) * LANES] = u_ref[pl.ds(t, N_CHUNKS, stride=CHUNK_T), :].astype(bf16)
    catp_ref[...] = jnp.dot(cat_ref[...], perm_ref[...], preferred_element_type=f32).astype(bf16)

    for p in range(PAIRS_PER_BLOCK):
        z = jnp.dot(catp_ref[:, p * PAIR_W:(p + 1) * PAIR_W], w_ref[0, p, 0], preferred_element_type=f32)
        for part in range(4):
            st_ref[:, part * sw + p * LANES:part * sw + (p + 1) * LANES] = z[:, part * LANES:(part + 1) * LANES]

    afr = dec_ref[0, :, 0 * sw:1 * sw]
    afi = dec_ref[0, :, 1 * sw:2 * sw]
    abr = dec_ref[0, :, 2 * sw:3 * sw]
    abi = dec_ref[0, :, 3 * sw:4 * sw]

    def step(i, carry):
        sfr, sfi, sbr, sbi = carry
        r = N_CHUNKS - 1 - i
        zf = st_ref[pl.ds(i, 1), 0:2 * sw]
        zb = st_ref[pl.ds(r, 1), 2 * sw:4 * sw]
        st_ref[pl.ds(i, 1), 0:2 * sw] = jnp.concatenate([sfr, sfi], axis=-1)
        st_ref[pl.ds(r, 1), 2 * sw:4 * sw] = jnp.concatenate([sbr, sbi], axis=-1)
        nfr = afr * sfr - afi * sfi + zf[:, :sw]
        nfi = afi * sfr + afr * sfi + zf[:, sw:]
        nbr = abr * sbr - abi * sbi + zb[:, :sw]
        nbi = abi * sbr + abr * sbi + zb[:, sw:]
        return nfr, nfi, nbr, nbi

    zero = jnp.zeros((1, sw), f32)
    lax.fori_loop(0, N_CHUNKS, step, (zero, zero, zero, zero))

    for p in range(PAIRS_PER_BLOCK):
        s_in = jnp.concatenate(
            [st_ref[:, part * sw + p * LANES:part * sw + (p + 1) * LANES] for part in range(4)], axis=-1)
        yp = jnp.dot(catp_ref[:, p * PAIR_W:(p + 1) * PAIR_W], w_ref[0, p, 1], preferred_element_type=f32)
        yp = yp + jnp.dot(s_in.astype(bf16), w_ref[0, p, 2], preferred_element_type=f32)
        cat_ref[:, p * PAIR_W:(p + 1) * PAIR_W] = yp.astype(bf16)

    st_ref[...] = lax.dot_general(cat_ref[...], perm_ref[...], (((1,), (1,)), ((), ())),
                                  preferred_element_type=f32)
    d = d_ref[...]
    for t in range(CHUNK_T):
        rows = pl.ds(t, N_CHUNKS, stride=CHUNK_T)
        v = st_ref[:, t * LANES:(t + 1) * LANES] + d * u_ref[rows, :]
        y_ref[rows, :] = _gelu_tanh(v)


def _s5(u, w_all, dec, ssm_d):
    return pl.pallas_call(
        _s5_kernel,
        name="s5_scan",
        grid=(N_LANE_BLOCKS,),
        in_specs=[
            pl.BlockSpec((SEQ, LANES), lambda j: (0, j)),
            pl.BlockSpec((1, PAIRS_PER_BLOCK, 3, PAIR_W, PAIR_W), lambda j: (j, 0, 0, 0, 0)),
            pl.BlockSpec((1, 1, 4 * STATE_W), lambda j: (j, 0, 0)),
            pl.BlockSpec((1, LANES), lambda j: (0, j)),
        ],
        out_specs=pl.BlockSpec((SEQ, LANES), lambda j: (0, j)),
        out_shape=jax.ShapeDtypeStruct((SEQ, SSM_WIDTH), f32),
        scratch_shapes=[
            pltpu.VMEM((N_CHUNKS, CAT_W), bf16),
            pltpu.VMEM((N_CHUNKS, CAT_W), bf16),
            pltpu.VMEM((N_CHUNKS, 4 * STATE_W), f32),
            pltpu.VMEM((CAT_W, CAT_W), bf16),
        ],
        compiler_params=_cparams(("arbitrary",)),
    )(u, w_all, dec, ssm_d.reshape(1, SSM_WIDTH))


GLU_TM = 512


def _glu_kernel(y_ref, w_ref, b_ref, g_ref, o_ref):
    y = y_ref[...]
    z = jnp.dot(y.astype(bf16), w_ref[...], preferred_element_type=f32) + b_ref[...]
    o = y * (1.0 / (1.0 + jnp.exp(-z)))
    o_ref[...] = _rms(o, g_ref[...]).astype(bf16)


def _glu(y, w_glu_bf, b_glu, g_ssm_out):
    return pl.pallas_call(
        _glu_kernel,
        name="glu_norm",
        grid=(SEQ // GLU_TM,),
        in_specs=[
            pl.BlockSpec((GLU_TM, SSM_WIDTH), lambda i: (i, 0)),
            pl.BlockSpec((SSM_WIDTH, SSM_WIDTH), lambda i: (0, 0)),
            pl.BlockSpec((1, SSM_WIDTH), lambda i: (0, 0)),
            pl.BlockSpec((1, SSM_WIDTH), lambda i: (0, 0)),
        ],
        out_specs=pl.BlockSpec((GLU_TM, SSM_WIDTH), lambda i: (i, 0)),
        out_shape=jax.ShapeDtypeStruct((SEQ, SSM_WIDTH), bf16),
        compiler_params=_cparams(("arbitrary",)),
    )(y, w_glu_bf, b_glu.reshape(1, SSM_WIDTH), g_ssm_out.reshape(1, SSM_WIDTH))


NA_ROWS_PER_STEP = 8
NA_WIN = NA_KH * GRID_W
HEADS_PER_BLOCK = LANES // NA_HEAD_DIM


def _na_bias(rpb):
    c = jnp.arange(GRID_W)
    col_start = jnp.clip(c - NA_KW // 2, 0, GRID_W - NA_KW)
    valid = (c[None, :] >= col_start[:, None]) & (c[None, :] < col_start[:, None] + NA_KW)
    dc = jnp.clip(c[None, :] - c[:, None], -(NA_KW - 1), NA_KW - 1) + (NA_KW - 1)
    sel = (dc[None] == jnp.arange(2 * NA_KW - 1)[:, None, None]).astype(f32)
    tab = jnp.einsum('hrc,cqk->hrqk', rpb.astype(f32), sel, precision=lax.Precision.HIGHEST)
    tab = jnp.where(valid[None, None], tab, NEG_BIG)
    b = jnp.stack([tab[:, NA_KH - 1 - v:2 * NA_KH - 1 - v] for v in range(NA_KH)], axis=1)
    return jnp.transpose(b, (0, 1, 3, 2, 4)).reshape(NA_HEADS, NA_KH, GRID_W, NA_WIN)


def _natten_kernel(q_ref, k_ref, v_ref, b_ref, o_ref):
    rb = pl.program_id(1)
    lane = lax.broadcasted_iota(jnp.int32, (GRID_W, LANES), 1)
    head0 = lane < NA_HEAD_DIM
    scale = NA_HEAD_DIM ** -0.5

    for i in range(NA_ROWS_PER_STEP):
        r = rb * NA_ROWS_PER_STEP + i
        rs = jnp.clip(r - NA_KH // 2, 0, GRID_ROWS - NA_KH)
        var = r - rs
        q = q_ref[i * GRID_W:(i + 1) * GRID_W, :] * scale
        start = pl.multiple_of(rs * GRID_W, GRID_W)
        kw = k_ref[pl.ds(start, NA_WIN), :]
        vw = v_ref[pl.ds(start, NA_WIN), :]
        outs = []
        for h in range(HEADS_PER_BLOCK):
            qh = jnp.where(head0 if h == 0 else ~head0, q, jnp.zeros_like(q))
            s = lax.dot_general(qh, kw, (((1,), (1,)), ((), ())), preferred_element_type=f32)
            s = s + b_ref[h, var]
            m = jnp.max(s, axis=-1, keepdims=True)
            p = jnp.exp(s - m)
            l = jnp.sum(p, axis=-1, keepdims=True)
            o = jnp.dot(p.astype(bf16), vw, preferred_element_type=f32)
            outs.append(o / l)
        o_ref[i * GRID_W:(i + 1) * GRID_W, :] = jnp.where(head0, outs[0], outs[1]).astype(bf16)


def _natten(qkv, bias):
    tm = NA_ROWS_PER_STEP * GRID_W
    n_hb = NA_WIDTH // LANES
    return pl.pallas_call(
        _natten_kernel,
        name="natten",
        grid=(n_hb, GRID_ROWS // NA_ROWS_PER_STEP),
        in_specs=[
            pl.BlockSpec((tm, LANES), lambda h, r: (r, h)),
            pl.BlockSpec((SEQ, LANES), lambda h, r: (0, n_hb + h)),
            pl.BlockSpec((SEQ, LANES), lambda h, r: (0, 2 * n_hb + h)),
            pl.BlockSpec((HEADS_PER_BLOCK, NA_KH, GRID_W, NA_WIN), lambda h, r: (h, 0, 0, 0)),
        ],
        out_specs=pl.BlockSpec((tm, LANES), lambda h, r: (r, h)),
        out_shape=jax.ShapeDtypeStruct((SEQ, NA_WIDTH), bf16),
        compiler_params=_cparams(("arbitrary", "arbitrary")),
    )(qkv, qkv, qkv, bias)


OUT_TM = 512


def _out_proj_kernel(ssm_ref, na_ref, x_ref, gna_ref, w_ref, gmoe_ref, wr_ref, br_ref,
                     x1_ref, hn_ref, lg_ref):
    na = _rms(na_ref[...].astype(f32), gna_ref[...]).astype(bf16)
    y = jnp.dot(ssm_ref[...], w_ref[0:SSM_WIDTH, :], preferred_element_type=f32)
    y = y + jnp.dot(na, w_ref[SSM_WIDTH:, :], preferred_element_type=f32)
    x1 = x_ref[...] + y
    x1_ref[...] = x1
    hn = _rms(x1, gmoe_ref[...])
    hn_ref[...] = hn
    lg_ref[...] = jnp.dot(hn.astype(bf16), wr_ref[...], preferred_element_type=f32) + br_ref[...]


def _out_proj(ssm_n, y_na, x, g_na_out, w_out_bf, g_moe, w_router_pad, b_router_pad):
    row = lambda i: (i, 0)
    fixed = lambda i: (0, 0)
    return pl.pallas_call(
        _out_proj_kernel,
        name="out_proj",
        grid=(SEQ // OUT_TM,),
        in_specs=[
            pl.BlockSpec((OUT_TM, SSM_WIDTH), row),
            pl.BlockSpec((OUT_TM, NA_WIDTH), row),
            pl.BlockSpec((OUT_TM, D_MODEL), row),
            pl.BlockSpec((1, NA_WIDTH), fixed),
            pl.BlockSpec((D_MODEL, D_MODEL), fixed),
            pl.BlockSpec((1, D_MODEL), fixed),
            pl.BlockSpec((D_MODEL, LANES), fixed),
            pl.BlockSpec((1, LANES), fixed),
        ],
        out_specs=[
            pl.BlockSpec((OUT_TM, D_MODEL), row),
            pl.BlockSpec((OUT_TM, D_MODEL), row),
            pl.BlockSpec((OUT_TM, LANES), row),
        ],
        out_shape=[
            jax.ShapeDtypeStruct((SEQ, D_MODEL), f32),
            jax.ShapeDtypeStruct((SEQ, D_MODEL), f32),
            jax.ShapeDtypeStruct((SEQ, LANES), f32),
        ],
        compiler_params=_cparams(("arbitrary",)),
    )(ssm_n, y_na, x, g_na_out.reshape(1, NA_WIDTH), w_out_bf, g_moe.reshape(1, D_MODEL),
      w_router_pad, b_router_pad)


ROUTE_TM = 512


def _route_kernel(lg_ref, tri_ref, meta_ref, gate_ref, cnt_ref, carry_ref):
    i = pl.program_id(0)

    @pl.when(i == 0)
    def _():
        carry_ref[...] = jnp.zeros_like(carry_ref)

    lane = lax.broadcasted_iota(jnp.int32, (ROUTE_TM, LANES), 1)
    lane_f = lane.astype(f32)
    work = lg_ref[...]
    vals, hits = [], []
    for _ in range(TOP_K):
        m = jnp.max(work, axis=-1, keepdims=True)
        idx = jnp.min(jnp.where(work == m, lane_f, float(LANES)), axis=-1, keepdims=True)
        hit = lane_f == idx
        vals.append(m)
        hits.append((idx, hit))
        work = jnp.where(hit, -jnp.inf, work)

    exps = [jnp.exp(v - vals[0]) for v in vals]
    denom = exps[0] + exps[1] + exps[2] + exps[3]

    onehot = jnp.zeros((ROUTE_TM, LANES), f32)
    for _, hit in hits:
        onehot = onehot + hit.astype(f32)
    before = jnp.dot(tri_ref[...], onehot.astype(bf16), preferred_element_type=f32) + carry_ref[...]

    meta = jnp.zeros((ROUTE_TM, LANES), jnp.int32)
    gate = jnp.zeros((ROUTE_TM, LANES), f32)
    for k, (idx, hit) in enumerate(hits):
        rank = jnp.sum(jnp.where(hit, before, 0.0), axis=-1, keepdims=True).astype(jnp.int32)
        meta = jnp.where(lane == k, idx.astype(jnp.int32), meta)
        meta = jnp.where(lane == TOP_K + k, rank, meta)
        gate = jnp.where(lane == k, exps[k] / denom, gate)
    meta_ref[...] = meta
    gate_ref[...] = gate
    carry_ref[...] += jnp.sum(onehot, axis=0, keepdims=True)
    cnt_ref[...] = carry_ref[...]


def _route(logits):
    tri = (jnp.arange(ROUTE_TM)[:, None] > jnp.arange(ROUTE_TM)[None, :]).astype(bf16)
    row = lambda i: (i, 0)
    return pl.pallas_call(
        _route_kernel,
        name="route",
        grid=(SEQ // ROUTE_TM,),
        in_specs=[
            pl.BlockSpec((ROUTE_TM, LANES), row),
            pl.BlockSpec((ROUTE_TM, ROUTE_TM), lambda i: (0, 0)),
        ],
        out_specs=[
            pl.BlockSpec((ROUTE_TM, LANES), row),
            pl.BlockSpec((ROUTE_TM, LANES), row),
            pl.BlockSpec((1, LANES), lambda i: (0, 0)),
        ],
        out_shape=[
            jax.ShapeDtypeStruct((SEQ, LANES), jnp.int32),
            jax.ShapeDtypeStruct((SEQ, LANES), f32),
            jax.ShapeDtypeStruct((1, LANES), f32),
        ],
        scratch_shapes=[pltpu.VMEM((1, LANES), f32)],
        compiler_params=_cparams(("arbitrary",)),
    )(logits, tri)


def _routing_tables(meta, counts):
    idx = meta[:, :TOP_K]
    rank = meta[:, TOP_K:2 * TOP_K]
    cnt = counts[0, :N_EXPERTS].astype(jnp.int32)
    n_chunks = (cnt + ROW_CHUNK - 1) // ROW_CHUNK
    chunk_base = jnp.cumsum(n_chunks) - n_chunks
    dest = (chunk_base[idx] * ROW_CHUNK + rank).astype(jnp.int32).reshape(-1)
    total_chunks = jnp.sum(n_chunks)
    last_chunk = jnp.concatenate([jnp.where(cnt > 0, chunk_base + n_chunks - 1, -1),
                                  total_chunks[None]]).astype(jnp.int32)

    n_sb = (n_chunks + CHUNKS_PER_SB - 1) // CHUNKS_PER_SB
    sb_end = jnp.cumsum(n_sb)
    sb_start = sb_end - n_sb
    n_used = sb_end[-1]
    s = jnp.arange(MAX_SB)
    s_eff = jnp.minimum(s, n_used - 1)
    e = jnp.minimum(jnp.searchsorted(sb_end, s_eff, side='right'), N_EXPERTS - 1)
    kk = s_eff - sb_start[e]
    sb_chunk0 = chunk_base[e] + kk * CHUNKS_PER_SB
    sb_n = jnp.where(s < n_used, jnp.clip(n_chunks[e] - kk * CHUNKS_PER_SB, 0, CHUNKS_PER_SB), 0)
    used = jnp.stack([n_used, total_chunks]).astype(jnp.int32)
    return (dest, last_chunk, e.astype(jnp.int32), sb_chunk0.astype(jnp.int32),
            sb_n.astype(jnp.int32), used)


DISP_TM = 256


def _row_copy(src, src_row, dst, dst_row, sem):
    return pltpu.make_async_copy(src.at[pl.ds(src_row, 1), :], dst.at[pl.ds(dst_row, 1), :], sem)


def _dispatch_kernel(last_ref, dest_ref, hn_ref, xs_ref, zero_ref, sem_ref):
    i = pl.program_id(0)

    @pl.when(i == 0)
    def _():
        zero_ref[...] = jnp.zeros_like(zero_ref)

        def chunk_copy(c):
            row0 = pl.multiple_of(c * ROW_CHUNK, ROW_CHUNK)
            return pltpu.make_async_copy(zero_ref, xs_ref.at[pl.ds(row0, ROW_CHUNK), :], sem_ref.at[1])

        def start(e, _):
            @pl.when(last_ref[e] >= 0)
            def _():
                chunk_copy(last_ref[e]).start()
            return 0

        def wait(e, _):
            @pl.when(last_ref[e] >= 0)
            def _():
                chunk_copy(last_ref[e]).wait()
            return 0

        def start_tail(c, _):
            chunk_copy(c).start()
            return 0

        def wait_tail(c, _):
            chunk_copy(c).wait()
            return 0

        lax.fori_loop(0, N_EXPERTS, start, 0)
        lax.fori_loop(last_ref[N_EXPERTS], MAX_CHUNKS, start_tail, 0)
        lax.fori_loop(0, N_EXPERTS, wait, 0)
        lax.fori_loop(last_ref[N_EXPERTS], MAX_CHUNKS, wait_tail, 0)

    def issue(t, _):
        for k in range(TOP_K):
            _row_copy(hn_ref, t, xs_ref, dest_ref[t * TOP_K + k], sem_ref.at[0]).start()
        return 0

    lax.fori_loop(0, DISP_TM, issue, 0)
    for k in range(TOP_K):
        pltpu.make_async_copy(hn_ref, xs_ref.at[pl.ds(0, DISP_TM), :], sem_ref.at[0]).wait()


def _dispatch(last_chunk, dest, hn):
    return pl.pallas_call(
        _dispatch_kernel,
        name="dispatch",
        grid_spec=pltpu.PrefetchScalarGridSpec(
            num_scalar_prefetch=1,
            grid=(SEQ // DISP_TM,),
            in_specs=[
                pl.BlockSpec((DISP_TM * TOP_K,), lambda i, last: (i,), memory_space=pltpu.SMEM),
                pl.BlockSpec((DISP_TM, D_MODEL), lambda i, last: (i, 0)),
            ],
            out_specs=pl.BlockSpec(memory_space=pl.ANY),
            scratch_shapes=[
                pltpu.VMEM((ROW_CHUNK, D_MODEL), f32),
                pltpu.SemaphoreType.DMA((2,)),
            ],
        ),
        out_shape=jax.ShapeDtypeStruct((MAX_ROWS, D_MODEL), f32),
        compiler_params=_cparams(("arbitrary",)),
    )(last_chunk, dest, hn)


def _experts_kernel(e_ref, c0_ref, n_ref, used_ref,
                    xs_ref, wg_ref, wu_ref, wd_ref, bg_ref, bu_ref, bd_ref, ys_ref,
                    xin_ref, xbf_ref, act_ref, yt_ref, pend_ref, sem_ref):
    s = pl.program_id(0)
    j = pl.program_id(1)
    n = n_ref[s]
    c0 = c0_ref[s]
    n_steps = N_FF_TILES + N_OUT_TILES

    def rows(c, k=1):
        return pl.ds(pl.multiple_of(c * ROW_CHUNK, ROW_CHUNK), k * ROW_CHUNK)

    def cover(body):
        n4 = n // 4

        def quad(i, _):
            body(i * 4, 4)
            return 0

        lax.fori_loop(0, n4, quad, 0)

        @pl.when((n & 2) != 0)
        def _():
            body(n4 * 4, 2)

        @pl.when((n & 1) != 0)
        def _():
            body(n4 * 4 + (n & 2), 1)

    def drain(slot):
        def wait_one(i, _):
            pltpu.make_async_copy(yt_ref.at[slot, 0:ROW_CHUNK, :], ys_ref.at[0:ROW_CHUNK, 0:FF_TILE],
                                  sem_ref.at[1 + slot]).wait()
            return 0

        lax.fori_loop(0, pend_ref[slot], wait_one, 0)
        pend_ref[slot] = 0

    @pl.when(jnp.logical_and(s == 0, j == 0))
    def _():
        pend_ref[0] = 0
        pend_ref[1] = 0

    def in_copy(c):
        return pltpu.make_async_copy(xs_ref.at[rows(c0 + c), :], xin_ref.at[rows(c), :], sem_ref.at[0])

    @pl.when(jnp.logical_and(n > 0, j == 0))
    def _():
        def start(c, _):
            in_copy(c).start()
            return 0

        def finish(c, _):
            in_copy(c).wait()
            return 0

        def cast(c, _):
            xbf_ref[rows(c), :] = xin_ref[rows(c), :].astype(bf16)
            return 0

        lax.fori_loop(0, n, start, 0)
        lax.fori_loop(0, n, finish, 0)
        lax.fori_loop(0, n, cast, 0)

    @pl.when(jnp.logical_and(n > 0, j < N_FF_TILES))
    def _():
        bg = bg_ref[0]
        bu = bu_ref[0]

        def body(c, k):
            x = xbf_ref[rows(c, k), :]
            g = jnp.dot(x, wg_ref[0].astype(bf16), preferred_element_type=f32) + bg
            u = jnp.dot(x, wu_ref[0].astype(bf16), preferred_element_type=f32) + bu
            g = jnp.minimum(g, SWIGLU_LIMIT)
            u = jnp.clip(u, -SWIGLU_LIMIT, SWIGLU_LIMIT)
            a = (u + 1.0) * (g * (1.0 / (1.0 + jnp.exp(-SWIGLU_ALPHA * g))))
            act_ref[j, rows(c, k), :] = a.astype(bf16)

        cover(body)

    @pl.when(jnp.logical_and(n > 0, j >= N_FF_TILES))
    def _():
        d = j - N_FF_TILES
        slot = d % 2
        bd = bd_ref[0]
        cols = pl.ds(pl.multiple_of(d * FF_TILE, FF_TILE), FF_TILE)
        drain(slot)

        def body(c, k):
            a = jnp.concatenate([act_ref[f, rows(c, k), :] for f in range(N_FF_TILES)], axis=-1)
            yt_ref[slot, rows(c, k), :] = jnp.dot(a, wd_ref[0].astype(bf16), preferred_element_type=f32) + bd
            for i in range(k):
                pltpu.make_async_copy(yt_ref.at[slot, rows(c + i), :], ys_ref.at[rows(c0 + c + i), cols],
                                      sem_ref.at[1 + slot]).start()

        cover(body)
        pend_ref[slot] = n

    @pl.when(jnp.logical_and(s == MAX_SB - 1, j == n_steps - 1))
    def _():
        drain(0)
        drain(1)
        xin_ref[0:ROW_CHUNK, :] = jnp.zeros((ROW_CHUNK, D_MODEL), f32)

        def tail_copy(c):
            return pltpu.make_async_copy(xin_ref.at[0:ROW_CHUNK, :], ys_ref.at[rows(c), :], sem_ref.at[0])

        def start(c, _):
            tail_copy(c).start()
            return 0

        def finish(c, _):
            tail_copy(c).wait()
            return 0

        lax.fori_loop(used_ref[1], MAX_CHUNKS, start, 0)
        lax.fori_loop(used_ref[1], MAX_CHUNKS, finish, 0)


def _experts(sb_e, sb_c0, sb_n, n_used, xs, w_gate, b_gate, w_up, b_up, w_down, b_down):
    last_up = N_FF_TILES - 1

    def up_map(s, j, e, c0, n, used):
        live = s < used[0]
        return (e[s], 0, jnp.where(live, jnp.minimum(j, last_up), last_up))

    def down_map(s, j, e, c0, n, used):
        live = s < used[0]
        return (e[s], 0, jnp.where(live, jnp.maximum(j - N_FF_TILES, 0), N_OUT_TILES - 1))

    return pl.pallas_call(
        _experts_kernel,
        name="experts",
        grid_spec=pltpu.PrefetchScalarGridSpec(
            num_scalar_prefetch=4,
            grid=(MAX_SB, N_FF_TILES + N_OUT_TILES),
            in_specs=[
                pl.BlockSpec(memory_space=pl.ANY),
                pl.BlockSpec((1, D_MODEL, FF_TILE), up_map),
                pl.BlockSpec((1, D_MODEL, FF_TILE), up_map),
                pl.BlockSpec((1, D_FF, FF_TILE), down_map),
                pl.BlockSpec((1, 1, FF_TILE), up_map),
                pl.BlockSpec((1, 1, FF_TILE), up_map),
                pl.BlockSpec((1, 1, FF_TILE), down_map),
            ],
            out_specs=pl.BlockSpec(memory_space=pl.ANY),
            scratch_shapes=[
                pltpu.VMEM((SB_ROWS, D_MODEL), f32),
                pltpu.VMEM((SB_ROWS, D_MODEL), bf16),
                pltpu.VMEM((N_FF_TILES, SB_ROWS, FF_TILE), bf16),
                pltpu.VMEM((2, SB_ROWS, FF_TILE), f32),
                pltpu.SMEM((2,), jnp.int32),
                pltpu.SemaphoreType.DMA((3,)),
            ],
        ),
        out_shape=jax.ShapeDtypeStruct((MAX_ROWS, D_MODEL), f32),
        compiler_params=_cparams(("arbitrary", "arbitrary")),
    )(sb_e, sb_c0, sb_n, n_used, xs, w_gate, w_up, w_down,
      b_gate.reshape(N_EXPERTS, 1, D_FF), b_up.reshape(N_EXPERTS, 1, D_FF),
      b_down.reshape(N_EXPERTS, 1, D_MODEL))


COMB_TM = 256


def _combine_kernel(dest_ref, ys_ref, x1_ref, gate_ref, gf_ref, o_ref, buf_ref, sem_ref):
    def issue(t, _):
        for k in range(TOP_K):
            pltpu.make_async_copy(ys_ref.at[pl.ds(dest_ref[t * TOP_K + k], 1), :],
                                  buf_ref.at[k, pl.ds(t, 1), :], sem_ref.at[0]).start()
        return 0

    lax.fori_loop(0, COMB_TM, issue, 0)
    for k in range(TOP_K):
        pltpu.make_async_copy(ys_ref.at[pl.ds(0, COMB_TM), :], buf_ref.at[k], sem_ref.at[0]).wait()

    gate = gate_ref[...]
    acc = x1_ref[...]
    for k in range(TOP_K):
        acc = acc + gate[:, k:k + 1] * buf_ref[k]
    o_ref[...] = _rms(acc, gf_ref[...])


def _combine(dest, ys, x1, gates, g_final):
    return pl.pallas_call(
        _combine_kernel,
        name="combine",
        grid=(SEQ // COMB_TM,),
        in_specs=[
            pl.BlockSpec((COMB_TM * TOP_K,), lambda i: (i,), memory_space=pltpu.SMEM),
            pl.BlockSpec(memory_space=pl.ANY),
            pl.BlockSpec((COMB_TM, D_MODEL), lambda i: (i, 0)),
            pl.BlockSpec((COMB_TM, LANES), lambda i: (i, 0)),
            pl.BlockSpec((1, D_MODEL), lambda i: (0, 0)),
        ],
        out_specs=pl.BlockSpec((COMB_TM, D_MODEL), lambda i: (i, 0)),
        out_shape=jax.ShapeDtypeStruct((SEQ, D_MODEL), f32),
        scratch_shapes=[
            pltpu.VMEM((TOP_K, COMB_TM, D_MODEL), f32),
            pltpu.SemaphoreType.DMA((1,)),
        ],
        compiler_params=_cparams(("arbitrary",)),
    )(dest, ys, x1, gates, g_final.reshape(1, D_MODEL))


def kernel(x, g_mix, w_in, lam_re_fwd, lam_im_fwd, log_dt_fwd, b_re_fwd, b_im_fwd, c_re_fwd, c_im_fwd, lam_re_bwd, lam_im_bwd, log_dt_bwd, b_re_bwd, b_im_bwd, c_re_bwd, c_im_bwd, ssm_d, w_glu, b_glu, na_rpb, g_ssm_out, g_na_out, w_out, g_moe, w_router, b_router, w_gate, b_gate, w_up, b_up, w_down, b_down, g_final):
    x2 = x.reshape(SEQ, D_MODEL)

    u, qkv = _in_proj(x2, g_mix[0], w_in[0].astype(bf16))

    s5_w, s5_dec = _s5_weights(
        (lam_re_fwd[0], lam_im_fwd[0], log_dt_fwd[0], b_re_fwd[0], b_im_fwd[0], c_re_fwd[0], c_im_fwd[0]),
        (lam_re_bwd[0], lam_im_bwd[0], log_dt_bwd[0], b_re_bwd[0], b_im_bwd[0], c_re_bwd[0], c_im_bwd[0]))
    y = _s5(u, s5_w, s5_dec, ssm_d[0])
    ssm_n = _glu(y, w_glu[0].astype(bf16), b_glu[0], g_ssm_out[0])

    y_na = _natten(qkv, _na_bias(na_rpb[0]))

    w_router_pad = jnp.zeros((D_MODEL, LANES), bf16).at[:, :N_EXPERTS].set(w_router[0].astype(bf16))
    b_router_pad = jnp.full((1, LANES), NEG_BIG, f32).at[0, :N_EXPERTS].set(b_router[0].astype(f32))
    x1, hn, logits = _out_proj(ssm_n, y_na, x2, g_na_out[0], w_out[0].astype(bf16), g_moe[0],
                               w_router_pad, b_router_pad)

    meta, gates, counts = _route(logits)
    dest, last_chunk, sb_e, sb_c0, sb_n, n_used = _routing_tables(meta, counts)

    xs = _dispatch(last_chunk, dest, hn)
    ys = _experts(sb_e, sb_c0, sb_n, n_used, xs, w_gate[0], b_gate[0], w_up[0], b_up[0],
                  w_down[0], b_down[0])
    out = _combine(dest, ys, x1, gates, g_final)
    return out.reshape(x.shape)
```

```python
import functools
import math

import jax
import jax.numpy as jnp
from jax import lax
from jax.experimental import pallas as pl
from jax.experimental.pallas import tpu as pltpu

f32 = jnp.float32
bf16 = jnp.bfloat16

D_MODEL = 2048
SEQ = 8192
SSM_WIDTH = 1024
NA_WIDTH = 1024
SSM_GROUP = 16
SSM_GROUPS = 64
SSM_STATE = 64
NA_HEAD_DIM = 64
NA_HEADS = 16
GRID_W = 64
GRID_ROWS = SEQ // GRID_W
NA_KH = 8
NA_KW = 16
N_EXPERTS = 32
TOP_K = 4
D_FF = 2048
SWIGLU_LIMIT = 7.0
SWIGLU_ALPHA = 1.702
RMS_EPS = 1e-5

LANES = 128
NEG_BIG = -1e30

CHUNK_T = 16
N_CHUNKS = SEQ // CHUNK_T
GROUPS_PER_BLOCK = LANES // SSM_GROUP
N_LANE_BLOCKS = SSM_WIDTH // LANES
CAT_W = CHUNK_T * LANES
STATE_W = GROUPS_PER_BLOCK * SSM_STATE

ROW_CHUNK = 128
CHUNKS_PER_SB = 10
SB_ROWS = ROW_CHUNK * CHUNKS_PER_SB
MAX_CHUNKS = SEQ * TOP_K // ROW_CHUNK + N_EXPERTS
MAX_ROWS = MAX_CHUNKS * ROW_CHUNK
MAX_SB = MAX_CHUNKS // CHUNKS_PER_SB + N_EXPERTS
FF_TILE = 512
N_FF_TILES = D_FF // FF_TILE
N_OUT_TILES = D_MODEL // FF_TILE

VMEM_LIMIT = 56 * 1024 * 1024


def _cparams(semantics, vmem=VMEM_LIMIT):
    return pltpu.CompilerParams(dimension_semantics=semantics, vmem_limit_bytes=vmem)


def _rms(x, g):
    return x * lax.rsqrt(jnp.mean(x * x, axis=-1, keepdims=True) + RMS_EPS) * g


IN_TM = 512
IN_TN = 1024


def _in_proj_kernel(x_ref, g_ref, w_ref, u_ref, qkv_ref, h_ref):
    j = pl.program_id(1)

    @pl.when(j == 0)
    def _():
        h_ref[...] = _rms(x_ref[...], g_ref[...]).astype(bf16)

    acc = jnp.dot(h_ref[...], w_ref[...], preferred_element_type=f32)

    @pl.when(j == 0)
    def _():
        u_ref[...] = acc

    @pl.when(j > 0)
    def _():
        qkv_ref[...] = acc.astype(bf16)


def _in_proj(x, g_mix, w_in_bf):
    n_out = w_in_bf.shape[1]
    return pl.pallas_call(
        _in_proj_kernel,
        name="in_proj",
        grid=(SEQ // IN_TM, n_out // IN_TN),
        in_specs=[
            pl.BlockSpec((IN_TM, D_MODEL), lambda i, j: (i, 0)),
            pl.BlockSpec((1, D_MODEL), lambda i, j: (0, 0)),
            pl.BlockSpec((D_MODEL, IN_TN), lambda i, j: (0, j)),
        ],
        out_specs=[
            pl.BlockSpec((IN_TM, IN_TN), lambda i, j: (i, 0)),
            pl.BlockSpec((IN_TM, IN_TN), lambda i, j: (i, jnp.maximum(j - 1, 0))),
        ],
        out_shape=[
            jax.ShapeDtypeStruct((SEQ, SSM_WIDTH), f32),
            jax.ShapeDtypeStruct((SEQ, 3 * NA_WIDTH), bf16),
        ],
        scratch_shapes=[pltpu.VMEM((IN_TM, D_MODEL), bf16)],
        compiler_params=_cparams(("arbitrary", "arbitrary")),
    )(x, g_mix.reshape(1, D_MODEL), w_in_bf)


def _cmul(ar, ai, br, bi):
    return ar * br - ai * bi, ar * bi + ai * br


def _s5_discretise(lam_re, lam_im, log_dt, b_re, b_im, c_re, c_im):
    a = jnp.minimum(lam_re.astype(f32), -1e-4)
    w = lam_im.astype(f32)
    dt = jnp.exp(log_dt.astype(f32))[:, None]
    steps = jnp.arange(CHUNK_T + 1, dtype=f32)[:, None, None]
    mag = jnp.exp((a * dt)[None] * steps)
    ang = (w * dt)[None] * steps
    pw = (mag * jnp.cos(ang), mag * jnp.sin(ang))
    xr, xi = pw[0][1] - 1.0, pw[1][1]
    den = a * a + w * w
    qr, qi = (xr * a + xi * w) / den, (xi * a - xr * w) / den
    bb = _cmul(qr[..., None], qi[..., None], b_re.astype(f32), b_im.astype(f32))
    return pw, bb, (c_re.astype(f32), c_im.astype(f32))


def _pair_blockdiag(m):
    z = jnp.zeros_like(m[..., 0, :, :])
    top = jnp.concatenate([m[..., 0, :, :], z], axis=-1)
    bot = jnp.concatenate([z, m[..., 1, :, :]], axis=-1)
    return jnp.concatenate([top, bot], axis=-2)


def _s5_weights(fwd, bwd):
    hi = lax.Precision.HIGH
    t = CHUNK_T
    pw_f, bb_f, c_f = _s5_discretise(*fwd)
    pw_b, bb_b, c_b = _s5_discretise(*bwd)
    pairs = (N_LANE_BLOCKS, GROUPS_PER_BLOCK // 2, 2)

    def lag_kernel(pw, bb, c):
        m_re, m_im = _cmul(c[0][None], c[1][None], pw[0][:t, :, None, :], pw[1][:t, :, None, :])
        m = jnp.concatenate([m_re, m_im], axis=-1)
        b = jnp.concatenate([bb[0], -bb[1]], axis=1)
        return jnp.einsum('jgcp,gpd->jgcd', m, b, precision=hi)

    kf, kb = lag_kernel(pw_f, bb_f, c_f), lag_kernel(pw_b, bb_b, c_b)
    k_lag = jnp.concatenate([kb[:0:-1], (kf[0] + kb[0])[None], kf[1:]], axis=0)
    lag = jnp.arange(t)[:, None] - jnp.arange(t)[None, :] + (t - 1)
    sel = (lag[:, :, None] == jnp.arange(2 * t - 1)).astype(f32)
    w_intra = jnp.einsum('oil,lgcd->gidoc', sel, k_lag, precision=hi)
    w_intra = _pair_blockdiag(w_intra.reshape(*pairs, t * SSM_GROUP, t * SSM_GROUP))

    in_f = _cmul(pw_f[0][:t][::-1][..., None], pw_f[1][:t][::-1][..., None], bb_f[0][None], bb_f[1][None])
    in_b = _cmul(pw_b[0][:t][..., None], pw_b[1][:t][..., None], bb_b[0][None], bb_b[1][None])

    def state_in(m):
        m = jnp.transpose(m, (1, 0, 3, 2)).reshape(*pairs, t * SSM_GROUP, SSM_STATE)
        return _pair_blockdiag(m)

    w_in = jnp.concatenate([state_in(in_f[0]), state_in(in_f[1]),
                            state_in(in_b[0]), state_in(in_b[1])], axis=-1)

    out_f = _cmul(c_f[0][None], c_f[1][None], pw_f[0][1:t + 1][:, :, None, :], pw_f[1][1:t + 1][:, :, None, :])
    out_b = _cmul(c_b[0][None], c_b[1][None],
                  pw_b[0][1:t + 1][::-1][:, :, None, :], pw_b[1][1:t + 1][::-1][:, :, None, :])

    def state_out(m):
        m = jnp.transpose(m, (1, 3, 0, 2)).reshape(*pairs, SSM_STATE, t * SSM_GROUP)
        return _pair_blockdiag(m)

    w_out = jnp.concatenate([state_out(out_f[0]), state_out(-out_f[1]),
                             state_out(out_b[0]), state_out(-out_b[1])], axis=-2)

    def decay(a):
        return a.reshape(N_LANE_BLOCKS, 1, STATE_W)

    dec = jnp.concatenate([decay(pw_f[0][t]), decay(pw_f[1][t]), decay(pw_b[0][t]), decay(pw_b[1][t])],
                          axis=-1)
    w_all = jnp.stack([w_in, w_intra, w_out], axis=2).astype(bf16)
    return w_all, dec.astype(f32)


def _gelu_tanh(x):
    return 0.5 * x * (1.0 + jnp.tanh(math.sqrt(2.0 / math.pi) * (x + 0.044715 * (x * x * x))))


PAIRS_PER_BLOCK = GROUPS_PER_BLOCK // 2
PAIR_W = 2 * CHUNK_T * SSM_GROUP


def _s5_kernel(u_ref, w_ref, dec_ref, d_ref, y_ref, cat_ref, catp_ref, st_ref, perm_ref):
    sw = STATE_W

    @pl.when(pl.program_id(0) == 0)
    def _():
        def strip(t, _):
            row = lax.broadcasted_iota(jnp.int32, (LANES, CAT_W), 0)
            col = lax.broadcasted_iota(jnp.int32, (LANES, CAT_W), 1)
            target = (row // SSM_GROUP) * (CHUNK_T * SSM_GROUP) + t * SSM_GROUP + row % SSM_GROUP
            perm_ref[pl.ds(pl.multiple_of(t * LANES, LANES), LANES), :] = (
                jnp.where(col == target, 1.0, 0.0).astype(bf16))
            return 0

        lax.fori_loop(0, CHUNK_T, strip, 0)

    for t in range(CHUNK_T):
        cat_ref[:, t * LANES:(t + 1) * LANES] = u_ref[pl.ds(t, N_CHUNKS, stride=CHUNK_T), :].astype(bf16)
    catp_ref[...] = jnp.dot(cat_ref[...], perm_ref[...], preferred_element_type=f32).astype(bf16)

    for p in range(PAIRS_PER_BLOCK):
        z = jnp.dot(catp_ref[:, p * PAIR_W:(p + 1) * PAIR_W], w_ref[0, p, 0], preferred_element_type=f32)
        for part in range(4):
            st_ref[:, part * sw + p * LANES:part * sw + (p + 1) * LANES] = z[:, part * LANES:(part + 1) * LANES]

    afr = dec_ref[0, :, 0 * sw:1 * sw]
    afi = dec_ref[0, :, 1 * sw:2 * sw]
    abr = dec_ref[0, :, 2 * sw:3 * sw]
    abi = dec_ref[0, :, 3 * sw:4 * sw]

    def step(i, carry):
        sfr, sfi, sbr, sbi = carry
        r = N_CHUNKS - 1 - i
        zf = st_ref[pl.ds(i, 1), 0:2 * sw]
        zb = st_ref[pl.ds(r, 1), 2 * sw:4 * sw]
        st_ref[pl.ds(i, 1), 0:2 * sw] = jnp.concatenate([sfr, sfi], axis=-1)
        st_ref[pl.ds(r, 1), 2 * sw:4 * sw] = jnp.concatenate([sbr, sbi], axis=-1)
        nfr = afr * sfr - afi * sfi + zf[:, :sw]
        nfi = afi * sfr + afr * sfi + zf[:, sw:]
        nbr = abr * sbr - abi * sbi + zb[:, :sw]
        nbi = abi * sbr + abr * sbi + zb[:, sw:]
        return nfr, nfi, nbr, nbi

    zero = jnp.zeros((1, sw), f32)
    lax.fori_loop(0, N_CHUNKS, step, (zero, zero, zero, zero))

    for p in range(PAIRS_PER_BLOCK):
        s_in = jnp.concatenate(
            [st_ref[:, part * sw + p * LANES:part * sw + (p + 1) * LANES] for part in range(4)], axis=-1)
        yp = jnp.dot(catp_ref[:, p * PAIR_W:(p + 1) * PAIR_W], w_ref[0, p, 1], preferred_element_type=f32)
        yp = yp + jnp.dot(s_in.astype(bf16), w_ref[0, p, 2], preferred_element_type=f32)
        cat_ref[:, p * PAIR_W:(p + 1) * PAIR_W] = yp.astype(bf16)

    st_ref[...] = lax.dot_general(cat_ref[...], perm_ref[...], (((1,), (1,)), ((), ())),
                                  preferred_element_type=f32)
    d = d_ref[...]
    for t in range(CHUNK_T):
        rows = pl.ds(t, N_CHUNKS, stride=CHUNK_T)
        v = st_ref[:, t * LANES:(t + 1) * LANES] + d * u_ref[rows, :]
        y_ref[rows, :] = _gelu_tanh(v)


def _s5(u, w_all, dec, ssm_d):
    return pl.pallas_call(
        _s5_kernel,
        name="s5_scan",
        grid=(N_LANE_BLOCKS,),
        in_specs=[
            pl.BlockSpec((SEQ, LANES), lambda j: (0, j)),
            pl.BlockSpec((1, PAIRS_PER_BLOCK, 3, PAIR_W, PAIR_W), lambda j: (j, 0, 0, 0, 0)),
            pl.BlockSpec((1, 1, 4 * STATE_W), lambda j: (j, 0, 0)),
            pl.BlockSpec((1, LANES), lambda j: (0, j)),
        ],
        out_specs=pl.BlockSpec((SEQ, LANES), lambda j: (0, j)),
        out_shape=jax.ShapeDtypeStruct((SEQ, SSM_WIDTH), f32),
        scratch_shapes=[
            pltpu.VMEM((N_CHUNKS, CAT_W), bf16),
            pltpu.VMEM((N_CHUNKS, CAT_W), bf16),
            pltpu.VMEM((N_CHUNKS, 4 * STATE_W), f32),
            pltpu.VMEM((CAT_W, CAT_W), bf16),
        ],
        compiler_params=_cparams(("arbitrary",)),
    )(u, w_all, dec, ssm_d.reshape(1, SSM_WIDTH))


GLU_TM = 512


def _glu_kernel(y_ref, w_ref, b_ref, g_ref, o_ref):
    y = y_ref[...]
    z = jnp.dot(y.astype(bf16), w_ref[...], preferred_element_type=f32) + b_ref[...]
    o = y * (1.0 / (1.0 + jnp.exp(-z)))
    o_ref[...] = _rms(o, g_ref[...]).astype(bf16)


def _glu(y, w_glu_bf, b_glu, g_ssm_out):
    return pl.pallas_call(
        _glu_kernel,
        name="glu_norm",
        grid=(SEQ // GLU_TM,),
        in_specs=[
            pl.BlockSpec((GLU_TM, SSM_WIDTH), lambda i: (i, 0)),
            pl.BlockSpec((SSM_WIDTH, SSM_WIDTH), lambda i: (0, 0)),
            pl.BlockSpec((1, SSM_WIDTH), lambda i: (0, 0)),
            pl.BlockSpec((1, SSM_WIDTH), lambda i: (0, 0)),
        ],
        out_specs=pl.BlockSpec((GLU_TM, SSM_WIDTH), lambda i: (i, 0)),
        out_shape=jax.ShapeDtypeStruct((SEQ, SSM_WIDTH), bf16),
        compiler_params=_cparams(("arbitrary",)),
    )(y, w_glu_bf, b_glu.reshape(1, SSM_WIDTH), g_ssm_out.reshape(1, SSM_WIDTH))


NA_ROWS_PER_STEP = 8
NA_WIN = NA_KH * GRID_W
HEADS_PER_BLOCK = LANES // NA_HEAD_DIM


def _na_bias(rpb):
    c = jnp.arange(GRID_W)
    col_start = jnp.clip(c - NA_KW // 2, 0, GRID_W - NA_KW)
    valid = (c[None, :] >= col_start[:, None]) & (c[None, :] < col_start[:, None] + NA_KW)
    dc = jnp.clip(c[None, :] - c[:, None], -(NA_KW - 1), NA_KW - 1) + (NA_KW - 1)
    sel = (dc[None] == jnp.arange(2 * NA_KW - 1)[:, None, None]).astype(f32)
    tab = jnp.einsum('hrc,cqk->hrqk', rpb.astype(f32), sel, precision=lax.Precision.HIGHEST)
    tab = jnp.where(valid[None, None], tab, NEG_BIG)
    b = jnp.stack([tab[:, NA_KH - 1 - v:2 * NA_KH - 1 - v] for v in range(NA_KH)], axis=1)
    return jnp.transpose(b, (0, 1, 3, 2, 4)).reshape(NA_HEADS, NA_KH, GRID_W, NA_WIN)


def _natten_kernel(q_ref, k_ref, v_ref, b_ref, o_ref, s_ref, p_ref):
    rb = pl.program_id(1)
    lane = lax.broadcasted_iota(jnp.int32, (GRID_W, LANES), 1)
    head0 = lane < NA_HEAD_DIM
    scale = NA_HEAD_DIM ** -0.5

    starts, variants = [], []
    for i in range(NA_ROWS_PER_STEP):
        r = rb * NA_ROWS_PER_STEP + i
        rs = jnp.clip(r - NA_KH // 2, 0, GRID_ROWS - NA_KH)
        starts.append(pl.multiple_of(rs * GRID_W, GRID_W))
        variants.append(r - rs)

    for i in range(NA_ROWS_PER_STEP):
        q = q_ref[i * GRID_W:(i + 1) * GRID_W, :] * scale
        kw = k_ref[pl.ds(starts[i], NA_WIN), :]
        for h in range(HEADS_PER_BLOCK):
            qh = jnp.where(head0 if h == 0 else ~head0, q, jnp.zeros_like(q))
            s = lax.dot_general(qh, kw, (((1,), (1,)), ((), ())), preferred_element_type=f32)
            s_ref[i * HEADS_PER_BLOCK + h] = s + b_ref[h, variants[i]]

    inv_sums = []
    for t in range(NA_ROWS_PER_STEP * HEADS_PER_BLOCK):
        s = s_ref[t]
        p = jnp.exp(s - jnp.max(s, axis=-1, keepdims=True))
        inv_sums.append(1.0 / jnp.sum(p, axis=-1, keepdims=True))
        p_ref[t] = p.astype(bf16)

    for i in range(NA_ROWS_PER_STEP):
        vw = v_ref[pl.ds(starts[i], NA_WIN), :]
        outs = []
        for h in range(HEADS_PER_BLOCK):
            t = i * HEADS_PER_BLOCK + h
            outs.append(jnp.dot(p_ref[t], vw, preferred_element_type=f32) * inv_sums[t])
        o_ref[i * GRID_W:(i + 1) * GRID_W, :] = jnp.where(head0, outs[0], outs[1]).astype(bf16)


def _natten(qkv, bias):
    tm = NA_ROWS_PER_STEP * GRID_W
    n_hb = NA_WIDTH // LANES
    return pl.pallas_call(
        _natten_kernel,
        name="natten",
        grid=(n_hb, GRID_ROWS // NA_ROWS_PER_STEP),
        in_specs=[
            pl.BlockSpec((tm, LANES), lambda h, r: (r, h)),
            pl.BlockSpec((SEQ, LANES), lambda h, r: (0, n_hb + h)),
            pl.BlockSpec((SEQ, LANES), lambda h, r: (0, 2 * n_hb + h)),
            pl.BlockSpec((HEADS_PER_BLOCK, NA_KH, GRID_W, NA_WIN), lambda h, r: (h, 0, 0, 0)),
        ],
        out_specs=pl.BlockSpec((tm, LANES), lambda h, r: (r, h)),
        out_shape=jax.ShapeDtypeStruct((SEQ, NA_WIDTH), bf16),
        scratch_shapes=[
            pltpu.VMEM((NA_ROWS_PER_STEP * HEADS_PER_BLOCK, GRID_W, NA_WIN), f32),
            pltpu.VMEM((NA_ROWS_PER_STEP * HEADS_PER_BLOCK, GRID_W, NA_WIN), bf16),
        ],
        compiler_params=_cparams(("arbitrary", "arbitrary")),
    )(qkv, qkv, qkv, bias)


OUT_TM = 512


def _out_proj_kernel(ssm_ref, na_ref, x_ref, gna_ref, w_ref, gmoe_ref, wr_ref, br_ref,
                     x1_ref, hn_ref, lg_ref):
    na = _rms(na_ref[...].astype(f32), gna_ref[...]).astype(bf16)
    y = jnp.dot(ssm_ref[...], w_ref[0:SSM_WIDTH, :], preferred_element_type=f32)
    y = y + jnp.dot(na, w_ref[SSM_WIDTH:, :], preferred_element_type=f32)
    x1 = x_ref[...] + y
    x1_ref[...] = x1
    hn = _rms(x1, gmoe_ref[...])
    hn_ref[...] = hn
    lg_ref[...] = jnp.dot(hn.astype(bf16), wr_ref[...], preferred_element_type=f32) + br_ref[...]


def _out_proj(ssm_n, y_na, x, g_na_out, w_out_bf, g_moe, w_router_pad, b_router_pad):
    row = lambda i: (i, 0)
    fixed = lambda i: (0, 0)
    return pl.pallas_call(
        _out_proj_kernel,
        name="out_proj",
        grid=(SEQ // OUT_TM,),
        in_specs=[
            pl.BlockSpec((OUT_TM, SSM_WIDTH), row),
            pl.BlockSpec((OUT_TM, NA_WIDTH), row),
            pl.BlockSpec((OUT_TM, D_MODEL), row),
            pl.BlockSpec((1, NA_WIDTH), fixed),
            pl.BlockSpec((D_MODEL, D_MODEL), fixed),
            pl.BlockSpec((1, D_MODEL), fixed),
            pl.BlockSpec((D_MODEL, LANES), fixed),
            pl.BlockSpec((1, LANES), fixed),
        ],
        out_specs=[
            pl.BlockSpec((OUT_TM, D_MODEL), row),
            pl.BlockSpec((OUT_TM, D_MODEL), row),
            pl.BlockSpec((OUT_TM, LANES), row),
        ],
        out_shape=[
            jax.ShapeDtypeStruct((SEQ, D_MODEL), f32),
            jax.ShapeDtypeStruct((SEQ, D_MODEL), f32),
            jax.ShapeDtypeStruct((SEQ, LANES), f32),
        ],
        compiler_params=_cparams(("arbitrary",)),
    )(ssm_n, y_na, x, g_na_out.reshape(1, NA_WIDTH), w_out_bf, g_moe.reshape(1, D_MODEL),
      w_router_pad, b_router_pad)


ROUTE_TM = 512


def _route_kernel(lg_ref, tri_ref, meta_ref, gate_ref, cnt_ref, carry_ref):
    i = pl.program_id(0)

    @pl.when(i == 0)
    def _():
        carry_ref[...] = jnp.zeros_like(carry_ref)

    lane = lax.broadcasted_iota(jnp.int32, (ROUTE_TM, LANES), 1)
    lane_f = lane.astype(f32)
    work = lg_ref[...]
    vals, hits = [], []
    for _ in range(TOP_K):
        m = jnp.max(work, axis=-1, keepdims=True)
        idx = jnp.min(jnp.where(work == m, lane_f, float(LANES)), axis=-1, keepdims=True)
        hit = lane_f == idx
        vals.append(m)
        hits.append((idx, hit))
        work = jnp.where(hit, -jnp.inf, work)

    exps = [jnp.exp(v - vals[0]) for v in vals]
    denom = exps[0] + exps[1] + exps[2] + exps[3]

    onehot = jnp.zeros((ROUTE_TM, LANES), f32)
    for _, hit in hits:
        onehot = onehot + hit.astype(f32)
    before = jnp.dot(tri_ref[...], onehot.astype(bf16), preferred_element_type=f32) + carry_ref[...]

    meta = jnp.zeros((ROUTE_TM, LANES), jnp.int32)
    gate = jnp.zeros((ROUTE_TM, LANES), f32)
    for k, (idx, hit) in enumerate(hits):
        rank = jnp.sum(jnp.where(hit, before, 0.0), axis=-1, keepdims=True).astype(jnp.int32)
        meta = jnp.where(lane == k, idx.astype(jnp.int32), meta)
        meta = jnp.where(lane == TOP_K + k, rank, meta)
        gate = jnp.where(lane == k, exps[k] / denom, gate)
    meta_ref[...] = meta
    gate_ref[...] = gate
    carry_ref[...] += jnp.sum(onehot, axis=0, keepdims=True)
    cnt_ref[...] = carry_ref[...]


def _route(logits):
    tri = (jnp.arange(ROUTE_TM)[:, None] > jnp.arange(ROUTE_TM)[None, :]).astype(bf16)
    row = lambda i: (i, 0)
    return pl.pallas_call(
        _route_kernel,
        name="route",
        grid=(SEQ // ROUTE_TM,),
        in_specs=[
            pl.BlockSpec((ROUTE_TM, LANES), row),
            pl.BlockSpec((ROUTE_TM, ROUTE_TM), lambda i: (0, 0)),
        ],
        out_specs=[
            pl.BlockSpec((ROUTE_TM, LANES), row),
            pl.BlockSpec((ROUTE_TM, LANES), row),
            pl.BlockSpec((1, LANES), lambda i: (0, 0)),
        ],
        out_shape=[
            jax.ShapeDtypeStruct((SEQ, LANES), jnp.int32),
            jax.ShapeDtypeStruct((SEQ, LANES), f32),
            jax.ShapeDtypeStruct((1, LANES), f32),
        ],
        scratch_shapes=[pltpu.VMEM((1, LANES), f32)],
        compiler_params=_cparams(("arbitrary",)),
    )(logits, tri)


def _routing_tables(meta, counts):
    idx = meta[:, :TOP_K]
    rank = meta[:, TOP_K:2 * TOP_K]
    cnt = counts[0, :N_EXPERTS].astype(jnp.int32)
    n_chunks = (cnt + ROW_CHUNK - 1) // ROW_CHUNK
    chunk_base = jnp.cumsum(n_chunks) - n_chunks
    dest = (chunk_base[idx] * ROW_CHUNK + rank).astype(jnp.int32).reshape(-1)
    total_chunks = jnp.sum(n_chunks)
    last_chunk = jnp.concatenate([jnp.where(cnt > 0, chunk_base + n_chunks - 1, -1),
                                  total_chunks[None]]).astype(jnp.int32)

    n_sb = (n_chunks + CHUNKS_PER_SB - 1) // CHUNKS_PER_SB
    sb_end = jnp.cumsum(n_sb)
    sb_start = sb_end - n_sb
    n_used = sb_end[-1]
    s = jnp.arange(MAX_SB)
    s_eff = jnp.minimum(s, n_used - 1)
    e = jnp.minimum(jnp.searchsorted(sb_end, s_eff, side='right'), N_EXPERTS - 1)
    kk = s_eff - sb_start[e]
    sb_chunk0 = chunk_base[e] + kk * CHUNKS_PER_SB
    sb_n = jnp.where(s < n_used, jnp.clip(n_chunks[e] - kk * CHUNKS_PER_SB, 0, CHUNKS_PER_SB), 0)
    used = jnp.stack([n_used, total_chunks]).astype(jnp.int32)
    return (dest, last_chunk, e.astype(jnp.int32), sb_chunk0.astype(jnp.int32),
            sb_n.astype(jnp.int32), used)


DISP_TM = 256


def _row_copy(src, src_row, dst, dst_row, sem):
    return pltpu.make_async_copy(src.at[pl.ds(src_row, 1), :], dst.at[pl.ds(dst_row, 1), :], sem)


def _dispatch_kernel(last_ref, dest_ref, hn_ref, xs_ref, zero_ref, sem_ref):
    i = pl.program_id(0)

    @pl.when(i == 0)
    def _():
        zero_ref[...] = jnp.zeros_like(zero_ref)

        def chunk_copy(c):
            row0 = pl.multiple_of(c * ROW_CHUNK, ROW_CHUNK)
            return pltpu.make_async_copy(zero_ref, xs_ref.at[pl.ds(row0, ROW_CHUNK), :], sem_ref.at[1])

        def start(e, _):
            @pl.when(last_ref[e] >= 0)
            def _():
                chunk_copy(last_ref[e]).start()
            return 0

        def wait(e, _):
            @pl.when(last_ref[e] >= 0)
            def _():
                chunk_copy(last_ref[e]).wait()
            return 0

        def start_tail(c, _):
            chunk_copy(c).start()
            return 0

        def wait_tail(c, _):
            chunk_copy(c).wait()
            return 0

        lax.fori_loop(0, N_EXPERTS, start, 0)
        lax.fori_loop(last_ref[N_EXPERTS], MAX_CHUNKS, start_tail, 0)
        lax.fori_loop(0, N_EXPERTS, wait, 0)
        lax.fori_loop(last_ref[N_EXPERTS], MAX_CHUNKS, wait_tail, 0)

    def issue(t, _):
        for k in range(TOP_K):
            _row_copy(hn_ref, t, xs_ref, dest_ref[t * TOP_K + k], sem_ref.at[0]).start()
        return 0

    lax.fori_loop(0, DISP_TM, issue, 0, unroll=8)
    for k in range(TOP_K):
        pltpu.make_async_copy(hn_ref, xs_ref.at[pl.ds(0, DISP_TM), :], sem_ref.at[0]).wait()


def _dispatch(last_chunk, dest, hn):
    return pl.pallas_call(
        _dispatch_kernel,
        name="dispatch",
        grid_spec=pltpu.PrefetchScalarGridSpec(
            num_scalar_prefetch=1,
            grid=(SEQ // DISP_TM,),
            in_specs=[
                pl.BlockSpec((DISP_TM * TOP_K,), lambda i, last: (i,), memory_space=pltpu.SMEM),
                pl.BlockSpec((DISP_TM, D_MODEL), lambda i, last: (i, 0)),
            ],
            out_specs=pl.BlockSpec(memory_space=pl.ANY),
            scratch_shapes=[
                pltpu.VMEM((ROW_CHUNK, D_MODEL), f32),
                pltpu.SemaphoreType.DMA((2,)),
            ],
        ),
        out_shape=jax.ShapeDtypeStruct((MAX_ROWS, D_MODEL), f32),
        compiler_params=_cparams(("arbitrary",)),
    )(last_chunk, dest, hn)


def _experts_kernel(e_ref, c0_ref, n_ref, used_ref,
                    xs_ref, wg_ref, wu_ref, wd_ref, bg_ref, bu_ref, bd_ref, ys_ref,
                    xin_ref, xbf_ref, act_ref, yt_ref, pend_ref, sem_ref):
    s = pl.program_id(0)
    j = pl.program_id(1)
    n = n_ref[s]
    c0 = c0_ref[s]
    n_steps = N_FF_TILES + N_OUT_TILES

    def rows(c, k=1):
        return pl.ds(pl.multiple_of(c * ROW_CHUNK, ROW_CHUNK), k * ROW_CHUNK)

    def cover(body):
        n4 = n // 4

        def quad(i, _):
            body(i * 4, 4)
            return 0

        lax.fori_loop(0, n4, quad, 0)

        @pl.when((n & 2) != 0)
        def _():
            body(n4 * 4, 2)

        @pl.when((n & 1) != 0)
        def _():
            body(n4 * 4 + (n & 2), 1)

    def drain(slot):
        def wait_one(i, _):
            pltpu.make_async_copy(yt_ref.at[slot, 0:ROW_CHUNK, :], ys_ref.at[0:ROW_CHUNK, 0:FF_TILE],
                                  sem_ref.at[1 + slot]).wait()
            return 0

        lax.fori_loop(0, pend_ref[slot], wait_one, 0)
        pend_ref[slot] = 0

    @pl.when(jnp.logical_and(s == 0, j == 0))
    def _():
        pend_ref[0] = 0
        pend_ref[1] = 0

    def in_copy(c, slot):
        return pltpu.make_async_copy(xs_ref.at[rows(c0 + c), :], xin_ref.at[slot], sem_ref.at[3 + slot])

    @pl.when(jnp.logical_and(n > 0, j == 0))
    def _():
        in_copy(0, 0).start()

        def stage(c, _):
            slot = c % 2

            @pl.when(c + 1 < n)
            def _():
                in_copy(c + 1, 1 - slot).start()

            in_copy(c, slot).wait()
            xbf_ref[rows(c), :] = xin_ref[slot].astype(bf16)
            return 0

        lax.fori_loop(0, n, stage, 0)

    @pl.when(jnp.logical_and(n > 0, j < N_FF_TILES))
    def _():
        bg = bg_ref[0]
        bu = bu_ref[0]

        def body(c, k):
            x = xbf_ref[rows(c, k), :]
            g = jnp.dot(x, wg_ref[0].astype(bf16), preferred_element_type=f32) + bg
            u = jnp.dot(x, wu_ref[0].astype(bf16), preferred_element_type=f32) + bu
            g = jnp.minimum(g, SWIGLU_LIMIT)
            u = jnp.clip(u, -SWIGLU_LIMIT, SWIGLU_LIMIT)
            a = (u + 1.0) * (g * (1.0 / (1.0 + jnp.exp(-SWIGLU_ALPHA * g))))
            act_ref[j, rows(c, k), :] = a.astype(bf16)

        cover(body)

    @pl.when(jnp.logical_and(n > 0, j >= N_FF_TILES))
    def _():
        d = j - N_FF_TILES
        slot = d % 2
        bd = bd_ref[0]
        cols = pl.ds(pl.multiple_of(d * FF_TILE, FF_TILE), FF_TILE)
        drain(slot)

        def body(c, k):
            a = jnp.concatenate([act_ref[f, rows(c, k), :] for f in range(N_FF_TILES)], axis=-1)
            yt_ref[slot, rows(c, k), :] = jnp.dot(a, wd_ref[0].astype(bf16), preferred_element_type=f32) + bd
            for i in range(k):
                pltpu.make_async_copy(yt_ref.at[slot, rows(c + i), :], ys_ref.at[rows(c0 + c + i), cols],
                                      sem_ref.at[1 + slot]).start()

        cover(body)
        pend_ref[slot] = n

    @pl.when(jnp.logical_and(s == MAX_SB - 1, j == n_steps - 1))
    def _():
        drain(0)
        drain(1)
        xin_ref[0] = jnp.zeros((ROW_CHUNK, D_MODEL), f32)

        def tail_copy(c):
            return pltpu.make_async_copy(xin_ref.at[0], ys_ref.at[rows(c), :], sem_ref.at[0])

        def start(c, _):
            tail_copy(c).start()
            return 0

        def finish(c, _):
            tail_copy(c).wait()
            return 0

        lax.fori_loop(used_ref[1], MAX_CHUNKS, start, 0)
        lax.fori_loop(used_ref[1], MAX_CHUNKS, finish, 0)


def _experts(sb_e, sb_c0, sb_n, n_used, xs, w_gate, b_gate, w_up, b_up, w_down, b_down):
    last_up = N_FF_TILES - 1

    def up_map(s, j, e, c0, n, used):
        live = s < used[0]
        return (e[s], 0, jnp.where(live, jnp.minimum(j, last_up), last_up))

    def down_map(s, j, e, c0, n, used):
        live = s < used[0]
        return (e[s], 0, jnp.where(live, jnp.maximum(j - N_FF_TILES, 0), N_OUT_TILES - 1))

    return pl.pallas_call(
        _experts_kernel,
        name="experts",
        grid_spec=pltpu.PrefetchScalarGridSpec(
            num_scalar_prefetch=4,
            grid=(MAX_SB, N_FF_TILES + N_OUT_TILES),
            in_specs=[
                pl.BlockSpec(memory_space=pl.ANY),
                pl.BlockSpec((1, D_MODEL, FF_TILE), up_map),
                pl.BlockSpec((1, D_MODEL, FF_TILE), up_map),
                pl.BlockSpec((1, D_FF, FF_TILE), down_map),
                pl.BlockSpec((1, 1, FF_TILE), up_map),
                pl.BlockSpec((1, 1, FF_TILE), up_map),
                pl.BlockSpec((1, 1, FF_TILE), down_map),
            ],
            out_specs=pl.BlockSpec(memory_space=pl.ANY),
            scratch_shapes=[
                pltpu.VMEM((2, ROW_CHUNK, D_MODEL), f32),
                pltpu.VMEM((SB_ROWS, D_MODEL), bf16),
                pltpu.VMEM((N_FF_TILES, SB_ROWS, FF_TILE), bf16),
                pltpu.VMEM((2, SB_ROWS, FF_TILE), f32),
                pltpu.SMEM((2,), jnp.int32),
                pltpu.SemaphoreType.DMA((5,)),
            ],
        ),
        out_shape=jax.ShapeDtypeStruct((MAX_ROWS, D_MODEL), f32),
        compiler_params=_cparams(("arbitrary", "arbitrary")),
    )(sb_e, sb_c0, sb_n, n_used, xs, w_gate, w_up, w_down,
      b_gate.reshape(N_EXPERTS, 1, D_FF), b_up.reshape(N_EXPERTS, 1, D_FF),
      b_down.reshape(N_EXPERTS, 1, D_MODEL))


COMB_TM = 256


def _combine_kernel(dest_ref, ys_ref, x1_ref, gate_ref, gf_ref, o_ref, buf_ref, sem_ref):
    def issue(t, _):
        for k in range(TOP_K):
            pltpu.make_async_copy(ys_ref.at[pl.ds(dest_ref[t * TOP_K + k], 1), :],
                                  buf_ref.at[k, pl.ds(t, 1), :], sem_ref.at[0]).start()
        return 0

    lax.fori_loop(0, COMB_TM, issue, 0, unroll=8)
    for k in range(TOP_K):
        pltpu.make_async_copy(ys_ref.at[pl.ds(0, COMB_TM), :], buf_ref.at[k], sem_ref.at[0]).wait()

    gate = gate_ref[...]
    acc = x1_ref[...]
    for k in range(TOP_K):
        acc = acc + gate[:, k:k + 1] * buf_ref[k]
    o_ref[...] = _rms(acc, gf_ref[...])


def _combine(dest, ys, x1, gates, g_final):
    return pl.pallas_call(
        _combine_kernel,
        name="combine",
        grid=(SEQ // COMB_TM,),
        in_specs=[
            pl.BlockSpec((COMB_TM * TOP_K,), lambda i: (i,), memory_space=pltpu.SMEM),
            pl.BlockSpec(memory_space=pl.ANY),
            pl.BlockSpec((COMB_TM, D_MODEL), lambda i: (i, 0)),
            pl.BlockSpec((COMB_TM, LANES), lambda i: (i, 0)),
            pl.BlockSpec((1, D_MODEL), lambda i: (0, 0)),
        ],
        out_specs=pl.BlockSpec((COMB_TM, D_MODEL), lambda i: (i, 0)),
        out_shape=jax.ShapeDtypeStruct((SEQ, D_MODEL), f32),
        scratch_shapes=[
            pltpu.VMEM((TOP_K, COMB_TM, D_MODEL), f32),
            pltpu.SemaphoreType.DMA((1,)),
        ],
        compiler_params=_cparams(("arbitrary",)),
    )(dest, ys, x1, gates, g_final.reshape(1, D_MODEL))


def kernel(x, g_mix, w_in, lam_re_fwd, lam_im_fwd, log_dt_fwd, b_re_fwd, b_im_fwd, c_re_fwd, c_im_fwd, lam_re_bwd, lam_im_bwd, log_dt_bwd, b_re_bwd, b_im_bwd, c_re_bwd, c_im_bwd, ssm_d, w_glu, b_glu, na_rpb, g_ssm_out, g_na_out, w_out, g_moe, w_router, b_router, w_gate, b_gate, w_up, b_up, w_down, b_down, g_final):
    x2 = x.reshape(SEQ, D_MODEL)

    u, qkv = _in_proj(x2, g_mix[0], w_in[0].astype(bf16))

    s5_w, s5_dec = _s5_weights(
        (lam_re_fwd[0], lam_im_fwd[0], log_dt_fwd[0], b_re_fwd[0], b_im_fwd[0], c_re_fwd[0], c_im_fwd[0]),
        (lam_re_bwd[0], lam_im_bwd[0], log_dt_bwd[0], b_re_bwd[0], b_im_bwd[0], c_re_bwd[0], c_im_bwd[0]))
    y = _s5(u, s5_w, s5_dec, ssm_d[0])
    ssm_n = _glu(y, w_glu[0].astype(bf16), b_glu[0], g_ssm_out[0])

    y_na = _natten(qkv, _na_bias(na_rpb[0]))

    w_router_pad = jnp.zeros((D_MODEL, LANES), bf16).at[:, :N_EXPERTS].set(w_router[0].astype(bf16))
    b_router_pad = jnp.full((1, LANES), NEG_BIG, f32).at[0, :N_EXPERTS].set(b_router[0].astype(f32))
    x1, hn, logits = _out_proj(ssm_n, y_na, x2, g_na_out[0], w_out[0].astype(bf16), g_moe[0],
                               w_router_pad, b_router_pad)

    meta, gates, counts = _route(logits)
    dest, last_chunk, sb_e, sb_c0, sb_n, n_used = _routing_tables(meta, counts)

    xs = _dispatch(last_chunk, dest, hn)
    ys = _experts(sb_e, sb_c0, sb_n, n_used, xs, w_gate[0], b_gate[0], w_up[0], b_up[0],
                  w_down[0], b_down[0])
    out = _combine(dest, ys, x1, gates, g_final)
    return out.reshape(x.shape)
```

```python
import functools
import math

import jax
import jax.numpy as jnp
from jax import lax
from jax.experimental import pallas as pl
from jax.experimental.pallas import tpu as pltpu

f32 = jnp.float32
bf16 = jnp.bfloat16

D_MODEL = 2048
SEQ = 8192
SSM_WIDTH = 1024
NA_WIDTH = 1024
SSM_GROUP = 16
SSM_GROUPS = 64
SSM_STATE = 64
NA_HEAD_DIM = 64
NA_HEADS = 16
GRID_W = 64
GRID_ROWS = SEQ // GRID_W
NA_KH = 8
NA_KW = 16
N_EXPERTS = 32
TOP_K = 4
D_FF = 2048
SWIGLU_LIMIT = 7.0
SWIGLU_ALPHA = 1.702
RMS_EPS = 1e-5

LANES = 128
HALF_D = D_MODEL // 2
NEG_BIG = -1e30

CHUNK_T = 16
N_CHUNKS = SEQ // CHUNK_T
GROUPS_PER_BLOCK = LANES // SSM_GROUP
N_LANE_BLOCKS = SSM_WIDTH // LANES
CAT_W = CHUNK_T * LANES
STATE_W = GROUPS_PER_BLOCK * SSM_STATE

ROW_CHUNK = 128
CHUNKS_PER_SB = 10
SB_ROWS = ROW_CHUNK * CHUNKS_PER_SB
MAX_CHUNKS = SEQ * TOP_K // ROW_CHUNK + N_EXPERTS
MAX_ROWS = MAX_CHUNKS * ROW_CHUNK
MAX_SB = MAX_CHUNKS // CHUNKS_PER_SB + N_EXPERTS
FF_TILE = 512
N_FF_TILES = D_FF // FF_TILE
N_OUT_TILES = D_MODEL // FF_TILE

VMEM_LIMIT = 56 * 1024 * 1024


def _cparams(semantics, vmem=VMEM_LIMIT):
    return pltpu.CompilerParams(dimension_semantics=semantics, vmem_limit_bytes=vmem)


def _rms(x, g):
    return x * lax.rsqrt(jnp.mean(x * x, axis=-1, keepdims=True) + RMS_EPS) * g


IN_TM = 1024
IN_TN = 1024


def _in_proj_kernel(x_ref, g_ref, w_ref, u_ref, qkv_ref, h_ref):
    j = pl.program_id(1)

    @pl.when(j == 0)
    def _():
        h_ref[...] = _rms(x_ref[...], g_ref[...]).astype(bf16)

    acc = jnp.dot(h_ref[...], w_ref[...], preferred_element_type=f32)

    @pl.when(j == 0)
    def _():
        u_ref[...] = acc

    @pl.when(j > 0)
    def _():
        qkv_ref[...] = acc.astype(bf16)


def _in_proj(x, g_mix, w_in_bf):
    n_out = w_in_bf.shape[1]
    return pl.pallas_call(
        _in_proj_kernel,
        name="in_proj",
        grid=(SEQ // IN_TM, n_out // IN_TN),
        in_specs=[
            pl.BlockSpec((IN_TM, D_MODEL), lambda i, j: (i, 0)),
            pl.BlockSpec((1, D_MODEL), lambda i, j: (0, 0)),
            pl.BlockSpec((D_MODEL, IN_TN), lambda i, j: (0, j)),
        ],
        out_specs=[
            pl.BlockSpec((IN_TM, IN_TN), lambda i, j: (i, 0)),
            pl.BlockSpec((IN_TM, IN_TN), lambda i, j: (i, jnp.maximum(j - 1, 0))),
        ],
        out_shape=[
            jax.ShapeDtypeStruct((SEQ, SSM_WIDTH), f32),
            jax.ShapeDtypeStruct((SEQ, 3 * NA_WIDTH), bf16),
        ],
        scratch_shapes=[pltpu.VMEM((IN_TM, D_MODEL), bf16)],
        compiler_params=_cparams(("arbitrary", "arbitrary")),
    )(x, g_mix.reshape(1, D_MODEL), w_in_bf)


def _cmul(ar, ai, br, bi):
    return ar * br - ai * bi, ar * bi + ai * br


def _s5_discretise(lam_re, lam_im, log_dt, b_re, b_im, c_re, c_im):
    a = jnp.minimum(lam_re.astype(f32), -1e-4)
    w = lam_im.astype(f32)
    dt = jnp.exp(log_dt.astype(f32))[:, None]
    steps = jnp.arange(CHUNK_T + 1, dtype=f32)[:, None, None]
    mag = jnp.exp((a * dt)[None] * steps)
    ang = (w * dt)[None] * steps
    pw = (mag * jnp.cos(ang), mag * jnp.sin(ang))
    xr, xi = pw[0][1] - 1.0, pw[1][1]
    den = a * a + w * w
    qr, qi = (xr * a + xi * w) / den, (xi * a - xr * w) / den
    bb = _cmul(qr[..., None], qi[..., None], b_re.astype(f32), b_im.astype(f32))
    return pw, bb, (c_re.astype(f32), c_im.astype(f32))


def _pair_blockdiag(m):
    z = jnp.zeros_like(m[..., 0, :, :])
    top = jnp.concatenate([m[..., 0, :, :], z], axis=-1)
    bot = jnp.concatenate([z, m[..., 1, :, :]], axis=-1)
    return jnp.concatenate([top, bot], axis=-2)


def _s5_weights(fwd, bwd):
    hi = lax.Precision.HIGH
    t = CHUNK_T
    pw_f, bb_f, c_f = _s5_discretise(*fwd)
    pw_b, bb_b, c_b = _s5_discretise(*bwd)
    pairs = (N_LANE_BLOCKS, GROUPS_PER_BLOCK // 2, 2)

    def lag_kernel(pw, bb, c):
        m_re, m_im = _cmul(c[0][None], c[1][None], pw[0][:t, :, None, :], pw[1][:t, :, None, :])
        m = jnp.concatenate([m_re, m_im], axis=-1)
        b = jnp.concatenate([bb[0], -bb[1]], axis=1)
        return jnp.einsum('jgcp,gpd->jgcd', m, b, precision=hi)

    kf, kb = lag_kernel(pw_f, bb_f, c_f), lag_kernel(pw_b, bb_b, c_b)
    k_lag = jnp.concatenate([kb[:0:-1], (kf[0] + kb[0])[None], kf[1:]], axis=0)
    lag = jnp.arange(t)[:, None] - jnp.arange(t)[None, :] + (t - 1)
    sel = (lag[:, :, None] == jnp.arange(2 * t - 1)).astype(f32)
    w_intra = jnp.einsum('oil,lgcd->gidoc', sel, k_lag, precision=hi)
    w_intra = _pair_blockdiag(w_intra.reshape(*pairs, t * SSM_GROUP, t * SSM_GROUP))

    in_f = _cmul(pw_f[0][:t][::-1][..., None], pw_f[1][:t][::-1][..., None], bb_f[0][None], bb_f[1][None])
    in_b = _cmul(pw_b[0][:t][..., None], pw_b[1][:t][..., None], bb_b[0][None], bb_b[1][None])

    def state_in(m):
        m = jnp.transpose(m, (1, 0, 3, 2)).reshape(*pairs, t * SSM_GROUP, SSM_STATE)
        return _pair_blockdiag(m)

    w_in = jnp.concatenate([state_in(in_f[0]), state_in(in_f[1]),
                            state_in(in_b[0]), state_in(in_b[1])], axis=-1)

    out_f = _cmul(c_f[0][None], c_f[1][None], pw_f[0][1:t + 1][:, :, None, :], pw_f[1][1:t + 1][:, :, None, :])
    out_b = _cmul(c_b[0][None], c_b[1][None],
                  pw_b[0][1:t + 1][::-1][:, :, None, :], pw_b[1][1:t + 1][::-1][:, :, None, :])

    def state_out(m):
        m = jnp.transpose(m, (1, 3, 0, 2)).reshape(*pairs, SSM_STATE, t * SSM_GROUP)
        return _pair_blockdiag(m)

    w_out = jnp.concatenate([state_out(out_f[0]), state_out(-out_f[1]),
                             state_out(out_b[0]), state_out(-out_b[1])], axis=-2)

    def decay(a):
        return a.reshape(N_LANE_BLOCKS, 1, STATE_W)

    dec = jnp.concatenate([decay(pw_f[0][t]), decay(pw_f[1][t]), decay(pw_b[0][t]), decay(pw_b[1][t])],
                          axis=-1)
    w_all = jnp.stack([w_in, w_intra, w_out], axis=2).astype(bf16)
    return w_all, dec.astype(f32)


def _gelu_tanh(x):
    return 0.5 * x * (1.0 + jnp.tanh(math.sqrt(2.0 / math.pi) * (x + 0.044715 * (x * x * x))))


PAIRS_PER_BLOCK = GROUPS_PER_BLOCK // 2
PAIR_W = 2 * CHUNK_T * SSM_GROUP


def _s5_kernel(u_ref, w_ref, dec_ref, d_ref, y_ref, cat_ref, catp_ref, st_ref, perm_ref):
    sw = STATE_W

    @pl.when(pl.program_id(0) == 0)
    def _():
        def strip(t, _):
            row = lax.broadcasted_iota(jnp.int32, (LANES, CAT_W), 0)
            col = lax.broadcasted_iota(jnp.int32, (LANES, CAT_W), 1)
            target = (row // SSM_GROUP) * (CHUNK_T * SSM_GROUP) + t * SSM_GROUP + row % SSM_GROUP
            perm_ref[pl.ds(pl.multiple_of(t * LANES, LANES), LANES), :] = (
                jnp.where(col == target, 1.0, 0.0).astype(bf16))
            return 0

        lax.fori_loop(0, CHUNK_T, strip, 0)

    for t in range(CHUNK_T):
        cat_ref[:, t * LANES:(t + 1) * LANES] = u_ref[pl.ds(t, N_CHUNKS, stride=CHUNK_T), :].astype(bf16)
    catp_ref[...] = jnp.dot(cat_ref[...], perm_ref[...], preferred_element_type=f32).astype(bf16)

    for p in range(PAIRS_PER_BLOCK):
        z = jnp.dot(catp_ref[:, p * PAIR_W:(p + 1) * PAIR_W], w_ref[0, p, 0], preferred_element_type=f32)
        for part in range(4):
            st_ref[:, part * sw + p * LANES:part * sw + (p + 1) * LANES] = z[:, part * LANES:(part + 1) * LANES]

    afr = dec_ref[0, :, 0 * sw:1 * sw]
    afi = dec_ref[0, :, 1 * sw:2 * sw]
    abr = dec_ref[0, :, 2 * sw:3 * sw]
    abi = dec_ref[0, :, 3 * sw:4 * sw]

    def step(i, carry):
        sfr, sfi, sbr, sbi = carry
        r = N_CHUNKS - 1 - i
        zf = st_ref[pl.ds(i, 1), 0:2 * sw]
        zb = st_ref[pl.ds(r, 1), 2 * sw:4 * sw]
        st_ref[pl.ds(i, 1), 0:2 * sw] = jnp.concatenate([sfr, sfi], axis=-1)
        st_ref[pl.ds(r, 1), 2 * sw:4 * sw] = jnp.concatenate([sbr, sbi], axis=-1)
        nfr = afr * sfr - afi * sfi + zf[:, :sw]
        nfi = afi * sfr + afr * sfi + zf[:, sw:]
        nbr = abr * sbr - abi * sbi + zb[:, :sw]
        nbi = abi * sbr + abr * sbi + zb[:, sw:]
        return nfr, nfi, nbr, nbi

    zero = jnp.zeros((1, sw), f32)
    lax.fori_loop(0, N_CHUNKS, step, (zero, zero, zero, zero))

    for p in range(PAIRS_PER_BLOCK):
        s_in = jnp.concatenate(
            [st_ref[:, part * sw + p * LANES:part * sw + (p + 1) * LANES] for part in range(4)], axis=-1)
        yp = jnp.dot(catp_ref[:, p * PAIR_W:(p + 1) * PAIR_W], w_ref[0, p, 1], preferred_element_type=f32)
        yp = yp + jnp.dot(s_in.astype(bf16), w_ref[0, p, 2], preferred_element_type=f32)
        cat_ref[:, p * PAIR_W:(p + 1) * PAIR_W] = yp.astype(bf16)

    st_ref[...] = lax.dot_general(cat_ref[...], perm_ref[...], (((1,), (1,)), ((), ())),
                                  preferred_element_type=f32)
    d = d_ref[...]
    for t in range(CHUNK_T):
        rows = pl.ds(t, N_CHUNKS, stride=CHUNK_T)
        v = st_ref[:, t * LANES:(t + 1) * LANES] + d * u_ref[rows, :]
        y_ref[rows, :] = _gelu_tanh(v)


def _s5(u, w_all, dec, ssm_d):
    return pl.pallas_call(
        _s5_kernel,
        name="s5_scan",
        grid=(N_LANE_BLOCKS,),
        in_specs=[
            pl.BlockSpec((SEQ, LANES), lambda j: (0, j)),
            pl.BlockSpec((1, PAIRS_PER_BLOCK, 3, PAIR_W, PAIR_W), lambda j: (j, 0, 0, 0, 0)),
            pl.BlockSpec((1, 1, 4 * STATE_W), lambda j: (j, 0, 0)),
            pl.BlockSpec((1, LANES), lambda j: (0, j)),
        ],
        out_specs=pl.BlockSpec((SEQ, LANES), lambda j: (0, j)),
        out_shape=jax.ShapeDtypeStruct((SEQ, SSM_WIDTH), f32),
        scratch_shapes=[
            pltpu.VMEM((N_CHUNKS, CAT_W), bf16),
            pltpu.VMEM((N_CHUNKS, CAT_W), bf16),
            pltpu.VMEM((N_CHUNKS, 4 * STATE_W), f32),
            pltpu.VMEM((CAT_W, CAT_W), bf16),
        ],
        compiler_params=_cparams(("arbitrary",)),
    )(u, w_all, dec, ssm_d.reshape(1, SSM_WIDTH))


GLU_TM = 512


def _glu_kernel(y_ref, w_ref, b_ref, g_ref, o_ref):
    y = y_ref[...]
    z = jnp.dot(y.astype(bf16), w_ref[...], preferred_element_type=f32) + b_ref[...]
    o = y * (1.0 / (1.0 + jnp.exp(-z)))
    o_ref[...] = _rms(o, g_ref[...]).astype(bf16)


def _glu(y, w_glu_bf, b_glu, g_ssm_out):
    return pl.pallas_call(
        _glu_kernel,
        name="glu_norm",
        grid=(SEQ // GLU_TM,),
        in_specs=[
            pl.BlockSpec((GLU_TM, SSM_WIDTH), lambda i: (i, 0)),
            pl.BlockSpec((SSM_WIDTH, SSM_WIDTH), lambda i: (0, 0)),
            pl.BlockSpec((1, SSM_WIDTH), lambda i: (0, 0)),
            pl.BlockSpec((1, SSM_WIDTH), lambda i: (0, 0)),
        ],
        out_specs=pl.BlockSpec((GLU_TM, SSM_WIDTH), lambda i: (i, 0)),
        out_shape=jax.ShapeDtypeStruct((SEQ, SSM_WIDTH), bf16),
        compiler_params=_cparams(("arbitrary",)),
    )(y, w_glu_bf, b_glu.reshape(1, SSM_WIDTH), g_ssm_out.reshape(1, SSM_WIDTH))


NA_ROWS_PER_STEP = 8
NA_WIN = NA_KH * GRID_W
HEADS_PER_BLOCK = LANES // NA_HEAD_DIM


def _na_bias(rpb):
    c = jnp.arange(GRID_W)
    col_start = jnp.clip(c - NA_KW // 2, 0, GRID_W - NA_KW)
    valid = (c[None, :] >= col_start[:, None]) & (c[None, :] < col_start[:, None] + NA_KW)
    dc = jnp.clip(c[None, :] - c[:, None], -(NA_KW - 1), NA_KW - 1) + (NA_KW - 1)
    sel = (dc[None] == jnp.arange(2 * NA_KW - 1)[:, None, None]).astype(f32)
    tab = jnp.einsum('hrc,cqk->hrqk', rpb.astype(f32), sel, precision=lax.Precision.HIGHEST)
    tab = jnp.where(valid[None, None], tab, NEG_BIG)
    b = jnp.stack([tab[:, NA_KH - 1 - v:2 * NA_KH - 1 - v] for v in range(NA_KH)], axis=1)
    return jnp.transpose(b, (0, 1, 3, 2, 4)).reshape(NA_HEADS, NA_KH, GRID_W, NA_WIN)


def _natten_kernel(q_ref, k_ref, v_ref, b_ref, o_ref, s_ref, p_ref):
    rb = pl.program_id(1)
    lane = lax.broadcasted_iota(jnp.int32, (GRID_W, LANES), 1)
    head0 = lane < NA_HEAD_DIM
    scale = NA_HEAD_DIM ** -0.5

    starts, variants = [], []
    for i in range(NA_ROWS_PER_STEP):
        r = rb * NA_ROWS_PER_STEP + i
        rs = jnp.clip(r - NA_KH // 2, 0, GRID_ROWS - NA_KH)
        starts.append(pl.multiple_of(rs * GRID_W, GRID_W))
        variants.append(r - rs)

    for i in range(NA_ROWS_PER_STEP):
        q = q_ref[i * GRID_W:(i + 1) * GRID_W, :] * scale
        kw = k_ref[pl.ds(starts[i], NA_WIN), :]
        for h in range(HEADS_PER_BLOCK):
            qh = jnp.where(head0 if h == 0 else ~head0, q, jnp.zeros_like(q))
            s = lax.dot_general(qh, kw, (((1,), (1,)), ((), ())), preferred_element_type=f32)
            s_ref[i * HEADS_PER_BLOCK + h] = s + b_ref[h, variants[i]]

    inv_sums = []
    for t in range(NA_ROWS_PER_STEP * HEADS_PER_BLOCK):
        s = s_ref[t]
        p = jnp.exp(s - jnp.max(s, axis=-1, keepdims=True))
        inv_sums.append(1.0 / jnp.sum(p, axis=-1, keepdims=True))
        p_ref[t] = p.astype(bf16)

    for i in range(NA_ROWS_PER_STEP):
        vw = v_ref[pl.ds(starts[i], NA_WIN), :]
        outs = []
        for h in range(HEADS_PER_BLOCK):
            t = i * HEADS_PER_BLOCK + h
            outs.append(jnp.dot(p_ref[t], vw, preferred_element_type=f32) * inv_sums[t])
        o_ref[i * GRID_W:(i + 1) * GRID_W, :] = jnp.where(head0, outs[0], outs[1]).astype(bf16)


def _natten(qkv, bias):
    tm = NA_ROWS_PER_STEP * GRID_W
    n_hb = NA_WIDTH // LANES
    return pl.pallas_call(
        _natten_kernel,
        name="natten",
        grid=(n_hb, GRID_ROWS // NA_ROWS_PER_STEP),
        in_specs=[
            pl.BlockSpec((tm, LANES), lambda h, r: (r, h)),
            pl.BlockSpec((SEQ, LANES), lambda h, r: (0, n_hb + h)),
            pl.BlockSpec((SEQ, LANES), lambda h, r: (0, 2 * n_hb + h)),
            pl.BlockSpec((HEADS_PER_BLOCK, NA_KH, GRID_W, NA_WIN), lambda h, r: (h, 0, 0, 0)),
        ],
        out_specs=pl.BlockSpec((tm, LANES), lambda h, r: (r, h)),
        out_shape=jax.ShapeDtypeStruct((SEQ, NA_WIDTH), bf16),
        scratch_shapes=[
            pltpu.VMEM((NA_ROWS_PER_STEP * HEADS_PER_BLOCK, GRID_W, NA_WIN), f32),
            pltpu.VMEM((NA_ROWS_PER_STEP * HEADS_PER_BLOCK, GRID_W, NA_WIN), bf16),
        ],
        compiler_params=_cparams(("arbitrary", "arbitrary")),
    )(qkv, qkv, qkv, bias)


OUT_TM = 512


def _out_proj_kernel(ssm_ref, na_ref, x_ref, gna_ref, w_ref, gmoe_ref, wr_ref, br_ref,
                     x1_ref, hn_ref, lg_ref):
    na = _rms(na_ref[...].astype(f32), gna_ref[...]).astype(bf16)
    y = jnp.dot(ssm_ref[...], w_ref[0:SSM_WIDTH, :], preferred_element_type=f32)
    y = y + jnp.dot(na, w_ref[SSM_WIDTH:, :], preferred_element_type=f32)
    x1 = x_ref[...] + y
    x1_ref[...] = x1
    hn = _rms(x1, gmoe_ref[...]).astype(bf16)
    lg_ref[...] = jnp.dot(hn, wr_ref[...], preferred_element_type=f32) + br_ref[...]
    bits = lax.bitcast_convert_type(hn.astype(f32), jnp.uint32)
    hn_ref[...] = (bits[:, HALF_D:] & jnp.uint32(0xFFFF0000)) | (bits[:, :HALF_D] >> 16)


def _out_proj(ssm_n, y_na, x, g_na_out, w_out_bf, g_moe, w_router_pad, b_router_pad):
    row = lambda i: (i, 0)
    fixed = lambda i: (0, 0)
    return pl.pallas_call(
        _out_proj_kernel,
        name="out_proj",
        grid=(SEQ // OUT_TM,),
        in_specs=[
            pl.BlockSpec((OUT_TM, SSM_WIDTH), row),
            pl.BlockSpec((OUT_TM, NA_WIDTH), row),
            pl.BlockSpec((OUT_TM, D_MODEL), row),
            pl.BlockSpec((1, NA_WIDTH), fixed),
            pl.BlockSpec((D_MODEL, D_MODEL), fixed),
            pl.BlockSpec((1, D_MODEL), fixed),
            pl.BlockSpec((D_MODEL, LANES), fixed),
            pl.BlockSpec((1, LANES), fixed),
        ],
        out_specs=[
            pl.BlockSpec((OUT_TM, D_MODEL), row),
            pl.BlockSpec((OUT_TM, HALF_D), row),
            pl.BlockSpec((OUT_TM, LANES), row),
        ],
        out_shape=[
            jax.ShapeDtypeStruct((SEQ, D_MODEL), f32),
            jax.ShapeDtypeStruct((SEQ, HALF_D), jnp.uint32),
            jax.ShapeDtypeStruct((SEQ, LANES), f32),
        ],
        compiler_params=_cparams(("arbitrary",)),
    )(ssm_n, y_na, x, g_na_out.reshape(1, NA_WIDTH), w_out_bf, g_moe.reshape(1, D_MODEL),
      w_router_pad, b_router_pad)


ROUTE_TM = 512


def _route_kernel(lg_ref, tri_ref, meta_ref, gate_ref, cnt_ref, carry_ref):
    i = pl.program_id(0)

    @pl.when(i == 0)
    def _():
        carry_ref[...] = jnp.zeros_like(carry_ref)

    lane = lax.broadcasted_iota(jnp.int32, (ROUTE_TM, LANES), 1)
    lane_f = lane.astype(f32)
    work = lg_ref[...]
    vals, hits = [], []
    for _ in range(TOP_K):
        m = jnp.max(work, axis=-1, keepdims=True)
        idx = jnp.min(jnp.where(work == m, lane_f, float(LANES)), axis=-1, keepdims=True)
        hit = lane_f == idx
        vals.append(m)
        hits.append((idx, hit))
        work = jnp.where(hit, -jnp.inf, work)

    exps = [jnp.exp(v - vals[0]) for v in vals]
    denom = exps[0] + exps[1] + exps[2] + exps[3]

    onehot = jnp.zeros((ROUTE_TM, LANES), f32)
    for _, hit in hits:
        onehot = onehot + hit.astype(f32)
    before = jnp.dot(tri_ref[...], onehot.astype(bf16), preferred_element_type=f32) + carry_ref[...]

    meta = jnp.zeros((ROUTE_TM, LANES), jnp.int32)
    gate = jnp.zeros((ROUTE_TM, LANES), f32)
    for k, (idx, hit) in enumerate(hits):
        rank = jnp.sum(jnp.where(hit, before, 0.0), axis=-1, keepdims=True).astype(jnp.int32)
        meta = jnp.where(lane == k, idx.astype(jnp.int32), meta)
        meta = jnp.where(lane == TOP_K + k, rank, meta)
        gate = jnp.where(lane == k, exps[k] / denom, gate)
    meta_ref[...] = meta
    gate_ref[...] = gate
    carry_ref[...] += jnp.sum(onehot, axis=0, keepdims=True)
    cnt_ref[...] = carry_ref[...]


def _route(logits):
    tri = (jnp.arange(ROUTE_TM)[:, None] > jnp.arange(ROUTE_TM)[None, :]).astype(bf16)
    row = lambda i: (i, 0)
    return pl.pallas_call(
        _route_kernel,
        name="route",
        grid=(SEQ // ROUTE_TM,),
        in_specs=[
            pl.BlockSpec((ROUTE_TM, LANES), row),
            pl.BlockSpec((ROUTE_TM, ROUTE_TM), lambda i: (0, 0)),
        ],
        out_specs=[
            pl.BlockSpec((ROUTE_TM, LANES), row),
            pl.BlockSpec((ROUTE_TM, LANES), row),
            pl.BlockSpec((1, LANES), lambda i: (0, 0)),
        ],
        out_shape=[
            jax.ShapeDtypeStruct((SEQ, LANES), jnp.int32),
            jax.ShapeDtypeStruct((SEQ, LANES), f32),
            jax.ShapeDtypeStruct((1, LANES), f32),
        ],
        scratch_shapes=[pltpu.VMEM((1, LANES), f32)],
        compiler_params=_cparams(("arbitrary",)),
    )(logits, tri)


def _routing_tables(meta, counts):
    idx = meta[:, :TOP_K]
    rank = meta[:, TOP_K:2 * TOP_K]
    cnt = counts[0, :N_EXPERTS].astype(jnp.int32)
    n_chunks = (cnt + ROW_CHUNK - 1) // ROW_CHUNK
    chunk_base = jnp.cumsum(n_chunks) - n_chunks
    dest = (chunk_base[idx] * ROW_CHUNK + rank).astype(jnp.int32).reshape(-1)
    total_chunks = jnp.sum(n_chunks)
    last_chunk = jnp.concatenate([jnp.where(cnt > 0, chunk_base + n_chunks - 1, -1),
                                  total_chunks[None]]).astype(jnp.int32)

    n_sb = (n_chunks + CHUNKS_PER_SB - 1) // CHUNKS_PER_SB
    sb_end = jnp.cumsum(n_sb)
    sb_start = sb_end - n_sb
    n_used = sb_end[-1]
    s = jnp.arange(MAX_SB)
    s_eff = jnp.minimum(s, n_used - 1)
    e = jnp.minimum(jnp.searchsorted(sb_end, s_eff, side='right'), N_EXPERTS - 1)
    kk = s_eff - sb_start[e]
    sb_chunk0 = chunk_base[e] + kk * CHUNKS_PER_SB
    sb_n = jnp.where(s < n_used, jnp.clip(n_chunks[e] - kk * CHUNKS_PER_SB, 0, CHUNKS_PER_SB), 0)
    used = jnp.stack([n_used, total_chunks]).astype(jnp.int32)
    return (dest, last_chunk, e.astype(jnp.int32), sb_chunk0.astype(jnp.int32),
            sb_n.astype(jnp.int32), used)


DISP_TM = 256


def _row_copy(src, src_row, dst, dst_row, sem):
    return pltpu.make_async_copy(src.at[pl.ds(src_row, 1), :], dst.at[pl.ds(dst_row, 1), :], sem)


def _dispatch_kernel(last_ref, dest_ref, hn_ref, xs_ref, zero_ref, sem_ref):
    i = pl.program_id(0)

    @pl.when(i == 0)
    def _():
        zero_ref[...] = jnp.zeros_like(zero_ref)

        def chunk_copy(c):
            row0 = pl.multiple_of(c * ROW_CHUNK, ROW_CHUNK)
            return pltpu.make_async_copy(zero_ref, xs_ref.at[pl.ds(row0, ROW_CHUNK), :], sem_ref.at[1])

        def start(e, _):
            @pl.when(last_ref[e] >= 0)
            def _():
                chunk_copy(last_ref[e]).start()
            return 0

        def wait(e, _):
            @pl.when(last_ref[e] >= 0)
            def _():
                chunk_copy(last_ref[e]).wait()
            return 0

        def start_tail(c, _):
            chunk_copy(c).start()
            return 0

        def wait_tail(c, _):
            chunk_copy(c).wait()
            return 0

        lax.fori_loop(0, N_EXPERTS, start, 0)
        lax.fori_loop(last_ref[N_EXPERTS], MAX_CHUNKS, start_tail, 0)
        lax.fori_loop(0, N_EXPERTS, wait, 0)
        lax.fori_loop(last_ref[N_EXPERTS], MAX_CHUNKS, wait_tail, 0)

    def issue(t, _):
        for k in range(TOP_K):
            _row_copy(hn_ref, t, xs_ref, dest_ref[t * TOP_K + k], sem_ref.at[0]).start()
        return 0

    lax.fori_loop(0, DISP_TM, issue, 0, unroll=8)
    for k in range(TOP_K):
        pltpu.make_async_copy(hn_ref, xs_ref.at[pl.ds(0, DISP_TM), :], sem_ref.at[0]).wait()


def _dispatch(last_chunk, dest, hn):
    return pl.pallas_call(
        _dispatch_kernel,
        name="dispatch",
        grid_spec=pltpu.PrefetchScalarGridSpec(
            num_scalar_prefetch=1,
            grid=(SEQ // DISP_TM,),
            in_specs=[
                pl.BlockSpec((DISP_TM * TOP_K,), lambda i, last: (i,), memory_space=pltpu.SMEM),
                pl.BlockSpec((DISP_TM, HALF_D), lambda i, last: (i, 0)),
            ],
            out_specs=pl.BlockSpec(memory_space=pl.ANY),
            scratch_shapes=[
                pltpu.VMEM((ROW_CHUNK, HALF_D), jnp.uint32),
                pltpu.SemaphoreType.DMA((2,)),
            ],
        ),
        out_shape=jax.ShapeDtypeStruct((MAX_ROWS, HALF_D), jnp.uint32),
        compiler_params=_cparams(("arbitrary",)),
    )(last_chunk, dest, hn)


def _experts_kernel(e_ref, c0_ref, n_ref, used_ref,
                    xs_ref, wg_ref, wu_ref, wd_ref, bg_ref, bu_ref, bd_ref, ys_ref,
                    xin_ref, xbf_ref, act_ref, yt_ref, pend_ref, sem_ref):
    s = pl.program_id(0)
    j = pl.program_id(1)
    n = n_ref[s]
    c0 = c0_ref[s]
    n_steps = N_FF_TILES + N_OUT_TILES

    def rows(c, k=1):
        return pl.ds(pl.multiple_of(c * ROW_CHUNK, ROW_CHUNK), k * ROW_CHUNK)

    def cover(body):
        n4 = n // 4

        def quad(i, _):
            body(i * 4, 4)
            return 0

        lax.fori_loop(0, n4, quad, 0)

        @pl.when((n & 2) != 0)
        def _():
            body(n4 * 4, 2)

        @pl.when((n & 1) != 0)
        def _():
            body(n4 * 4 + (n & 2), 1)

    def drain(slot):
        def wait_one(i, _):
            pltpu.make_async_copy(yt_ref.at[slot, 0:ROW_CHUNK, :], ys_ref.at[0:ROW_CHUNK, 0:FF_TILE],
                                  sem_ref.at[1 + slot]).wait()
            return 0

        lax.fori_loop(0, pend_ref[slot], wait_one, 0)
        pend_ref[slot] = 0

    @pl.when(jnp.logical_and(s == 0, j == 0))
    def _():
        pend_ref[0] = 0
        pend_ref[1] = 0

    def fetch(first_chunk, count):
        def start(c, _):
            pltpu.make_async_copy(xs_ref.at[rows(first_chunk + c), :], xin_ref.at[rows(c), :],
                                  sem_ref.at[0]).start()
            return 0

        lax.fori_loop(0, count, start, 0)

    @pl.when(jnp.logical_and(s == 0, j == 0))
    def _():
        fetch(c0, n)

    @pl.when(jnp.logical_and(n > 0, j == 0))
    def _():
        def finish(c, _):
            pltpu.make_async_copy(xs_ref.at[rows(c0 + c), :], xin_ref.at[rows(c), :], sem_ref.at[0]).wait()
            return 0

        def unpack(c, _):
            w = xin_ref[rows(c), :]
            low = lax.bitcast_convert_type(w << 16, f32)
            high = lax.bitcast_convert_type(w & jnp.uint32(0xFFFF0000), f32)
            xbf_ref[rows(c), 0:HALF_D] = low.astype(bf16)
            xbf_ref[rows(c), HALF_D:D_MODEL] = high.astype(bf16)
            return 0

        lax.fori_loop(0, n, finish, 0)
        lax.fori_loop(0, n, unpack, 0)

    @pl.when(jnp.logical_and(s + 1 < MAX_SB, j == N_FF_TILES))
    def _():
        nxt = jnp.minimum(s + 1, MAX_SB - 1)
        fetch(c0_ref[nxt], n_ref[nxt])

    @pl.when(jnp.logical_and(n > 0, j < N_FF_TILES))
    def _():
        bg = bg_ref[0]
        bu = bu_ref[0]

        def body(c, k):
            x = xbf_ref[rows(c, k), :]
            g = jnp.dot(x, wg_ref[0].astype(bf16), preferred_element_type=f32) + bg
            u = jnp.dot(x, wu_ref[0].astype(bf16), preferred_element_type=f32) + bu
            g = jnp.minimum(g, SWIGLU_LIMIT)
            u = jnp.clip(u, -SWIGLU_LIMIT, SWIGLU_LIMIT)
            a = (u + 1.0) * (g * (1.0 / (1.0 + jnp.exp(-SWIGLU_ALPHA * g))))
            act_ref[j, rows(c, k), :] = a.astype(bf16)

        cover(body)

    @pl.when(jnp.logical_and(n > 0, j >= N_FF_TILES))
    def _():
        d = j - N_FF_TILES
        slot = d % 2
        bd = bd_ref[0]
        cols = pl.ds(pl.multiple_of(d * FF_TILE, FF_TILE), FF_TILE)
        drain(slot)

        def body(c, k):
            a = jnp.concatenate([act_ref[f, rows(c, k), :] for f in range(N_FF_TILES)], axis=-1)
            yt_ref[slot, rows(c, k), :] = jnp.dot(a, wd_ref[0].astype(bf16), preferred_element_type=f32) + bd
            for i in range(k):
                pltpu.make_async_copy(yt_ref.at[slot, rows(c + i), :], ys_ref.at[rows(c0 + c + i), cols],
                                      sem_ref.at[1 + slot]).start()

        cover(body)
        pend_ref[slot] = n

    @pl.when(jnp.logical_and(s == MAX_SB - 1, j == n_steps - 1))
    def _():
        drain(0)
        drain(1)
        yt_ref[0, 0:ROW_CHUNK, :] = jnp.zeros((ROW_CHUNK, FF_TILE), f32)

        def tail_copy(c, d):
            return pltpu.make_async_copy(yt_ref.at[0, 0:ROW_CHUNK, :],
                                         ys_ref.at[rows(c), d * FF_TILE:(d + 1) * FF_TILE], sem_ref.at[0])

        def start(c, _):
            for d in range(N_OUT_TILES):
                tail_copy(c, d).start()
            return 0

        def finish(c, _):
            for d in range(N_OUT_TILES):
                tail_copy(c, d).wait()
            return 0

        lax.fori_loop(used_ref[1], MAX_CHUNKS, start, 0)
        lax.fori_loop(used_ref[1], MAX_CHUNKS, finish, 0)


def _experts(sb_e, sb_c0, sb_n, n_used, xs, w_gate, b_gate, w_up, b_up, w_down, b_down):
    last_up = N_FF_TILES - 1

    def up_map(s, j, e, c0, n, used):
        live = s < used[0]
        return (e[s], 0, jnp.where(live, jnp.minimum(j, last_up), last_up))

    def down_map(s, j, e, c0, n, used):
        live = s < used[0]
        return (e[s], 0, jnp.where(live, jnp.maximum(j - N_FF_TILES, 0), N_OUT_TILES - 1))

    return pl.pallas_call(
        _experts_kernel,
        name="experts",
        grid_spec=pltpu.PrefetchScalarGridSpec(
            num_scalar_prefetch=4,
            grid=(MAX_SB, N_FF_TILES + N_OUT_TILES),
            in_specs=[
                pl.BlockSpec(memory_space=pl.ANY),
                pl.BlockSpec((1, D_MODEL, FF_TILE), up_map),
                pl.BlockSpec((1, D_MODEL, FF_TILE), up_map),
                pl.BlockSpec((1, D_FF, FF_TILE), down_map),
                pl.BlockSpec((1, 1, FF_TILE), up_map),
                pl.BlockSpec((1, 1, FF_TILE), up_map),
                pl.BlockSpec((1, 1, FF_TILE), down_map),
            ],
            out_specs=pl.BlockSpec(memory_space=pl.ANY),
            scratch_shapes=[
                pltpu.VMEM((SB_ROWS, HALF_D), jnp.uint32),
                pltpu.VMEM((SB_ROWS, D_MODEL), bf16),
                pltpu.VMEM((N_FF_TILES, SB_ROWS, FF_TILE), bf16),
                pltpu.VMEM((2, SB_ROWS, FF_TILE), f32),
                pltpu.SMEM((2,), jnp.int32),
                pltpu.SemaphoreType.DMA((3,)),
            ],
        ),
        out_shape=jax.ShapeDtypeStruct((MAX_ROWS, D_MODEL), f32),
        compiler_params=_cparams(("arbitrary", "arbitrary")),
    )(sb_e, sb_c0, sb_n, n_used, xs, w_gate, w_up, w_down,
      b_gate.reshape(N_EXPERTS, 1, D_FF), b_up.reshape(N_EXPERTS, 1, D_FF),
      b_down.reshape(N_EXPERTS, 1, D_MODEL))


COMB_TM = 256


def _combine_kernel(dest_ref, ys_ref, x1_ref, gate_ref, gf_ref, o_ref, buf_ref, sem_ref):
    def issue(t, _):
        for k in range(TOP_K):
            pltpu.make_async_copy(ys_ref.at[pl.ds(dest_ref[t * TOP_K + k], 1), :],
                                  buf_ref.at[k, pl.ds(t, 1), :], sem_ref.at[0]).start()
        return 0

    lax.fori_loop(0, COMB_TM, issue, 0, unroll=8)
    for k in range(TOP_K):
        pltpu.make_async_copy(ys_ref.at[pl.ds(0, COMB_TM), :], buf_ref.at[k], sem_ref.at[0]).wait()

    gate = gate_ref[...]
    acc = x1_ref[...]
    for k in range(TOP_K):
        acc = acc + gate[:, k:k + 1] * buf_ref[k]
    o_ref[...] = _rms(acc, gf_ref[...])


def _combine(dest, ys, x1, gates, g_final):
    return pl.pallas_call(
        _combine_kernel,
        name="combine",
        grid=(SEQ // COMB_TM,),
        in_specs=[
            pl.BlockSpec((COMB_TM * TOP_K,), lambda i: (i,), memory_space=pltpu.SMEM),
            pl.BlockSpec(memory_space=pl.ANY),
            pl.BlockSpec((COMB_TM, D_MODEL), lambda i: (i, 0)),
            pl.BlockSpec((COMB_TM, LANES), lambda i: (i, 0)),
            pl.BlockSpec((1, D_MODEL), lambda i: (0, 0)),
        ],
        out_specs=pl.BlockSpec((COMB_TM, D_MODEL), lambda i: (i, 0)),
        out_shape=jax.ShapeDtypeStruct((SEQ, D_MODEL), f32),
        scratch_shapes=[
            pltpu.VMEM((TOP_K, COMB_TM, D_MODEL), f32),
            pltpu.SemaphoreType.DMA((1,)),
        ],
        compiler_params=_cparams(("arbitrary",)),
    )(dest, ys, x1, gates, g_final.reshape(1, D_MODEL))


def kernel(x, g_mix, w_in, lam_re_fwd, lam_im_fwd, log_dt_fwd, b_re_fwd, b_im_fwd, c_re_fwd, c_im_fwd, lam_re_bwd, lam_im_bwd, log_dt_bwd, b_re_bwd, b_im_bwd, c_re_bwd, c_im_bwd, ssm_d, w_glu, b_glu, na_rpb, g_ssm_out, g_na_out, w_out, g_moe, w_router, b_router, w_gate, b_gate, w_up, b_up, w_down, b_down, g_final):
    x2 = x.reshape(SEQ, D_MODEL)

    u, qkv = _in_proj(x2, g_mix[0], w_in[0].astype(bf16))

    s5_w, s5_dec = _s5_weights(
        (lam_re_fwd[0], lam_im_fwd[0], log_dt_fwd[0], b_re_fwd[0], b_im_fwd[0], c_re_fwd[0], c_im_fwd[0]),
        (lam_re_bwd[0], lam_im_bwd[0], log_dt_bwd[0], b_re_bwd[0], b_im_bwd[0], c_re_bwd[0], c_im_bwd[0]))
    y = _s5(u, s5_w, s5_dec, ssm_d[0])
    ssm_n = _glu(y, w_glu[0].astype(bf16), b_glu[0], g_ssm_out[0])

    y_na = _natten(qkv, _na_bias(na_rpb[0]))

    w_router_pad = jnp.zeros((D_MODEL, LANES), bf16).at[:, :N_EXPERTS].set(w_router[0].astype(bf16))
    b_router_pad = jnp.full((1, LANES), NEG_BIG, f32).at[0, :N_EXPERTS].set(b_router[0].astype(f32))
    x1, hn, logits = _out_proj(ssm_n, y_na, x2, g_na_out[0], w_out[0].astype(bf16), g_moe[0],
                               w_router_pad, b_router_pad)

    meta, gates, counts = _route(logits)
    dest, last_chunk, sb_e, sb_c0, sb_n, n_used = _routing_tables(meta, counts)

    xs = _dispatch(last_chunk, dest, hn)
    ys = _experts(sb_e, sb_c0, sb_n, n_used, xs, w_gate[0], b_gate[0], w_up[0], b_up[0],
                  w_down[0], b_down[0])
    out = _combine(dest, ys, x1, gates, g_final)
    return out.reshape(x.shape)
```

```python
import functools
import math

import jax
import jax.numpy as jnp
from jax import lax
from jax.experimental import pallas as pl
from jax.experimental.pallas import tpu as pltpu

f32 = jnp.float32
bf16 = jnp.bfloat16

D_MODEL = 2048
SEQ = 8192
SSM_WIDTH = 1024
NA_WIDTH = 1024
SSM_GROUP = 16
SSM_GROUPS = 64
SSM_STATE = 64
NA_HEAD_DIM = 64
NA_HEADS = 16
GRID_W = 64
GRID_ROWS = SEQ // GRID_W
NA_KH = 8
NA_KW = 16
N_EXPERTS = 32
TOP_K = 4
D_FF = 2048
SWIGLU_LIMIT = 7.0
SWIGLU_ALPHA = 1.702
RMS_EPS = 1e-5

LANES = 128
HALF_D = D_MODEL // 2
NEG_BIG = -1e30

CHUNK_T = 16
N_CHUNKS = SEQ // CHUNK_T
GROUPS_PER_BLOCK = LANES // SSM_GROUP
N_LANE_BLOCKS = SSM_WIDTH // LANES
CAT_W = CHUNK_T * LANES
STATE_W = GROUPS_PER_BLOCK * SSM_STATE

ROW_CHUNK = 128
CHUNKS_PER_SB = 10
SB_ROWS = ROW_CHUNK * CHUNKS_PER_SB
MAX_CHUNKS = SEQ * TOP_K // ROW_CHUNK + N_EXPERTS
MAX_ROWS = MAX_CHUNKS * ROW_CHUNK
MAX_SB = MAX_CHUNKS // CHUNKS_PER_SB + N_EXPERTS
FF_TILE = 512
N_FF_TILES = D_FF // FF_TILE

VMEM_LIMIT = 56 * 1024 * 1024
EXPERTS_VMEM_LIMIT = 60 * 1024 * 1024


def _cparams(semantics, vmem=VMEM_LIMIT):
    return pltpu.CompilerParams(dimension_semantics=semantics, vmem_limit_bytes=vmem)


def _rms(x, g):
    return x * lax.rsqrt(jnp.mean(x * x, axis=-1, keepdims=True) + RMS_EPS) * g


IN_TM = 1024
IN_TN = 1024


def _in_proj_kernel(x_ref, g_ref, w_ref, u_ref, qkv_ref, h_ref):
    j = pl.program_id(1)

    @pl.when(j == 0)
    def _():
        h_ref[...] = _rms(x_ref[...], g_ref[...]).astype(bf16)

    acc = jnp.dot(h_ref[...], w_ref[...], preferred_element_type=f32)

    @pl.when(j == 0)
    def _():
        u_ref[...] = acc

    @pl.when(j > 0)
    def _():
        qkv_ref[...] = acc.astype(bf16)


def _in_proj(x, g_mix, w_in_bf):
    n_out = w_in_bf.shape[1]
    return pl.pallas_call(
        _in_proj_kernel,
        name="in_proj",
        grid=(SEQ // IN_TM, n_out // IN_TN),
        in_specs=[
            pl.BlockSpec((IN_TM, D_MODEL), lambda i, j: (i, 0)),
            pl.BlockSpec((1, D_MODEL), lambda i, j: (0, 0)),
            pl.BlockSpec((D_MODEL, IN_TN), lambda i, j: (0, j)),
        ],
        out_specs=[
            pl.BlockSpec((IN_TM, IN_TN), lambda i, j: (i, 0)),
            pl.BlockSpec((IN_TM, IN_TN), lambda i, j: (i, jnp.maximum(j - 1, 0))),
        ],
        out_shape=[
            jax.ShapeDtypeStruct((SEQ, SSM_WIDTH), f32),
            jax.ShapeDtypeStruct((SEQ, 3 * NA_WIDTH), bf16),
        ],
        scratch_shapes=[pltpu.VMEM((IN_TM, D_MODEL), bf16)],
        compiler_params=_cparams(("arbitrary", "arbitrary")),
    )(x, g_mix.reshape(1, D_MODEL), w_in_bf)


def _cmul(ar, ai, br, bi):
    return ar * br - ai * bi, ar * bi + ai * br


def _s5_discretise(lam_re, lam_im, log_dt, b_re, b_im, c_re, c_im):
    a = jnp.minimum(lam_re.astype(f32), -1e-4)
    w = lam_im.astype(f32)
    dt = jnp.exp(log_dt.astype(f32))[:, None]
    steps = jnp.arange(CHUNK_T + 1, dtype=f32)[:, None, None]
    mag = jnp.exp((a * dt)[None] * steps)
    ang = (w * dt)[None] * steps
    pw = (mag * jnp.cos(ang), mag * jnp.sin(ang))
    xr, xi = pw[0][1] - 1.0, pw[1][1]
    den = a * a + w * w
    qr, qi = (xr * a + xi * w) / den, (xi * a - xr * w) / den
    bb = _cmul(qr[..., None], qi[..., None], b_re.astype(f32), b_im.astype(f32))
    return pw, bb, (c_re.astype(f32), c_im.astype(f32))


def _pair_blockdiag(m):
    z = jnp.zeros_like(m[..., 0, :, :])
    top = jnp.concatenate([m[..., 0, :, :], z], axis=-1)
    bot = jnp.concatenate([z, m[..., 1, :, :]], axis=-1)
    return jnp.concatenate([top, bot], axis=-2)


def _s5_weights(fwd, bwd):
    hi = lax.Precision.HIGH
    exact = lax.Precision.HIGHEST
    t = CHUNK_T
    pw_f, bb_f, c_f = _s5_discretise(*fwd)
    pw_b, bb_b, c_b = _s5_discretise(*bwd)
    pairs = (N_LANE_BLOCKS, GROUPS_PER_BLOCK // 2, 2)

    x = jnp.arange(t * SSM_GROUP)
    t_of_x = x // SSM_GROUP
    tile_c = (jnp.arange(SSM_GROUP)[:, None] == x[None, :] % SSM_GROUP).astype(f32)

    def c_over_x(c):
        return [jnp.einsum('gcp,cx->gpx', part, tile_c, precision=exact) for part in c]

    def c_times_power(c_x, pw, power_of_x):
        rep = (jnp.arange(t + 1)[:, None] == power_of_x[None, :]).astype(f32)
        pw_x = [jnp.einsum('jgp,jx->gpx', part, rep, precision=exact) for part in pw]
        return _cmul(c_x[0], c_x[1], pw_x[0], pw_x[1])

    cx_f, cx_b = c_over_x(c_f), c_over_x(c_b)

    def lag_kernel(c_x, pw, bb, power_of_x):
        m_re, m_im = c_times_power(c_x, pw, power_of_x)
        m = jnp.concatenate([m_re, m_im], axis=1)
        b = jnp.concatenate([jnp.swapaxes(bb[0], 1, 2), -jnp.swapaxes(bb[1], 1, 2)], axis=-1)
        return jnp.einsum('gdp,gpx->gdx', b, m, precision=hi)

    kf = lag_kernel(cx_f, pw_f, bb_f, t_of_x)
    kb = lag_kernel(cx_b, pw_b, bb_b, t - 1 - t_of_x)
    keep = (t - 1) * SSM_GROUP
    k_lag = jnp.concatenate([kb[..., :keep], kb[..., keep:] + kf[..., :SSM_GROUP], kf[..., SSM_GROUP:]],
                            axis=-1)
    w_intra = jnp.stack([k_lag[..., (t - 1 - ti) * SSM_GROUP:(2 * t - 1 - ti) * SSM_GROUP]
                         for ti in range(t)], axis=1)
    w_intra = _pair_blockdiag(w_intra.astype(bf16).reshape(*pairs, t * SSM_GROUP, t * SSM_GROUP))

    def state_in(pw_t, bb):
        pw_g = [jnp.transpose(part, (1, 0, 2))[:, :, None, :] for part in pw_t]
        bb_g = [jnp.swapaxes(part, 1, 2)[:, None] for part in bb]
        return [_pair_blockdiag(part.astype(bf16).reshape(*pairs, t * SSM_GROUP, SSM_STATE))
                for part in _cmul(pw_g[0], pw_g[1], bb_g[0], bb_g[1])]

    w_in = jnp.concatenate(state_in([part[:t][::-1] for part in pw_f], bb_f)
                           + state_in([part[:t] for part in pw_b], bb_b), axis=-1)

    def state_out(c_x, pw, power_of_x):
        m_re, m_im = c_times_power(c_x, pw, power_of_x)
        return [_pair_blockdiag(part.astype(bf16).reshape(*pairs, SSM_STATE, t * SSM_GROUP))
                for part in (m_re, -m_im)]

    w_out = jnp.concatenate(state_out(cx_f, pw_f, t_of_x + 1) + state_out(cx_b, pw_b, t - t_of_x), axis=-2)

    def decay(a):
        return a.reshape(N_LANE_BLOCKS, 1, STATE_W)

    dec = jnp.concatenate([decay(pw_f[0][t]), decay(pw_f[1][t]), decay(pw_b[0][t]), decay(pw_b[1][t])],
                          axis=-1)
    return (w_in, w_intra, w_out), dec.astype(f32)


def _gelu_tanh(x):
    return 0.5 * x * (1.0 + jnp.tanh(math.sqrt(2.0 / math.pi) * (x + 0.044715 * (x * x * x))))


PAIRS_PER_BLOCK = GROUPS_PER_BLOCK // 2
PAIR_W = 2 * CHUNK_T * SSM_GROUP


def _s5_kernel(u_ref, win_ref, wintra_ref, wout_ref, dec_ref, d_ref, y_ref,
               cat_ref, catp_ref, st_ref, perm_ref):
    sw = STATE_W

    @pl.when(pl.program_id(0) == 0)
    def _():
        def strip(t, _):
            row = lax.broadcasted_iota(jnp.int32, (LANES, CAT_W), 0)
            col = lax.broadcasted_iota(jnp.int32, (LANES, CAT_W), 1)
            target = (row // SSM_GROUP) * (CHUNK_T * SSM_GROUP) + t * SSM_GROUP + row % SSM_GROUP
            perm_ref[pl.ds(pl.multiple_of(t * LANES, LANES), LANES), :] = (
                jnp.where(col == target, 1.0, 0.0).astype(bf16))
            return 0

        lax.fori_loop(0, CHUNK_T, strip, 0)

    for t in range(CHUNK_T):
        cat_ref[:, t * LANES:(t + 1) * LANES] = u_ref[pl.ds(t, N_CHUNKS, stride=CHUNK_T), :].astype(bf16)
    catp_ref[...] = jnp.dot(cat_ref[...], perm_ref[...], preferred_element_type=f32).astype(bf16)

    for p in range(PAIRS_PER_BLOCK):
        z = jnp.dot(catp_ref[:, p * PAIR_W:(p + 1) * PAIR_W], win_ref[0, p], preferred_element_type=f32)
        for part in range(4):
            st_ref[:, part * sw + p * LANES:part * sw + (p + 1) * LANES] = z[:, part * LANES:(part + 1) * LANES]

    afr = dec_ref[0, :, 0 * sw:1 * sw]
    afi = dec_ref[0, :, 1 * sw:2 * sw]
    abr = dec_ref[0, :, 2 * sw:3 * sw]
    abi = dec_ref[0, :, 3 * sw:4 * sw]

    def step(i, carry):
        sfr, sfi, sbr, sbi = carry
        r = N_CHUNKS - 1 - i
        zf = st_ref[pl.ds(i, 1), 0:2 * sw]
        zb = st_ref[pl.ds(r, 1), 2 * sw:4 * sw]
        st_ref[pl.ds(i, 1), 0:2 * sw] = jnp.concatenate([sfr, sfi], axis=-1)
        st_ref[pl.ds(r, 1), 2 * sw:4 * sw] = jnp.concatenate([sbr, sbi], axis=-1)
        nfr = afr * sfr - afi * sfi + zf[:, :sw]
        nfi = afi * sfr + afr * sfi + zf[:, sw:]
        nbr = abr * sbr - abi * sbi + zb[:, :sw]
        nbi = abi * sbr + abr * sbi + zb[:, sw:]
        return nfr, nfi, nbr, nbi

    zero = jnp.zeros((1, sw), f32)
    lax.fori_loop(0, N_CHUNKS, step, (zero, zero, zero, zero))

    for p in range(PAIRS_PER_BLOCK):
        s_in = jnp.concatenate(
            [st_ref[:, part * sw + p * LANES:part * sw + (p + 1) * LANES] for part in range(4)], axis=-1)
        yp = jnp.dot(catp_ref[:, p * PAIR_W:(p + 1) * PAIR_W], wintra_ref[0, p], preferred_element_type=f32)
        yp = yp + jnp.dot(s_in.astype(bf16), wout_ref[0, p], preferred_element_type=f32)
        cat_ref[:, p * PAIR_W:(p + 1) * PAIR_W] = yp.astype(bf16)

    st_ref[...] = lax.dot_general(cat_ref[...], perm_ref[...], (((1,), (1,)), ((), ())),
                                  preferred_element_type=f32)
    d = d_ref[...]
    for t in range(CHUNK_T):
        rows = pl.ds(t, N_CHUNKS, stride=CHUNK_T)
        v = st_ref[:, t * LANES:(t + 1) * LANES] + d * u_ref[rows, :]
        y_ref[rows, :] = _gelu_tanh(v)


def _s5(u, weights, dec, ssm_d):
    w_spec = pl.BlockSpec((1, PAIRS_PER_BLOCK, PAIR_W, PAIR_W), lambda j: (j, 0, 0, 0))
    return pl.pallas_call(
        _s5_kernel,
        name="s5_scan",
        grid=(N_LANE_BLOCKS,),
        in_specs=[
            pl.BlockSpec((SEQ, LANES), lambda j: (0, j)),
            w_spec, w_spec, w_spec,
            pl.BlockSpec((1, 1, 4 * STATE_W), lambda j: (j, 0, 0)),
            pl.BlockSpec((1, LANES), lambda j: (0, j)),
        ],
        out_specs=pl.BlockSpec((SEQ, LANES), lambda j: (0, j)),
        out_shape=jax.ShapeDtypeStruct((SEQ, SSM_WIDTH), f32),
        scratch_shapes=[
            pltpu.VMEM((N_CHUNKS, CAT_W), bf16),
            pltpu.VMEM((N_CHUNKS, CAT_W), bf16),
            pltpu.VMEM((N_CHUNKS, 4 * STATE_W), f32),
            pltpu.VMEM((CAT_W, CAT_W), bf16),
        ],
        compiler_params=_cparams(("arbitrary",)),
    )(u, *weights, dec, ssm_d.reshape(1, SSM_WIDTH))


GLU_TM = 512


def _glu_kernel(y_ref, w_ref, b_ref, g_ref, o_ref):
    y = y_ref[...]
    z = jnp.dot(y.astype(bf16), w_ref[...], preferred_element_type=f32) + b_ref[...]
    o = y * (1.0 / (1.0 + jnp.exp(-z)))
    o_ref[...] = _rms(o, g_ref[...]).astype(bf16)


def _glu(y, w_glu_bf, b_glu, g_ssm_out):
    return pl.pallas_call(
        _glu_kernel,
        name="glu_norm",
        grid=(SEQ // GLU_TM,),
        in_specs=[
            pl.BlockSpec((GLU_TM, SSM_WIDTH), lambda i: (i, 0)),
            pl.BlockSpec((SSM_WIDTH, SSM_WIDTH), lambda i: (0, 0)),
            pl.BlockSpec((1, SSM_WIDTH), lambda i: (0, 0)),
            pl.BlockSpec((1, SSM_WIDTH), lambda i: (0, 0)),
        ],
        out_specs=pl.BlockSpec((GLU_TM, SSM_WIDTH), lambda i: (i, 0)),
        out_shape=jax.ShapeDtypeStruct((SEQ, SSM_WIDTH), bf16),
        compiler_params=_cparams(("arbitrary",)),
    )(y, w_glu_bf, b_glu.reshape(1, SSM_WIDTH), g_ssm_out.reshape(1, SSM_WIDTH))


NA_ROWS_PER_STEP = 8
NA_WIN = NA_KH * GRID_W
HEADS_PER_BLOCK = LANES // NA_HEAD_DIM


def _na_bias(rpb):
    c = jnp.arange(GRID_W)
    col_start = jnp.clip(c - NA_KW // 2, 0, GRID_W - NA_KW)
    valid = (c[None, :] >= col_start[:, None]) & (c[None, :] < col_start[:, None] + NA_KW)
    dc = jnp.clip(c[None, :] - c[:, None], -(NA_KW - 1), NA_KW - 1) + (NA_KW - 1)
    sel = (dc[None] == jnp.arange(2 * NA_KW - 1)[:, None, None]).astype(f32)
    tab = jnp.einsum('hrc,cqk->hqrk', rpb.astype(f32), sel, precision=lax.Precision.HIGHEST)
    tab = jnp.where(valid[None, :, None, :], tab, NEG_BIG)
    tab = tab.reshape(NA_HEADS, GRID_W, (2 * NA_KH - 1) * GRID_W)
    return jnp.stack([tab[..., (NA_KH - 1 - v) * GRID_W:(2 * NA_KH - 1 - v) * GRID_W]
                      for v in range(NA_KH)], axis=1)


def _natten_kernel(q_ref, k_ref, v_ref, b_ref, o_ref, s_ref, p_ref):
    rb = pl.program_id(1)
    lane = lax.broadcasted_iota(jnp.int32, (GRID_W, LANES), 1)
    head0 = lane < NA_HEAD_DIM
    scale = NA_HEAD_DIM ** -0.5

    starts, variants = [], []
    for i in range(NA_ROWS_PER_STEP):
        r = rb * NA_ROWS_PER_STEP + i
        rs = jnp.clip(r - NA_KH // 2, 0, GRID_ROWS - NA_KH)
        starts.append(pl.multiple_of(rs * GRID_W, GRID_W))
        variants.append(r - rs)

    for i in range(NA_ROWS_PER_STEP):
        q = q_ref[i * GRID_W:(i + 1) * GRID_W, :] * scale
        kw = k_ref[pl.ds(starts[i], NA_WIN), :]
        for h in range(HEADS_PER_BLOCK):
            qh = jnp.where(head0 if h == 0 else ~head0, q, jnp.zeros_like(q))
            s = lax.dot_general(qh, kw, (((1,), (1,)), ((), ())), preferred_element_type=f32)
            s_ref[i * HEADS_PER_BLOCK + h] = s + b_ref[h, variants[i]]

    inv_sums = []
    for t in range(NA_ROWS_PER_STEP * HEADS_PER_BLOCK):
        s = s_ref[t]
        p = jnp.exp(s - jnp.max(s, axis=-1, keepdims=True))
        inv_sums.append(1.0 / jnp.sum(p, axis=-1, keepdims=True))
        p_ref[t] = p.astype(bf16)

    for i in range(NA_ROWS_PER_STEP):
        vw = v_ref[pl.ds(starts[i], NA_WIN), :]
        outs = []
        for h in range(HEADS_PER_BLOCK):
            t = i * HEADS_PER_BLOCK + h
            outs.append(jnp.dot(p_ref[t], vw, preferred_element_type=f32) * inv_sums[t])
        o_ref[i * GRID_W:(i + 1) * GRID_W, :] = jnp.where(head0, outs[0], outs[1]).astype(bf16)


def _natten(qkv, bias):
    tm = NA_ROWS_PER_STEP * GRID_W
    n_hb = NA_WIDTH // LANES
    return pl.pallas_call(
        _natten_kernel,
        name="natten",
        grid=(n_hb, GRID_ROWS // NA_ROWS_PER_STEP),
        in_specs=[
            pl.BlockSpec((tm, LANES), lambda h, r: (r, h)),
            pl.BlockSpec((SEQ, LANES), lambda h, r: (0, n_hb + h)),
            pl.BlockSpec((SEQ, LANES), lambda h, r: (0, 2 * n_hb + h)),
            pl.BlockSpec((HEADS_PER_BLOCK, NA_KH, GRID_W, NA_WIN), lambda h, r: (h, 0, 0, 0)),
        ],
        out_specs=pl.BlockSpec((tm, LANES), lambda h, r: (r, h)),
        out_shape=jax.ShapeDtypeStruct((SEQ, NA_WIDTH), bf16),
        scratch_shapes=[
            pltpu.VMEM((NA_ROWS_PER_STEP * HEADS_PER_BLOCK, GRID_W, NA_WIN), f32),
            pltpu.VMEM((NA_ROWS_PER_STEP * HEADS_PER_BLOCK, GRID_W, NA_WIN), bf16),
        ],
        compiler_params=_cparams(("arbitrary", "arbitrary")),
    )(qkv, qkv, qkv, bias)


OUT_TM = 512


def _out_proj_kernel(ssm_ref, na_ref, x_ref, gna_ref, w_ref, gmoe_ref, wr_ref, br_ref,
                     x1_ref, hn_ref, lg_ref):
    na = _rms(na_ref[...].astype(f32), gna_ref[...]).astype(bf16)
    y = jnp.dot(ssm_ref[...], w_ref[0:SSM_WIDTH, :], preferred_element_type=f32)
    y = y + jnp.dot(na, w_ref[SSM_WIDTH:, :], preferred_element_type=f32)
    x1 = x_ref[...] + y
    x1_ref[...] = x1
    hn = _rms(x1, gmoe_ref[...]).astype(bf16)
    lg_ref[...] = jnp.dot(hn, wr_ref[...], preferred_element_type=f32) + br_ref[...]
    bits = lax.bitcast_convert_type(hn.astype(f32), jnp.uint32)
    hn_ref[...] = (bits[:, HALF_D:] & jnp.uint32(0xFFFF0000)) | (bits[:, :HALF_D] >> 16)


def _out_proj(ssm_n, y_na, x, g_na_out, w_out_bf, g_moe, w_router_pad, b_router_pad):
    row = lambda i: (i, 0)
    fixed = lambda i: (0, 0)
    return pl.pallas_call(
        _out_proj_kernel,
        name="out_proj",
        grid=(SEQ // OUT_TM,),
        in_specs=[
            pl.BlockSpec((OUT_TM, SSM_WIDTH), row),
            pl.BlockSpec((OUT_TM, NA_WIDTH), row),
            pl.BlockSpec((OUT_TM, D_MODEL), row),
            pl.BlockSpec((1, NA_WIDTH), fixed),
            pl.BlockSpec((D_MODEL, D_MODEL), fixed),
            pl.BlockSpec((1, D_MODEL), fixed),
            pl.BlockSpec((D_MODEL, LANES), fixed),
            pl.BlockSpec((1, LANES), fixed),
        ],
        out_specs=[
            pl.BlockSpec((OUT_TM, D_MODEL), row),
            pl.BlockSpec((OUT_TM, HALF_D), row),
            pl.BlockSpec((OUT_TM, LANES), row),
        ],
        out_shape=[
            jax.ShapeDtypeStruct((SEQ, D_MODEL), f32),
            jax.ShapeDtypeStruct((SEQ, HALF_D), jnp.uint32),
            jax.ShapeDtypeStruct((SEQ, LANES), f32),
        ],
        compiler_params=_cparams(("arbitrary",)),
    )(ssm_n, y_na, x, g_na_out.reshape(1, NA_WIDTH), w_out_bf, g_moe.reshape(1, D_MODEL),
      w_router_pad, b_router_pad)


ROUTE_TM = 512


def _route_kernel(lg_ref, tri_ref, meta_ref, gate_ref, cnt_ref, carry_ref):
    i = pl.program_id(0)

    @pl.when(i == 0)
    def _():
        carry_ref[...] = jnp.zeros_like(carry_ref)

    lane = lax.broadcasted_iota(jnp.int32, (ROUTE_TM, LANES), 1)
    lane_f = lane.astype(f32)
    work = lg_ref[...]
    vals, hits = [], []
    for _ in range(TOP_K):
        m = jnp.max(work, axis=-1, keepdims=True)
        idx = jnp.min(jnp.where(work == m, lane_f, float(LANES)), axis=-1, keepdims=True)
        hit = lane_f == idx
        vals.append(m)
        hits.append((idx, hit))
        work = jnp.where(hit, -jnp.inf, work)

    exps = [jnp.exp(v - vals[0]) for v in vals]
    denom = exps[0] + exps[1] + exps[2] + exps[3]

    onehot = jnp.zeros((ROUTE_TM, LANES), f32)
    for _, hit in hits:
        onehot = onehot + hit.astype(f32)
    before = jnp.dot(tri_ref[...], onehot.astype(bf16), preferred_element_type=f32) + carry_ref[...]

    meta = jnp.zeros((ROUTE_TM, LANES), jnp.int32)
    gate = jnp.zeros((ROUTE_TM, LANES), f32)
    for k, (idx, hit) in enumerate(hits):
        rank = jnp.sum(jnp.where(hit, before, 0.0), axis=-1, keepdims=True).astype(jnp.int32)
        meta = jnp.where(lane == k, idx.astype(jnp.int32), meta)
        meta = jnp.where(lane == TOP_K + k, rank, meta)
        gate = jnp.where(lane == k, exps[k] / denom, gate)
    meta_ref[...] = meta
    gate_ref[...] = gate
    carry_ref[...] += jnp.sum(onehot, axis=0, keepdims=True)
    cnt_ref[...] = carry_ref[...]


def _route(logits):
    tri = (jnp.arange(ROUTE_TM)[:, None] > jnp.arange(ROUTE_TM)[None, :]).astype(bf16)
    row = lambda i: (i, 0)
    return pl.pallas_call(
        _route_kernel,
        name="route",
        grid=(SEQ // ROUTE_TM,),
        in_specs=[
            pl.BlockSpec((ROUTE_TM, LANES), row),
            pl.BlockSpec((ROUTE_TM, ROUTE_TM), lambda i: (0, 0)),
        ],
        out_specs=[
            pl.BlockSpec((ROUTE_TM, LANES), row),
            pl.BlockSpec((ROUTE_TM, LANES), row),
            pl.BlockSpec((1, LANES), lambda i: (0, 0)),
        ],
        out_shape=[
            jax.ShapeDtypeStruct((SEQ, LANES), jnp.int32),
            jax.ShapeDtypeStruct((SEQ, LANES), f32),
            jax.ShapeDtypeStruct((1, LANES), f32),
        ],
        scratch_shapes=[pltpu.VMEM((1, LANES), f32)],
        compiler_params=_cparams(("arbitrary",)),
    )(logits, tri)


def _routing_tables(meta, counts):
    idx = meta[:, :TOP_K]
    rank = meta[:, TOP_K:2 * TOP_K]
    cnt = counts[0, :N_EXPERTS].astype(jnp.int32)
    n_chunks = (cnt + ROW_CHUNK - 1) // ROW_CHUNK
    chunk_base = jnp.cumsum(n_chunks) - n_chunks
    dest = (chunk_base[idx] * ROW_CHUNK + rank).astype(jnp.int32).reshape(-1)
    total_chunks = jnp.sum(n_chunks)
    last_chunk = jnp.concatenate([jnp.where(cnt > 0, chunk_base + n_chunks - 1, -1),
                                  total_chunks[None]]).astype(jnp.int32)

    n_sb = (n_chunks + CHUNKS_PER_SB - 1) // CHUNKS_PER_SB
    sb_end = jnp.cumsum(n_sb)
    sb_start = sb_end - n_sb
    n_used = sb_end[-1]
    s = jnp.arange(MAX_SB)
    s_eff = jnp.minimum(s, n_used - 1)
    e = jnp.minimum(jnp.searchsorted(sb_end, s_eff, side='right'), N_EXPERTS - 1)
    kk = s_eff - sb_start[e]
    sb_chunk0 = chunk_base[e] + kk * CHUNKS_PER_SB
    sb_n = jnp.where(s < n_used, jnp.clip(n_chunks[e] - kk * CHUNKS_PER_SB, 0, CHUNKS_PER_SB), 0)
    used = jnp.stack([n_used, total_chunks]).astype(jnp.int32)
    return (dest, last_chunk, e.astype(jnp.int32), sb_chunk0.astype(jnp.int32),
            sb_n.astype(jnp.int32), used)


DISP_TM = 256


def _row_copy(src, src_row, dst, dst_row, sem):
    return pltpu.make_async_copy(src.at[pl.ds(src_row, 1), :], dst.at[pl.ds(dst_row, 1), :], sem)


def _dispatch_kernel(last_ref, dest_ref, hn_ref, xs_ref, zero_ref, sem_ref):
    i = pl.program_id(0)

    @pl.when(i == 0)
    def _():
        zero_ref[...] = jnp.zeros_like(zero_ref)

        def chunk_copy(c):
            row0 = pl.multiple_of(c * ROW_CHUNK, ROW_CHUNK)
            return pltpu.make_async_copy(zero_ref, xs_ref.at[pl.ds(row0, ROW_CHUNK), :], sem_ref.at[1])

        def start(e, _):
            @pl.when(last_ref[e] >= 0)
            def _():
                chunk_copy(last_ref[e]).start()
            return 0

        def wait(e, _):
            @pl.when(last_ref[e] >= 0)
            def _():
                chunk_copy(last_ref[e]).wait()
            return 0

        def start_tail(c, _):
            chunk_copy(c).start()
            return 0

        def wait_tail(c, _):
            chunk_copy(c).wait()
            return 0

        lax.fori_loop(0, N_EXPERTS, start, 0)
        lax.fori_loop(last_ref[N_EXPERTS], MAX_CHUNKS, start_tail, 0)
        lax.fori_loop(0, N_EXPERTS, wait, 0)
        lax.fori_loop(last_ref[N_EXPERTS], MAX_CHUNKS, wait_tail, 0)

    def issue(t, _):
        for k in range(TOP_K):
            _row_copy(hn_ref, t, xs_ref, dest_ref[t * TOP_K + k], sem_ref.at[0]).start()
        return 0

    lax.fori_loop(0, DISP_TM, issue, 0, unroll=8)
    for k in range(TOP_K):
        pltpu.make_async_copy(hn_ref, xs_ref.at[pl.ds(0, DISP_TM), :], sem_ref.at[0]).wait()


def _dispatch(last_chunk, dest, hn):
    return pl.pallas_call(
        _dispatch_kernel,
        name="dispatch",
        grid_spec=pltpu.PrefetchScalarGridSpec(
            num_scalar_prefetch=1,
            grid=(SEQ // DISP_TM,),
            in_specs=[
                pl.BlockSpec((DISP_TM * TOP_K,), lambda i, last: (i,), memory_space=pltpu.SMEM),
                pl.BlockSpec((DISP_TM, HALF_D), lambda i, last: (i, 0)),
            ],
            out_specs=pl.BlockSpec(memory_space=pl.ANY),
            scratch_shapes=[
                pltpu.VMEM((ROW_CHUNK, HALF_D), jnp.uint32),
                pltpu.SemaphoreType.DMA((2,)),
            ],
        ),
        out_shape=jax.ShapeDtypeStruct((MAX_ROWS, HALF_D), jnp.uint32),
        compiler_params=_cparams(("arbitrary",)),
    )(last_chunk, dest, hn)


def _experts_kernel(e_ref, c0_ref, n_ref, used_ref,
                    xs_ref, wg_ref, wu_ref, wd_ref, bg_ref, bu_ref, bd_ref, ys_ref,
                    xin_ref, xbf_ref, act_ref, acc_ref, pend_ref, sem_ref):
    s = pl.program_id(0)
    f = pl.program_id(1)
    n = n_ref[s]
    c0 = c0_ref[s]

    def rows(c, k=1):
        return pl.ds(pl.multiple_of(c * ROW_CHUNK, ROW_CHUNK), k * ROW_CHUNK)

    def cover(body):
        n4 = n // 4

        def quad(i, _):
            body(i * 4, 4)
            return 0

        lax.fori_loop(0, n4, quad, 0)

        @pl.when((n & 2) != 0)
        def _():
            body(n4 * 4, 2)

        @pl.when((n & 1) != 0)
        def _():
            body(n4 * 4 + (n & 2), 1)

    def drain():
        def wait_one(i, _):
            pltpu.make_async_copy(acc_ref.at[0:ROW_CHUNK, :], ys_ref.at[0:ROW_CHUNK, :], sem_ref.at[1]).wait()
            return 0

        lax.fori_loop(0, pend_ref[0], wait_one, 0)
        pend_ref[0] = 0

    @pl.when(jnp.logical_and(s == 0, f == 0))
    def _():
        pend_ref[0] = 0

    def fetch(first_chunk, count):
        def start(c, _):
            pltpu.make_async_copy(xs_ref.at[rows(first_chunk + c), :], xin_ref.at[rows(c), :],
                                  sem_ref.at[0]).start()
            return 0

        lax.fori_loop(0, count, start, 0)

    @pl.when(jnp.logical_and(s == 0, f == 0))
    def _():
        fetch(c0, n)

    @pl.when(jnp.logical_and(n > 0, f == 0))
    def _():
        def finish(c, _):
            pltpu.make_async_copy(xs_ref.at[rows(c0 + c), :], xin_ref.at[rows(c), :], sem_ref.at[0]).wait()
            return 0

        def unpack(c, _):
            w = xin_ref[rows(c), :]
            low = lax.bitcast_convert_type(w << 16, f32)
            high = lax.bitcast_convert_type(w & jnp.uint32(0xFFFF0000), f32)
            xbf_ref[rows(c), 0:HALF_D] = low.astype(bf16)
            xbf_ref[rows(c), HALF_D:D_MODEL] = high.astype(bf16)
            return 0

        lax.fori_loop(0, n, finish, 0)
        lax.fori_loop(0, n, unpack, 0)

    @pl.when(jnp.logical_and(s + 1 < MAX_SB, f == 1))
    def _():
        nxt = jnp.minimum(s + 1, MAX_SB - 1)
        fetch(c0_ref[nxt], n_ref[nxt])

    @pl.when(n > 0)
    def _():
        bg = bg_ref[0]
        bu = bu_ref[0]

        def up_body(c, k):
            x = xbf_ref[rows(c, k), :]
            g = jnp.dot(x, wg_ref[0].astype(bf16), preferred_element_type=f32) + bg
            u = jnp.dot(x, wu_ref[0].astype(bf16), preferred_element_type=f32) + bu
            g = jnp.minimum(g, SWIGLU_LIMIT)
            u = jnp.clip(u, -SWIGLU_LIMIT, SWIGLU_LIMIT)
            a = (u + 1.0) * (g * (1.0 / (1.0 + jnp.exp(-SWIGLU_ALPHA * g))))
            act_ref[rows(c, k), :] = a.astype(bf16)

        cover(up_body)

        @pl.when(f == 0)
        def _():
            drain()
            bias = jnp.broadcast_to(bd_ref[0], (ROW_CHUNK, D_MODEL))

            def init(c, _):
                acc_ref[rows(c), :] = bias
                return 0

            lax.fori_loop(0, n, init, 0)

        def down_body(c, k):
            acc_ref[rows(c, k), :] += jnp.dot(act_ref[rows(c, k), :], wd_ref[0].astype(bf16),
                                              preferred_element_type=f32)

        cover(down_body)

        @pl.when(f == N_FF_TILES - 1)
        def _():
            def write(c, _):
                pltpu.make_async_copy(acc_ref.at[rows(c), :], ys_ref.at[rows(c0 + c), :], sem_ref.at[1]).start()
                return 0

            lax.fori_loop(0, n, write, 0)
            pend_ref[0] = n

    @pl.when(jnp.logical_and(s == MAX_SB - 1, f == N_FF_TILES - 1))
    def _():
        drain()
        acc_ref[0:ROW_CHUNK, :] = jnp.zeros((ROW_CHUNK, D_MODEL), f32)

        def tail_copy(c):
            return pltpu.make_async_copy(acc_ref.at[0:ROW_CHUNK, :], ys_ref.at[rows(c), :], sem_ref.at[0])

        def start(c, _):
            tail_copy(c).start()
            return 0

        def finish(c, _):
            tail_copy(c).wait()
            return 0

        lax.fori_loop(used_ref[1], MAX_CHUNKS, start, 0)
        lax.fori_loop(used_ref[1], MAX_CHUNKS, finish, 0)


def _experts(sb_e, sb_c0, sb_n, n_used, xs, w_gate, b_gate, w_up, b_up, w_down, b_down):
    last = N_FF_TILES - 1

    def tile(s, f, used):
        return jnp.where(s < used[0], f, last)

    def up_map(s, f, e, c0, n, used):
        return (e[s], 0, tile(s, f, used))

    def down_map(s, f, e, c0, n, used):
        return (e[s], tile(s, f, used), 0)

    def bias_map(s, f, e, c0, n, used):
        return (e[s], 0, 0)

    return pl.pallas_call(
        _experts_kernel,
        name="experts",
        grid_spec=pltpu.PrefetchScalarGridSpec(
            num_scalar_prefetch=4,
            grid=(MAX_SB, N_FF_TILES),
            in_specs=[
                pl.BlockSpec(memory_space=pl.ANY),
                pl.BlockSpec((1, D_MODEL, FF_TILE), up_map),
                pl.BlockSpec((1, D_MODEL, FF_TILE), up_map),
                pl.BlockSpec((1, FF_TILE, D_MODEL), down_map),
                pl.BlockSpec((1, 1, FF_TILE), up_map),
                pl.BlockSpec((1, 1, FF_TILE), up_map),
                pl.BlockSpec((1, 1, D_MODEL), bias_map),
            ],
            out_specs=pl.BlockSpec(memory_space=pl.ANY),
            scratch_shapes=[
                pltpu.VMEM((SB_ROWS, HALF_D), jnp.uint32),
                pltpu.VMEM((SB_ROWS, D_MODEL), bf16),
                pltpu.VMEM((SB_ROWS, FF_TILE), bf16),
                pltpu.VMEM((SB_ROWS, D_MODEL), f32),
                pltpu.SMEM((1,), jnp.int32),
                pltpu.SemaphoreType.DMA((2,)),
            ],
        ),
        out_shape=jax.ShapeDtypeStruct((MAX_ROWS, D_MODEL), f32),
        compiler_params=_cparams(("arbitrary", "arbitrary"), EXPERTS_VMEM_LIMIT),
    )(sb_e, sb_c0, sb_n, n_used, xs, w_gate, w_up, w_down,
      b_gate.reshape(N_EXPERTS, 1, D_FF), b_up.reshape(N_EXPERTS, 1, D_FF),
      b_down.reshape(N_EXPERTS, 1, D_MODEL))


COMB_TM = 256


def _combine_kernel(dest_ref, ys_ref, x1_ref, gate_ref, gf_ref, o_ref, buf_ref, sem_ref):
    def issue(t, _):
        for k in range(TOP_K):
            pltpu.make_async_copy(ys_ref.at[pl.ds(dest_ref[t * TOP_K + k], 1), :],
                                  buf_ref.at[k, pl.ds(t, 1), :], sem_ref.at[0]).start()
        return 0

    lax.fori_loop(0, COMB_TM, issue, 0, unroll=8)
    for k in range(TOP_K):
        pltpu.make_async_copy(ys_ref.at[pl.ds(0, COMB_TM), :], buf_ref.at[k], sem_ref.at[0]).wait()

    gate = gate_ref[...]
    acc = x1_ref[...]
    for k in range(TOP_K):
        acc = acc + gate[:, k:k + 1] * buf_ref[k]
    o_ref[...] = _rms(acc, gf_ref[...])


def _combine(dest, ys, x1, gates, g_final):
    return pl.pallas_call(
        _combine_kernel,
        name="combine",
        grid=(SEQ // COMB_TM,),
        in_specs=[
            pl.BlockSpec((COMB_TM * TOP_K,), lambda i: (i,), memory_space=pltpu.SMEM),
            pl.BlockSpec(memory_space=pl.ANY),
            pl.BlockSpec((COMB_TM, D_MODEL), lambda i: (i, 0)),
            pl.BlockSpec((COMB_TM, LANES), lambda i: (i, 0)),
            pl.BlockSpec((1, D_MODEL), lambda i: (0, 0)),
        ],
        out_specs=pl.BlockSpec((COMB_TM, D_MODEL), lambda i: (i, 0)),
        out_shape=jax.ShapeDtypeStruct((SEQ, D_MODEL), f32),
        scratch_shapes=[
            pltpu.VMEM((TOP_K, COMB_TM, D_MODEL), f32),
            pltpu.SemaphoreType.DMA((1,)),
        ],
        compiler_params=_cparams(("arbitrary",)),
    )(dest, ys, x1, gates, g_final.reshape(1, D_MODEL))


def kernel(x, g_mix, w_in, lam_re_fwd, lam_im_fwd, log_dt_fwd, b_re_fwd, b_im_fwd, c_re_fwd, c_im_fwd, lam_re_bwd, lam_im_bwd, log_dt_bwd, b_re_bwd, b_im_bwd, c_re_bwd, c_im_bwd, ssm_d, w_glu, b_glu, na_rpb, g_ssm_out, g_na_out, w_out, g_moe, w_router, b_router, w_gate, b_gate, w_up, b_up, w_down, b_down, g_final):
    x2 = x.reshape(SEQ, D_MODEL)

    u, qkv = _in_proj(x2, g_mix[0], w_in[0].astype(bf16))

    s5_w, s5_dec = _s5_weights(
        (lam_re_fwd[0], lam_im_fwd[0], log_dt_fwd[0], b_re_fwd[0], b_im_fwd[0], c_re_fwd[0], c_im_fwd[0]),
        (lam_re_bwd[0], lam_im_bwd[0], log_dt_bwd[0], b_re_bwd[0], b_im_bwd[0], c_re_bwd[0], c_im_bwd[0]))
    y = _s5(u, s5_w, s5_dec, ssm_d[0])
    ssm_n = _glu(y, w_glu[0].astype(bf16), b_glu[0], g_ssm_out[0])

    y_na = _natten(qkv, _na_bias(na_rpb[0]))

    w_router_pad = jnp.zeros((D_MODEL, LANES), bf16).at[:, :N_EXPERTS].set(w_router[0].astype(bf16))
    b_router_pad = jnp.full((1, LANES), NEG_BIG, f32).at[0, :N_EXPERTS].set(b_router[0].astype(f32))
    x1, hn, logits = _out_proj(ssm_n, y_na, x2, g_na_out[0], w_out[0].astype(bf16), g_moe[0],
                               w_router_pad, b_router_pad)

    meta, gates, counts = _route(logits)
    dest, last_chunk, sb_e, sb_c0, sb_n, n_used = _routing_tables(meta, counts)

    xs = _dispatch(last_chunk, dest, hn)
    ys = _experts(sb_e, sb_c0, sb_n, n_used, xs, w_gate[0], b_gate[0], w_up[0], b_up[0],
                  w_down[0], b_down[0])
    out = _combine(dest, ys, x1, gates, g_final)
    return out.reshape(x.shape)
```

```python
import functools
import math

import jax
import jax.numpy as jnp
from jax import lax
from jax.experimental import pallas as pl
from jax.experimental.pallas import tpu as pltpu

f32 = jnp.float32
bf16 = jnp.bfloat16

D_MODEL = 2048
SEQ = 8192
SSM_WIDTH = 1024
NA_WIDTH = 1024
SSM_GROUP = 16
SSM_GROUPS = 64
SSM_STATE = 64
NA_HEAD_DIM = 64
NA_HEADS = 16
GRID_W = 64
GRID_ROWS = SEQ // GRID_W
NA_KH = 8
NA_KW = 16
N_EXPERTS = 32
TOP_K = 4
D_FF = 2048
SWIGLU_LIMIT = 7.0
SWIGLU_ALPHA = 1.702
RMS_EPS = 1e-5

LANES = 128
HALF_D = D_MODEL // 2
NEG_BIG = -1e30

CHUNK_T = 16
N_CHUNKS = SEQ // CHUNK_T
GROUPS_PER_BLOCK = LANES // SSM_GROUP
N_LANE_BLOCKS = SSM_WIDTH // LANES
CAT_W = CHUNK_T * LANES
STATE_W = GROUPS_PER_BLOCK * SSM_STATE

ROW_CHUNK = 128
CHUNKS_PER_SB = 12
SB_ROWS = ROW_CHUNK * CHUNKS_PER_SB
MAX_CHUNKS = SEQ * TOP_K // ROW_CHUNK + N_EXPERTS
MAX_ROWS = MAX_CHUNKS * ROW_CHUNK
MAX_SB = MAX_CHUNKS // CHUNKS_PER_SB + N_EXPERTS
FF_TILE = 512
N_FF_TILES = D_FF // FF_TILE

VMEM_LIMIT = 56 * 1024 * 1024
EXPERTS_VMEM_LIMIT = 60 * 1024 * 1024


def _cparams(semantics, vmem=VMEM_LIMIT):
    return pltpu.CompilerParams(dimension_semantics=semantics, vmem_limit_bytes=vmem)


def _rms(x, g):
    return x * lax.rsqrt(jnp.mean(x * x, axis=-1, keepdims=True) + RMS_EPS) * g


IN_TM = 1024
IN_TN = 1024


def _in_proj_kernel(x_ref, g_ref, w_ref, u_ref, qkv_ref, h_ref):
    j = pl.program_id(1)

    @pl.when(j == 0)
    def _():
        h_ref[...] = _rms(x_ref[...], g_ref[...]).astype(bf16)

    acc = jnp.dot(h_ref[...], w_ref[...], preferred_element_type=f32)

    @pl.when(j == 0)
    def _():
        u_ref[...] = acc

    @pl.when(j > 0)
    def _():
        qkv_ref[...] = acc.astype(bf16)


def _in_proj(x, g_mix, w_in_bf):
    n_out = w_in_bf.shape[1]
    return pl.pallas_call(
        _in_proj_kernel,
        name="in_proj",
        grid=(SEQ // IN_TM, n_out // IN_TN),
        in_specs=[
            pl.BlockSpec((IN_TM, D_MODEL), lambda i, j: (i, 0)),
            pl.BlockSpec((1, D_MODEL), lambda i, j: (0, 0)),
            pl.BlockSpec((D_MODEL, IN_TN), lambda i, j: (0, j)),
        ],
        out_specs=[
            pl.BlockSpec((IN_TM, IN_TN), lambda i, j: (i, 0)),
            pl.BlockSpec((IN_TM, IN_TN), lambda i, j: (i, jnp.maximum(j - 1, 0))),
        ],
        out_shape=[
            jax.ShapeDtypeStruct((SEQ, SSM_WIDTH), f32),
            jax.ShapeDtypeStruct((SEQ, 3 * NA_WIDTH), bf16),
        ],
        scratch_shapes=[pltpu.VMEM((IN_TM, D_MODEL), bf16)],
        compiler_params=_cparams(("arbitrary", "arbitrary")),
    )(x, g_mix.reshape(1, D_MODEL), w_in_bf)


def _cmul(ar, ai, br, bi):
    return ar * br - ai * bi, ar * bi + ai * br


def _s5_discretise(lam_re, lam_im, log_dt, b_re, b_im, c_re, c_im):
    a = jnp.minimum(lam_re.astype(f32), -1e-4)
    w = lam_im.astype(f32)
    dt = jnp.exp(log_dt.astype(f32))[:, None]
    steps = jnp.arange(CHUNK_T + 1, dtype=f32)[:, None, None]
    mag = jnp.exp((a * dt)[None] * steps)
    ang = (w * dt)[None] * steps
    pw = (mag * jnp.cos(ang), mag * jnp.sin(ang))
    xr, xi = pw[0][1] - 1.0, pw[1][1]
    den = a * a + w * w
    qr, qi = (xr * a + xi * w) / den, (xi * a - xr * w) / den
    bb = _cmul(qr[..., None], qi[..., None], b_re.astype(f32), b_im.astype(f32))
    return pw, bb, (c_re.astype(f32), c_im.astype(f32))


def _pair_blockdiag(m):
    z = jnp.zeros_like(m[..., 0, :, :])
    top = jnp.concatenate([m[..., 0, :, :], z], axis=-1)
    bot = jnp.concatenate([z, m[..., 1, :, :]], axis=-1)
    return jnp.concatenate([top, bot], axis=-2)


def _s5_weights(fwd, bwd):
    hi = lax.Precision.HIGH
    exact = lax.Precision.HIGHEST
    t = CHUNK_T
    pw_f, bb_f, c_f = _s5_discretise(*fwd)
    pw_b, bb_b, c_b = _s5_discretise(*bwd)
    pairs = (N_LANE_BLOCKS, GROUPS_PER_BLOCK // 2, 2)

    x = jnp.arange(t * SSM_GROUP)
    t_of_x = x // SSM_GROUP
    tile_c = (jnp.arange(SSM_GROUP)[:, None] == x[None, :] % SSM_GROUP).astype(f32)

    def c_over_x(c):
        return [jnp.einsum('gcp,cx->gpx', part, tile_c, precision=exact) for part in c]

    def c_times_power(c_x, pw, power_of_x):
        rep = (jnp.arange(t + 1)[:, None] == power_of_x[None, :]).astype(f32)
        pw_x = [jnp.einsum('jgp,jx->gpx', part, rep, precision=exact) for part in pw]
        return _cmul(c_x[0], c_x[1], pw_x[0], pw_x[1])

    cx_f, cx_b = c_over_x(c_f), c_over_x(c_b)

    def lag_kernel(c_x, pw, bb, power_of_x):
        m_re, m_im = c_times_power(c_x, pw, power_of_x)
        m = jnp.concatenate([m_re, m_im], axis=1)
        b = jnp.concatenate([jnp.swapaxes(bb[0], 1, 2), -jnp.swapaxes(bb[1], 1, 2)], axis=-1)
        return jnp.einsum('gdp,gpx->gdx', b, m, precision=hi)

    kf = lag_kernel(cx_f, pw_f, bb_f, t_of_x)
    kb = lag_kernel(cx_b, pw_b, bb_b, t - 1 - t_of_x)
    keep = (t - 1) * SSM_GROUP
    k_lag = jnp.concatenate([kb[..., :keep], kb[..., keep:] + kf[..., :SSM_GROUP], kf[..., SSM_GROUP:]],
                            axis=-1)
    w_intra = jnp.stack([k_lag[..., (t - 1 - ti) * SSM_GROUP:(2 * t - 1 - ti) * SSM_GROUP]
                         for ti in range(t)], axis=1)
    w_intra = _pair_blockdiag(w_intra.astype(bf16).reshape(*pairs, t * SSM_GROUP, t * SSM_GROUP))

    def state_in(pw_t, bb):
        pw_g = [jnp.transpose(part, (1, 0, 2))[:, :, None, :] for part in pw_t]
        bb_g = [jnp.swapaxes(part, 1, 2)[:, None] for part in bb]
        return [_pair_blockdiag(part.astype(bf16).reshape(*pairs, t * SSM_GROUP, SSM_STATE))
                for part in _cmul(pw_g[0], pw_g[1], bb_g[0], bb_g[1])]

    w_in = jnp.concatenate(state_in([part[:t][::-1] for part in pw_f], bb_f)
                           + state_in([part[:t] for part in pw_b], bb_b), axis=-1)

    def state_out(c_x, pw, power_of_x):
        m_re, m_im = c_times_power(c_x, pw, power_of_x)
        return [_pair_blockdiag(part.astype(bf16).reshape(*pairs, SSM_STATE, t * SSM_GROUP))
                for part in (m_re, -m_im)]

    w_out = jnp.concatenate(state_out(cx_f, pw_f, t_of_x + 1) + state_out(cx_b, pw_b, t - t_of_x), axis=-2)

    def decay(a):
        return a.reshape(N_LANE_BLOCKS, 1, STATE_W)

    dec = jnp.concatenate([decay(pw_f[0][t]), decay(pw_f[1][t]), decay(pw_b[0][t]), decay(pw_b[1][t])],
                          axis=-1)
    return (w_in, w_intra, w_out), dec.astype(f32)


def _gelu_tanh(x):
    return 0.5 * x * (1.0 + jnp.tanh(math.sqrt(2.0 / math.pi) * (x + 0.044715 * (x * x * x))))


PAIRS_PER_BLOCK = GROUPS_PER_BLOCK // 2
PAIR_W = 2 * CHUNK_T * SSM_GROUP


def _s5_kernel(u_ref, win_ref, wintra_ref, wout_ref, dec_ref, d_ref, y_ref,
               cat_ref, catp_ref, st_ref, perm_ref):
    sw = STATE_W

    @pl.when(pl.program_id(0) == 0)
    def _():
        def strip(t, _):
            row = lax.broadcasted_iota(jnp.int32, (LANES, CAT_W), 0)
            col = lax.broadcasted_iota(jnp.int32, (LANES, CAT_W), 1)
            target = (row // SSM_GROUP) * (CHUNK_T * SSM_GROUP) + t * SSM_GROUP + row % SSM_GROUP
            perm_ref[pl.ds(pl.multiple_of(t * LANES, LANES), LANES), :] = (
                jnp.where(col == target, 1.0, 0.0).astype(bf16))
            return 0

        lax.fori_loop(0, CHUNK_T, strip, 0)

    for t in range(CHUNK_T):
        cat_ref[:, t * LANES:(t + 1) * LANES] = u_ref[pl.ds(t, N_CHUNKS, stride=CHUNK_T), :].astype(bf16)
    catp_ref[...] = jnp.dot(cat_ref[...], perm_ref[...], preferred_element_type=f32).astype(bf16)

    for p in range(PAIRS_PER_BLOCK):
        z = jnp.dot(catp_ref[:, p * PAIR_W:(p + 1) * PAIR_W], win_ref[0, p], preferred_element_type=f32)
        for part in range(4):
            st_ref[:, part * sw + p * LANES:part * sw + (p + 1) * LANES] = z[:, part * LANES:(part + 1) * LANES]

    afr = dec_ref[0, :, 0 * sw:1 * sw]
    afi = dec_ref[0, :, 1 * sw:2 * sw]
    abr = dec_ref[0, :, 2 * sw:3 * sw]
    abi = dec_ref[0, :, 3 * sw:4 * sw]

    def step(i, carry):
        sfr, sfi, sbr, sbi = carry
        r = N_CHUNKS - 1 - i
        zf = st_ref[pl.ds(i, 1), 0:2 * sw]
        zb = st_ref[pl.ds(r, 1), 2 * sw:4 * sw]
        st_ref[pl.ds(i, 1), 0:2 * sw] = jnp.concatenate([sfr, sfi], axis=-1)
        st_ref[pl.ds(r, 1), 2 * sw:4 * sw] = jnp.concatenate([sbr, sbi], axis=-1)
        nfr = afr * sfr - afi * sfi + zf[:, :sw]
        nfi = afi * sfr + afr * sfi + zf[:, sw:]
        nbr = abr * sbr - abi * sbi + zb[:, :sw]
        nbi = abi * sbr + abr * sbi + zb[:, sw:]
        return nfr, nfi, nbr, nbi

    zero = jnp.zeros((1, sw), f32)
    lax.fori_loop(0, N_CHUNKS, step, (zero, zero, zero, zero))

    for p in range(PAIRS_PER_BLOCK):
        s_in = jnp.concatenate(
            [st_ref[:, part * sw + p * LANES:part * sw + (p + 1) * LANES] for part in range(4)], axis=-1)
        yp = jnp.dot(catp_ref[:, p * PAIR_W:(p + 1) * PAIR_W], wintra_ref[0, p], preferred_element_type=f32)
        yp = yp + jnp.dot(s_in.astype(bf16), wout_ref[0, p], preferred_element_type=f32)
        cat_ref[:, p * PAIR_W:(p + 1) * PAIR_W] = yp.astype(bf16)

    st_ref[...] = lax.dot_general(cat_ref[...], perm_ref[...], (((1,), (1,)), ((), ())),
                                  preferred_element_type=f32)
    d = d_ref[...]
    for t in range(CHUNK_T):
        rows = pl.ds(t, N_CHUNKS, stride=CHUNK_T)
        v = st_ref[:, t * LANES:(t + 1) * LANES] + d * u_ref[rows, :]
        y_ref[rows, :] = _gelu_tanh(v)


def _s5(u, weights, dec, ssm_d):
    w_spec = pl.BlockSpec((1, PAIRS_PER_BLOCK, PAIR_W, PAIR_W), lambda j: (j, 0, 0, 0))
    return pl.pallas_call(
        _s5_kernel,
        name="s5_scan",
        grid=(N_LANE_BLOCKS,),
        in_specs=[
            pl.BlockSpec((SEQ, LANES), lambda j: (0, j)),
            w_spec, w_spec, w_spec,
            pl.BlockSpec((1, 1, 4 * STATE_W), lambda j: (j, 0, 0)),
            pl.BlockSpec((1, LANES), lambda j: (0, j)),
        ],
        out_specs=pl.BlockSpec((SEQ, LANES), lambda j: (0, j)),
        out_shape=jax.ShapeDtypeStruct((SEQ, SSM_WIDTH), f32),
        scratch_shapes=[
            pltpu.VMEM((N_CHUNKS, CAT_W), bf16),
            pltpu.VMEM((N_CHUNKS, CAT_W), bf16),
            pltpu.VMEM((N_CHUNKS, 4 * STATE_W), f32),
            pltpu.VMEM((CAT_W, CAT_W), bf16),
        ],
        compiler_params=_cparams(("arbitrary",)),
    )(u, *weights, dec, ssm_d.reshape(1, SSM_WIDTH))


GLU_TM = 1024


def _glu_kernel(y_ref, w_ref, b_ref, g_ref, o_ref):
    y = y_ref[...]
    z = jnp.dot(y.astype(bf16), w_ref[...], preferred_element_type=f32) + b_ref[...]
    o = y * (1.0 / (1.0 + jnp.exp(-z)))
    o_ref[...] = _rms(o, g_ref[...]).astype(bf16)


def _glu(y, w_glu_bf, b_glu, g_ssm_out):
    return pl.pallas_call(
        _glu_kernel,
        name="glu_norm",
        grid=(SEQ // GLU_TM,),
        in_specs=[
            pl.BlockSpec((GLU_TM, SSM_WIDTH), lambda i: (i, 0)),
            pl.BlockSpec((SSM_WIDTH, SSM_WIDTH), lambda i: (0, 0)),
            pl.BlockSpec((1, SSM_WIDTH), lambda i: (0, 0)),
            pl.BlockSpec((1, SSM_WIDTH), lambda i: (0, 0)),
        ],
        out_specs=pl.BlockSpec((GLU_TM, SSM_WIDTH), lambda i: (i, 0)),
        out_shape=jax.ShapeDtypeStruct((SEQ, SSM_WIDTH), bf16),
        compiler_params=_cparams(("arbitrary",)),
    )(y, w_glu_bf, b_glu.reshape(1, SSM_WIDTH), g_ssm_out.reshape(1, SSM_WIDTH))


NA_ROWS_PER_STEP = 16
NA_WIN = NA_KH * GRID_W
HEADS_PER_BLOCK = LANES // NA_HEAD_DIM


def _na_bias(rpb):
    c = jnp.arange(GRID_W)
    col_start = jnp.clip(c - NA_KW // 2, 0, GRID_W - NA_KW)
    valid = (c[None, :] >= col_start[:, None]) & (c[None, :] < col_start[:, None] + NA_KW)
    dc = jnp.clip(c[None, :] - c[:, None], -(NA_KW - 1), NA_KW - 1) + (NA_KW - 1)
    sel = (dc[None] == jnp.arange(2 * NA_KW - 1)[:, None, None]).astype(f32)
    tab = jnp.einsum('hrc,cqk->hqrk', rpb.astype(f32), sel, precision=lax.Precision.HIGHEST)
    tab = jnp.where(valid[None, :, None, :], tab, NEG_BIG)
    tab = tab.reshape(NA_HEADS, GRID_W, (2 * NA_KH - 1) * GRID_W)
    return jnp.stack([tab[..., (NA_KH - 1 - v) * GRID_W:(2 * NA_KH - 1 - v) * GRID_W]
                      for v in range(NA_KH)], axis=1)


def _natten_kernel(q_ref, k_ref, v_ref, b_ref, o_ref, s_ref, p_ref):
    rb = pl.program_id(1)
    lane = lax.broadcasted_iota(jnp.int32, (GRID_W, LANES), 1)
    head0 = lane < NA_HEAD_DIM
    scale = NA_HEAD_DIM ** -0.5

    starts, variants = [], []
    for i in range(NA_ROWS_PER_STEP):
        r = rb * NA_ROWS_PER_STEP + i
        rs = jnp.clip(r - NA_KH // 2, 0, GRID_ROWS - NA_KH)
        starts.append(pl.multiple_of(rs * GRID_W, GRID_W))
        variants.append(r - rs)

    for i in range(NA_ROWS_PER_STEP):
        q = q_ref[i * GRID_W:(i + 1) * GRID_W, :] * scale
        kw = k_ref[pl.ds(starts[i], NA_WIN), :]
        for h in range(HEADS_PER_BLOCK):
            qh = jnp.where(head0 if h == 0 else ~head0, q, jnp.zeros_like(q))
            s = lax.dot_general(qh, kw, (((1,), (1,)), ((), ())), preferred_element_type=f32)
            s_ref[i * HEADS_PER_BLOCK + h] = s + b_ref[h, variants[i]]

    inv_sums = []
    for t in range(NA_ROWS_PER_STEP * HEADS_PER_BLOCK):
        s = s_ref[t]
        p = jnp.exp(s - jnp.max(s, axis=-1, keepdims=True))
        inv_sums.append(1.0 / jnp.sum(p, axis=-1, keepdims=True))
        p_ref[t] = p.astype(bf16)

    for i in range(NA_ROWS_PER_STEP):
        vw = v_ref[pl.ds(starts[i], NA_WIN), :]
        outs = []
        for h in range(HEADS_PER_BLOCK):
            t = i * HEADS_PER_BLOCK + h
            outs.append(jnp.dot(p_ref[t], vw, preferred_element_type=f32) * inv_sums[t])
        o_ref[i * GRID_W:(i + 1) * GRID_W, :] = jnp.where(head0, outs[0], outs[1]).astype(bf16)


def _natten(qkv, bias):
    tm = NA_ROWS_PER_STEP * GRID_W
    n_hb = NA_WIDTH // LANES
    return pl.pallas_call(
        _natten_kernel,
        name="natten",
        grid=(n_hb, GRID_ROWS // NA_ROWS_PER_STEP),
        in_specs=[
            pl.BlockSpec((tm, LANES), lambda h, r: (r, h)),
            pl.BlockSpec((SEQ, LANES), lambda h, r: (0, n_hb + h)),
            pl.BlockSpec((SEQ, LANES), lambda h, r: (0, 2 * n_hb + h)),
            pl.BlockSpec((HEADS_PER_BLOCK, NA_KH, GRID_W, NA_WIN), lambda h, r: (h, 0, 0, 0)),
        ],
        out_specs=pl.BlockSpec((tm, LANES), lambda h, r: (r, h)),
        out_shape=jax.ShapeDtypeStruct((SEQ, NA_WIDTH), bf16),
        scratch_shapes=[
            pltpu.VMEM((NA_ROWS_PER_STEP * HEADS_PER_BLOCK, GRID_W, NA_WIN), f32),
            pltpu.VMEM((NA_ROWS_PER_STEP * HEADS_PER_BLOCK, GRID_W, NA_WIN), bf16),
        ],
        compiler_params=_cparams(("arbitrary", "arbitrary")),
    )(qkv, qkv, qkv, bias)


OUT_TM = 512


def _out_proj_kernel(ssm_ref, na_ref, x_ref, gna_ref, w_ref, gmoe_ref, wr_ref, br_ref,
                     x1_ref, hn_ref, lg_ref):
    na = _rms(na_ref[...].astype(f32), gna_ref[...]).astype(bf16)
    y = jnp.dot(ssm_ref[...], w_ref[0:SSM_WIDTH, :], preferred_element_type=f32)
    y = y + jnp.dot(na, w_ref[SSM_WIDTH:, :], preferred_element_type=f32)
    x1 = x_ref[...] + y
    x1_ref[...] = x1
    hn = _rms(x1, gmoe_ref[...]).astype(bf16)
    lg_ref[...] = jnp.dot(hn, wr_ref[...], preferred_element_type=f32) + br_ref[...]
    bits = lax.bitcast_convert_type(hn.astype(f32), jnp.uint32)
    hn_ref[...] = (bits[:, HALF_D:] & jnp.uint32(0xFFFF0000)) | (bits[:, :HALF_D] >> 16)


def _out_proj(ssm_n, y_na, x, g_na_out, w_out_bf, g_moe, w_router_pad, b_router_pad):
    row = lambda i: (i, 0)
    fixed = lambda i: (0, 0)
    return pl.pallas_call(
        _out_proj_kernel,
        name="out_proj",
        grid=(SEQ // OUT_TM,),
        in_specs=[
            pl.BlockSpec((OUT_TM, SSM_WIDTH), row),
            pl.BlockSpec((OUT_TM, NA_WIDTH), row),
            pl.BlockSpec((OUT_TM, D_MODEL), row),
            pl.BlockSpec((1, NA_WIDTH), fixed),
            pl.BlockSpec((D_MODEL, D_MODEL), fixed),
            pl.BlockSpec((1, D_MODEL), fixed),
            pl.BlockSpec((D_MODEL, LANES), fixed),
            pl.BlockSpec((1, LANES), fixed),
        ],
        out_specs=[
            pl.BlockSpec((OUT_TM, D_MODEL), row),
            pl.BlockSpec((OUT_TM, HALF_D), row),
            pl.BlockSpec((OUT_TM, LANES), row),
        ],
        out_shape=[
            jax.ShapeDtypeStruct((SEQ, D_MODEL), f32),
            jax.ShapeDtypeStruct((SEQ, HALF_D), jnp.uint32),
            jax.ShapeDtypeStruct((SEQ, LANES), f32),
        ],
        compiler_params=_cparams(("arbitrary",)),
    )(ssm_n, y_na, x, g_na_out.reshape(1, NA_WIDTH), w_out_bf, g_moe.reshape(1, D_MODEL),
      w_router_pad, b_router_pad)


ROUTE_TM = 1024


def _route_kernel(lg_ref, tri_ref, meta_ref, gate_ref, cnt_ref, carry_ref):
    i = pl.program_id(0)

    @pl.when(i == 0)
    def _():
        carry_ref[...] = jnp.zeros_like(carry_ref)

    lane = lax.broadcasted_iota(jnp.int32, (ROUTE_TM, LANES), 1)
    lane_f = lane.astype(f32)
    work = lg_ref[...]
    vals, hits = [], []
    for _ in range(TOP_K):
        m = jnp.max(work, axis=-1, keepdims=True)
        idx = jnp.min(jnp.where(work == m, lane_f, float(LANES)), axis=-1, keepdims=True)
        hit = lane_f == idx
        vals.append(m)
        hits.append((idx, hit))
        work = jnp.where(hit, -jnp.inf, work)

    exps = [jnp.exp(v - vals[0]) for v in vals]
    denom = exps[0] + exps[1] + exps[2] + exps[3]

    onehot = jnp.zeros((ROUTE_TM, LANES), f32)
    for _, hit in hits:
        onehot = onehot + hit.astype(f32)
    before = jnp.dot(tri_ref[...], onehot.astype(bf16), preferred_element_type=f32) + carry_ref[...]

    meta = jnp.zeros((ROUTE_TM, LANES), jnp.int32)
    gate = jnp.zeros((ROUTE_TM, LANES), f32)
    for k, (idx, hit) in enumerate(hits):
        rank = jnp.sum(jnp.where(hit, before, 0.0), axis=-1, keepdims=True).astype(jnp.int32)
        meta = jnp.where(lane == k, idx.astype(jnp.int32), meta)
        meta = jnp.where(lane == TOP_K + k, rank, meta)
        gate = jnp.where(lane == k, exps[k] / denom, gate)
    meta_ref[...] = meta
    gate_ref[...] = gate
    carry_ref[...] += jnp.sum(onehot, axis=0, keepdims=True)
    cnt_ref[...] = carry_ref[...]


def _route(logits):
    tri = (jnp.arange(ROUTE_TM)[:, None] > jnp.arange(ROUTE_TM)[None, :]).astype(bf16)
    row = lambda i: (i, 0)
    return pl.pallas_call(
        _route_kernel,
        name="route",
        grid=(SEQ // ROUTE_TM,),
        in_specs=[
            pl.BlockSpec((ROUTE_TM, LANES), row),
            pl.BlockSpec((ROUTE_TM, ROUTE_TM), lambda i: (0, 0)),
        ],
        out_specs=[
            pl.BlockSpec((ROUTE_TM, LANES), row),
            pl.BlockSpec((ROUTE_TM, LANES), row),
            pl.BlockSpec((1, LANES), lambda i: (0, 0)),
        ],
        out_shape=[
            jax.ShapeDtypeStruct((SEQ, LANES), jnp.int32),
            jax.ShapeDtypeStruct((SEQ, LANES), f32),
            jax.ShapeDtypeStruct((1, LANES), f32),
        ],
        scratch_shapes=[pltpu.VMEM((1, LANES), f32)],
        compiler_params=_cparams(("arbitrary",)),
    )(logits, tri)


def _routing_tables(meta, counts):
    idx = meta[:, :TOP_K]
    rank = meta[:, TOP_K:2 * TOP_K]
    cnt = counts[0, :N_EXPERTS].astype(jnp.int32)
    n_chunks = (cnt + ROW_CHUNK - 1) // ROW_CHUNK
    chunk_base = jnp.cumsum(n_chunks) - n_chunks
    dest = (chunk_base[idx] * ROW_CHUNK + rank).astype(jnp.int32).reshape(-1)
    total_chunks = jnp.sum(n_chunks)
    last_chunk = jnp.concatenate([jnp.where(cnt > 0, chunk_base + n_chunks - 1, -1),
                                  total_chunks[None]]).astype(jnp.int32)

    n_sb = (n_chunks + CHUNKS_PER_SB - 1) // CHUNKS_PER_SB
    sb_end = jnp.cumsum(n_sb)
    sb_start = sb_end - n_sb
    n_used = sb_end[-1]
    s = jnp.arange(MAX_SB)
    s_eff = jnp.minimum(s, n_used - 1)
    e = jnp.minimum(jnp.searchsorted(sb_end, s_eff, side='right'), N_EXPERTS - 1)
    kk = s_eff - sb_start[e]
    per_sb = n_chunks[e] // jnp.maximum(n_sb[e], 1)
    extra = n_chunks[e] - per_sb * n_sb[e]
    sb_chunk0 = chunk_base[e] + kk * per_sb + jnp.minimum(kk, extra)
    sb_n = jnp.where(s < n_used, per_sb + (kk < extra), 0)
    used = jnp.stack([n_used, total_chunks]).astype(jnp.int32)
    return (dest, last_chunk, e.astype(jnp.int32), sb_chunk0.astype(jnp.int32),
            sb_n.astype(jnp.int32), used)


DISP_TM = 256


def _row_copy(src, src_row, dst, dst_row, sem):
    return pltpu.make_async_copy(src.at[pl.ds(src_row, 1), :], dst.at[pl.ds(dst_row, 1), :], sem)


def _dispatch_kernel(last_ref, dest_ref, hn_ref, xs_ref, zero_ref, sem_ref):
    i = pl.program_id(0)

    @pl.when(i == 0)
    def _():
        zero_ref[...] = jnp.zeros_like(zero_ref)

        def chunk_copy(c):
            row0 = pl.multiple_of(c * ROW_CHUNK, ROW_CHUNK)
            return pltpu.make_async_copy(zero_ref, xs_ref.at[pl.ds(row0, ROW_CHUNK), :], sem_ref.at[1])

        def start(e, _):
            @pl.when(last_ref[e] >= 0)
            def _():
                chunk_copy(last_ref[e]).start()
            return 0

        def wait(e, _):
            @pl.when(last_ref[e] >= 0)
            def _():
                chunk_copy(last_ref[e]).wait()
            return 0

        def start_tail(c, _):
            chunk_copy(c).start()
            return 0

        def wait_tail(c, _):
            chunk_copy(c).wait()
            return 0

        lax.fori_loop(0, N_EXPERTS, start, 0)
        lax.fori_loop(last_ref[N_EXPERTS], MAX_CHUNKS, start_tail, 0)
        lax.fori_loop(0, N_EXPERTS, wait, 0)
        lax.fori_loop(last_ref[N_EXPERTS], MAX_CHUNKS, wait_tail, 0)

    def issue(t, _):
        for k in range(TOP_K):
            _row_copy(hn_ref, t, xs_ref, dest_ref[t * TOP_K + k], sem_ref.at[0]).start()
        return 0

    lax.fori_loop(0, DISP_TM, issue, 0, unroll=8)
    for k in range(TOP_K):
        pltpu.make_async_copy(hn_ref, xs_ref.at[pl.ds(0, DISP_TM), :], sem_ref.at[0]).wait()


def _dispatch(last_chunk, dest, hn):
    return pl.pallas_call(
        _dispatch_kernel,
        name="dispatch",
        grid_spec=pltpu.PrefetchScalarGridSpec(
            num_scalar_prefetch=1,
            grid=(SEQ // DISP_TM,),
            in_specs=[
                pl.BlockSpec((DISP_TM * TOP_K,), lambda i, last: (i,), memory_space=pltpu.SMEM),
                pl.BlockSpec((DISP_TM, HALF_D), lambda i, last: (i, 0)),
            ],
            out_specs=pl.BlockSpec(memory_space=pl.ANY),
            scratch_shapes=[
                pltpu.VMEM((ROW_CHUNK, HALF_D), jnp.uint32),
                pltpu.SemaphoreType.DMA((2,)),
            ],
        ),
        out_shape=jax.ShapeDtypeStruct((MAX_ROWS, HALF_D), jnp.uint32),
        compiler_params=_cparams(("arbitrary",)),
    )(last_chunk, dest, hn)


def _experts_kernel(e_ref, c0_ref, n_ref, used_ref,
                    xs_ref, wg_ref, wu_ref, wd_ref, bg_ref, bu_ref, bd_ref, ys_ref,
                    xin_ref, xbf_ref, act_ref, acc_ref, pend_ref, sem_ref):
    s = pl.program_id(0)
    f = pl.program_id(1)
    n = n_ref[s]
    c0 = c0_ref[s]

    def rows(c, k=1):
        return pl.ds(pl.multiple_of(c * ROW_CHUNK, ROW_CHUNK), k * ROW_CHUNK)

    def cover(body):
        n4 = n // 4

        def quad(i, _):
            body(i * 4, 4)
            return 0

        lax.fori_loop(0, n4, quad, 0)

        @pl.when((n & 2) != 0)
        def _():
            body(n4 * 4, 2)

        @pl.when((n & 1) != 0)
        def _():
            body(n4 * 4 + (n & 2), 1)

    def drain():
        def wait_one(i, _):
            pltpu.make_async_copy(acc_ref.at[0:ROW_CHUNK, :], ys_ref.at[0:ROW_CHUNK, :], sem_ref.at[1]).wait()
            return 0

        lax.fori_loop(0, pend_ref[0], wait_one, 0)
        pend_ref[0] = 0

    @pl.when(jnp.logical_and(s == 0, f == 0))
    def _():
        pend_ref[0] = 0

    def fetch(first_chunk, count):
        def start(c, _):
            pltpu.make_async_copy(xs_ref.at[rows(first_chunk + c), :], xin_ref.at[rows(c), :],
                                  sem_ref.at[0]).start()
            return 0

        lax.fori_loop(0, count, start, 0)

    @pl.when(jnp.logical_and(s == 0, f == 0))
    def _():
        fetch(c0, n)

    @pl.when(jnp.logical_and(n > 0, f == 0))
    def _():
        def finish(c, _):
            pltpu.make_async_copy(xs_ref.at[rows(c0 + c), :], xin_ref.at[rows(c), :], sem_ref.at[0]).wait()
            return 0

        def unpack(c, _):
            w = xin_ref[rows(c), :]
            low = lax.bitcast_convert_type(w << 16, f32)
            high = lax.bitcast_convert_type(w & jnp.uint32(0xFFFF0000), f32)
            xbf_ref[rows(c), 0:HALF_D] = low.astype(bf16)
            xbf_ref[rows(c), HALF_D:D_MODEL] = high.astype(bf16)
            return 0

        lax.fori_loop(0, n, finish, 0)
        lax.fori_loop(0, n, unpack, 0)

    @pl.when(jnp.logical_and(s + 1 < MAX_SB, f == 1))
    def _():
        nxt = jnp.minimum(s + 1, MAX_SB - 1)
        fetch(c0_ref[nxt], n_ref[nxt])

    @pl.when(n > 0)
    def _():
        bg = bg_ref[0]
        bu = bu_ref[0]

        def up_body(c, k):
            x = xbf_ref[rows(c, k), :]
            g = jnp.dot(x, wg_ref[0].astype(bf16), preferred_element_type=f32) + bg
            u = jnp.dot(x, wu_ref[0].astype(bf16), preferred_element_type=f32) + bu
            g = jnp.minimum(g, SWIGLU_LIMIT)
            u = jnp.clip(u, -SWIGLU_LIMIT, SWIGLU_LIMIT)
            a = (u + 1.0) * (g * (1.0 / (1.0 + jnp.exp(-SWIGLU_ALPHA * g))))
            act_ref[rows(c, k), :] = a.astype(bf16)

        cover(up_body)

        @pl.when(f == 0)
        def _():
            drain()
            bias = jnp.broadcast_to(bd_ref[0], (ROW_CHUNK, D_MODEL))

            def init(c, _):
                acc_ref[rows(c), :] = bias
                return 0

            lax.fori_loop(0, n, init, 0)

        def down_body(c, k):
            acc_ref[rows(c, k), :] += jnp.dot(act_ref[rows(c, k), :], wd_ref[0].astype(bf16),
                                              preferred_element_type=f32)

        cover(down_body)

        @pl.when(f == N_FF_TILES - 1)
        def _():
            def write(c, _):
                pltpu.make_async_copy(acc_ref.at[rows(c), :], ys_ref.at[rows(c0 + c), :], sem_ref.at[1]).start()
                return 0

            lax.fori_loop(0, n, write, 0)
            pend_ref[0] = n

    @pl.when(jnp.logical_and(s == MAX_SB - 1, f == N_FF_TILES - 1))
    def _():
        drain()
        acc_ref[0:ROW_CHUNK, :] = jnp.zeros((ROW_CHUNK, D_MODEL), f32)

        def tail_copy(c):
            return pltpu.make_async_copy(acc_ref.at[0:ROW_CHUNK, :], ys_ref.at[rows(c), :], sem_ref.at[0])

        def start(c, _):
            tail_copy(c).start()
            return 0

        def finish(c, _):
            tail_copy(c).wait()
            return 0

        lax.fori_loop(used_ref[1], MAX_CHUNKS, start, 0)
        lax.fori_loop(used_ref[1], MAX_CHUNKS, finish, 0)


def _experts(sb_e, sb_c0, sb_n, n_used, xs, w_gate, b_gate, w_up, b_up, w_down, b_down):
    last = N_FF_TILES - 1

    def tile(s, f, used):
        return jnp.where(s < used[0], f, last)

    def up_map(s, f, e, c0, n, used):
        return (e[s], 0, tile(s, f, used))

    def down_map(s, f, e, c0, n, used):
        return (e[s], tile(s, f, used), 0)

    def bias_map(s, f, e, c0, n, used):
        return (e[s], 0, 0)

    return pl.pallas_call(
        _experts_kernel,
        name="experts",
        grid_spec=pltpu.PrefetchScalarGridSpec(
            num_scalar_prefetch=4,
            grid=(MAX_SB, N_FF_TILES),
            in_specs=[
                pl.BlockSpec(memory_space=pl.ANY),
                pl.BlockSpec((1, D_MODEL, FF_TILE), up_map),
                pl.BlockSpec((1, D_MODEL, FF_TILE), up_map),
                pl.BlockSpec((1, FF_TILE, D_MODEL), down_map),
                pl.BlockSpec((1, 1, FF_TILE), up_map),
                pl.BlockSpec((1, 1, FF_TILE), up_map),
                pl.BlockSpec((1, 1, D_MODEL), bias_map),
            ],
            out_specs=pl.BlockSpec(memory_space=pl.ANY),
            scratch_shapes=[
                pltpu.VMEM((SB_ROWS, HALF_D), jnp.uint32),
                pltpu.VMEM((SB_ROWS, D_MODEL), bf16),
                pltpu.VMEM((SB_ROWS, FF_TILE), bf16),
                pltpu.VMEM((SB_ROWS, D_MODEL), f32),
                pltpu.SMEM((1,), jnp.int32),
                pltpu.SemaphoreType.DMA((2,)),
            ],
        ),
        out_shape=jax.ShapeDtypeStruct((MAX_ROWS, D_MODEL), f32),
        compiler_params=_cparams(("arbitrary", "arbitrary"), EXPERTS_VMEM_LIMIT),
    )(sb_e, sb_c0, sb_n, n_used, xs, w_gate, w_up, w_down,
      b_gate.reshape(N_EXPERTS, 1, D_FF), b_up.reshape(N_EXPERTS, 1, D_FF),
      b_down.reshape(N_EXPERTS, 1, D_MODEL))


COMB_TM = 256


def _combine_kernel(dest_ref, next_ref, ys_ref, x1_ref, gate_ref, gf_ref, o_ref, buf_ref, sem_ref):
    i = pl.program_id(0)
    slot = i % 2

    def gather(rows_ref, into):
        def issue(t, _):
            for k in range(TOP_K):
                pltpu.make_async_copy(ys_ref.at[pl.ds(rows_ref[t * TOP_K + k], 1), :],
                                      buf_ref.at[into, k, pl.ds(t, 1), :], sem_ref.at[into]).start()
            return 0

        lax.fori_loop(0, COMB_TM, issue, 0, unroll=8)

    @pl.when(i == 0)
    def _():
        gather(dest_ref, 0)

    @pl.when(i + 1 < pl.num_programs(0))
    def _():
        gather(next_ref, 1 - slot)

    for k in range(TOP_K):
        pltpu.make_async_copy(ys_ref.at[pl.ds(0, COMB_TM), :], buf_ref.at[slot, k], sem_ref.at[slot]).wait()

    gate = gate_ref[...]
    acc = x1_ref[...]
    for k in range(TOP_K):
        acc = acc + gate[:, k:k + 1] * buf_ref[slot, k]
    o_ref[...] = _rms(acc, gf_ref[...])


def _combine(dest, ys, x1, gates, g_final):
    n_tiles = SEQ // COMB_TM
    return pl.pallas_call(
        _combine_kernel,
        name="combine",
        grid=(SEQ // COMB_TM,),
        in_specs=[
            pl.BlockSpec((COMB_TM * TOP_K,), lambda i: (i,), memory_space=pltpu.SMEM),
            pl.BlockSpec((COMB_TM * TOP_K,), lambda i: (jnp.minimum(i + 1, n_tiles - 1),),
                         memory_space=pltpu.SMEM),
            pl.BlockSpec(memory_space=pl.ANY),
            pl.BlockSpec((COMB_TM, D_MODEL), lambda i: (i, 0)),
            pl.BlockSpec((COMB_TM, LANES), lambda i: (i, 0)),
            pl.BlockSpec((1, D_MODEL), lambda i: (0, 0)),
        ],
        out_specs=pl.BlockSpec((COMB_TM, D_MODEL), lambda i: (i, 0)),
        out_shape=jax.ShapeDtypeStruct((SEQ, D_MODEL), f32),
        scratch_shapes=[
            pltpu.VMEM((2, TOP_K, COMB_TM, D_MODEL), f32),
            pltpu.SemaphoreType.DMA((2,)),
        ],
        compiler_params=_cparams(("arbitrary",)),
    )(dest, dest, ys, x1, gates, g_final.reshape(1, D_MODEL))


def kernel(x, g_mix, w_in, lam_re_fwd, lam_im_fwd, log_dt_fwd, b_re_fwd, b_im_fwd, c_re_fwd, c_im_fwd, lam_re_bwd, lam_im_bwd, log_dt_bwd, b_re_bwd, b_im_bwd, c_re_bwd, c_im_bwd, ssm_d, w_glu, b_glu, na_rpb, g_ssm_out, g_na_out, w_out, g_moe, w_router, b_router, w_gate, b_gate, w_up, b_up, w_down, b_down, g_final):
    x2 = x.reshape(SEQ, D_MODEL)

    u, qkv = _in_proj(x2, g_mix[0], w_in[0].astype(bf16))

    s5_w, s5_dec = _s5_weights(
        (lam_re_fwd[0], lam_im_fwd[0], log_dt_fwd[0], b_re_fwd[0], b_im_fwd[0], c_re_fwd[0], c_im_fwd[0]),
        (lam_re_bwd[0], lam_im_bwd[0], log_dt_bwd[0], b_re_bwd[0], b_im_bwd[0], c_re_bwd[0], c_im_bwd[0]))
    y = _s5(u, s5_w, s5_dec, ssm_d[0])
    ssm_n = _glu(y, w_glu[0].astype(bf16), b_glu[0], g_ssm_out[0])

    y_na = _natten(qkv, _na_bias(na_rpb[0]))

    w_router_pad = jnp.zeros((D_MODEL, LANES), bf16).at[:, :N_EXPERTS].set(w_router[0].astype(bf16))
    b_router_pad = jnp.full((1, LANES), NEG_BIG, f32).at[0, :N_EXPERTS].set(b_router[0].astype(f32))
    x1, hn, logits = _out_proj(ssm_n, y_na, x2, g_na_out[0], w_out[0].astype(bf16), g_moe[0],
                               w_router_pad, b_router_pad)

    meta, gates, counts = _route(logits)
    dest, last_chunk, sb_e, sb_c0, sb_n, n_used = _routing_tables(meta, counts)

    xs = _dispatch(last_chunk, dest, hn)
    ys = _experts(sb_e, sb_c0, sb_n, n_used, xs, w_gate[0], b_gate[0], w_up[0], b_up[0],
                  w_down[0], b_down[0])
    out = _combine(dest, ys, x1, gates, g_final)
    return out.reshape(x.shape)
```

```python
import functools
import math

import jax
import jax.numpy as jnp
from jax import lax
from jax.experimental import pallas as pl
from jax.experimental.pallas import tpu as pltpu

f32 = jnp.float32
bf16 = jnp.bfloat16

D_MODEL = 2048
SEQ = 8192
SSM_WIDTH = 1024
NA_WIDTH = 1024
SSM_GROUP = 16
SSM_GROUPS = 64
SSM_STATE = 64
NA_HEAD_DIM = 64
NA_HEADS = 16
GRID_W = 64
GRID_ROWS = SEQ // GRID_W
NA_KH = 8
NA_KW = 16
N_EXPERTS = 32
TOP_K = 4
D_FF = 2048
SWIGLU_LIMIT = 7.0
SWIGLU_ALPHA = 1.702
RMS_EPS = 1e-5

LANES = 128
HALF_D = D_MODEL // 2
NEG_BIG = -1e30

CHUNK_T = 16
N_CHUNKS = SEQ // CHUNK_T
GROUPS_PER_BLOCK = LANES // SSM_GROUP
N_LANE_BLOCKS = SSM_WIDTH // LANES
CAT_W = CHUNK_T * LANES
STATE_W = GROUPS_PER_BLOCK * SSM_STATE

ROW_CHUNK = 128
CHUNKS_PER_SB = 12
SB_ROWS = ROW_CHUNK * CHUNKS_PER_SB
MAX_CHUNKS = SEQ * TOP_K // ROW_CHUNK + N_EXPERTS
MAX_ROWS = MAX_CHUNKS * ROW_CHUNK
MAX_SB = MAX_CHUNKS // CHUNKS_PER_SB + N_EXPERTS
FF_TILE = 512
N_FF_TILES = D_FF // FF_TILE

VMEM_LIMIT = 56 * 1024 * 1024
EXPERTS_VMEM_LIMIT = 60 * 1024 * 1024


def _cparams(semantics, vmem=VMEM_LIMIT):
    return pltpu.CompilerParams(dimension_semantics=semantics, vmem_limit_bytes=vmem)


def _rms(x, g):
    return x * lax.rsqrt(jnp.mean(x * x, axis=-1, keepdims=True) + RMS_EPS) * g


IN_TM = 1024
IN_TN = 1024


def _in_proj_kernel(x_ref, g_ref, w_ref, u_ref, qkv_ref, h_ref):
    j = pl.program_id(1)

    @pl.when(j == 0)
    def _():
        h_ref[...] = _rms(x_ref[...], g_ref[...]).astype(bf16)

    acc = jnp.dot(h_ref[...], w_ref[...], preferred_element_type=f32)

    @pl.when(j == 0)
    def _():
        u_ref[...] = acc

    @pl.when(j > 0)
    def _():
        qkv_ref[...] = acc.astype(bf16)


def _in_proj(x, g_mix, w_in_bf):
    n_out = w_in_bf.shape[1]
    return pl.pallas_call(
        _in_proj_kernel,
        name="in_proj",
        grid=(SEQ // IN_TM, n_out // IN_TN),
        in_specs=[
            pl.BlockSpec((IN_TM, D_MODEL), lambda i, j: (i, 0)),
            pl.BlockSpec((1, D_MODEL), lambda i, j: (0, 0)),
            pl.BlockSpec((D_MODEL, IN_TN), lambda i, j: (0, j)),
        ],
        out_specs=[
            pl.BlockSpec((IN_TM, IN_TN), lambda i, j: (i, 0)),
            pl.BlockSpec((IN_TM, IN_TN), lambda i, j: (i, jnp.maximum(j - 1, 0))),
        ],
        out_shape=[
            jax.ShapeDtypeStruct((SEQ, SSM_WIDTH), f32),
            jax.ShapeDtypeStruct((SEQ, 3 * NA_WIDTH), bf16),
        ],
        scratch_shapes=[pltpu.VMEM((IN_TM, D_MODEL), bf16)],
        compiler_params=_cparams(("arbitrary", "arbitrary")),
    )(x, g_mix.reshape(1, D_MODEL), w_in_bf)


def _cmul(ar, ai, br, bi):
    return ar * br - ai * bi, ar * bi + ai * br


def _s5_discretise(lam_re, lam_im, log_dt, b_re, b_im, c_re, c_im):
    a = jnp.minimum(lam_re.astype(f32), -1e-4)
    w = lam_im.astype(f32)
    dt = jnp.exp(log_dt.astype(f32))[:, None]
    steps = jnp.arange(CHUNK_T + 1, dtype=f32)[:, None, None]
    mag = jnp.exp((a * dt)[None] * steps)
    ang = (w * dt)[None] * steps
    pw = (mag * jnp.cos(ang), mag * jnp.sin(ang))
    xr, xi = pw[0][1] - 1.0, pw[1][1]
    den = a * a + w * w
    qr, qi = (xr * a + xi * w) / den, (xi * a - xr * w) / den
    bb = _cmul(qr[..., None], qi[..., None], b_re.astype(f32), b_im.astype(f32))
    return pw, bb, (c_re.astype(f32), c_im.astype(f32))


def _pair_blockdiag(m):
    z = jnp.zeros_like(m[..., 0, :, :])
    top = jnp.concatenate([m[..., 0, :, :], z], axis=-1)
    bot = jnp.concatenate([z, m[..., 1, :, :]], axis=-1)
    return jnp.concatenate([top, bot], axis=-2)


def _s5_weights(fwd, bwd):
    hi = lax.Precision.HIGH
    exact = lax.Precision.HIGHEST
    t = CHUNK_T
    pw_f, bb_f, c_f = _s5_discretise(*fwd)
    pw_b, bb_b, c_b = _s5_discretise(*bwd)
    pairs = (N_LANE_BLOCKS, GROUPS_PER_BLOCK // 2, 2)

    x = jnp.arange(t * SSM_GROUP)
    t_of_x = x // SSM_GROUP
    tile_c = (jnp.arange(SSM_GROUP)[:, None] == x[None, :] % SSM_GROUP).astype(f32)

    def c_over_x(c):
        return [jnp.einsum('gcp,cx->gpx', part, tile_c, precision=exact) for part in c]

    def c_times_power(c_x, pw, power_of_x):
        rep = (jnp.arange(t + 1)[:, None] == power_of_x[None, :]).astype(f32)
        pw_x = [jnp.einsum('jgp,jx->gpx', part, rep, precision=exact) for part in pw]
        return _cmul(c_x[0], c_x[1], pw_x[0], pw_x[1])

    cx_f, cx_b = c_over_x(c_f), c_over_x(c_b)

    def lag_kernel(c_x, pw, bb, power_of_x):
        m_re, m_im = c_times_power(c_x, pw, power_of_x)
        m = jnp.concatenate([m_re, m_im], axis=1)
        b = jnp.concatenate([jnp.swapaxes(bb[0], 1, 2), -jnp.swapaxes(bb[1], 1, 2)], axis=-1)
        return jnp.einsum('gdp,gpx->gdx', b, m, precision=hi)

    kf = lag_kernel(cx_f, pw_f, bb_f, t_of_x)
    kb = lag_kernel(cx_b, pw_b, bb_b, t - 1 - t_of_x)
    keep = (t - 1) * SSM_GROUP
    k_lag = jnp.concatenate([kb[..., :keep], kb[..., keep:] + kf[..., :SSM_GROUP], kf[..., SSM_GROUP:]],
                            axis=-1)
    w_intra = jnp.stack([k_lag[..., (t - 1 - ti) * SSM_GROUP:(2 * t - 1 - ti) * SSM_GROUP]
                         for ti in range(t)], axis=1)
    w_intra = _pair_blockdiag(w_intra.astype(bf16).reshape(*pairs, t * SSM_GROUP, t * SSM_GROUP))

    def state_in(pw_t, bb):
        pw_g = [jnp.transpose(part, (1, 0, 2))[:, :, None, :] for part in pw_t]
        bb_g = [jnp.swapaxes(part, 1, 2)[:, None] for part in bb]
        return [_pair_blockdiag(part.astype(bf16).reshape(*pairs, t * SSM_GROUP, SSM_STATE))
                for part in _cmul(pw_g[0], pw_g[1], bb_g[0], bb_g[1])]

    w_in = jnp.concatenate(state_in([part[:t][::-1] for part in pw_f], bb_f)
                           + state_in([part[:t] for part in pw_b], bb_b), axis=-1)

    def state_out(c_x, pw, power_of_x):
        m_re, m_im = c_times_power(c_x, pw, power_of_x)
        return [_pair_blockdiag(part.astype(bf16).reshape(*pairs, SSM_STATE, t * SSM_GROUP))
                for part in (m_re, -m_im)]

    w_out = jnp.concatenate(state_out(cx_f, pw_f, t_of_x + 1) + state_out(cx_b, pw_b, t - t_of_x), axis=-2)

    def decay(a):
        return a.reshape(N_LANE_BLOCKS, 1, STATE_W)

    dec = jnp.concatenate([decay(pw_f[0][t]), decay(pw_f[1][t]), decay(pw_b[0][t]), decay(pw_b[1][t])],
                          axis=-1)
    return (w_in, w_intra, w_out), dec.astype(f32)


def _gelu_tanh(x):
    return 0.5 * x * (1.0 + jnp.tanh(math.sqrt(2.0 / math.pi) * (x + 0.044715 * (x * x * x))))


PAIRS_PER_BLOCK = GROUPS_PER_BLOCK // 2
PAIR_W = 2 * CHUNK_T * SSM_GROUP


def _s5_kernel(u_ref, win_ref, wintra_ref, wout_ref, dec_ref, d_ref, y_ref,
               cat_ref, catp_ref, st_ref, perm_ref):
    sw = STATE_W

    @pl.when(pl.program_id(0) == 0)
    def _():
        def strip(t, _):
            row = lax.broadcasted_iota(jnp.int32, (LANES, CAT_W), 0)
            col = lax.broadcasted_iota(jnp.int32, (LANES, CAT_W), 1)
            target = (row // SSM_GROUP) * (CHUNK_T * SSM_GROUP) + t * SSM_GROUP + row % SSM_GROUP
            perm_ref[pl.ds(pl.multiple_of(t * LANES, LANES), LANES), :] = (
                jnp.where(col == target, 1.0, 0.0).astype(bf16))
            return 0

        lax.fori_loop(0, CHUNK_T, strip, 0)

    for t in range(CHUNK_T):
        cat_ref[:, t * LANES:(t + 1) * LANES] = u_ref[pl.ds(t, N_CHUNKS, stride=CHUNK_T), :].astype(bf16)
    catp_ref[...] = jnp.dot(cat_ref[...], perm_ref[...], preferred_element_type=f32).astype(bf16)

    for p in range(PAIRS_PER_BLOCK):
        z = jnp.dot(catp_ref[:, p * PAIR_W:(p + 1) * PAIR_W], win_ref[0, p], preferred_element_type=f32)
        for part in range(4):
            st_ref[:, part * sw + p * LANES:part * sw + (p + 1) * LANES] = z[:, part * LANES:(part + 1) * LANES]

    afr = dec_ref[0, :, 0 * sw:1 * sw]
    afi = dec_ref[0, :, 1 * sw:2 * sw]
    abr = dec_ref[0, :, 2 * sw:3 * sw]
    abi = dec_ref[0, :, 3 * sw:4 * sw]

    def step(i, carry):
        sfr, sfi, sbr, sbi = carry
        r = N_CHUNKS - 1 - i
        zf = st_ref[pl.ds(i, 1), 0:2 * sw]
        zb = st_ref[pl.ds(r, 1), 2 * sw:4 * sw]
        st_ref[pl.ds(i, 1), 0:2 * sw] = jnp.concatenate([sfr, sfi], axis=-1)
        st_ref[pl.ds(r, 1), 2 * sw:4 * sw] = jnp.concatenate([sbr, sbi], axis=-1)
        nfr = afr * sfr - afi * sfi + zf[:, :sw]
        nfi = afi * sfr + afr * sfi + zf[:, sw:]
        nbr = abr * sbr - abi * sbi + zb[:, :sw]
        nbi = abi * sbr + abr * sbi + zb[:, sw:]
        return nfr, nfi, nbr, nbi

    zero = jnp.zeros((1, sw), f32)
    lax.fori_loop(0, N_CHUNKS, step, (zero, zero, zero, zero))

    for p in range(PAIRS_PER_BLOCK):
        s_in = jnp.concatenate(
            [st_ref[:, part * sw + p * LANES:part * sw + (p + 1) * LANES] for part in range(4)], axis=-1)
        yp = jnp.dot(catp_ref[:, p * PAIR_W:(p + 1) * PAIR_W], wintra_ref[0, p], preferred_element_type=f32)
        yp = yp + jnp.dot(s_in.astype(bf16), wout_ref[0, p], preferred_element_type=f32)
        cat_ref[:, p * PAIR_W:(p + 1) * PAIR_W] = yp.astype(bf16)

    st_ref[...] = lax.dot_general(cat_ref[...], perm_ref[...], (((1,), (1,)), ((), ())),
                                  preferred_element_type=f32)
    d = d_ref[...]
    for t in range(CHUNK_T):
        rows = pl.ds(t, N_CHUNKS, stride=CHUNK_T)
        v = st_ref[:, t * LANES:(t + 1) * LANES] + d * u_ref[rows, :]
        y_ref[rows, :] = _gelu_tanh(v)


def _s5(u, weights, dec, ssm_d):
    w_spec = pl.BlockSpec((1, PAIRS_PER_BLOCK, PAIR_W, PAIR_W), lambda j: (j, 0, 0, 0))
    return pl.pallas_call(
        _s5_kernel,
        name="s5_scan",
        grid=(N_LANE_BLOCKS,),
        in_specs=[
            pl.BlockSpec((SEQ, LANES), lambda j: (0, j)),
            w_spec, w_spec, w_spec,
            pl.BlockSpec((1, 1, 4 * STATE_W), lambda j: (j, 0, 0)),
            pl.BlockSpec((1, LANES), lambda j: (0, j)),
        ],
        out_specs=pl.BlockSpec((SEQ, LANES), lambda j: (0, j)),
        out_shape=jax.ShapeDtypeStruct((SEQ, SSM_WIDTH), f32),
        scratch_shapes=[
            pltpu.VMEM((N_CHUNKS, CAT_W), bf16),
            pltpu.VMEM((N_CHUNKS, CAT_W), bf16),
            pltpu.VMEM((N_CHUNKS, 4 * STATE_W), f32),
            pltpu.VMEM((CAT_W, CAT_W), bf16),
        ],
        compiler_params=_cparams(("arbitrary",)),
    )(u, *weights, dec, ssm_d.reshape(1, SSM_WIDTH))


GLU_TM = 1024


def _glu_kernel(y_ref, w_ref, b_ref, g_ref, o_ref):
    y = y_ref[...]
    z = jnp.dot(y.astype(bf16), w_ref[...], preferred_element_type=f32) + b_ref[...]
    o = y * (1.0 / (1.0 + jnp.exp(-z)))
    o_ref[...] = _rms(o, g_ref[...]).astype(bf16)


def _glu(y, w_glu_bf, b_glu, g_ssm_out):
    return pl.pallas_call(
        _glu_kernel,
        name="glu_norm",
        grid=(SEQ // GLU_TM,),
        in_specs=[
            pl.BlockSpec((GLU_TM, SSM_WIDTH), lambda i: (i, 0)),
            pl.BlockSpec((SSM_WIDTH, SSM_WIDTH), lambda i: (0, 0)),
            pl.BlockSpec((1, SSM_WIDTH), lambda i: (0, 0)),
            pl.BlockSpec((1, SSM_WIDTH), lambda i: (0, 0)),
        ],
        out_specs=pl.BlockSpec((GLU_TM, SSM_WIDTH), lambda i: (i, 0)),
        out_shape=jax.ShapeDtypeStruct((SEQ, SSM_WIDTH), bf16),
        compiler_params=_cparams(("arbitrary",)),
    )(y, w_glu_bf, b_glu.reshape(1, SSM_WIDTH), g_ssm_out.reshape(1, SSM_WIDTH))


NA_ROWS_PER_STEP = 16
NA_WIN = NA_KH * GRID_W
HEADS_PER_BLOCK = LANES // NA_HEAD_DIM


def _na_bias(rpb):
    c = jnp.arange(GRID_W)
    col_start = jnp.clip(c - NA_KW // 2, 0, GRID_W - NA_KW)
    valid = (c[None, :] >= col_start[:, None]) & (c[None, :] < col_start[:, None] + NA_KW)
    dc = jnp.clip(c[None, :] - c[:, None], -(NA_KW - 1), NA_KW - 1) + (NA_KW - 1)
    sel = (dc[None] == jnp.arange(2 * NA_KW - 1)[:, None, None]).astype(f32)
    tab = jnp.einsum('hrc,cqk->hqrk', rpb.astype(f32), sel, precision=lax.Precision.HIGHEST)
    tab = jnp.where(valid[None, :, None, :], tab, NEG_BIG)
    tab = tab.reshape(NA_HEADS, GRID_W, (2 * NA_KH - 1) * GRID_W)
    return jnp.stack([tab[..., (NA_KH - 1 - v) * GRID_W:(2 * NA_KH - 1 - v) * GRID_W]
                      for v in range(NA_KH)], axis=1)


def _natten_kernel(q_ref, k_ref, v_ref, b_ref, o_ref, s_ref, p_ref):
    rb = pl.program_id(1)
    lane = lax.broadcasted_iota(jnp.int32, (GRID_W, LANES), 1)
    head0 = lane < NA_HEAD_DIM
    scale = NA_HEAD_DIM ** -0.5

    starts, variants = [], []
    for i in range(NA_ROWS_PER_STEP):
        r = rb * NA_ROWS_PER_STEP + i
        rs = jnp.clip(r - NA_KH // 2, 0, GRID_ROWS - NA_KH)
        starts.append(pl.multiple_of(rs * GRID_W, GRID_W))
        variants.append(r - rs)

    for i in range(NA_ROWS_PER_STEP):
        q = q_ref[i * GRID_W:(i + 1) * GRID_W, :] * scale
        kw = k_ref[pl.ds(starts[i], NA_WIN), :]
        for h in range(HEADS_PER_BLOCK):
            qh = jnp.where(head0 if h == 0 else ~head0, q, jnp.zeros_like(q))
            s = lax.dot_general(qh, kw, (((1,), (1,)), ((), ())), preferred_element_type=f32)
            s_ref[i * HEADS_PER_BLOCK + h] = s + b_ref[h, variants[i]]

    inv_sums = []
    for t in range(NA_ROWS_PER_STEP * HEADS_PER_BLOCK):
        s = s_ref[t]
        p = jnp.exp(s - jnp.max(s, axis=-1, keepdims=True))
        inv_sums.append(1.0 / jnp.sum(p, axis=-1, keepdims=True))
        p_ref[t] = p.astype(bf16)

    for i in range(NA_ROWS_PER_STEP):
        vw = v_ref[pl.ds(starts[i], NA_WIN), :]
        outs = []
        for h in range(HEADS_PER_BLOCK):
            t = i * HEADS_PER_BLOCK + h
            outs.append(jnp.dot(p_ref[t], vw, preferred_element_type=f32) * inv_sums[t])
        o_ref[i * GRID_W:(i + 1) * GRID_W, :] = jnp.where(head0, outs[0], outs[1]).astype(bf16)


def _natten(qkv, bias):
    tm = NA_ROWS_PER_STEP * GRID_W
    n_hb = NA_WIDTH // LANES
    return pl.pallas_call(
        _natten_kernel,
        name="natten",
        grid=(n_hb, GRID_ROWS // NA_ROWS_PER_STEP),
        in_specs=[
            pl.BlockSpec((tm, LANES), lambda h, r: (r, h)),
            pl.BlockSpec((SEQ, LANES), lambda h, r: (0, n_hb + h)),
            pl.BlockSpec((SEQ, LANES), lambda h, r: (0, 2 * n_hb + h)),
            pl.BlockSpec((HEADS_PER_BLOCK, NA_KH, GRID_W, NA_WIN), lambda h, r: (h, 0, 0, 0)),
        ],
        out_specs=pl.BlockSpec((tm, LANES), lambda h, r: (r, h)),
        out_shape=jax.ShapeDtypeStruct((SEQ, NA_WIDTH), bf16),
        scratch_shapes=[
            pltpu.VMEM((NA_ROWS_PER_STEP * HEADS_PER_BLOCK, GRID_W, NA_WIN), f32),
            pltpu.VMEM((NA_ROWS_PER_STEP * HEADS_PER_BLOCK, GRID_W, NA_WIN), bf16),
        ],
        compiler_params=_cparams(("arbitrary", "arbitrary")),
    )(qkv, qkv, qkv, bias)


OUT_TM = 512


def _out_proj_kernel(ssm_ref, na_ref, x_ref, gna_ref, w_ref, gmoe_ref, wr_ref, br_ref,
                     x1_ref, hn_ref, lg_ref):
    na = _rms(na_ref[...].astype(f32), gna_ref[...]).astype(bf16)
    y = jnp.dot(ssm_ref[...], w_ref[0:SSM_WIDTH, :], preferred_element_type=f32)
    y = y + jnp.dot(na, w_ref[SSM_WIDTH:, :], preferred_element_type=f32)
    x1 = x_ref[...] + y
    x1_ref[...] = x1
    hn = _rms(x1, gmoe_ref[...]).astype(bf16)
    lg_ref[...] = jnp.dot(hn, wr_ref[...], preferred_element_type=f32) + br_ref[...]
    bits = lax.bitcast_convert_type(hn.astype(f32), jnp.uint32)
    hn_ref[...] = (bits[:, HALF_D:] & jnp.uint32(0xFFFF0000)) | (bits[:, :HALF_D] >> 16)


def _out_proj(ssm_n, y_na, x, g_na_out, w_out_bf, g_moe, w_router_pad, b_router_pad):
    row = lambda i: (i, 0)
    fixed = lambda i: (0, 0)
    return pl.pallas_call(
        _out_proj_kernel,
        name="out_proj",
        grid=(SEQ // OUT_TM,),
        in_specs=[
            pl.BlockSpec((OUT_TM, SSM_WIDTH), row),
            pl.BlockSpec((OUT_TM, NA_WIDTH), row),
            pl.BlockSpec((OUT_TM, D_MODEL), row),
            pl.BlockSpec((1, NA_WIDTH), fixed),
            pl.BlockSpec((D_MODEL, D_MODEL), fixed),
            pl.BlockSpec((1, D_MODEL), fixed),
            pl.BlockSpec((D_MODEL, LANES), fixed),
            pl.BlockSpec((1, LANES), fixed),
        ],
        out_specs=[
            pl.BlockSpec((OUT_TM, D_MODEL), row),
            pl.BlockSpec((OUT_TM, HALF_D), row),
            pl.BlockSpec((OUT_TM, LANES), row),
        ],
        out_shape=[
            jax.ShapeDtypeStruct((SEQ, D_MODEL), f32),
            jax.ShapeDtypeStruct((SEQ, HALF_D), jnp.uint32),
            jax.ShapeDtypeStruct((SEQ, LANES), f32),
        ],
        compiler_params=_cparams(("arbitrary",)),
    )(ssm_n, y_na, x, g_na_out.reshape(1, NA_WIDTH), w_out_bf, g_moe.reshape(1, D_MODEL),
      w_router_pad, b_router_pad)


ROUTE_TM = 1024


def _route_kernel(lg_ref, tri_ref, meta_ref, gate_ref, cnt_ref, carry_ref):
    i = pl.program_id(0)

    @pl.when(i == 0)
    def _():
        carry_ref[...] = jnp.zeros_like(carry_ref)

    lane = lax.broadcasted_iota(jnp.int32, (ROUTE_TM, LANES), 1)
    lane_f = lane.astype(f32)
    work = lg_ref[...]
    vals, hits = [], []
    for _ in range(TOP_K):
        m = jnp.max(work, axis=-1, keepdims=True)
        idx = jnp.min(jnp.where(work == m, lane_f, float(LANES)), axis=-1, keepdims=True)
        hit = lane_f == idx
        vals.append(m)
        hits.append((idx, hit))
        work = jnp.where(hit, -jnp.inf, work)

    exps = [jnp.exp(v - vals[0]) for v in vals]
    denom = exps[0] + exps[1] + exps[2] + exps[3]

    onehot = jnp.zeros((ROUTE_TM, LANES), f32)
    for _, hit in hits:
        onehot = onehot + hit.astype(f32)
    before = jnp.dot(tri_ref[...], onehot.astype(bf16), preferred_element_type=f32) + carry_ref[...]

    meta = jnp.zeros((ROUTE_TM, LANES), jnp.int32)
    gate = jnp.zeros((ROUTE_TM, LANES), f32)
    for k, (idx, hit) in enumerate(hits):
        rank = jnp.sum(jnp.where(hit, before, 0.0), axis=-1, keepdims=True).astype(jnp.int32)
        meta = jnp.where(lane == k, idx.astype(jnp.int32), meta)
        meta = jnp.where(lane == TOP_K + k, rank, meta)
        gate = jnp.where(lane == k, exps[k] / denom, gate)
    meta_ref[...] = meta
    gate_ref[...] = gate
    carry_ref[...] += jnp.sum(onehot, axis=0, keepdims=True)
    cnt_ref[...] = carry_ref[...]


def _route(logits):
    tri = (jnp.arange(ROUTE_TM)[:, None] > jnp.arange(ROUTE_TM)[None, :]).astype(bf16)
    row = lambda i: (i, 0)
    return pl.pallas_call(
        _route_kernel,
        name="route",
        grid=(SEQ // ROUTE_TM,),
        in_specs=[
            pl.BlockSpec((ROUTE_TM, LANES), row),
            pl.BlockSpec((ROUTE_TM, ROUTE_TM), lambda i: (0, 0)),
        ],
        out_specs=[
            pl.BlockSpec((ROUTE_TM, LANES), row),
            pl.BlockSpec((ROUTE_TM, LANES), row),
            pl.BlockSpec((1, LANES), lambda i: (0, 0)),
        ],
        out_shape=[
            jax.ShapeDtypeStruct((SEQ, LANES), jnp.int32),
            jax.ShapeDtypeStruct((SEQ, LANES), f32),
            jax.ShapeDtypeStruct((1, LANES), f32),
        ],
        scratch_shapes=[pltpu.VMEM((1, LANES), f32)],
        compiler_params=_cparams(("arbitrary",)),
    )(logits, tri)


def _routing_tables(meta, counts):
    idx = meta[:, :TOP_K]
    rank = meta[:, TOP_K:2 * TOP_K]
    cnt = counts[0, :N_EXPERTS].astype(jnp.int32)
    n_chunks = (cnt + ROW_CHUNK - 1) // ROW_CHUNK
    chunk_base = jnp.cumsum(n_chunks) - n_chunks
    dest = (chunk_base[idx] * ROW_CHUNK + rank).astype(jnp.int32).reshape(-1)
    total_chunks = jnp.sum(n_chunks)
    last_chunk = jnp.concatenate([jnp.where(cnt > 0, chunk_base + n_chunks - 1, -1),
                                  total_chunks[None]]).astype(jnp.int32)

    n_sb = (n_chunks + CHUNKS_PER_SB - 1) // CHUNKS_PER_SB
    sb_end = jnp.cumsum(n_sb)
    sb_start = sb_end - n_sb
    n_used = sb_end[-1]
    s = jnp.arange(MAX_SB)
    s_eff = jnp.minimum(s, n_used - 1)
    e = jnp.minimum(jnp.searchsorted(sb_end, s_eff, side='right'), N_EXPERTS - 1)
    kk = s_eff - sb_start[e]
    per_sb = n_chunks[e] // jnp.maximum(n_sb[e], 1)
    extra = n_chunks[e] - per_sb * n_sb[e]
    sb_chunk0 = chunk_base[e] + kk * per_sb + jnp.minimum(kk, extra)
    sb_n = jnp.where(s < n_used, per_sb + (kk < extra), 0)
    used = jnp.stack([n_used, total_chunks]).astype(jnp.int32)
    return (dest, last_chunk, e.astype(jnp.int32), sb_chunk0.astype(jnp.int32),
            sb_n.astype(jnp.int32), used)


DISP_TM = 256


def _row_copy(src, src_row, dst, dst_row, sem):
    return pltpu.make_async_copy(src.at[pl.ds(src_row, 1), :], dst.at[pl.ds(dst_row, 1), :], sem)


def _dispatch_kernel(last_ref, dest_ref, hn_ref, xs_ref, zero_ref, sem_ref):
    i = pl.program_id(0)

    @pl.when(i == 0)
    def _():
        zero_ref[...] = jnp.zeros_like(zero_ref)

        def chunk_copy(c):
            row0 = pl.multiple_of(c * ROW_CHUNK, ROW_CHUNK)
            return pltpu.make_async_copy(zero_ref, xs_ref.at[pl.ds(row0, ROW_CHUNK), :], sem_ref.at[1])

        def start(e, _):
            @pl.when(last_ref[e] >= 0)
            def _():
                chunk_copy(last_ref[e]).start()
            return 0

        def wait(e, _):
            @pl.when(last_ref[e] >= 0)
            def _():
                chunk_copy(last_ref[e]).wait()
            return 0

        def start_tail(c, _):
            chunk_copy(c).start()
            return 0

        def wait_tail(c, _):
            chunk_copy(c).wait()
            return 0

        lax.fori_loop(0, N_EXPERTS, start, 0)
        lax.fori_loop(last_ref[N_EXPERTS], MAX_CHUNKS, start_tail, 0)
        lax.fori_loop(0, N_EXPERTS, wait, 0)
        lax.fori_loop(last_ref[N_EXPERTS], MAX_CHUNKS, wait_tail, 0)

    def issue(t, _):
        for k in range(TOP_K):
            _row_copy(hn_ref, t, xs_ref, dest_ref[t * TOP_K + k], sem_ref.at[0]).start(priority=k % 2)
        return 0

    lax.fori_loop(0, DISP_TM, issue, 0, unroll=8)
    for k in range(TOP_K):
        pltpu.make_async_copy(hn_ref, xs_ref.at[pl.ds(0, DISP_TM), :], sem_ref.at[0]).wait()


def _dispatch(last_chunk, dest, hn):
    return pl.pallas_call(
        _dispatch_kernel,
        name="dispatch",
        grid_spec=pltpu.PrefetchScalarGridSpec(
            num_scalar_prefetch=1,
            grid=(SEQ // DISP_TM,),
            in_specs=[
                pl.BlockSpec((DISP_TM * TOP_K,), lambda i, last: (i,), memory_space=pltpu.SMEM),
                pl.BlockSpec((DISP_TM, HALF_D), lambda i, last: (i, 0)),
            ],
            out_specs=pl.BlockSpec(memory_space=pl.ANY),
            scratch_shapes=[
                pltpu.VMEM((ROW_CHUNK, HALF_D), jnp.uint32),
                pltpu.SemaphoreType.DMA((2,)),
            ],
        ),
        out_shape=jax.ShapeDtypeStruct((MAX_ROWS, HALF_D), jnp.uint32),
        compiler_params=_cparams(("arbitrary",)),
    )(last_chunk, dest, hn)


def _experts_kernel(e_ref, c0_ref, n_ref, used_ref,
                    xs_ref, wg_ref, wu_ref, wd_ref, bg_ref, bu_ref, bd_ref, ys_ref,
                    xin_ref, xbf_ref, act_ref, acc_ref, pend_ref, sem_ref):
    s = pl.program_id(0)
    f = pl.program_id(1)
    n = n_ref[s]
    c0 = c0_ref[s]

    def rows(c, k=1):
        return pl.ds(pl.multiple_of(c * ROW_CHUNK, ROW_CHUNK), k * ROW_CHUNK)

    def cover(body):
        n4 = n // 4

        def quad(i, _):
            body(i * 4, 4)
            return 0

        lax.fori_loop(0, n4, quad, 0)

        @pl.when((n & 2) != 0)
        def _():
            body(n4 * 4, 2)

        @pl.when((n & 1) != 0)
        def _():
            body(n4 * 4 + (n & 2), 1)

    def drain():
        def wait_one(i, _):
            pltpu.make_async_copy(acc_ref.at[0:ROW_CHUNK, :], ys_ref.at[0:ROW_CHUNK, :], sem_ref.at[1]).wait()
            return 0

        lax.fori_loop(0, pend_ref[0], wait_one, 0)
        pend_ref[0] = 0

    @pl.when(jnp.logical_and(s == 0, f == 0))
    def _():
        pend_ref[0] = 0

    def fetch(first_chunk, count):
        def start(c, _):
            pltpu.make_async_copy(xs_ref.at[rows(first_chunk + c), :], xin_ref.at[rows(c), :],
                                  sem_ref.at[0]).start()
            return 0

        lax.fori_loop(0, count, start, 0)

    @pl.when(jnp.logical_and(s == 0, f == 0))
    def _():
        fetch(c0, n)

    @pl.when(jnp.logical_and(n > 0, f == 0))
    def _():
        def finish(c, _):
            pltpu.make_async_copy(xs_ref.at[rows(c0 + c), :], xin_ref.at[rows(c), :], sem_ref.at[0]).wait()
            return 0

        def unpack(c, _):
            w = xin_ref[rows(c), :]
            low = lax.bitcast_convert_type(w << 16, f32)
            high = lax.bitcast_convert_type(w & jnp.uint32(0xFFFF0000), f32)
            xbf_ref[rows(c), 0:HALF_D] = low.astype(bf16)
            xbf_ref[rows(c), HALF_D:D_MODEL] = high.astype(bf16)
            return 0

        lax.fori_loop(0, n, finish, 0)
        lax.fori_loop(0, n, unpack, 0)

    @pl.when(jnp.logical_and(s + 1 < MAX_SB, f == 1))
    def _():
        nxt = jnp.minimum(s + 1, MAX_SB - 1)
        fetch(c0_ref[nxt], n_ref[nxt])

    @pl.when(n > 0)
    def _():
        bg = bg_ref[0]
        bu = bu_ref[0]

        def up_body(c, k):
            x = xbf_ref[rows(c, k), :]
            g = jnp.dot(x, wg_ref[0].astype(bf16), preferred_element_type=f32) + bg
            u = jnp.dot(x, wu_ref[0].astype(bf16), preferred_element_type=f32) + bu
            g = jnp.minimum(g, SWIGLU_LIMIT)
            u = jnp.clip(u, -SWIGLU_LIMIT, SWIGLU_LIMIT)
            a = (u + 1.0) * (g * (1.0 / (1.0 + jnp.exp(-SWIGLU_ALPHA * g))))
            act_ref[rows(c, k), :] = a.astype(bf16)

        cover(up_body)

        @pl.when(f == 0)
        def _():
            drain()
            bias = jnp.broadcast_to(bd_ref[0], (ROW_CHUNK, D_MODEL))

            def init(c, _):
                acc_ref[rows(c), :] = bias
                return 0

            lax.fori_loop(0, n, init, 0)

        def down_body(c, k):
            acc_ref[rows(c, k), :] += jnp.dot(act_ref[rows(c, k), :], wd_ref[0].astype(bf16),
                                              preferred_element_type=f32)

        cover(down_body)

        @pl.when(f == N_FF_TILES - 1)
        def _():
            def write(c, _):
                pltpu.make_async_copy(acc_ref.at[rows(c), :], ys_ref.at[rows(c0 + c), :], sem_ref.at[1]).start()
                return 0

            lax.fori_loop(0, n, write, 0)
            pend_ref[0] = n

    @pl.when(jnp.logical_and(s == MAX_SB - 1, f == N_FF_TILES - 1))
    def _():
        drain()
        acc_ref[0:ROW_CHUNK, :] = jnp.zeros((ROW_CHUNK, D_MODEL), f32)

        def tail_copy(c):
            return pltpu.make_async_copy(acc_ref.at[0:ROW_CHUNK, :], ys_ref.at[rows(c), :], sem_ref.at[0])

        def start(c, _):
            tail_copy(c).start()
            return 0

        def finish(c, _):
            tail_copy(c).wait()
            return 0

        lax.fori_loop(used_ref[1], MAX_CHUNKS, start, 0)
        lax.fori_loop(used_ref[1], MAX_CHUNKS, finish, 0)


def _experts(sb_e, sb_c0, sb_n, n_used, xs, w_gate, b_gate, w_up, b_up, w_down, b_down):
    last = N_FF_TILES - 1

    def tile(s, f, used):
        return jnp.where(s < used[0], f, last)

    def up_map(s, f, e, c0, n, used):
        return (e[s], 0, tile(s, f, used))

    def down_map(s, f, e, c0, n, used):
        return (e[s], tile(s, f, used), 0)

    def bias_map(s, f, e, c0, n, used):
        return (e[s], 0, 0)

    return pl.pallas_call(
        _experts_kernel,
        name="experts",
        grid_spec=pltpu.PrefetchScalarGridSpec(
            num_scalar_prefetch=4,
            grid=(MAX_SB, N_FF_TILES),
            in_specs=[
                pl.BlockSpec(memory_space=pl.ANY),
                pl.BlockSpec((1, D_MODEL, FF_TILE), up_map),
                pl.BlockSpec((1, D_MODEL, FF_TILE), up_map),
                pl.BlockSpec((1, FF_TILE, D_MODEL), down_map),
                pl.BlockSpec((1, 1, FF_TILE), up_map),
                pl.BlockSpec((1, 1, FF_TILE), up_map),
                pl.BlockSpec((1, 1, D_MODEL), bias_map),
            ],
            out_specs=pl.BlockSpec(memory_space=pl.ANY),
            scratch_shapes=[
                pltpu.VMEM((SB_ROWS, HALF_D), jnp.uint32),
                pltpu.VMEM((SB_ROWS, D_MODEL), bf16),
                pltpu.VMEM((SB_ROWS, FF_TILE), bf16),
                pltpu.VMEM((SB_ROWS, D_MODEL), f32),
                pltpu.SMEM((1,), jnp.int32),
                pltpu.SemaphoreType.DMA((2,)),
            ],
        ),
        out_shape=jax.ShapeDtypeStruct((MAX_ROWS, D_MODEL), f32),
        compiler_params=_cparams(("arbitrary", "arbitrary"), EXPERTS_VMEM_LIMIT),
    )(sb_e, sb_c0, sb_n, n_used, xs, w_gate, w_up, w_down,
      b_gate.reshape(N_EXPERTS, 1, D_FF), b_up.reshape(N_EXPERTS, 1, D_FF),
      b_down.reshape(N_EXPERTS, 1, D_MODEL))


COMB_TM = 256


def _combine_kernel(dest_ref, next_ref, ys_ref, x1_ref, gate_ref, gf_ref, o_ref, buf_ref, sem_ref):
    i = pl.program_id(0)
    slot = i % 2

    def gather(rows_ref, into):
        def issue(t, _):
            for k in range(TOP_K):
                pltpu.make_async_copy(ys_ref.at[pl.ds(rows_ref[t * TOP_K + k], 1), :],
                                      buf_ref.at[into, k, pl.ds(t, 1), :],
                                      sem_ref.at[into]).start(priority=k % 2)
            return 0

        lax.fori_loop(0, COMB_TM, issue, 0, unroll=8)

    @pl.when(i == 0)
    def _():
        gather(dest_ref, 0)

    has_next = i + 1 < pl.num_programs(0)
    for into in range(2):
        @pl.when(jnp.logical_and(has_next, slot == 1 - into))
        def _():
            gather(next_ref, into)

    for k in range(TOP_K):
        pltpu.make_async_copy(ys_ref.at[pl.ds(0, COMB_TM), :], buf_ref.at[slot, k], sem_ref.at[slot]).wait()

    gate = gate_ref[...]
    acc = x1_ref[...]
    for k in range(TOP_K):
        acc = acc + gate[:, k:k + 1] * buf_ref[slot, k]
    o_ref[...] = _rms(acc, gf_ref[...])


def _combine(dest, ys, x1, gates, g_final):
    n_tiles = SEQ // COMB_TM
    return pl.pallas_call(
        _combine_kernel,
        name="combine",
        grid=(SEQ // COMB_TM,),
        in_specs=[
            pl.BlockSpec((COMB_TM * TOP_K,), lambda i: (i,), memory_space=pltpu.SMEM),
            pl.BlockSpec((COMB_TM * TOP_K,), lambda i: (jnp.minimum(i + 1, n_tiles - 1),),
                         memory_space=pltpu.SMEM),
            pl.BlockSpec(memory_space=pl.ANY),
            pl.BlockSpec((COMB_TM, D_MODEL), lambda i: (i, 0)),
            pl.BlockSpec((COMB_TM, LANES), lambda i: (i, 0)),
            pl.BlockSpec((1, D_MODEL), lambda i: (0, 0)),
        ],
        out_specs=pl.BlockSpec((COMB_TM, D_MODEL), lambda i: (i, 0)),
        out_shape=jax.ShapeDtypeStruct((SEQ, D_MODEL), f32),
        scratch_shapes=[
            pltpu.VMEM((2, TOP_K, COMB_TM, D_MODEL), f32),
            pltpu.SemaphoreType.DMA((2,)),
        ],
        compiler_params=_cparams(("arbitrary",)),
    )(dest, dest, ys, x1, gates, g_final.reshape(1, D_MODEL))


def kernel(x, g_mix, w_in, lam_re_fwd, lam_im_fwd, log_dt_fwd, b_re_fwd, b_im_fwd, c_re_fwd, c_im_fwd, lam_re_bwd, lam_im_bwd, log_dt_bwd, b_re_bwd, b_im_bwd, c_re_bwd, c_im_bwd, ssm_d, w_glu, b_glu, na_rpb, g_ssm_out, g_na_out, w_out, g_moe, w_router, b_router, w_gate, b_gate, w_up, b_up, w_down, b_down, g_final):
    x2 = x.reshape(SEQ, D_MODEL)

    u, qkv = _in_proj(x2, g_mix[0], w_in[0].astype(bf16))

    s5_w, s5_dec = _s5_weights(
        (lam_re_fwd[0], lam_im_fwd[0], log_dt_fwd[0], b_re_fwd[0], b_im_fwd[0], c_re_fwd[0], c_im_fwd[0]),
        (lam_re_bwd[0], lam_im_bwd[0], log_dt_bwd[0], b_re_bwd[0], b_im_bwd[0], c_re_bwd[0], c_im_bwd[0]))
    y = _s5(u, s5_w, s5_dec, ssm_d[0])
    ssm_n = _glu(y, w_glu[0].astype(bf16), b_glu[0], g_ssm_out[0])

    y_na = _natten(qkv, _na_bias(na_rpb[0]))

    w_router_pad = jnp.zeros((D_MODEL, LANES), bf16).at[:, :N_EXPERTS].set(w_router[0].astype(bf16))
    b_router_pad = jnp.full((1, LANES), NEG_BIG, f32).at[0, :N_EXPERTS].set(b_router[0].astype(f32))
    x1, hn, logits = _out_proj(ssm_n, y_na, x2, g_na_out[0], w_out[0].astype(bf16), g_moe[0],
                               w_router_pad, b_router_pad)

    meta, gates, counts = _route(logits)
    dest, last_chunk, sb_e, sb_c0, sb_n, n_used = _routing_tables(meta, counts)

    xs = _dispatch(last_chunk, dest, hn)
    ys = _experts(sb_e, sb_c0, sb_n, n_used, xs, w_gate[0], b_gate[0], w_up[0], b_up[0],
                  w_down[0], b_down[0])
    out = _combine(dest, ys, x1, gates, g_final)
    return out.reshape(x.shape)
```

```python
import functools
import math

import jax
import jax.numpy as jnp
from jax import lax
from jax.experimental import pallas as pl
from jax.experimental.pallas import tpu as pltpu

f32 = jnp.float32
bf16 = jnp.bfloat16

D_MODEL = 2048
SEQ = 8192
SSM_WIDTH = 1024
NA_WIDTH = 1024
SSM_GROUP = 16
SSM_GROUPS = 64
SSM_STATE = 64
NA_HEAD_DIM = 64
NA_HEADS = 16
GRID_W = 64
GRID_ROWS = SEQ // GRID_W
NA_KH = 8
NA_KW = 16
N_EXPERTS = 32
TOP_K = 4
D_FF = 2048
SWIGLU_LIMIT = 7.0
SWIGLU_ALPHA = 1.702
RMS_EPS = 1e-5

LANES = 128
HALF_D = D_MODEL // 2
SUBLANES = 8
ROW_TILE = HALF_D // LANES
assert ROW_TILE == SUBLANES
NEG_BIG = -1e30

CHUNK_T = 16
N_CHUNKS = SEQ // CHUNK_T
GROUPS_PER_BLOCK = LANES // SSM_GROUP
N_LANE_BLOCKS = SSM_WIDTH // LANES
CAT_W = CHUNK_T * LANES
STATE_W = GROUPS_PER_BLOCK * SSM_STATE

ROW_CHUNK = 128
CHUNKS_PER_SB = 12
SB_ROWS = ROW_CHUNK * CHUNKS_PER_SB
MAX_CHUNKS = SEQ * TOP_K // ROW_CHUNK + N_EXPERTS
MAX_ROWS = MAX_CHUNKS * ROW_CHUNK
MAX_SB = MAX_CHUNKS // CHUNKS_PER_SB + N_EXPERTS
FF_TILE = 512
N_FF_TILES = D_FF // FF_TILE

VMEM_LIMIT = 56 * 1024 * 1024
EXPERTS_VMEM_LIMIT = 60 * 1024 * 1024


def _cparams(semantics, vmem=VMEM_LIMIT):
    return pltpu.CompilerParams(dimension_semantics=semantics, vmem_limit_bytes=vmem)


def _rms(x, g):
    return x * lax.rsqrt(jnp.mean(x * x, axis=-1, keepdims=True) + RMS_EPS) * g


IN_TM = 1024
IN_TN = 1024


def _in_proj_kernel(x_ref, g_ref, w_ref, u_ref, qkv_ref, h_ref):
    j = pl.program_id(1)

    @pl.when(j == 0)
    def _():
        h_ref[...] = _rms(x_ref[...], g_ref[...]).astype(bf16)

    acc = jnp.dot(h_ref[...], w_ref[...], preferred_element_type=f32)

    @pl.when(j == 0)
    def _():
        u_ref[...] = acc

    @pl.when(j > 0)
    def _():
        qkv_ref[...] = acc.astype(bf16)


def _in_proj(x, g_mix, w_in_bf):
    n_out = w_in_bf.shape[1]
    return pl.pallas_call(
        _in_proj_kernel,
        name="in_proj",
        grid=(SEQ // IN_TM, n_out // IN_TN),
        in_specs=[
            pl.BlockSpec((IN_TM, D_MODEL), lambda i, j: (i, 0)),
            pl.BlockSpec((1, D_MODEL), lambda i, j: (0, 0)),
            pl.BlockSpec((D_MODEL, IN_TN), lambda i, j: (0, j)),
        ],
        out_specs=[
            pl.BlockSpec((IN_TM, IN_TN), lambda i, j: (i, 0)),
            pl.BlockSpec((IN_TM, IN_TN), lambda i, j: (i, jnp.maximum(j - 1, 0))),
        ],
        out_shape=[
            jax.ShapeDtypeStruct((SEQ, SSM_WIDTH), f32),
            jax.ShapeDtypeStruct((SEQ, 3 * NA_WIDTH), bf16),
        ],
        scratch_shapes=[pltpu.VMEM((IN_TM, D_MODEL), bf16)],
        compiler_params=_cparams(("arbitrary", "arbitrary")),
    )(x, g_mix.reshape(1, D_MODEL), w_in_bf)


def _cmul(ar, ai, br, bi):
    return ar * br - ai * bi, ar * bi + ai * br


def _s5_discretise(lam_re, lam_im, log_dt, b_re, b_im, c_re, c_im):
    a = jnp.minimum(lam_re.astype(f32), -1e-4)
    w = lam_im.astype(f32)
    dt = jnp.exp(log_dt.astype(f32))[:, None]
    steps = jnp.arange(CHUNK_T + 1, dtype=f32)[:, None, None]
    mag = jnp.exp((a * dt)[None] * steps)
    ang = (w * dt)[None] * steps
    pw = (mag * jnp.cos(ang), mag * jnp.sin(ang))
    xr, xi = pw[0][1] - 1.0, pw[1][1]
    den = a * a + w * w
    qr, qi = (xr * a + xi * w) / den, (xi * a - xr * w) / den
    bb = _cmul(qr[..., None], qi[..., None], b_re.astype(f32), b_im.astype(f32))
    return pw, bb, (c_re.astype(f32), c_im.astype(f32))


def _pair_blockdiag(m):
    z = jnp.zeros_like(m[..., 0, :, :])
    top = jnp.concatenate([m[..., 0, :, :], z], axis=-1)
    bot = jnp.concatenate([z, m[..., 1, :, :]], axis=-1)
    return jnp.concatenate([top, bot], axis=-2)


def _s5_weights(fwd, bwd):
    hi = lax.Precision.HIGH
    exact = lax.Precision.HIGHEST
    t = CHUNK_T
    pw_f, bb_f, c_f = _s5_discretise(*fwd)
    pw_b, bb_b, c_b = _s5_discretise(*bwd)
    pairs = (N_LANE_BLOCKS, GROUPS_PER_BLOCK // 2, 2)

    x = jnp.arange(t * SSM_GROUP)
    t_of_x = x // SSM_GROUP
    tile_c = (jnp.arange(SSM_GROUP)[:, None] == x[None, :] % SSM_GROUP).astype(f32)

    def c_over_x(c):
        return [jnp.einsum('gcp,cx->gpx', part, tile_c, precision=exact) for part in c]

    def c_times_power(c_x, pw, power_of_x):
        rep = (jnp.arange(t + 1)[:, None] == power_of_x[None, :]).astype(f32)
        pw_x = [jnp.einsum('jgp,jx->gpx', part, rep, precision=exact) for part in pw]
        return _cmul(c_x[0], c_x[1], pw_x[0], pw_x[1])

    cx_f, cx_b = c_over_x(c_f), c_over_x(c_b)

    def lag_kernel(c_x, pw, bb, power_of_x):
        m_re, m_im = c_times_power(c_x, pw, power_of_x)
        m = jnp.concatenate([m_re, m_im], axis=1)
        b = jnp.concatenate([jnp.swapaxes(bb[0], 1, 2), -jnp.swapaxes(bb[1], 1, 2)], axis=-1)
        return jnp.einsum('gdp,gpx->gdx', b, m, precision=hi)

    kf = lag_kernel(cx_f, pw_f, bb_f, t_of_x)
    kb = lag_kernel(cx_b, pw_b, bb_b, t - 1 - t_of_x)
    keep = (t - 1) * SSM_GROUP
    k_lag = jnp.concatenate([kb[..., :keep], kb[..., keep:] + kf[..., :SSM_GROUP], kf[..., SSM_GROUP:]],
                            axis=-1)
    w_intra = jnp.stack([k_lag[..., (t - 1 - ti) * SSM_GROUP:(2 * t - 1 - ti) * SSM_GROUP]
                         for ti in range(t)], axis=1)
    w_intra = _pair_blockdiag(w_intra.astype(bf16).reshape(*pairs, t * SSM_GROUP, t * SSM_GROUP))

    def state_in(pw_t, bb):
        pw_g = [jnp.transpose(part, (1, 0, 2))[:, :, None, :] for part in pw_t]
        bb_g = [jnp.swapaxes(part, 1, 2)[:, None] for part in bb]
        return [_pair_blockdiag(part.astype(bf16).reshape(*pairs, t * SSM_GROUP, SSM_STATE))
                for part in _cmul(pw_g[0], pw_g[1], bb_g[0], bb_g[1])]

    w_in = jnp.concatenate(state_in([part[:t][::-1] for part in pw_f], bb_f)
                           + state_in([part[:t] for part in pw_b], bb_b), axis=-1)

    def state_out(c_x, pw, power_of_x):
        m_re, m_im = c_times_power(c_x, pw, power_of_x)
        return [_pair_blockdiag(part.astype(bf16).reshape(*pairs, SSM_STATE, t * SSM_GROUP))
                for part in (m_re, -m_im)]

    w_out = jnp.concatenate(state_out(cx_f, pw_f, t_of_x + 1) + state_out(cx_b, pw_b, t - t_of_x), axis=-2)

    def decay(a):
        return a.reshape(N_LANE_BLOCKS, 1, STATE_W)

    dec = jnp.concatenate([decay(pw_f[0][t]), decay(pw_f[1][t]), decay(pw_b[0][t]), decay(pw_b[1][t])],
                          axis=-1)
    return (w_in, w_intra, w_out), dec.astype(f32)


def _gelu_tanh(x):
    return 0.5 * x * (1.0 + jnp.tanh(math.sqrt(2.0 / math.pi) * (x + 0.044715 * (x * x * x))))


PAIRS_PER_BLOCK = GROUPS_PER_BLOCK // 2
PAIR_W = 2 * CHUNK_T * SSM_GROUP


def _s5_kernel(u_ref, win_ref, wintra_ref, wout_ref, dec_ref, d_ref, y_ref,
               cat_ref, catp_ref, st_ref, perm_ref):
    sw = STATE_W

    @pl.when(pl.program_id(0) == 0)
    def _():
        def strip(t, _):
            row = lax.broadcasted_iota(jnp.int32, (LANES, CAT_W), 0)
            col = lax.broadcasted_iota(jnp.int32, (LANES, CAT_W), 1)
            target = (row // SSM_GROUP) * (CHUNK_T * SSM_GROUP) + t * SSM_GROUP + row % SSM_GROUP
            perm_ref[pl.ds(pl.multiple_of(t * LANES, LANES), LANES), :] = (
                jnp.where(col == target, 1.0, 0.0).astype(bf16))
            return 0

        lax.fori_loop(0, CHUNK_T, strip, 0)

    for t in range(CHUNK_T):
        cat_ref[:, t * LANES:(t + 1) * LANES] = u_ref[pl.ds(t, N_CHUNKS, stride=CHUNK_T), :].astype(bf16)
    catp_ref[...] = jnp.dot(cat_ref[...], perm_ref[...], preferred_element_type=f32).astype(bf16)

    for p in range(PAIRS_PER_BLOCK):
        z = jnp.dot(catp_ref[:, p * PAIR_W:(p + 1) * PAIR_W], win_ref[0, p], preferred_element_type=f32)
        for part in range(4):
            st_ref[:, part * sw + p * LANES:part * sw + (p + 1) * LANES] = z[:, part * LANES:(part + 1) * LANES]

    afr = dec_ref[0, :, 0 * sw:1 * sw]
    afi = dec_ref[0, :, 1 * sw:2 * sw]
    abr = dec_ref[0, :, 2 * sw:3 * sw]
    abi = dec_ref[0, :, 3 * sw:4 * sw]

    def step(i, carry):
        sfr, sfi, sbr, sbi = carry
        r = N_CHUNKS - 1 - i
        zf = st_ref[pl.ds(i, 1), 0:2 * sw]
        zb = st_ref[pl.ds(r, 1), 2 * sw:4 * sw]
        st_ref[pl.ds(i, 1), 0:2 * sw] = jnp.concatenate([sfr, sfi], axis=-1)
        st_ref[pl.ds(r, 1), 2 * sw:4 * sw] = jnp.concatenate([sbr, sbi], axis=-1)
        nfr = afr * sfr - afi * sfi + zf[:, :sw]
        nfi = afi * sfr + afr * sfi + zf[:, sw:]
        nbr = abr * sbr - abi * sbi + zb[:, :sw]
        nbi = abi * sbr + abr * sbi + zb[:, sw:]
        return nfr, nfi, nbr, nbi

    zero = jnp.zeros((1, sw), f32)
    lax.fori_loop(0, N_CHUNKS, step, (zero, zero, zero, zero))

    for p in range(PAIRS_PER_BLOCK):
        s_in = jnp.concatenate(
            [st_ref[:, part * sw + p * LANES:part * sw + (p + 1) * LANES] for part in range(4)], axis=-1)
        yp = jnp.dot(catp_ref[:, p * PAIR_W:(p + 1) * PAIR_W], wintra_ref[0, p], preferred_element_type=f32)
        yp = yp + jnp.dot(s_in.astype(bf16), wout_ref[0, p], preferred_element_type=f32)
        cat_ref[:, p * PAIR_W:(p + 1) * PAIR_W] = yp.astype(bf16)

    st_ref[...] = lax.dot_general(cat_ref[...], perm_ref[...], (((1,), (1,)), ((), ())),
                                  preferred_element_type=f32)
    d = d_ref[...]
    for t in range(CHUNK_T):
        rows = pl.ds(t, N_CHUNKS, stride=CHUNK_T)
        v = st_ref[:, t * LANES:(t + 1) * LANES] + d * u_ref[rows, :]
        y_ref[rows, :] = _gelu_tanh(v)


def _s5(u, weights, dec, ssm_d):
    w_spec = pl.BlockSpec((1, PAIRS_PER_BLOCK, PAIR_W, PAIR_W), lambda j: (j, 0, 0, 0))
    return pl.pallas_call(
        _s5_kernel,
        name="s5_scan",
        grid=(N_LANE_BLOCKS,),
        in_specs=[
            pl.BlockSpec((SEQ, LANES), lambda j: (0, j)),
            w_spec, w_spec, w_spec,
            pl.BlockSpec((1, 1, 4 * STATE_W), lambda j: (j, 0, 0)),
            pl.BlockSpec((1, LANES), lambda j: (0, j)),
        ],
        out_specs=pl.BlockSpec((SEQ, LANES), lambda j: (0, j)),
        out_shape=jax.ShapeDtypeStruct((SEQ, SSM_WIDTH), f32),
        scratch_shapes=[
            pltpu.VMEM((N_CHUNKS, CAT_W), bf16),
            pltpu.VMEM((N_CHUNKS, CAT_W), bf16),
            pltpu.VMEM((N_CHUNKS, 4 * STATE_W), f32),
            pltpu.VMEM((CAT_W, CAT_W), bf16),
        ],
        compiler_params=_cparams(("arbitrary",)),
    )(u, *weights, dec, ssm_d.reshape(1, SSM_WIDTH))


GLU_TM = 1024


def _glu_kernel(y_ref, w_ref, b_ref, g_ref, o_ref):
    y = y_ref[...]
    z = jnp.dot(y.astype(bf16), w_ref[...], preferred_element_type=f32) + b_ref[...]
    o = y * (1.0 / (1.0 + jnp.exp(-z)))
    o_ref[...] = _rms(o, g_ref[...]).astype(bf16)


def _glu(y, w_glu_bf, b_glu, g_ssm_out):
    return pl.pallas_call(
        _glu_kernel,
        name="glu_norm",
        grid=(SEQ // GLU_TM,),
        in_specs=[
            pl.BlockSpec((GLU_TM, SSM_WIDTH), lambda i: (i, 0)),
            pl.BlockSpec((SSM_WIDTH, SSM_WIDTH), lambda i: (0, 0)),
            pl.BlockSpec((1, SSM_WIDTH), lambda i: (0, 0)),
            pl.BlockSpec((1, SSM_WIDTH), lambda i: (0, 0)),
        ],
        out_specs=pl.BlockSpec((GLU_TM, SSM_WIDTH), lambda i: (i, 0)),
        out_shape=jax.ShapeDtypeStruct((SEQ, SSM_WIDTH), bf16),
        compiler_params=_cparams(("arbitrary",)),
    )(y, w_glu_bf, b_glu.reshape(1, SSM_WIDTH), g_ssm_out.reshape(1, SSM_WIDTH))


NA_ROWS_PER_STEP = 16
NA_WIN = NA_KH * GRID_W
HEADS_PER_BLOCK = LANES // NA_HEAD_DIM


def _na_bias(rpb):
    c = jnp.arange(GRID_W)
    col_start = jnp.clip(c - NA_KW // 2, 0, GRID_W - NA_KW)
    valid = (c[None, :] >= col_start[:, None]) & (c[None, :] < col_start[:, None] + NA_KW)
    dc = jnp.clip(c[None, :] - c[:, None], -(NA_KW - 1), NA_KW - 1) + (NA_KW - 1)
    sel = (dc[None] == jnp.arange(2 * NA_KW - 1)[:, None, None]).astype(f32)
    tab = jnp.einsum('hrc,cqk->hqrk', rpb.astype(f32), sel, precision=lax.Precision.HIGHEST)
    tab = jnp.where(valid[None, :, None, :], tab, NEG_BIG)
    tab = tab.reshape(NA_HEADS, GRID_W, (2 * NA_KH - 1) * GRID_W)
    return jnp.stack([tab[..., (NA_KH - 1 - v) * GRID_W:(2 * NA_KH - 1 - v) * GRID_W]
                      for v in range(NA_KH)], axis=1)


def _natten_kernel(q_ref, k_ref, v_ref, b_ref, o_ref, s_ref, p_ref):
    rb = pl.program_id(1)
    lane = lax.broadcasted_iota(jnp.int32, (GRID_W, LANES), 1)
    head0 = lane < NA_HEAD_DIM
    scale = NA_HEAD_DIM ** -0.5

    starts, variants = [], []
    for i in range(NA_ROWS_PER_STEP):
        r = rb * NA_ROWS_PER_STEP + i
        rs = jnp.clip(r - NA_KH // 2, 0, GRID_ROWS - NA_KH)
        starts.append(pl.multiple_of(rs * GRID_W, GRID_W))
        variants.append(r - rs)

    for i in range(NA_ROWS_PER_STEP):
        q = q_ref[i * GRID_W:(i + 1) * GRID_W, :] * scale
        kw = k_ref[pl.ds(starts[i], NA_WIN), :]
        for h in range(HEADS_PER_BLOCK):
            qh = jnp.where(head0 if h == 0 else ~head0, q, jnp.zeros_like(q))
            s = lax.dot_general(qh, kw, (((1,), (1,)), ((), ())), preferred_element_type=f32)
            s_ref[i * HEADS_PER_BLOCK + h] = s + b_ref[h, variants[i]]

    inv_sums = []
    for t in range(NA_ROWS_PER_STEP * HEADS_PER_BLOCK):
        s = s_ref[t]
        p = jnp.exp(s - jnp.max(s, axis=-1, keepdims=True))
        inv_sums.append(1.0 / jnp.sum(p, axis=-1, keepdims=True))
        p_ref[t] = p.astype(bf16)

    for i in range(NA_ROWS_PER_STEP):
        vw = v_ref[pl.ds(starts[i], NA_WIN), :]
        outs = []
        for h in range(HEADS_PER_BLOCK):
            t = i * HEADS_PER_BLOCK + h
            outs.append(jnp.dot(p_ref[t], vw, preferred_element_type=f32) * inv_sums[t])
        o_ref[i * GRID_W:(i + 1) * GRID_W, :] = jnp.where(head0, outs[0], outs[1]).astype(bf16)


def _natten(qkv, bias):
    tm = NA_ROWS_PER_STEP * GRID_W
    n_hb = NA_WIDTH // LANES
    return pl.pallas_call(
        _natten_kernel,
        name="natten",
        grid=(n_hb, GRID_ROWS // NA_ROWS_PER_STEP),
        in_specs=[
            pl.BlockSpec((tm, LANES), lambda h, r: (r, h)),
            pl.BlockSpec((SEQ, LANES), lambda h, r: (0, n_hb + h)),
            pl.BlockSpec((SEQ, LANES), lambda h, r: (0, 2 * n_hb + h)),
            pl.BlockSpec((HEADS_PER_BLOCK, NA_KH, GRID_W, NA_WIN), lambda h, r: (h, 0, 0, 0)),
        ],
        out_specs=pl.BlockSpec((tm, LANES), lambda h, r: (r, h)),
        out_shape=jax.ShapeDtypeStruct((SEQ, NA_WIDTH), bf16),
        scratch_shapes=[
            pltpu.VMEM((NA_ROWS_PER_STEP * HEADS_PER_BLOCK, GRID_W, NA_WIN), f32),
            pltpu.VMEM((NA_ROWS_PER_STEP * HEADS_PER_BLOCK, GRID_W, NA_WIN), bf16),
        ],
        compiler_params=_cparams(("arbitrary", "arbitrary")),
    )(qkv, qkv, qkv, bias)


OUT_TM = 512


def _out_proj_kernel(ssm_ref, na_ref, x_ref, gna_ref, w_ref, gmoe_ref, wr_ref, br_ref,
                     x1_ref, hn_ref, lg_ref):
    na = _rms(na_ref[...].astype(f32), gna_ref[...]).astype(bf16)
    y = jnp.dot(ssm_ref[...], w_ref[0:SSM_WIDTH, :], preferred_element_type=f32)
    y = y + jnp.dot(na, w_ref[SSM_WIDTH:, :], preferred_element_type=f32)
    x1 = x_ref[...] + y
    x1_ref[...] = x1
    hn = _rms(x1, gmoe_ref[...]).astype(bf16)
    lg_ref[...] = jnp.dot(hn, wr_ref[...], preferred_element_type=f32) + br_ref[...]
    bits = lax.bitcast_convert_type(hn.astype(f32), jnp.uint32)
    packed = (bits[:, HALF_D:] & jnp.uint32(0xFFFF0000)) | (bits[:, :HALF_D] >> 16)
    for j in range(ROW_TILE):
        hn_ref[pl.ds(j, OUT_TM, stride=ROW_TILE), :] = packed[:, j * LANES:(j + 1) * LANES]


def _out_proj(ssm_n, y_na, x, g_na_out, w_out_bf, g_moe, w_router_pad, b_router_pad):
    row = lambda i: (i, 0)
    fixed = lambda i: (0, 0)
    return pl.pallas_call(
        _out_proj_kernel,
        name="out_proj",
        grid=(SEQ // OUT_TM,),
        in_specs=[
            pl.BlockSpec((OUT_TM, SSM_WIDTH), row),
            pl.BlockSpec((OUT_TM, NA_WIDTH), row),
            pl.BlockSpec((OUT_TM, D_MODEL), row),
            pl.BlockSpec((1, NA_WIDTH), fixed),
            pl.BlockSpec((D_MODEL, D_MODEL), fixed),
            pl.BlockSpec((1, D_MODEL), fixed),
            pl.BlockSpec((D_MODEL, LANES), fixed),
            pl.BlockSpec((1, LANES), fixed),
        ],
        out_specs=[
            pl.BlockSpec((OUT_TM, D_MODEL), row),
            pl.BlockSpec((OUT_TM * ROW_TILE, LANES), row),
            pl.BlockSpec((OUT_TM, LANES), row),
        ],
        out_shape=[
            jax.ShapeDtypeStruct((SEQ, D_MODEL), f32),
            jax.ShapeDtypeStruct((SEQ * ROW_TILE, LANES), jnp.uint32),
            jax.ShapeDtypeStruct((SEQ, LANES), f32),
        ],
        compiler_params=_cparams(("arbitrary",)),
    )(ssm_n, y_na, x, g_na_out.reshape(1, NA_WIDTH), w_out_bf, g_moe.reshape(1, D_MODEL),
      w_router_pad, b_router_pad)


ROUTE_TM = 1024


def _route_kernel(lg_ref, tri_ref, meta_ref, gate_ref, cnt_ref, carry_ref):
    i = pl.program_id(0)

    @pl.when(i == 0)
    def _():
        carry_ref[...] = jnp.zeros_like(carry_ref)

    lane = lax.broadcasted_iota(jnp.int32, (ROUTE_TM, LANES), 1)
    lane_f = lane.astype(f32)
    work = lg_ref[...]
    vals, hits = [], []
    for _ in range(TOP_K):
        m = jnp.max(work, axis=-1, keepdims=True)
        idx = jnp.min(jnp.where(work == m, lane_f, float(LANES)), axis=-1, keepdims=True)
        hit = lane_f == idx
        vals.append(m)
        hits.append((idx, hit))
        work = jnp.where(hit, -jnp.inf, work)

    exps = [jnp.exp(v - vals[0]) for v in vals]
    denom = exps[0] + exps[1] + exps[2] + exps[3]

    onehot = jnp.zeros((ROUTE_TM, LANES), f32)
    for _, hit in hits:
        onehot = onehot + hit.astype(f32)
    before = jnp.dot(tri_ref[...], onehot.astype(bf16), preferred_element_type=f32) + carry_ref[...]

    meta = jnp.zeros((ROUTE_TM, LANES), jnp.int32)
    gate = jnp.zeros((ROUTE_TM, LANES), f32)
    for k, (idx, hit) in enumerate(hits):
        rank = jnp.sum(jnp.where(hit, before, 0.0), axis=-1, keepdims=True).astype(jnp.int32)
        meta = jnp.where(lane == k, idx.astype(jnp.int32), meta)
        meta = jnp.where(lane == TOP_K + k, rank, meta)
        gate = jnp.where(lane == k, exps[k] / denom, gate)
    meta_ref[...] = meta
    gate_ref[...] = gate
    carry_ref[...] += jnp.sum(onehot, axis=0, keepdims=True)
    cnt_ref[...] = carry_ref[...]


def _route(logits):
    tri = (jnp.arange(ROUTE_TM)[:, None] > jnp.arange(ROUTE_TM)[None, :]).astype(bf16)
    row = lambda i: (i, 0)
    return pl.pallas_call(
        _route_kernel,
        name="route",
        grid=(SEQ // ROUTE_TM,),
        in_specs=[
            pl.BlockSpec((ROUTE_TM, LANES), row),
            pl.BlockSpec((ROUTE_TM, ROUTE_TM), lambda i: (0, 0)),
        ],
        out_specs=[
            pl.BlockSpec((ROUTE_TM, LANES), row),
            pl.BlockSpec((ROUTE_TM, LANES), row),
            pl.BlockSpec((1, LANES), lambda i: (0, 0)),
        ],
        out_shape=[
            jax.ShapeDtypeStruct((SEQ, LANES), jnp.int32),
            jax.ShapeDtypeStruct((SEQ, LANES), f32),
            jax.ShapeDtypeStruct((1, LANES), f32),
        ],
        scratch_shapes=[pltpu.VMEM((1, LANES), f32)],
        compiler_params=_cparams(("arbitrary",)),
    )(logits, tri)


def _routing_tables(meta, counts):
    idx = meta[:, :TOP_K]
    rank = meta[:, TOP_K:2 * TOP_K]
    cnt = counts[0, :N_EXPERTS].astype(jnp.int32)
    n_chunks = (cnt + ROW_CHUNK - 1) // ROW_CHUNK
    chunk_base = jnp.cumsum(n_chunks) - n_chunks
    dest = (chunk_base[idx] * ROW_CHUNK + rank).astype(jnp.int32).reshape(-1)
    total_chunks = jnp.sum(n_chunks)
    last_chunk = jnp.concatenate([jnp.where(cnt > 0, chunk_base + n_chunks - 1, -1),
                                  total_chunks[None]]).astype(jnp.int32)

    n_sb = (n_chunks + CHUNKS_PER_SB - 1) // CHUNKS_PER_SB
    sb_end = jnp.cumsum(n_sb)
    sb_start = sb_end - n_sb
    n_used = sb_end[-1]
    s = jnp.arange(MAX_SB)
    s_eff = jnp.minimum(s, n_used - 1)
    e = jnp.minimum(jnp.searchsorted(sb_end, s_eff, side='right'), N_EXPERTS - 1)
    kk = s_eff - sb_start[e]
    per_sb = n_chunks[e] // jnp.maximum(n_sb[e], 1)
    extra = n_chunks[e] - per_sb * n_sb[e]
    sb_chunk0 = chunk_base[e] + kk * per_sb + jnp.minimum(kk, extra)
    sb_n = jnp.where(s < n_used, per_sb + (kk < extra), 0)
    used = jnp.stack([n_used, total_chunks]).astype(jnp.int32)
    return (dest, last_chunk, e.astype(jnp.int32), sb_chunk0.astype(jnp.int32),
            sb_n.astype(jnp.int32), used)


DISP_TM = 256


def _packed_rows(first_row, n_rows=1):
    return pl.ds(pl.multiple_of(first_row * ROW_TILE, ROW_TILE), n_rows * ROW_TILE)


def _row_copy(src, src_row, dst, dst_row, sem):
    return pltpu.make_async_copy(src.at[_packed_rows(src_row), :], dst.at[_packed_rows(dst_row), :], sem)


def _dispatch_kernel(last_ref, dest_ref, hn_ref, xs_ref, zero_ref, sem_ref):
    i = pl.program_id(0)

    @pl.when(i == 0)
    def _():
        zero_ref[...] = jnp.zeros_like(zero_ref)

        def chunk_copy(c):
            return pltpu.make_async_copy(zero_ref, xs_ref.at[_packed_rows(c * ROW_CHUNK, ROW_CHUNK), :],
                                         sem_ref.at[1])

        def start(e, _):
            @pl.when(last_ref[e] >= 0)
            def _():
                chunk_copy(last_ref[e]).start()
            return 0

        def wait(e, _):
            @pl.when(last_ref[e] >= 0)
            def _():
                chunk_copy(last_ref[e]).wait()
            return 0

        def start_tail(c, _):
            chunk_copy(c).start()
            return 0

        def wait_tail(c, _):
            chunk_copy(c).wait()
            return 0

        lax.fori_loop(0, N_EXPERTS, start, 0)
        lax.fori_loop(last_ref[N_EXPERTS], MAX_CHUNKS, start_tail, 0)
        lax.fori_loop(0, N_EXPERTS, wait, 0)
        lax.fori_loop(last_ref[N_EXPERTS], MAX_CHUNKS, wait_tail, 0)

    def issue(t, _):
        for k in range(TOP_K):
            _row_copy(hn_ref, t, xs_ref, dest_ref[t * TOP_K + k], sem_ref.at[0]).start(priority=k % 2)
        return 0

    lax.fori_loop(0, DISP_TM, issue, 0, unroll=8)
    for k in range(TOP_K):
        pltpu.make_async_copy(hn_ref, xs_ref.at[_packed_rows(0, DISP_TM), :], sem_ref.at[0]).wait()


def _dispatch(last_chunk, dest, hn):
    return pl.pallas_call(
        _dispatch_kernel,
        name="dispatch",
        grid_spec=pltpu.PrefetchScalarGridSpec(
            num_scalar_prefetch=1,
            grid=(SEQ // DISP_TM,),
            in_specs=[
                pl.BlockSpec((DISP_TM * TOP_K,), lambda i, last: (i,), memory_space=pltpu.SMEM),
                pl.BlockSpec((DISP_TM * ROW_TILE, LANES), lambda i, last: (i, 0)),
            ],
            out_specs=pl.BlockSpec(memory_space=pl.ANY),
            scratch_shapes=[
                pltpu.VMEM((ROW_CHUNK * ROW_TILE, LANES), jnp.uint32),
                pltpu.SemaphoreType.DMA((2,)),
            ],
        ),
        out_shape=jax.ShapeDtypeStruct((MAX_ROWS * ROW_TILE, LANES), jnp.uint32),
        compiler_params=_cparams(("arbitrary",)),
    )(last_chunk, dest, hn)


def _experts_kernel(e_ref, c0_ref, n_ref, used_ref,
                    xs_ref, wg_ref, wu_ref, wd_ref, bg_ref, bu_ref, bd_ref, ys_ref,
                    xin_ref, xbf_ref, act_ref, acc_ref, pend_ref, sem_ref):
    s = pl.program_id(0)
    f = pl.program_id(1)
    n = n_ref[s]
    c0 = c0_ref[s]

    def rows(c, k=1):
        return pl.ds(pl.multiple_of(c * ROW_CHUNK, ROW_CHUNK), k * ROW_CHUNK)

    def cover(body):
        n4 = n // 4

        def quad(i, _):
            body(i * 4, 4)
            return 0

        lax.fori_loop(0, n4, quad, 0)

        @pl.when((n & 2) != 0)
        def _():
            body(n4 * 4, 2)

        @pl.when((n & 1) != 0)
        def _():
            body(n4 * 4 + (n & 2), 1)

    def drain():
        def wait_one(i, _):
            pltpu.make_async_copy(acc_ref.at[0:ROW_CHUNK, :], ys_ref.at[0:ROW_CHUNK, :], sem_ref.at[1]).wait()
            return 0

        lax.fori_loop(0, pend_ref[0], wait_one, 0)
        pend_ref[0] = 0

    @pl.when(jnp.logical_and(s == 0, f == 0))
    def _():
        pend_ref[0] = 0

    def chunk_copy(src_chunk, c):
        return pltpu.make_async_copy(xs_ref.at[_packed_rows(src_chunk * ROW_CHUNK, ROW_CHUNK), :],
                                     xin_ref.at[_packed_rows(c * ROW_CHUNK, ROW_CHUNK), :], sem_ref.at[0])

    def fetch(first_chunk, count):
        def start(c, _):
            chunk_copy(first_chunk + c, c).start()
            return 0

        lax.fori_loop(0, count, start, 0)

    @pl.when(jnp.logical_and(s == 0, f == 0))
    def _():
        fetch(c0, n)

    @pl.when(jnp.logical_and(n > 0, f == 0))
    def _():
        def finish(c, _):
            chunk_copy(c0 + c, c).wait()
            return 0

        def unpack(c, _):
            for j in range(ROW_TILE):
                w = xin_ref[pl.ds(c * (ROW_CHUNK * ROW_TILE) + j, ROW_CHUNK, stride=ROW_TILE), :]
                low = lax.bitcast_convert_type(w << 16, f32)
                high = lax.bitcast_convert_type(w & jnp.uint32(0xFFFF0000), f32)
                xbf_ref[rows(c), j * LANES:(j + 1) * LANES] = low.astype(bf16)
                xbf_ref[rows(c), HALF_D + j * LANES:HALF_D + (j + 1) * LANES] = high.astype(bf16)
            return 0

        lax.fori_loop(0, n, finish, 0)
        lax.fori_loop(0, n, unpack, 0)

    @pl.when(jnp.logical_and(s + 1 < MAX_SB, f == 1))
    def _():
        nxt = jnp.minimum(s + 1, MAX_SB - 1)
        fetch(c0_ref[nxt], n_ref[nxt])

    @pl.when(n > 0)
    def _():
        bg = bg_ref[0]
        bu = bu_ref[0]

        def up_body(c, k):
            x = xbf_ref[rows(c, k), :]
            g = jnp.dot(x, wg_ref[0].astype(bf16), preferred_element_type=f32) + bg
            u = jnp.dot(x, wu_ref[0].astype(bf16), preferred_element_type=f32) + bu
            g = jnp.minimum(g, SWIGLU_LIMIT)
            u = jnp.clip(u, -SWIGLU_LIMIT, SWIGLU_LIMIT)
            a = (u + 1.0) * (g * (1.0 / (1.0 + jnp.exp(-SWIGLU_ALPHA * g))))
            act_ref[rows(c, k), :] = a.astype(bf16)

        cover(up_body)

        @pl.when(f == 0)
        def _():
            drain()
            bias = jnp.broadcast_to(bd_ref[0], (ROW_CHUNK, D_MODEL))

            def init(c, _):
                acc_ref[rows(c), :] = bias
                return 0

            lax.fori_loop(0, n, init, 0)

        def down_body(c, k):
            acc_ref[rows(c, k), :] += jnp.dot(act_ref[rows(c, k), :], wd_ref[0].astype(bf16),
                                              preferred_element_type=f32)

        cover(down_body)

        @pl.when(f == N_FF_TILES - 1)
        def _():
            def write(c, _):
                pltpu.make_async_copy(acc_ref.at[rows(c), :], ys_ref.at[rows(c0 + c), :], sem_ref.at[1]).start()
                return 0

            lax.fori_loop(0, n, write, 0)
            pend_ref[0] = n

    @pl.when(jnp.logical_and(s == MAX_SB - 1, f == N_FF_TILES - 1))
    def _():
        drain()
        acc_ref[0:ROW_CHUNK, :] = jnp.zeros((ROW_CHUNK, D_MODEL), f32)

        def tail_copy(c):
            return pltpu.make_async_copy(acc_ref.at[0:ROW_CHUNK, :], ys_ref.at[rows(c), :], sem_ref.at[0])

        def start(c, _):
            tail_copy(c).start()
            return 0

        def finish(c, _):
            tail_copy(c).wait()
            return 0

        lax.fori_loop(used_ref[1], MAX_CHUNKS, start, 0)
        lax.fori_loop(used_ref[1], MAX_CHUNKS, finish, 0)


def _experts(sb_e, sb_c0, sb_n, n_used, xs, w_gate, b_gate, w_up, b_up, w_down, b_down):
    last = N_FF_TILES - 1

    def tile(s, f, used):
        return jnp.where(s < used[0], f, last)

    def up_map(s, f, e, c0, n, used):
        return (e[s], 0, tile(s, f, used))

    def down_map(s, f, e, c0, n, used):
        return (e[s], tile(s, f, used), 0)

    def bias_map(s, f, e, c0, n, used):
        return (e[s], 0, 0)

    return pl.pallas_call(
        _experts_kernel,
        name="experts",
        grid_spec=pltpu.PrefetchScalarGridSpec(
            num_scalar_prefetch=4,
            grid=(MAX_SB, N_FF_TILES),
            in_specs=[
                pl.BlockSpec(memory_space=pl.ANY),
                pl.BlockSpec((1, D_MODEL, FF_TILE), up_map),
                pl.BlockSpec((1, D_MODEL, FF_TILE), up_map),
                pl.BlockSpec((1, FF_TILE, D_MODEL), down_map),
                pl.BlockSpec((1, 1, FF_TILE), up_map),
                pl.BlockSpec((1, 1, FF_TILE), up_map),
                pl.BlockSpec((1, 1, D_MODEL), bias_map),
            ],
            out_specs=pl.BlockSpec(memory_space=pl.ANY),
            scratch_shapes=[
                pltpu.VMEM((SB_ROWS * ROW_TILE, LANES), jnp.uint32),
                pltpu.VMEM((SB_ROWS, D_MODEL), bf16),
                pltpu.VMEM((SB_ROWS, FF_TILE), bf16),
                pltpu.VMEM((SB_ROWS, D_MODEL), f32),
                pltpu.SMEM((1,), jnp.int32),
                pltpu.SemaphoreType.DMA((2,)),
            ],
        ),
        out_shape=jax.ShapeDtypeStruct((MAX_ROWS, D_MODEL), f32),
        compiler_params=_cparams(("arbitrary", "arbitrary"), EXPERTS_VMEM_LIMIT),
    )(sb_e, sb_c0, sb_n, n_used, xs, w_gate, w_up, w_down,
      b_gate.reshape(N_EXPERTS, 1, D_FF), b_up.reshape(N_EXPERTS, 1, D_FF),
      b_down.reshape(N_EXPERTS, 1, D_MODEL))


COMB_TM = 256


def _combine_kernel(dest_ref, next_ref, ys_ref, x1_ref, gate_ref, gf_ref, o_ref, buf_ref, sem_ref):
    i = pl.program_id(0)
    slot = i % 2

    def gather(rows_ref, into):
        def issue(t, _):
            for k in range(TOP_K):
                pltpu.make_async_copy(ys_ref.at[pl.ds(rows_ref[t * TOP_K + k], 1), :],
                                      buf_ref.at[into, k, pl.ds(t, 1), :],
                                      sem_ref.at[into]).start(priority=k % 2)
            return 0

        lax.fori_loop(0, COMB_TM, issue, 0, unroll=8)

    @pl.when(i == 0)
    def _():
        gather(dest_ref, 0)

    has_next = i + 1 < pl.num_programs(0)
    for into in range(2):
        @pl.when(jnp.logical_and(has_next, slot == 1 - into))
        def _():
            gather(next_ref, into)

    for k in range(TOP_K):
        pltpu.make_async_copy(ys_ref.at[pl.ds(0, COMB_TM), :], buf_ref.at[slot, k], sem_ref.at[slot]).wait()

    gate = gate_ref[...]
    acc = x1_ref[...]
    for k in range(TOP_K):
        acc = acc + gate[:, k:k + 1] * buf_ref[slot, k]
    o_ref[...] = _rms(acc, gf_ref[...])


def _combine(dest, ys, x1, gates, g_final):
    n_tiles = SEQ // COMB_TM
    return pl.pallas_call(
        _combine_kernel,
        name="combine",
        grid=(SEQ // COMB_TM,),
        in_specs=[
            pl.BlockSpec((COMB_TM * TOP_K,), lambda i: (i,), memory_space=pltpu.SMEM),
            pl.BlockSpec((COMB_TM * TOP_K,), lambda i: (jnp.minimum(i + 1, n_tiles - 1),),
                         memory_space=pltpu.SMEM),
            pl.BlockSpec(memory_space=pl.ANY),
            pl.BlockSpec((COMB_TM, D_MODEL), lambda i: (i, 0)),
            pl.BlockSpec((COMB_TM, LANES), lambda i: (i, 0)),
            pl.BlockSpec((1, D_MODEL), lambda i: (0, 0)),
        ],
        out_specs=pl.BlockSpec((COMB_TM, D_MODEL), lambda i: (i, 0)),
        out_shape=jax.ShapeDtypeStruct((SEQ, D_MODEL), f32),
        scratch_shapes=[
            pltpu.VMEM((2, TOP_K, COMB_TM, D_MODEL), f32),
            pltpu.SemaphoreType.DMA((2,)),
        ],
        compiler_params=_cparams(("arbitrary",)),
    )(dest, dest, ys, x1, gates, g_final.reshape(1, D_MODEL))


def kernel(x, g_mix, w_in, lam_re_fwd, lam_im_fwd, log_dt_fwd, b_re_fwd, b_im_fwd, c_re_fwd, c_im_fwd, lam_re_bwd, lam_im_bwd, log_dt_bwd, b_re_bwd, b_im_bwd, c_re_bwd, c_im_bwd, ssm_d, w_glu, b_glu, na_rpb, g_ssm_out, g_na_out, w_out, g_moe, w_router, b_router, w_gate, b_gate, w_up, b_up, w_down, b_down, g_final):
    x2 = x.reshape(SEQ, D_MODEL)

    u, qkv = _in_proj(x2, g_mix[0], w_in[0].astype(bf16))

    s5_w, s5_dec = _s5_weights(
        (lam_re_fwd[0], lam_im_fwd[0], log_dt_fwd[0], b_re_fwd[0], b_im_fwd[0], c_re_fwd[0], c_im_fwd[0]),
        (lam_re_bwd[0], lam_im_bwd[0], log_dt_bwd[0], b_re_bwd[0], b_im_bwd[0], c_re_bwd[0], c_im_bwd[0]))
    y = _s5(u, s5_w, s5_dec, ssm_d[0])
    ssm_n = _glu(y, w_glu[0].astype(bf16), b_glu[0], g_ssm_out[0])

    y_na = _natten(qkv, _na_bias(na_rpb[0]))

    w_router_pad = jnp.zeros((D_MODEL, LANES), bf16).at[:, :N_EXPERTS].set(w_router[0].astype(bf16))
    b_router_pad = jnp.full((1, LANES), NEG_BIG, f32).at[0, :N_EXPERTS].set(b_router[0].astype(f32))
    x1, hn, logits = _out_proj(ssm_n, y_na, x2, g_na_out[0], w_out[0].astype(bf16), g_moe[0],
                               w_router_pad, b_router_pad)

    meta, gates, counts = _route(logits)
    dest, last_chunk, sb_e, sb_c0, sb_n, n_used = _routing_tables(meta, counts)

    xs = _dispatch(last_chunk, dest, hn)
    ys = _experts(sb_e, sb_c0, sb_n, n_used, xs, w_gate[0], b_gate[0], w_up[0], b_up[0],
                  w_down[0], b_down[0])
    out = _combine(dest, ys, x1, gates, g_final)
    return out.reshape(x.shape)
```

```python
import functools
import math

import jax
import jax.numpy as jnp
from jax import lax
from jax.experimental import pallas as pl
from jax.experimental.pallas import tpu as pltpu

f32 = jnp.float32
bf16 = jnp.bfloat16

D_MODEL = 2048
SEQ = 8192
SSM_WIDTH = 1024
NA_WIDTH = 1024
SSM_GROUP = 16
SSM_GROUPS = 64
SSM_STATE = 64
NA_HEAD_DIM = 64
NA_HEADS = 16
GRID_W = 64
GRID_ROWS = SEQ // GRID_W
NA_KH = 8
NA_KW = 16
N_EXPERTS = 32
TOP_K = 4
D_FF = 2048
SWIGLU_LIMIT = 7.0
SWIGLU_ALPHA = 1.702
RMS_EPS = 1e-5

LANES = 128
HALF_D = D_MODEL // 2
SUBLANES = 8
ROW_TILE = HALF_D // LANES
assert ROW_TILE == SUBLANES
NEG_BIG = -1e30

CHUNK_T = 16
N_CHUNKS = SEQ // CHUNK_T
GROUPS_PER_BLOCK = LANES // SSM_GROUP
N_LANE_BLOCKS = SSM_WIDTH // LANES
CAT_W = CHUNK_T * LANES
STATE_W = GROUPS_PER_BLOCK * SSM_STATE

ROW_CHUNK = 128
CHUNKS_PER_SB = 12
SB_ROWS = ROW_CHUNK * CHUNKS_PER_SB
MAX_CHUNKS = SEQ * TOP_K // ROW_CHUNK + N_EXPERTS
MAX_ROWS = MAX_CHUNKS * ROW_CHUNK
MAX_SB = MAX_CHUNKS // CHUNKS_PER_SB + N_EXPERTS
FF_TILE = 512
N_FF_TILES = D_FF // FF_TILE

VMEM_LIMIT = 56 * 1024 * 1024
EXPERTS_VMEM_LIMIT = 60 * 1024 * 1024


def _cparams(semantics, vmem=VMEM_LIMIT):
    return pltpu.CompilerParams(dimension_semantics=semantics, vmem_limit_bytes=vmem)


def _rms(x, g):
    return x * lax.rsqrt(jnp.mean(x * x, axis=-1, keepdims=True) + RMS_EPS) * g


IN_TM = 1024
IN_TN = 1024


def _in_proj_kernel(x_ref, g_ref, w_ref, u_ref, qkv_ref, h_ref):
    j = pl.program_id(1)

    @pl.when(j == 0)
    def _():
        h_ref[...] = _rms(x_ref[...], g_ref[...]).astype(bf16)

    acc = jnp.dot(h_ref[...], w_ref[...], preferred_element_type=f32)

    @pl.when(j == 0)
    def _():
        u_ref[...] = acc

    @pl.when(j > 0)
    def _():
        qkv_ref[...] = acc.astype(bf16)


def _in_proj(x, g_mix, w_in_bf):
    n_out = w_in_bf.shape[1]
    return pl.pallas_call(
        _in_proj_kernel,
        name="in_proj",
        grid=(SEQ // IN_TM, n_out // IN_TN),
        in_specs=[
            pl.BlockSpec((IN_TM, D_MODEL), lambda i, j: (i, 0)),
            pl.BlockSpec((1, D_MODEL), lambda i, j: (0, 0)),
            pl.BlockSpec((D_MODEL, IN_TN), lambda i, j: (0, j)),
        ],
        out_specs=[
            pl.BlockSpec((IN_TM, IN_TN), lambda i, j: (i, 0)),
            pl.BlockSpec((IN_TM, IN_TN), lambda i, j: (i, jnp.maximum(j - 1, 0))),
        ],
        out_shape=[
            jax.ShapeDtypeStruct((SEQ, SSM_WIDTH), f32),
            jax.ShapeDtypeStruct((SEQ, 3 * NA_WIDTH), bf16),
        ],
        scratch_shapes=[pltpu.VMEM((IN_TM, D_MODEL), bf16)],
        compiler_params=_cparams(("arbitrary", "arbitrary")),
    )(x, g_mix.reshape(1, D_MODEL), w_in_bf)


def _cmul(ar, ai, br, bi):
    return ar * br - ai * bi, ar * bi + ai * br


def _s5_discretise(lam_re, lam_im, log_dt, b_re, b_im, c_re, c_im):
    a = jnp.minimum(lam_re.astype(f32), -1e-4)
    w = lam_im.astype(f32)
    dt = jnp.exp(log_dt.astype(f32))[:, None]
    steps = jnp.arange(CHUNK_T + 1, dtype=f32)[:, None, None]
    mag = jnp.exp((a * dt)[None] * steps)
    ang = (w * dt)[None] * steps
    pw = (mag * jnp.cos(ang), mag * jnp.sin(ang))
    xr, xi = pw[0][1] - 1.0, pw[1][1]
    den = a * a + w * w
    qr, qi = (xr * a + xi * w) / den, (xi * a - xr * w) / den
    bb = _cmul(qr[..., None], qi[..., None], b_re.astype(f32), b_im.astype(f32))
    return pw, bb, (c_re.astype(f32), c_im.astype(f32))


def _pair_blockdiag(m):
    z = jnp.zeros_like(m[..., 0, :, :])
    top = jnp.concatenate([m[..., 0, :, :], z], axis=-1)
    bot = jnp.concatenate([z, m[..., 1, :, :]], axis=-1)
    return jnp.concatenate([top, bot], axis=-2)


def _s5_weights(fwd, bwd):
    hi = lax.Precision.HIGH
    exact = lax.Precision.HIGHEST
    t = CHUNK_T
    pw_f, bb_f, c_f = _s5_discretise(*fwd)
    pw_b, bb_b, c_b = _s5_discretise(*bwd)
    pairs = (N_LANE_BLOCKS, GROUPS_PER_BLOCK // 2, 2)

    x = jnp.arange(t * SSM_GROUP)
    t_of_x = x // SSM_GROUP
    tile_c = (jnp.arange(SSM_GROUP)[:, None] == x[None, :] % SSM_GROUP).astype(f32)

    def c_over_x(c):
        return [jnp.einsum('gcp,cx->gpx', part, tile_c, precision=exact) for part in c]

    def c_times_power(c_x, pw, power_of_x):
        rep = (jnp.arange(t + 1)[:, None] == power_of_x[None, :]).astype(f32)
        pw_x = [jnp.einsum('jgp,jx->gpx', part, rep, precision=exact) for part in pw]
        return _cmul(c_x[0], c_x[1], pw_x[0], pw_x[1])

    cx_f, cx_b = c_over_x(c_f), c_over_x(c_b)

    def lag_kernel(c_x, pw, bb, power_of_x):
        m_re, m_im = c_times_power(c_x, pw, power_of_x)
        m = jnp.concatenate([m_re, m_im], axis=1)
        b = jnp.concatenate([jnp.swapaxes(bb[0], 1, 2), -jnp.swapaxes(bb[1], 1, 2)], axis=-1)
        return jnp.einsum('gdp,gpx->gdx', b, m, precision=hi)

    kf = lag_kernel(cx_f, pw_f, bb_f, t_of_x)
    kb = lag_kernel(cx_b, pw_b, bb_b, t - 1 - t_of_x)
    keep = (t - 1) * SSM_GROUP
    k_lag = jnp.concatenate([kb[..., :keep], kb[..., keep:] + kf[..., :SSM_GROUP], kf[..., SSM_GROUP:]],
                            axis=-1)
    w_intra = jnp.stack([k_lag[..., (t - 1 - ti) * SSM_GROUP:(2 * t - 1 - ti) * SSM_GROUP]
                         for ti in range(t)], axis=1)
    w_intra = _pair_blockdiag(w_intra.astype(bf16).reshape(*pairs, t * SSM_GROUP, t * SSM_GROUP))

    def state_in(pw_t, bb):
        pw_g = [jnp.transpose(part, (1, 0, 2))[:, :, None, :] for part in pw_t]
        bb_g = [jnp.swapaxes(part, 1, 2)[:, None] for part in bb]
        return [_pair_blockdiag(part.astype(bf16).reshape(*pairs, t * SSM_GROUP, SSM_STATE))
                for part in _cmul(pw_g[0], pw_g[1], bb_g[0], bb_g[1])]

    w_in = jnp.concatenate(state_in([part[:t][::-1] for part in pw_f], bb_f)
                           + state_in([part[:t] for part in pw_b], bb_b), axis=-1)

    def state_out(c_x, pw, power_of_x):
        m_re, m_im = c_times_power(c_x, pw, power_of_x)
        return [_pair_blockdiag(part.astype(bf16).reshape(*pairs, SSM_STATE, t * SSM_GROUP))
                for part in (m_re, -m_im)]

    w_out = jnp.concatenate(state_out(cx_f, pw_f, t_of_x + 1) + state_out(cx_b, pw_b, t - t_of_x), axis=-2)

    def decay(a):
        return a.reshape(N_LANE_BLOCKS, 1, STATE_W)

    dec = jnp.concatenate([decay(pw_f[0][t]), decay(pw_f[1][t]), decay(pw_b[0][t]), decay(pw_b[1][t])],
                          axis=-1)
    return (w_in, w_intra, w_out), dec.astype(f32)


def _gelu_tanh(x):
    return 0.5 * x * (1.0 + jnp.tanh(math.sqrt(2.0 / math.pi) * (x + 0.044715 * (x * x * x))))


PAIRS_PER_BLOCK = GROUPS_PER_BLOCK // 2
PAIR_W = 2 * CHUNK_T * SSM_GROUP


def _s5_kernel(u_ref, win_ref, wintra_ref, wout_ref, dec_ref, d_ref, y_ref,
               cat_ref, catp_ref, st_ref, perm_ref):
    sw = STATE_W

    @pl.when(pl.program_id(0) == 0)
    def _():
        def strip(t, _):
            row = lax.broadcasted_iota(jnp.int32, (LANES, CAT_W), 0)
            col = lax.broadcasted_iota(jnp.int32, (LANES, CAT_W), 1)
            target = (row // SSM_GROUP) * (CHUNK_T * SSM_GROUP) + t * SSM_GROUP + row % SSM_GROUP
            perm_ref[pl.ds(pl.multiple_of(t * LANES, LANES), LANES), :] = (
                jnp.where(col == target, 1.0, 0.0).astype(bf16))
            return 0

        lax.fori_loop(0, CHUNK_T, strip, 0)

    for t in range(CHUNK_T):
        cat_ref[:, t * LANES:(t + 1) * LANES] = u_ref[pl.ds(t, N_CHUNKS, stride=CHUNK_T), :].astype(bf16)
    catp_ref[...] = jnp.dot(cat_ref[...], perm_ref[...], preferred_element_type=f32).astype(bf16)

    for p in range(PAIRS_PER_BLOCK):
        z = jnp.dot(catp_ref[:, p * PAIR_W:(p + 1) * PAIR_W], win_ref[0, p], preferred_element_type=f32)
        for part in range(4):
            st_ref[:, part * sw + p * LANES:part * sw + (p + 1) * LANES] = z[:, part * LANES:(part + 1) * LANES]

    afr = dec_ref[0, :, 0 * sw:1 * sw]
    afi = dec_ref[0, :, 1 * sw:2 * sw]
    abr = dec_ref[0, :, 2 * sw:3 * sw]
    abi = dec_ref[0, :, 3 * sw:4 * sw]

    def step(i, carry):
        sfr, sfi, sbr, sbi = carry
        r = N_CHUNKS - 1 - i
        zf = st_ref[pl.ds(i, 1), 0:2 * sw]
        zb = st_ref[pl.ds(r, 1), 2 * sw:4 * sw]
        st_ref[pl.ds(i, 1), 0:2 * sw] = jnp.concatenate([sfr, sfi], axis=-1)
        st_ref[pl.ds(r, 1), 2 * sw:4 * sw] = jnp.concatenate([sbr, sbi], axis=-1)
        nfr = afr * sfr - afi * sfi + zf[:, :sw]
        nfi = afi * sfr + afr * sfi + zf[:, sw:]
        nbr = abr * sbr - abi * sbi + zb[:, :sw]
        nbi = abi * sbr + abr * sbi + zb[:, sw:]
        return nfr, nfi, nbr, nbi

    zero = jnp.zeros((1, sw), f32)
    lax.fori_loop(0, N_CHUNKS, step, (zero, zero, zero, zero))

    for p in range(PAIRS_PER_BLOCK):
        s_in = jnp.concatenate(
            [st_ref[:, part * sw + p * LANES:part * sw + (p + 1) * LANES] for part in range(4)], axis=-1)
        yp = jnp.dot(catp_ref[:, p * PAIR_W:(p + 1) * PAIR_W], wintra_ref[0, p], preferred_element_type=f32)
        yp = yp + jnp.dot(s_in.astype(bf16), wout_ref[0, p], preferred_element_type=f32)
        cat_ref[:, p * PAIR_W:(p + 1) * PAIR_W] = yp.astype(bf16)

    st_ref[...] = lax.dot_general(cat_ref[...], perm_ref[...], (((1,), (1,)), ((), ())),
                                  preferred_element_type=f32)
    d = d_ref[...]
    for t in range(CHUNK_T):
        rows = pl.ds(t, N_CHUNKS, stride=CHUNK_T)
        v = st_ref[:, t * LANES:(t + 1) * LANES] + d * u_ref[rows, :]
        y_ref[rows, :] = _gelu_tanh(v)


def _s5(u, weights, dec, ssm_d):
    w_spec = pl.BlockSpec((1, PAIRS_PER_BLOCK, PAIR_W, PAIR_W), lambda j: (j, 0, 0, 0))
    return pl.pallas_call(
        _s5_kernel,
        name="s5_scan",
        grid=(N_LANE_BLOCKS,),
        in_specs=[
            pl.BlockSpec((SEQ, LANES), lambda j: (0, j)),
            w_spec, w_spec, w_spec,
            pl.BlockSpec((1, 1, 4 * STATE_W), lambda j: (j, 0, 0)),
            pl.BlockSpec((1, LANES), lambda j: (0, j)),
        ],
        out_specs=pl.BlockSpec((SEQ, LANES), lambda j: (0, j)),
        out_shape=jax.ShapeDtypeStruct((SEQ, SSM_WIDTH), f32),
        scratch_shapes=[
            pltpu.VMEM((N_CHUNKS, CAT_W), bf16),
            pltpu.VMEM((N_CHUNKS, CAT_W), bf16),
            pltpu.VMEM((N_CHUNKS, 4 * STATE_W), f32),
            pltpu.VMEM((CAT_W, CAT_W), bf16),
        ],
        compiler_params=_cparams(("arbitrary",)),
    )(u, *weights, dec, ssm_d.reshape(1, SSM_WIDTH))


GLU_TM = 1024


def _glu_kernel(y_ref, w_ref, b_ref, g_ref, o_ref):
    y = y_ref[...]
    z = jnp.dot(y.astype(bf16), w_ref[...], preferred_element_type=f32) + b_ref[...]
    o = y * (1.0 / (1.0 + jnp.exp(-z)))
    o_ref[...] = _rms(o, g_ref[...]).astype(bf16)


def _glu(y, w_glu_bf, b_glu, g_ssm_out):
    return pl.pallas_call(
        _glu_kernel,
        name="glu_norm",
        grid=(SEQ // GLU_TM,),
        in_specs=[
            pl.BlockSpec((GLU_TM, SSM_WIDTH), lambda i: (i, 0)),
            pl.BlockSpec((SSM_WIDTH, SSM_WIDTH), lambda i: (0, 0)),
            pl.BlockSpec((1, SSM_WIDTH), lambda i: (0, 0)),
            pl.BlockSpec((1, SSM_WIDTH), lambda i: (0, 0)),
        ],
        out_specs=pl.BlockSpec((GLU_TM, SSM_WIDTH), lambda i: (i, 0)),
        out_shape=jax.ShapeDtypeStruct((SEQ, SSM_WIDTH), bf16),
        compiler_params=_cparams(("arbitrary",)),
    )(y, w_glu_bf, b_glu.reshape(1, SSM_WIDTH), g_ssm_out.reshape(1, SSM_WIDTH))


NA_ROWS_PER_STEP = 16
NA_WIN = NA_KH * GRID_W
HEADS_PER_BLOCK = LANES // NA_HEAD_DIM


def _na_bias(rpb):
    c = jnp.arange(GRID_W)
    col_start = jnp.clip(c - NA_KW // 2, 0, GRID_W - NA_KW)
    valid = (c[None, :] >= col_start[:, None]) & (c[None, :] < col_start[:, None] + NA_KW)
    dc = jnp.clip(c[None, :] - c[:, None], -(NA_KW - 1), NA_KW - 1) + (NA_KW - 1)
    sel = (dc[None] == jnp.arange(2 * NA_KW - 1)[:, None, None]).astype(f32)
    tab = jnp.einsum('hrc,cqk->hqrk', rpb.astype(f32), sel, precision=lax.Precision.HIGHEST)
    tab = jnp.where(valid[None, :, None, :], tab, NEG_BIG)
    tab = tab.reshape(NA_HEADS, GRID_W, (2 * NA_KH - 1) * GRID_W)
    return jnp.stack([tab[..., (NA_KH - 1 - v) * GRID_W:(2 * NA_KH - 1 - v) * GRID_W]
                      for v in range(NA_KH)], axis=1)


def _natten_kernel(q_ref, k_ref, v_ref, b_ref, o_ref, s_ref, p_ref):
    rb = pl.program_id(1)
    lane = lax.broadcasted_iota(jnp.int32, (GRID_W, LANES), 1)
    head0 = lane < NA_HEAD_DIM
    scale = NA_HEAD_DIM ** -0.5

    starts, variants = [], []
    for i in range(NA_ROWS_PER_STEP):
        r = rb * NA_ROWS_PER_STEP + i
        rs = jnp.clip(r - NA_KH // 2, 0, GRID_ROWS - NA_KH)
        starts.append(pl.multiple_of(rs * GRID_W, GRID_W))
        variants.append(r - rs)

    for i in range(NA_ROWS_PER_STEP):
        q = q_ref[i * GRID_W:(i + 1) * GRID_W, :] * scale
        kw = k_ref[pl.ds(starts[i], NA_WIN), :]
        for h in range(HEADS_PER_BLOCK):
            qh = jnp.where(head0 if h == 0 else ~head0, q, jnp.zeros_like(q))
            s = lax.dot_general(qh, kw, (((1,), (1,)), ((), ())), preferred_element_type=f32)
            s_ref[i * HEADS_PER_BLOCK + h] = s + b_ref[h, variants[i]]

    n_tiles = NA_ROWS_PER_STEP * HEADS_PER_BLOCK
    maxes = [jnp.max(s_ref[t], axis=-1, keepdims=True) for t in range(n_tiles)]
    inv_sums = []
    for t in range(n_tiles):
        p = jnp.exp(s_ref[t] - maxes[t])
        inv_sums.append(1.0 / jnp.sum(p, axis=-1, keepdims=True))
        p_ref[t] = p.astype(bf16)

    for i in range(NA_ROWS_PER_STEP):
        vw = v_ref[pl.ds(starts[i], NA_WIN), :]
        outs = []
        for h in range(HEADS_PER_BLOCK):
            t = i * HEADS_PER_BLOCK + h
            outs.append(jnp.dot(p_ref[t], vw, preferred_element_type=f32) * inv_sums[t])
        o_ref[i * GRID_W:(i + 1) * GRID_W, :] = jnp.where(head0, outs[0], outs[1]).astype(bf16)


def _natten(qkv, bias):
    tm = NA_ROWS_PER_STEP * GRID_W
    n_hb = NA_WIDTH // LANES
    return pl.pallas_call(
        _natten_kernel,
        name="natten",
        grid=(n_hb, GRID_ROWS // NA_ROWS_PER_STEP),
        in_specs=[
            pl.BlockSpec((tm, LANES), lambda h, r: (r, h)),
            pl.BlockSpec((SEQ, LANES), lambda h, r: (0, n_hb + h)),
            pl.BlockSpec((SEQ, LANES), lambda h, r: (0, 2 * n_hb + h)),
            pl.BlockSpec((HEADS_PER_BLOCK, NA_KH, GRID_W, NA_WIN), lambda h, r: (h, 0, 0, 0)),
        ],
        out_specs=pl.BlockSpec((tm, LANES), lambda h, r: (r, h)),
        out_shape=jax.ShapeDtypeStruct((SEQ, NA_WIDTH), bf16),
        scratch_shapes=[
            pltpu.VMEM((NA_ROWS_PER_STEP * HEADS_PER_BLOCK, GRID_W, NA_WIN), f32),
            pltpu.VMEM((NA_ROWS_PER_STEP * HEADS_PER_BLOCK, GRID_W, NA_WIN), bf16),
        ],
        compiler_params=_cparams(("arbitrary", "arbitrary")),
    )(qkv, qkv, qkv, bias)


OUT_TM = 512


def _out_proj_kernel(ssm_ref, na_ref, x_ref, gna_ref, w_ref, gmoe_ref, wr_ref, br_ref,
                     x1_ref, hn_ref, lg_ref):
    na = _rms(na_ref[...].astype(f32), gna_ref[...]).astype(bf16)
    y = jnp.dot(ssm_ref[...], w_ref[0:SSM_WIDTH, :], preferred_element_type=f32)
    y = y + jnp.dot(na, w_ref[SSM_WIDTH:, :], preferred_element_type=f32)
    x1 = x_ref[...] + y
    x1_ref[...] = x1
    hn = _rms(x1, gmoe_ref[...]).astype(bf16)
    lg_ref[...] = jnp.dot(hn, wr_ref[...], preferred_element_type=f32) + br_ref[...]
    bits = lax.bitcast_convert_type(hn.astype(f32), jnp.uint32)
    packed = (bits[:, HALF_D:] & jnp.uint32(0xFFFF0000)) | (bits[:, :HALF_D] >> 16)
    for j in range(ROW_TILE):
        hn_ref[pl.ds(j, OUT_TM, stride=ROW_TILE), :] = packed[:, j * LANES:(j + 1) * LANES]


def _out_proj(ssm_n, y_na, x, g_na_out, w_out_bf, g_moe, w_router_pad, b_router_pad):
    row = lambda i: (i, 0)
    fixed = lambda i: (0, 0)
    return pl.pallas_call(
        _out_proj_kernel,
        name="out_proj",
        grid=(SEQ // OUT_TM,),
        in_specs=[
            pl.BlockSpec((OUT_TM, SSM_WIDTH), row),
            pl.BlockSpec((OUT_TM, NA_WIDTH), row),
            pl.BlockSpec((OUT_TM, D_MODEL), row),
            pl.BlockSpec((1, NA_WIDTH), fixed),
            pl.BlockSpec((D_MODEL, D_MODEL), fixed),
            pl.BlockSpec((1, D_MODEL), fixed),
            pl.BlockSpec((D_MODEL, LANES), fixed),
            pl.BlockSpec((1, LANES), fixed),
        ],
        out_specs=[
            pl.BlockSpec((OUT_TM, D_MODEL), row),
            pl.BlockSpec((OUT_TM * ROW_TILE, LANES), row),
            pl.BlockSpec((OUT_TM, LANES), row),
        ],
        out_shape=[
            jax.ShapeDtypeStruct((SEQ, D_MODEL), f32),
            jax.ShapeDtypeStruct((SEQ * ROW_TILE, LANES), jnp.uint32),
            jax.ShapeDtypeStruct((SEQ, LANES), f32),
        ],
        compiler_params=_cparams(("arbitrary",)),
    )(ssm_n, y_na, x, g_na_out.reshape(1, NA_WIDTH), w_out_bf, g_moe.reshape(1, D_MODEL),
      w_router_pad, b_router_pad)


ROUTE_TM = 1024


def _route_kernel(lg_ref, tri_ref, dest_ref, gate_ref, cnt_ref, carry_ref, meta_ref):
    phase = pl.program_id(0)
    i = pl.program_id(1)
    rows = pl.ds(pl.multiple_of(i * ROUTE_TM, ROUTE_TM), ROUTE_TM)
    lane = lax.broadcasted_iota(jnp.int32, (ROUTE_TM, LANES), 1)

    @pl.when(phase == 0)
    def _():
        @pl.when(i == 0)
        def _():
            carry_ref[...] = jnp.zeros_like(carry_ref)

        lane_f = lane.astype(f32)
        work = lg_ref[...]
        vals, hits = [], []
        for _ in range(TOP_K):
            m = jnp.max(work, axis=-1, keepdims=True)
            idx = jnp.min(jnp.where(work == m, lane_f, float(LANES)), axis=-1, keepdims=True)
            hit = lane_f == idx
            vals.append(m)
            hits.append((idx, hit))
            work = jnp.where(hit, -jnp.inf, work)

        exps = [jnp.exp(v - vals[0]) for v in vals]
        denom = exps[0] + exps[1] + exps[2] + exps[3]

        onehot = jnp.zeros((ROUTE_TM, LANES), f32)
        for _, hit in hits:
            onehot = onehot + hit.astype(f32)
        before = jnp.dot(tri_ref[...], onehot.astype(bf16), preferred_element_type=f32) + carry_ref[...]

        meta = jnp.zeros((ROUTE_TM, LANES), jnp.int32)
        gate = jnp.zeros((ROUTE_TM, LANES), f32)
        for k, (idx, hit) in enumerate(hits):
            rank = jnp.sum(jnp.where(hit, before, 0.0), axis=-1, keepdims=True).astype(jnp.int32)
            meta = jnp.where(lane == k, idx.astype(jnp.int32), meta)
            meta = jnp.where(lane == TOP_K + k, rank, meta)
            gate = jnp.where(lane == k, exps[k] / denom, gate)
        meta_ref[rows, :] = meta
        gate_ref[...] = gate
        carry_ref[...] += jnp.sum(onehot, axis=0, keepdims=True)
        cnt_ref[...] = carry_ref[...]

    @pl.when(phase == 1)
    def _():
        n_chunks = jnp.floor((carry_ref[...] + float(ROW_CHUNK - 1)) * (1.0 / ROW_CHUNK))
        src = lax.broadcasted_iota(jnp.int32, (LANES, LANES), 0)
        dst = lax.broadcasted_iota(jnp.int32, (LANES, LANES), 1)
        earlier = jnp.where(src < dst, 1.0, 0.0).astype(bf16)
        first_chunk = jnp.dot(jnp.broadcast_to(n_chunks, (SUBLANES, LANES)).astype(bf16), earlier,
                              preferred_element_type=f32)[0:1]
        first_row = first_chunk * float(ROW_CHUNK)
        meta = meta_ref[rows, :]
        dest = jnp.zeros((ROUTE_TM, LANES), jnp.int32)
        for k in range(TOP_K):
            hit = lane == meta[:, k:k + 1]
            base = jnp.sum(jnp.where(hit, first_row, 0.0), axis=-1, keepdims=True).astype(jnp.int32)
            dest = jnp.where(lane == k, base + meta[:, TOP_K + k:TOP_K + k + 1], dest)
        dest_ref[...] = dest


def _route(logits):
    tri = (jnp.arange(ROUTE_TM)[:, None] > jnp.arange(ROUTE_TM)[None, :]).astype(bf16)
    n_tiles = SEQ // ROUTE_TM
    phase0 = lambda p, i: (jnp.where(p == 0, i, n_tiles - 1), 0)
    phase1 = lambda p, i: (i * p, 0)
    return pl.pallas_call(
        _route_kernel,
        name="route",
        grid=(2, n_tiles),
        in_specs=[
            pl.BlockSpec((ROUTE_TM, LANES), phase0),
            pl.BlockSpec((ROUTE_TM, ROUTE_TM), lambda p, i: (0, 0)),
        ],
        out_specs=[
            pl.BlockSpec((ROUTE_TM, LANES), phase1),
            pl.BlockSpec((ROUTE_TM, LANES), phase0),
            pl.BlockSpec((1, LANES), lambda p, i: (0, 0)),
        ],
        out_shape=[
            jax.ShapeDtypeStruct((SEQ, LANES), jnp.int32),
            jax.ShapeDtypeStruct((SEQ, LANES), f32),
            jax.ShapeDtypeStruct((1, LANES), f32),
        ],
        scratch_shapes=[pltpu.VMEM((1, LANES), f32), pltpu.VMEM((SEQ, LANES), jnp.int32)],
        compiler_params=_cparams(("arbitrary", "arbitrary")),
    )(logits, tri)


def _routing_tables(counts):
    cnt = counts[0, :N_EXPERTS].astype(jnp.int32)
    n_chunks = (cnt + ROW_CHUNK - 1) // ROW_CHUNK
    chunk_base = jnp.cumsum(n_chunks) - n_chunks
    total_chunks = jnp.sum(n_chunks)
    last_chunk = jnp.concatenate([jnp.where(cnt > 0, chunk_base + n_chunks - 1, -1),
                                  total_chunks[None]]).astype(jnp.int32)

    n_sb = (n_chunks + CHUNKS_PER_SB - 1) // CHUNKS_PER_SB
    sb_end = jnp.cumsum(n_sb)
    sb_start = sb_end - n_sb
    n_used = sb_end[-1]
    s = jnp.arange(MAX_SB)
    s_eff = jnp.minimum(s, n_used - 1)
    e = jnp.minimum(jnp.searchsorted(sb_end, s_eff, side='right'), N_EXPERTS - 1)
    kk = s_eff - sb_start[e]
    per_sb = n_chunks[e] // jnp.maximum(n_sb[e], 1)
    extra = n_chunks[e] - per_sb * n_sb[e]
    sb_chunk0 = chunk_base[e] + kk * per_sb + jnp.minimum(kk, extra)
    sb_n = jnp.where(s < n_used, per_sb + (kk < extra), 0)
    used = jnp.stack([n_used, total_chunks]).astype(jnp.int32)
    return (last_chunk, e.astype(jnp.int32), sb_chunk0.astype(jnp.int32), sb_n.astype(jnp.int32), used)


DISP_TM = 256


def _packed_rows(first_row, n_rows=1):
    return pl.ds(pl.multiple_of(first_row * ROW_TILE, ROW_TILE), n_rows * ROW_TILE)


def _row_copy(src, src_row, dst, dst_row, sem):
    return pltpu.make_async_copy(src.at[_packed_rows(src_row), :], dst.at[_packed_rows(dst_row), :], sem)


def _dispatch_kernel(last_ref, dest_ref, hn_ref, xs_ref, zero_ref, sem_ref):
    i = pl.program_id(0)

    @pl.when(i == 0)
    def _():
        zero_ref[...] = jnp.zeros_like(zero_ref)

        def chunk_copy(c):
            return pltpu.make_async_copy(zero_ref, xs_ref.at[_packed_rows(c * ROW_CHUNK, ROW_CHUNK), :],
                                         sem_ref.at[1])

        def start(e, _):
            @pl.when(last_ref[e] >= 0)
            def _():
                chunk_copy(last_ref[e]).start()
            return 0

        def wait(e, _):
            @pl.when(last_ref[e] >= 0)
            def _():
                chunk_copy(last_ref[e]).wait()
            return 0

        def start_tail(c, _):
            chunk_copy(c).start()
            return 0

        def wait_tail(c, _):
            chunk_copy(c).wait()
            return 0

        lax.fori_loop(0, N_EXPERTS, start, 0)
        lax.fori_loop(last_ref[N_EXPERTS], MAX_CHUNKS, start_tail, 0)
        lax.fori_loop(0, N_EXPERTS, wait, 0)
        lax.fori_loop(last_ref[N_EXPERTS], MAX_CHUNKS, wait_tail, 0)

    def issue(t, _):
        for k in range(TOP_K):
            _row_copy(hn_ref, t, xs_ref, dest_ref[t * TOP_K + k], sem_ref.at[0]).start(priority=k % 2)
        return 0

    lax.fori_loop(0, DISP_TM, issue, 0, unroll=8)
    for k in range(TOP_K):
        pltpu.make_async_copy(hn_ref, xs_ref.at[_packed_rows(0, DISP_TM), :], sem_ref.at[0]).wait()


def _dispatch(last_chunk, dest, hn):
    return pl.pallas_call(
        _dispatch_kernel,
        name="dispatch",
        grid_spec=pltpu.PrefetchScalarGridSpec(
            num_scalar_prefetch=1,
            grid=(SEQ // DISP_TM,),
            in_specs=[
                pl.BlockSpec((DISP_TM * TOP_K,), lambda i, last: (i,), memory_space=pltpu.SMEM),
                pl.BlockSpec((DISP_TM * ROW_TILE, LANES), lambda i, last: (i, 0)),
            ],
            out_specs=pl.BlockSpec(memory_space=pl.ANY),
            scratch_shapes=[
                pltpu.VMEM((ROW_CHUNK * ROW_TILE, LANES), jnp.uint32),
                pltpu.SemaphoreType.DMA((2,)),
            ],
        ),
        out_shape=jax.ShapeDtypeStruct((MAX_ROWS * ROW_TILE, LANES), jnp.uint32),
        compiler_params=_cparams(("arbitrary",)),
    )(last_chunk, dest, hn)


def _experts_kernel(e_ref, c0_ref, n_ref, used_ref,
                    xs_ref, wg_ref, wu_ref, wd_ref, bg_ref, bu_ref, bd_ref, ys_ref,
                    xin_ref, xbf_ref, act_ref, acc_ref, pend_ref, sem_ref):
    s = pl.program_id(0)
    f = pl.program_id(1)
    n = n_ref[s]
    c0 = c0_ref[s]

    def rows(c, k=1):
        return pl.ds(pl.multiple_of(c * ROW_CHUNK, ROW_CHUNK), k * ROW_CHUNK)

    def cover(body):
        n4 = n // 4

        def quad(i, _):
            body(i * 4, 4)
            return 0

        lax.fori_loop(0, n4, quad, 0)

        @pl.when((n & 2) != 0)
        def _():
            body(n4 * 4, 2)

        @pl.when((n & 1) != 0)
        def _():
            body(n4 * 4 + (n & 2), 1)

    def drain():
        def wait_one(i, _):
            pltpu.make_async_copy(acc_ref.at[0:ROW_CHUNK, :], ys_ref.at[0:ROW_CHUNK, :], sem_ref.at[1]).wait()
            return 0

        lax.fori_loop(0, pend_ref[0], wait_one, 0)
        pend_ref[0] = 0

    @pl.when(jnp.logical_and(s == 0, f == 0))
    def _():
        pend_ref[0] = 0

    def chunk_copy(src_chunk, c):
        return pltpu.make_async_copy(xs_ref.at[_packed_rows(src_chunk * ROW_CHUNK, ROW_CHUNK), :],
                                     xin_ref.at[_packed_rows(c * ROW_CHUNK, ROW_CHUNK), :], sem_ref.at[0])

    def fetch(first_chunk, count):
        def start(c, _):
            chunk_copy(first_chunk + c, c).start()
            return 0

        lax.fori_loop(0, count, start, 0)

    @pl.when(jnp.logical_and(s == 0, f == 0))
    def _():
        fetch(c0, n)

    @pl.when(jnp.logical_and(n > 0, f == 0))
    def _():
        def finish(c, _):
            chunk_copy(c0 + c, c).wait()
            return 0

        def unpack(c, _):
            for j in range(ROW_TILE):
                w = xin_ref[pl.ds(c * (ROW_CHUNK * ROW_TILE) + j, ROW_CHUNK, stride=ROW_TILE), :]
                low = lax.bitcast_convert_type(w << 16, f32)
                high = lax.bitcast_convert_type(w & jnp.uint32(0xFFFF0000), f32)
                xbf_ref[rows(c), j * LANES:(j + 1) * LANES] = low.astype(bf16)
                xbf_ref[rows(c), HALF_D + j * LANES:HALF_D + (j + 1) * LANES] = high.astype(bf16)
            return 0

        lax.fori_loop(0, n, finish, 0)
        lax.fori_loop(0, n, unpack, 0)

    @pl.when(jnp.logical_and(s + 1 < MAX_SB, f == 1))
    def _():
        nxt = jnp.minimum(s + 1, MAX_SB - 1)
        fetch(c0_ref[nxt], n_ref[nxt])

    @pl.when(n > 0)
    def _():
        bg = bg_ref[0]
        bu = bu_ref[0]

        def up_body(c, k):
            x = xbf_ref[rows(c, k), :]
            g = jnp.dot(x, wg_ref[0].astype(bf16), preferred_element_type=f32) + bg
            u = jnp.dot(x, wu_ref[0].astype(bf16), preferred_element_type=f32) + bu
            g = jnp.minimum(g, SWIGLU_LIMIT)
            u = jnp.clip(u, -SWIGLU_LIMIT, SWIGLU_LIMIT)
            a = (u + 1.0) * (g * (1.0 / (1.0 + jnp.exp(-SWIGLU_ALPHA * g))))
            act_ref[rows(c, k), :] = a.astype(bf16)

        cover(up_body)

        @pl.when(f == 0)
        def _():
            drain()
            bias = jnp.broadcast_to(bd_ref[0], (ROW_CHUNK, D_MODEL))

            def init(c, _):
                acc_ref[rows(c), :] = bias
                return 0

            lax.fori_loop(0, n, init, 0)

        def down_body(c, k):
            acc_ref[rows(c, k), :] += jnp.dot(act_ref[rows(c, k), :], wd_ref[0].astype(bf16),
                                              preferred_element_type=f32)

        cover(down_body)

        @pl.when(f == N_FF_TILES - 1)
        def _():
            def write(c, _):
                pltpu.make_async_copy(acc_ref.at[rows(c), :], ys_ref.at[rows(c0 + c), :], sem_ref.at[1]).start()
                return 0

            lax.fori_loop(0, n, write, 0)
            pend_ref[0] = n

    @pl.when(jnp.logical_and(s == MAX_SB - 1, f == N_FF_TILES - 1))
    def _():
        drain()
        acc_ref[0:ROW_CHUNK, :] = jnp.zeros((ROW_CHUNK, D_MODEL), f32)

        def tail_copy(c):
            return pltpu.make_async_copy(acc_ref.at[0:ROW_CHUNK, :], ys_ref.at[rows(c), :], sem_ref.at[0])

        def start(c, _):
            tail_copy(c).start()
            return 0

        def finish(c, _):
            tail_copy(c).wait()
            return 0

        lax.fori_loop(used_ref[1], MAX_CHUNKS, start, 0)
        lax.fori_loop(used_ref[1], MAX_CHUNKS, finish, 0)


def _experts(sb_e, sb_c0, sb_n, n_used, xs, w_gate, b_gate, w_up, b_up, w_down, b_down):
    last = N_FF_TILES - 1

    def tile(s, f, used):
        return jnp.where(s < used[0], f, last)

    def up_map(s, f, e, c0, n, used):
        return (e[s], 0, tile(s, f, used))

    def down_map(s, f, e, c0, n, used):
        return (e[s], tile(s, f, used), 0)

    def bias_map(s, f, e, c0, n, used):
        return (e[s], 0, 0)

    return pl.pallas_call(
        _experts_kernel,
        name="experts",
        grid_spec=pltpu.PrefetchScalarGridSpec(
            num_scalar_prefetch=4,
            grid=(MAX_SB, N_FF_TILES),
            in_specs=[
                pl.BlockSpec(memory_space=pl.ANY),
                pl.BlockSpec((1, D_MODEL, FF_TILE), up_map),
                pl.BlockSpec((1, D_MODEL, FF_TILE), up_map),
                pl.BlockSpec((1, FF_TILE, D_MODEL), down_map),
                pl.BlockSpec((1, 1, FF_TILE), up_map),
                pl.BlockSpec((1, 1, FF_TILE), up_map),
                pl.BlockSpec((1, 1, D_MODEL), bias_map),
            ],
            out_specs=pl.BlockSpec(memory_space=pl.ANY),
            scratch_shapes=[
                pltpu.VMEM((SB_ROWS * ROW_TILE, LANES), jnp.uint32),
                pltpu.VMEM((SB_ROWS, D_MODEL), bf16),
                pltpu.VMEM((SB_ROWS, FF_TILE), bf16),
                pltpu.VMEM((SB_ROWS, D_MODEL), f32),
                pltpu.SMEM((1,), jnp.int32),
                pltpu.SemaphoreType.DMA((2,)),
            ],
        ),
        out_shape=jax.ShapeDtypeStruct((MAX_ROWS, D_MODEL), f32),
        compiler_params=_cparams(("arbitrary", "arbitrary"), EXPERTS_VMEM_LIMIT),
    )(sb_e, sb_c0, sb_n, n_used, xs, w_gate, w_up, w_down,
      b_gate.reshape(N_EXPERTS, 1, D_FF), b_up.reshape(N_EXPERTS, 1, D_FF),
      b_down.reshape(N_EXPERTS, 1, D_MODEL))


COMB_TM = 256


def _combine_kernel(dest_ref, next_ref, ys_ref, x1_ref, gate_ref, gf_ref, o_ref, buf_ref, sem_ref):
    i = pl.program_id(0)
    slot = i % 2

    def gather(rows_ref, into):
        def issue(t, _):
            for k in range(TOP_K):
                pltpu.make_async_copy(ys_ref.at[pl.ds(rows_ref[t * TOP_K + k], 1), :],
                                      buf_ref.at[into, k, pl.ds(t, 1), :],
                                      sem_ref.at[into]).start(priority=k % 2)
            return 0

        lax.fori_loop(0, COMB_TM, issue, 0, unroll=8)

    @pl.when(i == 0)
    def _():
        gather(dest_ref, 0)

    has_next = i + 1 < pl.num_programs(0)
    for into in range(2):
        @pl.when(jnp.logical_and(has_next, slot == 1 - into))
        def _():
            gather(next_ref, into)

    for k in range(TOP_K):
        pltpu.make_async_copy(ys_ref.at[pl.ds(0, COMB_TM), :], buf_ref.at[slot, k], sem_ref.at[slot]).wait()

    gate = gate_ref[...]
    acc = x1_ref[...]
    for k in range(TOP_K):
        acc = acc + gate[:, k:k + 1] * buf_ref[slot, k]
    o_ref[...] = _rms(acc, gf_ref[...])


def _combine(dest, ys, x1, gates, g_final):
    n_tiles = SEQ // COMB_TM
    return pl.pallas_call(
        _combine_kernel,
        name="combine",
        grid=(SEQ // COMB_TM,),
        in_specs=[
            pl.BlockSpec((COMB_TM * TOP_K,), lambda i: (i,), memory_space=pltpu.SMEM),
            pl.BlockSpec((COMB_TM * TOP_K,), lambda i: (jnp.minimum(i + 1, n_tiles - 1),),
                         memory_space=pltpu.SMEM),
            pl.BlockSpec(memory_space=pl.ANY),
            pl.BlockSpec((COMB_TM, D_MODEL), lambda i: (i, 0)),
            pl.BlockSpec((COMB_TM, LANES), lambda i: (i, 0)),
            pl.BlockSpec((1, D_MODEL), lambda i: (0, 0)),
        ],
        out_specs=pl.BlockSpec((COMB_TM, D_MODEL), lambda i: (i, 0)),
        out_shape=jax.ShapeDtypeStruct((SEQ, D_MODEL), f32),
        scratch_shapes=[
            pltpu.VMEM((2, TOP_K, COMB_TM, D_MODEL), f32),
            pltpu.SemaphoreType.DMA((2,)),
        ],
        compiler_params=_cparams(("arbitrary",)),
    )(dest, dest, ys, x1, gates, g_final.reshape(1, D_MODEL))


def kernel(x, g_mix, w_in, lam_re_fwd, lam_im_fwd, log_dt_fwd, b_re_fwd, b_im_fwd, c_re_fwd, c_im_fwd, lam_re_bwd, lam_im_bwd, log_dt_bwd, b_re_bwd, b_im_bwd, c_re_bwd, c_im_bwd, ssm_d, w_glu, b_glu, na_rpb, g_ssm_out, g_na_out, w_out, g_moe, w_router, b_router, w_gate, b_gate, w_up, b_up, w_down, b_down, g_final):
    x2 = x.reshape(SEQ, D_MODEL)

    u, qkv = _in_proj(x2, g_mix[0], w_in[0].astype(bf16))

    s5_w, s5_dec = _s5_weights(
        (lam_re_fwd[0], lam_im_fwd[0], log_dt_fwd[0], b_re_fwd[0], b_im_fwd[0], c_re_fwd[0], c_im_fwd[0]),
        (lam_re_bwd[0], lam_im_bwd[0], log_dt_bwd[0], b_re_bwd[0], b_im_bwd[0], c_re_bwd[0], c_im_bwd[0]))
    y = _s5(u, s5_w, s5_dec, ssm_d[0])
    ssm_n = _glu(y, w_glu[0].astype(bf16), b_glu[0], g_ssm_out[0])

    y_na = _natten(qkv, _na_bias(na_rpb[0]))

    w_router_pad = jnp.zeros((D_MODEL, LANES), bf16).at[:, :N_EXPERTS].set(w_router[0].astype(bf16))
    b_router_pad = jnp.full((1, LANES), NEG_BIG, f32).at[0, :N_EXPERTS].set(b_router[0].astype(f32))
    x1, hn, logits = _out_proj(ssm_n, y_na, x2, g_na_out[0], w_out[0].astype(bf16), g_moe[0],
                               w_router_pad, b_router_pad)

    dest_lanes, gates, counts = _route(logits)
    dest = dest_lanes[:, :TOP_K].reshape(-1)
    last_chunk, sb_e, sb_c0, sb_n, n_used = _routing_tables(counts)

    xs = _dispatch(last_chunk, dest, hn)
    ys = _experts(sb_e, sb_c0, sb_n, n_used, xs, w_gate[0], b_gate[0], w_up[0], b_up[0],
                  w_down[0], b_down[0])
    out = _combine(dest, ys, x1, gates, g_final)
    return out.reshape(x.shape)
```

```python
import functools
import math

import jax
import jax.numpy as jnp
from jax import lax
from jax.experimental import pallas as pl
from jax.experimental.pallas import tpu as pltpu

f32 = jnp.float32
bf16 = jnp.bfloat16

D_MODEL = 2048
SEQ = 8192
SSM_WIDTH = 1024
NA_WIDTH = 1024
SSM_GROUP = 16
SSM_GROUPS = 64
SSM_STATE = 64
NA_HEAD_DIM = 64
NA_HEADS = 16
GRID_W = 64
GRID_ROWS = SEQ // GRID_W
NA_KH = 8
NA_KW = 16
N_EXPERTS = 32
TOP_K = 4
D_FF = 2048
SWIGLU_LIMIT = 7.0
SWIGLU_ALPHA = 1.702
RMS_EPS = 1e-5

LANES = 128
HALF_D = D_MODEL // 2
SUBLANES = 8
ROW_TILE = HALF_D // LANES
assert ROW_TILE == SUBLANES
NEG_BIG = -1e30

CHUNK_T = 16
N_CHUNKS = SEQ // CHUNK_T
GROUPS_PER_BLOCK = LANES // SSM_GROUP
N_LANE_BLOCKS = SSM_WIDTH // LANES
CAT_W = CHUNK_T * LANES
STATE_W = GROUPS_PER_BLOCK * SSM_STATE

ROW_CHUNK = 128
CHUNKS_PER_SB = 12
SB_ROWS = ROW_CHUNK * CHUNKS_PER_SB
MAX_CHUNKS = SEQ * TOP_K // ROW_CHUNK + N_EXPERTS
MAX_ROWS = MAX_CHUNKS * ROW_CHUNK
MAX_SB = MAX_CHUNKS // CHUNKS_PER_SB + N_EXPERTS
FF_TILE = 512
N_FF_TILES = D_FF // FF_TILE

VMEM_LIMIT = 56 * 1024 * 1024
EXPERTS_VMEM_LIMIT = 60 * 1024 * 1024


def _cparams(semantics, vmem=VMEM_LIMIT):
    return pltpu.CompilerParams(dimension_semantics=semantics, vmem_limit_bytes=vmem)


def _rms(x, g):
    return x * lax.rsqrt(jnp.mean(x * x, axis=-1, keepdims=True) + RMS_EPS) * g


IN_TM = 1024
IN_TN = 1024


def _in_proj_kernel(x_ref, g_ref, w_ref, u_ref, qkv_ref, h_ref):
    j = pl.program_id(1)

    @pl.when(j == 0)
    def _():
        h_ref[...] = _rms(x_ref[...], g_ref[...]).astype(bf16)

    acc = jnp.dot(h_ref[...], w_ref[...], preferred_element_type=f32)

    @pl.when(j == 0)
    def _():
        u_ref[...] = acc

    @pl.when(j > 0)
    def _():
        qkv_ref[...] = acc.astype(bf16)


def _in_proj(x, g_mix, w_in_bf):
    n_out = w_in_bf.shape[1]
    return pl.pallas_call(
        _in_proj_kernel,
        name="in_proj",
        grid=(SEQ // IN_TM, n_out // IN_TN),
        in_specs=[
            pl.BlockSpec((IN_TM, D_MODEL), lambda i, j: (i, 0)),
            pl.BlockSpec((1, D_MODEL), lambda i, j: (0, 0)),
            pl.BlockSpec((D_MODEL, IN_TN), lambda i, j: (0, j)),
        ],
        out_specs=[
            pl.BlockSpec((IN_TM, IN_TN), lambda i, j: (i, 0)),
            pl.BlockSpec((IN_TM, IN_TN), lambda i, j: (i, jnp.maximum(j - 1, 0))),
        ],
        out_shape=[
            jax.ShapeDtypeStruct((SEQ, SSM_WIDTH), f32),
            jax.ShapeDtypeStruct((SEQ, 3 * NA_WIDTH), bf16),
        ],
        scratch_shapes=[pltpu.VMEM((IN_TM, D_MODEL), bf16)],
        compiler_params=_cparams(("arbitrary", "arbitrary")),
    )(x, g_mix.reshape(1, D_MODEL), w_in_bf)


def _cmul(ar, ai, br, bi):
    return ar * br - ai * bi, ar * bi + ai * br


def _s5_discretise(lam_re, lam_im, log_dt, b_re, b_im, c_re, c_im):
    a = jnp.minimum(lam_re.astype(f32), -1e-4)
    w = lam_im.astype(f32)
    dt = jnp.exp(log_dt.astype(f32))[:, None]
    steps = jnp.arange(CHUNK_T + 1, dtype=f32)[:, None, None]
    mag = jnp.exp((a * dt)[None] * steps)
    ang = (w * dt)[None] * steps
    pw = (mag * jnp.cos(ang), mag * jnp.sin(ang))
    xr, xi = pw[0][1] - 1.0, pw[1][1]
    den = a * a + w * w
    qr, qi = (xr * a + xi * w) / den, (xi * a - xr * w) / den
    bb = _cmul(qr[..., None], qi[..., None], b_re.astype(f32), b_im.astype(f32))
    return pw, bb, (c_re.astype(f32), c_im.astype(f32))


def _pair_blockdiag(m):
    z = jnp.zeros_like(m[..., 0, :, :])
    top = jnp.concatenate([m[..., 0, :, :], z], axis=-1)
    bot = jnp.concatenate([z, m[..., 1, :, :]], axis=-1)
    return jnp.concatenate([top, bot], axis=-2)


def _s5_weights(fwd, bwd):
    hi = lax.Precision.HIGH
    exact = lax.Precision.HIGHEST
    t = CHUNK_T
    pw_f, bb_f, c_f = _s5_discretise(*fwd)
    pw_b, bb_b, c_b = _s5_discretise(*bwd)
    pairs = (N_LANE_BLOCKS, GROUPS_PER_BLOCK // 2, 2)

    x = jnp.arange(t * SSM_GROUP)
    t_of_x = x // SSM_GROUP
    tile_c = (jnp.arange(SSM_GROUP)[:, None] == x[None, :] % SSM_GROUP).astype(f32)

    def c_over_x(c):
        return [jnp.einsum('gcp,cx->gpx', part, tile_c, precision=exact) for part in c]

    def c_times_power(c_x, pw, power_of_x):
        rep = (jnp.arange(t + 1)[:, None] == power_of_x[None, :]).astype(f32)
        pw_x = [jnp.einsum('jgp,jx->gpx', part, rep, precision=exact) for part in pw]
        return _cmul(c_x[0], c_x[1], pw_x[0], pw_x[1])

    cx_f, cx_b = c_over_x(c_f), c_over_x(c_b)

    def lag_kernel(c_x, pw, bb, power_of_x):
        m_re, m_im = c_times_power(c_x, pw, power_of_x)
        m = jnp.concatenate([m_re, m_im], axis=1)
        b = jnp.concatenate([jnp.swapaxes(bb[0], 1, 2), -jnp.swapaxes(bb[1], 1, 2)], axis=-1)
        return jnp.einsum('gdp,gpx->gdx', b, m, precision=hi)

    kf = lag_kernel(cx_f, pw_f, bb_f, t_of_x)
    kb = lag_kernel(cx_b, pw_b, bb_b, t - 1 - t_of_x)
    keep = (t - 1) * SSM_GROUP
    k_lag = jnp.concatenate([kb[..., :keep], kb[..., keep:] + kf[..., :SSM_GROUP], kf[..., SSM_GROUP:]],
                            axis=-1)
    w_intra = jnp.stack([k_lag[..., (t - 1 - ti) * SSM_GROUP:(2 * t - 1 - ti) * SSM_GROUP]
                         for ti in range(t)], axis=1)
    w_intra = _pair_blockdiag(w_intra.astype(bf16).reshape(*pairs, t * SSM_GROUP, t * SSM_GROUP))

    def state_in(pw_t, bb):
        pw_g = [jnp.transpose(part, (1, 0, 2))[:, :, None, :] for part in pw_t]
        bb_g = [jnp.swapaxes(part, 1, 2)[:, None] for part in bb]
        return [_pair_blockdiag(part.astype(bf16).reshape(*pairs, t * SSM_GROUP, SSM_STATE))
                for part in _cmul(pw_g[0], pw_g[1], bb_g[0], bb_g[1])]

    w_in = jnp.concatenate(state_in([part[:t][::-1] for part in pw_f], bb_f)
                           + state_in([part[:t] for part in pw_b], bb_b), axis=-1)

    def state_out(c_x, pw, power_of_x):
        m_re, m_im = c_times_power(c_x, pw, power_of_x)
        return [_pair_blockdiag(part.astype(bf16).reshape(*pairs, SSM_STATE, t * SSM_GROUP))
                for part in (m_re, -m_im)]

    w_out = jnp.concatenate(state_out(cx_f, pw_f, t_of_x + 1) + state_out(cx_b, pw_b, t - t_of_x), axis=-2)

    def decay(a):
        return a.reshape(N_LANE_BLOCKS, 1, STATE_W)

    dec = jnp.concatenate([decay(pw_f[0][t]), decay(pw_f[1][t]), decay(pw_b[0][t]), decay(pw_b[1][t])],
                          axis=-1)
    return (w_in, w_intra, w_out), dec.astype(f32)


def _gelu_tanh(x):
    return 0.5 * x * (1.0 + jnp.tanh(math.sqrt(2.0 / math.pi) * (x + 0.044715 * (x * x * x))))


PAIRS_PER_BLOCK = GROUPS_PER_BLOCK // 2
PAIR_W = 2 * CHUNK_T * SSM_GROUP


def _s5_kernel(u_ref, win_ref, wintra_ref, wout_ref, dec_ref, d_ref, y_ref,
               cat_ref, catp_ref, st_ref, perm_ref):
    sw = STATE_W

    @pl.when(pl.program_id(0) == 0)
    def _():
        def strip(t, _):
            row = lax.broadcasted_iota(jnp.int32, (LANES, CAT_W), 0)
            col = lax.broadcasted_iota(jnp.int32, (LANES, CAT_W), 1)
            target = (row // SSM_GROUP) * (CHUNK_T * SSM_GROUP) + t * SSM_GROUP + row % SSM_GROUP
            perm_ref[pl.ds(pl.multiple_of(t * LANES, LANES), LANES), :] = (
                jnp.where(col == target, 1.0, 0.0).astype(bf16))
            return 0

        lax.fori_loop(0, CHUNK_T, strip, 0)

    for t in range(CHUNK_T):
        cat_ref[:, t * LANES:(t + 1) * LANES] = u_ref[pl.ds(t, N_CHUNKS, stride=CHUNK_T), :].astype(bf16)
    catp_ref[...] = jnp.dot(cat_ref[...], perm_ref[...], preferred_element_type=f32).astype(bf16)

    for p in range(PAIRS_PER_BLOCK):
        z = jnp.dot(catp_ref[:, p * PAIR_W:(p + 1) * PAIR_W], win_ref[0, p], preferred_element_type=f32)
        for part in range(4):
            st_ref[:, part * sw + p * LANES:part * sw + (p + 1) * LANES] = z[:, part * LANES:(part + 1) * LANES]

    afr = dec_ref[0, :, 0 * sw:1 * sw]
    afi = dec_ref[0, :, 1 * sw:2 * sw]
    abr = dec_ref[0, :, 2 * sw:3 * sw]
    abi = dec_ref[0, :, 3 * sw:4 * sw]

    def step(i, carry):
        sfr, sfi, sbr, sbi = carry
        r = N_CHUNKS - 1 - i
        zf = st_ref[pl.ds(i, 1), 0:2 * sw]
        zb = st_ref[pl.ds(r, 1), 2 * sw:4 * sw]
        st_ref[pl.ds(i, 1), 0:2 * sw] = jnp.concatenate([sfr, sfi], axis=-1)
        st_ref[pl.ds(r, 1), 2 * sw:4 * sw] = jnp.concatenate([sbr, sbi], axis=-1)
        nfr = afr * sfr - afi * sfi + zf[:, :sw]
        nfi = afi * sfr + afr * sfi + zf[:, sw:]
        nbr = abr * sbr - abi * sbi + zb[:, :sw]
        nbi = abi * sbr + abr * sbi + zb[:, sw:]
        return nfr, nfi, nbr, nbi

    zero = jnp.zeros((1, sw), f32)
    lax.fori_loop(0, N_CHUNKS, step, (zero, zero, zero, zero))

    for p in range(PAIRS_PER_BLOCK):
        s_in = jnp.concatenate(
            [st_ref[:, part * sw + p * LANES:part * sw + (p + 1) * LANES] for part in range(4)], axis=-1)
        yp = jnp.dot(catp_ref[:, p * PAIR_W:(p + 1) * PAIR_W], wintra_ref[0, p], preferred_element_type=f32)
        yp = yp + jnp.dot(s_in.astype(bf16), wout_ref[0, p], preferred_element_type=f32)
        cat_ref[:, p * PAIR_W:(p + 1) * PAIR_W] = yp.astype(bf16)

    st_ref[...] = lax.dot_general(cat_ref[...], perm_ref[...], (((1,), (1,)), ((), ())),
                                  preferred_element_type=f32)
    d = d_ref[...]
    for t in range(CHUNK_T):
        rows = pl.ds(t, N_CHUNKS, stride=CHUNK_T)
        v = st_ref[:, t * LANES:(t + 1) * LANES] + d * u_ref[rows, :]
        y_ref[rows, :] = _gelu_tanh(v)


def _s5(u, weights, dec, ssm_d):
    w_spec = pl.BlockSpec((1, PAIRS_PER_BLOCK, PAIR_W, PAIR_W), lambda j: (j, 0, 0, 0))
    return pl.pallas_call(
        _s5_kernel,
        name="s5_scan",
        grid=(N_LANE_BLOCKS,),
        in_specs=[
            pl.BlockSpec((SEQ, LANES), lambda j: (0, j)),
            w_spec, w_spec, w_spec,
            pl.BlockSpec((1, 1, 4 * STATE_W), lambda j: (j, 0, 0)),
            pl.BlockSpec((1, LANES), lambda j: (0, j)),
        ],
        out_specs=pl.BlockSpec((SEQ, LANES), lambda j: (0, j)),
        out_shape=jax.ShapeDtypeStruct((SEQ, SSM_WIDTH), f32),
        scratch_shapes=[
            pltpu.VMEM((N_CHUNKS, CAT_W), bf16),
            pltpu.VMEM((N_CHUNKS, CAT_W), bf16),
            pltpu.VMEM((N_CHUNKS, 4 * STATE_W), f32),
            pltpu.VMEM((CAT_W, CAT_W), bf16),
        ],
        compiler_params=_cparams(("arbitrary",)),
    )(u, *weights, dec, ssm_d.reshape(1, SSM_WIDTH))


GLU_TM = 1024


def _glu_kernel(y_ref, w_ref, b_ref, g_ref, o_ref):
    y = y_ref[...]
    z = jnp.dot(y.astype(bf16), w_ref[...], preferred_element_type=f32) + b_ref[...]
    o = y * (1.0 / (1.0 + jnp.exp(-z)))
    o_ref[...] = _rms(o, g_ref[...]).astype(bf16)


def _glu(y, w_glu_bf, b_glu, g_ssm_out):
    return pl.pallas_call(
        _glu_kernel,
        name="glu_norm",
        grid=(SEQ // GLU_TM,),
        in_specs=[
            pl.BlockSpec((GLU_TM, SSM_WIDTH), lambda i: (i, 0)),
            pl.BlockSpec((SSM_WIDTH, SSM_WIDTH), lambda i: (0, 0)),
            pl.BlockSpec((1, SSM_WIDTH), lambda i: (0, 0)),
            pl.BlockSpec((1, SSM_WIDTH), lambda i: (0, 0)),
        ],
        out_specs=pl.BlockSpec((GLU_TM, SSM_WIDTH), lambda i: (i, 0)),
        out_shape=jax.ShapeDtypeStruct((SEQ, SSM_WIDTH), bf16),
        compiler_params=_cparams(("arbitrary",)),
    )(y, w_glu_bf, b_glu.reshape(1, SSM_WIDTH), g_ssm_out.reshape(1, SSM_WIDTH))


NA_ROWS_PER_STEP = 16
NA_WIN = NA_KH * GRID_W
HEADS_PER_BLOCK = LANES // NA_HEAD_DIM


def _na_bias(rpb):
    c = jnp.arange(GRID_W)
    col_start = jnp.clip(c - NA_KW // 2, 0, GRID_W - NA_KW)
    valid = (c[None, :] >= col_start[:, None]) & (c[None, :] < col_start[:, None] + NA_KW)
    dc = jnp.clip(c[None, :] - c[:, None], -(NA_KW - 1), NA_KW - 1) + (NA_KW - 1)
    sel = (dc[None] == jnp.arange(2 * NA_KW - 1)[:, None, None]).astype(f32)
    tab = jnp.einsum('hrc,cqk->hqrk', rpb.astype(f32), sel, precision=lax.Precision.HIGHEST)
    tab = jnp.where(valid[None, :, None, :], tab, NEG_BIG)
    tab = tab.reshape(NA_HEADS, GRID_W, (2 * NA_KH - 1) * GRID_W)
    return jnp.stack([tab[..., (NA_KH - 1 - v) * GRID_W:(2 * NA_KH - 1 - v) * GRID_W]
                      for v in range(NA_KH)], axis=1)


def _natten_kernel(q_ref, k_ref, v_ref, b_ref, o_ref, s_ref, p_ref):
    rb = pl.program_id(1)
    lane = lax.broadcasted_iota(jnp.int32, (GRID_W, LANES), 1)
    head0 = lane < NA_HEAD_DIM
    scale = NA_HEAD_DIM ** -0.5

    starts, variants = [], []
    for i in range(NA_ROWS_PER_STEP):
        r = rb * NA_ROWS_PER_STEP + i
        rs = jnp.clip(r - NA_KH // 2, 0, GRID_ROWS - NA_KH)
        starts.append(pl.multiple_of(rs * GRID_W, GRID_W))
        variants.append(r - rs)

    for i in range(NA_ROWS_PER_STEP):
        q = q_ref[i * GRID_W:(i + 1) * GRID_W, :] * scale
        kw = k_ref[pl.ds(starts[i], NA_WIN), :]
        for h in range(HEADS_PER_BLOCK):
            qh = jnp.where(head0 if h == 0 else ~head0, q, jnp.zeros_like(q))
            s = lax.dot_general(qh, kw, (((1,), (1,)), ((), ())), preferred_element_type=f32)
            s_ref[i * HEADS_PER_BLOCK + h] = s + b_ref[h, variants[i]]

    n_tiles = NA_ROWS_PER_STEP * HEADS_PER_BLOCK
    maxes = [jnp.max(s_ref[t], axis=-1, keepdims=True) for t in range(n_tiles)]
    inv_sums = []
    for t in range(n_tiles):
        p = jnp.exp(s_ref[t] - maxes[t])
        inv_sums.append(1.0 / jnp.sum(p, axis=-1, keepdims=True))
        p_ref[t] = p.astype(bf16)

    for i in range(NA_ROWS_PER_STEP):
        vw = v_ref[pl.ds(starts[i], NA_WIN), :]
        outs = []
        for h in range(HEADS_PER_BLOCK):
            t = i * HEADS_PER_BLOCK + h
            outs.append(jnp.dot(p_ref[t], vw, preferred_element_type=f32) * inv_sums[t])
        o_ref[i * GRID_W:(i + 1) * GRID_W, :] = jnp.where(head0, outs[0], outs[1]).astype(bf16)


def _natten(qkv, bias):
    tm = NA_ROWS_PER_STEP * GRID_W
    n_hb = NA_WIDTH // LANES
    return pl.pallas_call(
        _natten_kernel,
        name="natten",
        grid=(n_hb, GRID_ROWS // NA_ROWS_PER_STEP),
        in_specs=[
            pl.BlockSpec((tm, LANES), lambda h, r: (r, h)),
            pl.BlockSpec((SEQ, LANES), lambda h, r: (0, n_hb + h)),
            pl.BlockSpec((SEQ, LANES), lambda h, r: (0, 2 * n_hb + h)),
            pl.BlockSpec((HEADS_PER_BLOCK, NA_KH, GRID_W, NA_WIN), lambda h, r: (h, 0, 0, 0)),
        ],
        out_specs=pl.BlockSpec((tm, LANES), lambda h, r: (r, h)),
        out_shape=jax.ShapeDtypeStruct((SEQ, NA_WIDTH), bf16),
        scratch_shapes=[
            pltpu.VMEM((NA_ROWS_PER_STEP * HEADS_PER_BLOCK, GRID_W, NA_WIN), f32),
            pltpu.VMEM((NA_ROWS_PER_STEP * HEADS_PER_BLOCK, GRID_W, NA_WIN), bf16),
        ],
        compiler_params=_cparams(("arbitrary", "arbitrary")),
    )(qkv, qkv, qkv, bias)


OUT_TM = 512


def _out_proj_kernel(ssm_ref, na_ref, x_ref, gna_ref, w_ref, gmoe_ref, wr_ref, br_ref,
                     x1_ref, hn_ref, lg_ref):
    na = _rms(na_ref[...].astype(f32), gna_ref[...]).astype(bf16)
    y = jnp.dot(ssm_ref[...], w_ref[0:SSM_WIDTH, :], preferred_element_type=f32)
    y = y + jnp.dot(na, w_ref[SSM_WIDTH:, :], preferred_element_type=f32)
    x1 = x_ref[...] + y
    x1_ref[...] = x1
    hn = _rms(x1, gmoe_ref[...]).astype(bf16)
    lg_ref[...] = jnp.dot(hn, wr_ref[...], preferred_element_type=f32) + br_ref[...]
    bits = lax.bitcast_convert_type(hn.astype(f32), jnp.uint32)
    packed = (bits[:, HALF_D:] & jnp.uint32(0xFFFF0000)) | (bits[:, :HALF_D] >> 16)
    for j in range(ROW_TILE):
        hn_ref[pl.ds(j, OUT_TM, stride=ROW_TILE), :] = packed[:, j * LANES:(j + 1) * LANES]


def _out_proj(ssm_n, y_na, x, g_na_out, w_out_bf, g_moe, w_router_pad, b_router_pad):
    row = lambda i: (i, 0)
    fixed = lambda i: (0, 0)
    return pl.pallas_call(
        _out_proj_kernel,
        name="out_proj",
        grid=(SEQ // OUT_TM,),
        in_specs=[
            pl.BlockSpec((OUT_TM, SSM_WIDTH), row),
            pl.BlockSpec((OUT_TM, NA_WIDTH), row),
            pl.BlockSpec((OUT_TM, D_MODEL), row),
            pl.BlockSpec((1, NA_WIDTH), fixed),
            pl.BlockSpec((D_MODEL, D_MODEL), fixed),
            pl.BlockSpec((1, D_MODEL), fixed),
            pl.BlockSpec((D_MODEL, LANES), fixed),
            pl.BlockSpec((1, LANES), fixed),
        ],
        out_specs=[
            pl.BlockSpec((OUT_TM, D_MODEL), row),
            pl.BlockSpec((OUT_TM * ROW_TILE, LANES), row),
            pl.BlockSpec((OUT_TM, LANES), row),
        ],
        out_shape=[
            jax.ShapeDtypeStruct((SEQ, D_MODEL), f32),
            jax.ShapeDtypeStruct((SEQ * ROW_TILE, LANES), jnp.uint32),
            jax.ShapeDtypeStruct((SEQ, LANES), f32),
        ],
        compiler_params=_cparams(("arbitrary",)),
    )(ssm_n, y_na, x, g_na_out.reshape(1, NA_WIDTH), w_out_bf, g_moe.reshape(1, D_MODEL),
      w_router_pad, b_router_pad)


ROUTE_TM = 1024


def _route_kernel(lg_ref, tri_ref, dest_ref, gate_ref, cnt_ref, carry_ref, meta_ref):
    phase = pl.program_id(0)
    i = pl.program_id(1)
    rows = pl.ds(pl.multiple_of(i * ROUTE_TM, ROUTE_TM), ROUTE_TM)
    lane = lax.broadcasted_iota(jnp.int32, (ROUTE_TM, LANES), 1)

    @pl.when(phase == 0)
    def _():
        @pl.when(i == 0)
        def _():
            carry_ref[...] = jnp.zeros_like(carry_ref)

        lane_f = lane.astype(f32)
        work = lg_ref[...]
        vals, hits = [], []
        for _ in range(TOP_K):
            m = jnp.max(work, axis=-1, keepdims=True)
            idx = jnp.min(jnp.where(work == m, lane_f, float(LANES)), axis=-1, keepdims=True)
            hit = lane_f == idx
            vals.append(m)
            hits.append((idx, hit))
            work = jnp.where(hit, -jnp.inf, work)

        exps = [jnp.exp(v - vals[0]) for v in vals]
        denom = exps[0] + exps[1] + exps[2] + exps[3]

        onehot = jnp.zeros((ROUTE_TM, LANES), f32)
        for _, hit in hits:
            onehot = onehot + hit.astype(f32)
        before = jnp.dot(tri_ref[...], onehot.astype(bf16), preferred_element_type=f32) + carry_ref[...]

        meta = jnp.zeros((ROUTE_TM, LANES), jnp.int32)
        gate = jnp.zeros((ROUTE_TM, LANES), f32)
        for k, (idx, hit) in enumerate(hits):
            rank = jnp.sum(jnp.where(hit, before, 0.0), axis=-1, keepdims=True).astype(jnp.int32)
            meta = jnp.where(lane == k, idx.astype(jnp.int32), meta)
            meta = jnp.where(lane == TOP_K + k, rank, meta)
            gate = jnp.where(lane == k, exps[k] / denom, gate)
        meta_ref[rows, :] = meta
        gate_ref[...] = gate
        carry_ref[...] += jnp.sum(onehot, axis=0, keepdims=True)
        cnt_ref[...] = carry_ref[...]

    @pl.when(phase == 1)
    def _():
        n_chunks = jnp.floor((carry_ref[...] + float(ROW_CHUNK - 1)) * (1.0 / ROW_CHUNK))
        src = lax.broadcasted_iota(jnp.int32, (LANES, LANES), 0)
        dst = lax.broadcasted_iota(jnp.int32, (LANES, LANES), 1)
        earlier = jnp.where(src < dst, 1.0, 0.0).astype(bf16)
        first_chunk = jnp.dot(jnp.broadcast_to(n_chunks, (SUBLANES, LANES)).astype(bf16), earlier,
                              preferred_element_type=f32)[0:1]
        first_row = first_chunk * float(ROW_CHUNK)
        meta = meta_ref[rows, :]
        dest = jnp.zeros((ROUTE_TM, LANES), jnp.int32)
        for k in range(TOP_K):
            hit = lane == meta[:, k:k + 1]
            base = jnp.sum(jnp.where(hit, first_row, 0.0), axis=-1, keepdims=True).astype(jnp.int32)
            dest = jnp.where(lane == k, base + meta[:, TOP_K + k:TOP_K + k + 1], dest)
        dest_ref[...] = dest


def _route(logits):
    tri = (jnp.arange(ROUTE_TM)[:, None] > jnp.arange(ROUTE_TM)[None, :]).astype(bf16)
    n_tiles = SEQ // ROUTE_TM
    phase0 = lambda p, i: (jnp.where(p == 0, i, n_tiles - 1), 0)
    phase1 = lambda p, i: (i * p, 0)
    return pl.pallas_call(
        _route_kernel,
        name="route",
        grid=(2, n_tiles),
        in_specs=[
            pl.BlockSpec((ROUTE_TM, LANES), phase0),
            pl.BlockSpec((ROUTE_TM, ROUTE_TM), lambda p, i: (0, 0)),
        ],
        out_specs=[
            pl.BlockSpec((ROUTE_TM, LANES), phase1),
            pl.BlockSpec((ROUTE_TM, LANES), phase0),
            pl.BlockSpec((1, LANES), lambda p, i: (0, 0)),
        ],
        out_shape=[
            jax.ShapeDtypeStruct((SEQ, LANES), jnp.int32),
            jax.ShapeDtypeStruct((SEQ, LANES), f32),
            jax.ShapeDtypeStruct((1, LANES), f32),
        ],
        scratch_shapes=[pltpu.VMEM((1, LANES), f32), pltpu.VMEM((SEQ, LANES), jnp.int32)],
        compiler_params=_cparams(("arbitrary", "arbitrary")),
    )(logits, tri)


def _routing_tables(counts):
    cnt = counts[0, :N_EXPERTS].astype(jnp.int32)
    n_chunks = (cnt + ROW_CHUNK - 1) // ROW_CHUNK
    chunk_base = jnp.cumsum(n_chunks) - n_chunks
    total_chunks = jnp.sum(n_chunks)
    last_chunk = jnp.concatenate([jnp.where(cnt > 0, chunk_base + n_chunks - 1, -1),
                                  total_chunks[None]]).astype(jnp.int32)

    n_sb = (n_chunks + CHUNKS_PER_SB - 1) // CHUNKS_PER_SB
    sb_end = jnp.cumsum(n_sb)
    sb_start = sb_end - n_sb
    n_used = sb_end[-1]
    s = jnp.arange(MAX_SB)
    s_eff = jnp.minimum(s, n_used - 1)
    e = jnp.minimum(jnp.searchsorted(sb_end, s_eff, side='right'), N_EXPERTS - 1)
    kk = s_eff - sb_start[e]
    per_sb = n_chunks[e] // jnp.maximum(n_sb[e], 1)
    extra = n_chunks[e] - per_sb * n_sb[e]
    sb_chunk0 = chunk_base[e] + kk * per_sb + jnp.minimum(kk, extra)
    sb_n = jnp.where(s < n_used, per_sb + (kk < extra), 0)
    used = jnp.stack([n_used, total_chunks]).astype(jnp.int32)
    return (last_chunk, e.astype(jnp.int32), sb_chunk0.astype(jnp.int32), sb_n.astype(jnp.int32), used)


DISP_TM = 512


def _packed_rows(first_row, n_rows=1):
    return pl.ds(pl.multiple_of(first_row * ROW_TILE, ROW_TILE), n_rows * ROW_TILE)


def _row_copy(src, src_row, dst, dst_row, sem):
    return pltpu.make_async_copy(src.at[_packed_rows(src_row), :], dst.at[_packed_rows(dst_row), :], sem)


def _dispatch_kernel(last_ref, dest_ref, hn_ref, xs_ref, zero_ref, sem_ref):
    i = pl.program_id(0)

    @pl.when(i == 0)
    def _():
        zero_ref[...] = jnp.zeros_like(zero_ref)

        def chunk_copy(c):
            return pltpu.make_async_copy(zero_ref, xs_ref.at[_packed_rows(c * ROW_CHUNK, ROW_CHUNK), :],
                                         sem_ref.at[1])

        def start(e, _):
            @pl.when(last_ref[e] >= 0)
            def _():
                chunk_copy(last_ref[e]).start()
            return 0

        def wait(e, _):
            @pl.when(last_ref[e] >= 0)
            def _():
                chunk_copy(last_ref[e]).wait()
            return 0

        def start_tail(c, _):
            chunk_copy(c).start()
            return 0

        def wait_tail(c, _):
            chunk_copy(c).wait()
            return 0

        lax.fori_loop(0, N_EXPERTS, start, 0)
        lax.fori_loop(last_ref[N_EXPERTS], MAX_CHUNKS, start_tail, 0)
        lax.fori_loop(0, N_EXPERTS, wait, 0)
        lax.fori_loop(last_ref[N_EXPERTS], MAX_CHUNKS, wait_tail, 0)

    def issue(t, _):
        for k in range(TOP_K):
            _row_copy(hn_ref, t, xs_ref, dest_ref[t * TOP_K + k], sem_ref.at[0]).start(priority=k % 2)
        return 0

    lax.fori_loop(0, DISP_TM, issue, 0, unroll=8)
    for k in range(TOP_K):
        pltpu.make_async_copy(hn_ref, xs_ref.at[_packed_rows(0, DISP_TM), :], sem_ref.at[0]).wait()


def _dispatch(last_chunk, dest, hn):
    return pl.pallas_call(
        _dispatch_kernel,
        name="dispatch",
        grid_spec=pltpu.PrefetchScalarGridSpec(
            num_scalar_prefetch=1,
            grid=(SEQ // DISP_TM,),
            in_specs=[
                pl.BlockSpec((DISP_TM * TOP_K,), lambda i, last: (i,), memory_space=pltpu.SMEM),
                pl.BlockSpec((DISP_TM * ROW_TILE, LANES), lambda i, last: (i, 0)),
            ],
            out_specs=pl.BlockSpec(memory_space=pl.ANY),
            scratch_shapes=[
                pltpu.VMEM((ROW_CHUNK * ROW_TILE, LANES), jnp.uint32),
                pltpu.SemaphoreType.DMA((2,)),
            ],
        ),
        out_shape=jax.ShapeDtypeStruct((MAX_ROWS * ROW_TILE, LANES), jnp.uint32),
        compiler_params=_cparams(("arbitrary",)),
    )(last_chunk, dest, hn)


def _experts_kernel(e_ref, c0_ref, n_ref, used_ref,
                    xs_ref, wg_ref, wu_ref, wd_ref, bg_ref, bu_ref, bd_ref, ys_ref,
                    xin_ref, xbf_ref, act_ref, acc_ref, pend_ref, sem_ref):
    s = pl.program_id(0)
    f = pl.program_id(1)
    n = n_ref[s]
    c0 = c0_ref[s]

    def rows(c, k=1):
        return pl.ds(pl.multiple_of(c * ROW_CHUNK, ROW_CHUNK), k * ROW_CHUNK)

    def cover(body):
        n4 = n // 4

        def quad(i, _):
            body(i * 4, 4)
            return 0

        lax.fori_loop(0, n4, quad, 0)

        @pl.when((n & 2) != 0)
        def _():
            body(n4 * 4, 2)

        @pl.when((n & 1) != 0)
        def _():
            body(n4 * 4 + (n & 2), 1)

    def drain():
        def wait_one(i, _):
            pltpu.make_async_copy(acc_ref.at[0:ROW_CHUNK, :], ys_ref.at[0:ROW_CHUNK, :], sem_ref.at[1]).wait()
            return 0

        lax.fori_loop(0, pend_ref[0], wait_one, 0)
        pend_ref[0] = 0

    @pl.when(jnp.logical_and(s == 0, f == 0))
    def _():
        pend_ref[0] = 0

    def chunk_copy(src_chunk, c):
        return pltpu.make_async_copy(xs_ref.at[_packed_rows(src_chunk * ROW_CHUNK, ROW_CHUNK), :],
                                     xin_ref.at[_packed_rows(c * ROW_CHUNK, ROW_CHUNK), :], sem_ref.at[0])

    def fetch(first_chunk, count):
        def start(c, _):
            chunk_copy(first_chunk + c, c).start()
            return 0

        lax.fori_loop(0, count, start, 0)

    @pl.when(jnp.logical_and(s == 0, f == 0))
    def _():
        fetch(c0, n)

    @pl.when(jnp.logical_and(n > 0, f == 0))
    def _():
        def finish(c, _):
            chunk_copy(c0 + c, c).wait()
            return 0

        def unpack(c, _):
            for j in range(ROW_TILE):
                w = xin_ref[pl.ds(c * (ROW_CHUNK * ROW_TILE) + j, ROW_CHUNK, stride=ROW_TILE), :]
                low = lax.bitcast_convert_type(w << 16, f32)
                high = lax.bitcast_convert_type(w & jnp.uint32(0xFFFF0000), f32)
                xbf_ref[rows(c), j * LANES:(j + 1) * LANES] = low.astype(bf16)
                xbf_ref[rows(c), HALF_D + j * LANES:HALF_D + (j + 1) * LANES] = high.astype(bf16)
            return 0

        lax.fori_loop(0, n, finish, 0)
        lax.fori_loop(0, n, unpack, 0)

    @pl.when(jnp.logical_and(s + 1 < MAX_SB, f == 1))
    def _():
        nxt = jnp.minimum(s + 1, MAX_SB - 1)
        fetch(c0_ref[nxt], n_ref[nxt])

    @pl.when(n > 0)
    def _():
        bg = bg_ref[0]
        bu = bu_ref[0]

        def up_body(c, k):
            x = xbf_ref[rows(c, k), :]
            g = jnp.dot(x, wg_ref[0].astype(bf16), preferred_element_type=f32) + bg
            u = jnp.dot(x, wu_ref[0].astype(bf16), preferred_element_type=f32) + bu
            g = jnp.minimum(g, SWIGLU_LIMIT)
            u = jnp.clip(u, -SWIGLU_LIMIT, SWIGLU_LIMIT)
            a = (u + 1.0) * (g * (1.0 / (1.0 + jnp.exp(-SWIGLU_ALPHA * g))))
            act_ref[rows(c, k), :] = a.astype(bf16)

        cover(up_body)

        @pl.when(f == 0)
        def _():
            drain()
            bias = jnp.broadcast_to(bd_ref[0], (ROW_CHUNK, D_MODEL))

            def init(c, _):
                acc_ref[rows(c), :] = bias
                return 0

            lax.fori_loop(0, n, init, 0)

        def down_body(c, k):
            acc_ref[rows(c, k), :] += jnp.dot(act_ref[rows(c, k), :], wd_ref[0].astype(bf16),
                                              preferred_element_type=f32)

        cover(down_body)

        @pl.when(f == N_FF_TILES - 1)
        def _():
            def write(c, _):
                pltpu.make_async_copy(acc_ref.at[rows(c), :], ys_ref.at[rows(c0 + c), :], sem_ref.at[1]).start()
                return 0

            lax.fori_loop(0, n, write, 0)
            pend_ref[0] = n

    @pl.when(jnp.logical_and(s == pl.num_programs(0) - 1, f == N_FF_TILES - 1))
    def _():
        drain()
        acc_ref[0:ROW_CHUNK, :] = jnp.zeros((ROW_CHUNK, D_MODEL), f32)

        def tail_copy(c):
            return pltpu.make_async_copy(acc_ref.at[0:ROW_CHUNK, :], ys_ref.at[rows(c), :], sem_ref.at[0])

        def start(c, _):
            tail_copy(c).start()
            return 0

        def finish(c, _):
            tail_copy(c).wait()
            return 0

        lax.fori_loop(used_ref[1], MAX_CHUNKS, start, 0)
        lax.fori_loop(used_ref[1], MAX_CHUNKS, finish, 0)


def _experts(sb_e, sb_c0, sb_n, n_used, xs, w_gate, b_gate, w_up, b_up, w_down, b_down):
    def up_map(s, f, e, c0, n, used):
        return (e[s], 0, f)

    def down_map(s, f, e, c0, n, used):
        return (e[s], f, 0)

    def bias_map(s, f, e, c0, n, used):
        return (e[s], 0, 0)

    return pl.pallas_call(
        _experts_kernel,
        name="experts",
        grid_spec=pltpu.PrefetchScalarGridSpec(
            num_scalar_prefetch=4,
            grid=(n_used[0], N_FF_TILES),
            in_specs=[
                pl.BlockSpec(memory_space=pl.ANY),
                pl.BlockSpec((1, D_MODEL, FF_TILE), up_map),
                pl.BlockSpec((1, D_MODEL, FF_TILE), up_map),
                pl.BlockSpec((1, FF_TILE, D_MODEL), down_map),
                pl.BlockSpec((1, 1, FF_TILE), up_map),
                pl.BlockSpec((1, 1, FF_TILE), up_map),
                pl.BlockSpec((1, 1, D_MODEL), bias_map),
            ],
            out_specs=pl.BlockSpec(memory_space=pl.ANY),
            scratch_shapes=[
                pltpu.VMEM((SB_ROWS * ROW_TILE, LANES), jnp.uint32),
                pltpu.VMEM((SB_ROWS, D_MODEL), bf16),
                pltpu.VMEM((SB_ROWS, FF_TILE), bf16),
                pltpu.VMEM((SB_ROWS, D_MODEL), f32),
                pltpu.SMEM((1,), jnp.int32),
                pltpu.SemaphoreType.DMA((2,)),
            ],
        ),
        out_shape=jax.ShapeDtypeStruct((MAX_ROWS, D_MODEL), f32),
        compiler_params=_cparams(("arbitrary", "arbitrary"), EXPERTS_VMEM_LIMIT),
    )(sb_e, sb_c0, sb_n, n_used, xs, w_gate, w_up, w_down,
      b_gate.reshape(N_EXPERTS, 1, D_FF), b_up.reshape(N_EXPERTS, 1, D_FF),
      b_down.reshape(N_EXPERTS, 1, D_MODEL))


COMB_TM = 256


def _combine_kernel(dest_ref, next_ref, ys_ref, x1_ref, gate_ref, gf_ref, o_ref, buf_ref, sem_ref):
    i = pl.program_id(0)
    slot = i % 2

    def gather(rows_ref, into):
        def issue(t, _):
            for k in range(TOP_K):
                pltpu.make_async_copy(ys_ref.at[pl.ds(rows_ref[t * TOP_K + k], 1), :],
                                      buf_ref.at[into, k, pl.ds(t, 1), :],
                                      sem_ref.at[into]).start(priority=k % 2)
            return 0

        lax.fori_loop(0, COMB_TM, issue, 0, unroll=8)

    @pl.when(i == 0)
    def _():
        gather(dest_ref, 0)

    has_next = i + 1 < pl.num_programs(0)
    for into in range(2):
        @pl.when(jnp.logical_and(has_next, slot == 1 - into))
        def _():
            gather(next_ref, into)

    for k in range(TOP_K):
        pltpu.make_async_copy(ys_ref.at[pl.ds(0, COMB_TM), :], buf_ref.at[slot, k], sem_ref.at[slot]).wait()

    gate = gate_ref[...]
    acc = x1_ref[...]
    for k in range(TOP_K):
        acc = acc + gate[:, k:k + 1] * buf_ref[slot, k]
    o_ref[...] = _rms(acc, gf_ref[...])


def _combine(dest, ys, x1, gates, g_final):
    n_tiles = SEQ // COMB_TM
    return pl.pallas_call(
        _combine_kernel,
        name="combine",
        grid=(SEQ // COMB_TM,),
        in_specs=[
            pl.BlockSpec((COMB_TM * TOP_K,), lambda i: (i,), memory_space=pltpu.SMEM),
            pl.BlockSpec((COMB_TM * TOP_K,), lambda i: (jnp.minimum(i + 1, n_tiles - 1),),
                         memory_space=pltpu.SMEM),
            pl.BlockSpec(memory_space=pl.ANY),
            pl.BlockSpec((COMB_TM, D_MODEL), lambda i: (i, 0)),
            pl.BlockSpec((COMB_TM, LANES), lambda i: (i, 0)),
            pl.BlockSpec((1, D_MODEL), lambda i: (0, 0)),
        ],
        out_specs=pl.BlockSpec((COMB_TM, D_MODEL), lambda i: (i, 0)),
        out_shape=jax.ShapeDtypeStruct((SEQ, D_MODEL), f32),
        scratch_shapes=[
            pltpu.VMEM((2, TOP_K, COMB_TM, D_MODEL), f32),
            pltpu.SemaphoreType.DMA((2,)),
        ],
        compiler_params=_cparams(("arbitrary",)),
    )(dest, dest, ys, x1, gates, g_final.reshape(1, D_MODEL))


def kernel(x, g_mix, w_in, lam_re_fwd, lam_im_fwd, log_dt_fwd, b_re_fwd, b_im_fwd, c_re_fwd, c_im_fwd, lam_re_bwd, lam_im_bwd, log_dt_bwd, b_re_bwd, b_im_bwd, c_re_bwd, c_im_bwd, ssm_d, w_glu, b_glu, na_rpb, g_ssm_out, g_na_out, w_out, g_moe, w_router, b_router, w_gate, b_gate, w_up, b_up, w_down, b_down, g_final):
    x2 = x.reshape(SEQ, D_MODEL)

    u, qkv = _in_proj(x2, g_mix[0], w_in[0].astype(bf16))

    s5_w, s5_dec = _s5_weights(
        (lam_re_fwd[0], lam_im_fwd[0], log_dt_fwd[0], b_re_fwd[0], b_im_fwd[0], c_re_fwd[0], c_im_fwd[0]),
        (lam_re_bwd[0], lam_im_bwd[0], log_dt_bwd[0], b_re_bwd[0], b_im_bwd[0], c_re_bwd[0], c_im_bwd[0]))
    y = _s5(u, s5_w, s5_dec, ssm_d[0])
    ssm_n = _glu(y, w_glu[0].astype(bf16), b_glu[0], g_ssm_out[0])

    y_na = _natten(qkv, _na_bias(na_rpb[0]))

    w_router_pad = jnp.zeros((D_MODEL, LANES), bf16).at[:, :N_EXPERTS].set(w_router[0].astype(bf16))
    b_router_pad = jnp.full((1, LANES), NEG_BIG, f32).at[0, :N_EXPERTS].set(b_router[0].astype(f32))
    x1, hn, logits = _out_proj(ssm_n, y_na, x2, g_na_out[0], w_out[0].astype(bf16), g_moe[0],
                               w_router_pad, b_router_pad)

    dest_lanes, gates, counts = _route(logits)
    dest = dest_lanes[:, :TOP_K].reshape(-1)
    last_chunk, sb_e, sb_c0, sb_n, n_used = _routing_tables(counts)

    xs = _dispatch(last_chunk, dest, hn)
    ys = _experts(sb_e, sb_c0, sb_n, n_used, xs, w_gate[0], b_gate[0], w_up[0], b_up[0],
                  w_down[0], b_down[0])
    out = _combine(dest, ys, x1, gates, g_final)
    return out.reshape(x.shape)
```

```python
import functools
import math

import jax
import jax.numpy as jnp
from jax import lax
from jax.experimental import pallas as pl
from jax.experimental.pallas import tpu as pltpu

f32 = jnp.float32
bf16 = jnp.bfloat16

D_MODEL = 2048
SEQ = 8192
SSM_WIDTH = 1024
NA_WIDTH = 1024
SSM_GROUP = 16
SSM_GROUPS = 64
SSM_STATE = 64
NA_HEAD_DIM = 64
NA_HEADS = 16
GRID_W = 64
GRID_ROWS = SEQ // GRID_W
NA_KH = 8
NA_KW = 16
N_EXPERTS = 32
TOP_K = 4
D_FF = 2048
SWIGLU_LIMIT = 7.0
SWIGLU_ALPHA = 1.702
RMS_EPS = 1e-5

LANES = 128
HALF_D = D_MODEL // 2
SUBLANES = 8
ROW_TILE = HALF_D // LANES
assert ROW_TILE == SUBLANES
NEG_BIG = -1e30

CHUNK_T = 16
N_CHUNKS = SEQ // CHUNK_T
GROUPS_PER_BLOCK = LANES // SSM_GROUP
N_LANE_BLOCKS = SSM_WIDTH // LANES
CAT_W = CHUNK_T * LANES
STATE_W = GROUPS_PER_BLOCK * SSM_STATE

ROW_CHUNK = 128
CHUNKS_PER_SB = 12
SB_ROWS = ROW_CHUNK * CHUNKS_PER_SB
MAX_CHUNKS = SEQ * TOP_K // ROW_CHUNK + N_EXPERTS
MAX_ROWS = MAX_CHUNKS * ROW_CHUNK
MAX_SB = MAX_CHUNKS // CHUNKS_PER_SB + N_EXPERTS
FF_TILE = 512
N_FF_TILES = D_FF // FF_TILE

VMEM_LIMIT = 56 * 1024 * 1024
EXPERTS_VMEM_LIMIT = 60 * 1024 * 1024


def _cparams(semantics, vmem=VMEM_LIMIT):
    return pltpu.CompilerParams(dimension_semantics=semantics, vmem_limit_bytes=vmem)


def _rms(x, g):
    return x * lax.rsqrt(jnp.mean(x * x, axis=-1, keepdims=True) + RMS_EPS) * g


IN_TM = 1024
IN_TN = 1024


def _in_proj_kernel(x_ref, g_ref, w_ref, u_ref, qkv_ref, h_ref):
    j = pl.program_id(1)

    @pl.when(j == 0)
    def _():
        h_ref[...] = _rms(x_ref[...], g_ref[...]).astype(bf16)

    acc = jnp.dot(h_ref[...], w_ref[...], preferred_element_type=f32)

    @pl.when(j == 0)
    def _():
        u_ref[...] = acc

    @pl.when(j > 0)
    def _():
        qkv_ref[...] = acc.astype(bf16)


def _in_proj(x, g_mix, w_in_bf):
    n_out = w_in_bf.shape[1]
    return pl.pallas_call(
        _in_proj_kernel,
        name="in_proj",
        grid=(SEQ // IN_TM, n_out // IN_TN),
        in_specs=[
            pl.BlockSpec((IN_TM, D_MODEL), lambda i, j: (i, 0)),
            pl.BlockSpec((1, D_MODEL), lambda i, j: (0, 0)),
            pl.BlockSpec((D_MODEL, IN_TN), lambda i, j: (0, j)),
        ],
        out_specs=[
            pl.BlockSpec((IN_TM, IN_TN), lambda i, j: (i, 0)),
            pl.BlockSpec((IN_TM, IN_TN), lambda i, j: (i, jnp.maximum(j - 1, 0))),
        ],
        out_shape=[
            jax.ShapeDtypeStruct((SEQ, SSM_WIDTH), f32),
            jax.ShapeDtypeStruct((SEQ, 3 * NA_WIDTH), bf16),
        ],
        scratch_shapes=[pltpu.VMEM((IN_TM, D_MODEL), bf16)],
        compiler_params=_cparams(("arbitrary", "arbitrary")),
    )(x, g_mix.reshape(1, D_MODEL), w_in_bf)


def _cmul(ar, ai, br, bi):
    return ar * br - ai * bi, ar * bi + ai * br


def _s5_discretise(lam_re, lam_im, log_dt, b_re, b_im, c_re, c_im):
    a = jnp.minimum(lam_re.astype(f32), -1e-4)
    w = lam_im.astype(f32)
    dt = jnp.exp(log_dt.astype(f32))[:, None]
    steps = jnp.arange(CHUNK_T + 1, dtype=f32)[:, None, None]
    mag = jnp.exp((a * dt)[None] * steps)
    ang = (w * dt)[None] * steps
    pw = (mag * jnp.cos(ang), mag * jnp.sin(ang))
    xr, xi = pw[0][1] - 1.0, pw[1][1]
    den = a * a + w * w
    qr, qi = (xr * a + xi * w) / den, (xi * a - xr * w) / den
    bb = _cmul(qr[..., None], qi[..., None], b_re.astype(f32), b_im.astype(f32))
    return pw, bb, (c_re.astype(f32), c_im.astype(f32))


def _pair_blockdiag(m):
    z = jnp.zeros_like(m[..., 0, :, :])
    top = jnp.concatenate([m[..., 0, :, :], z], axis=-1)
    bot = jnp.concatenate([z, m[..., 1, :, :]], axis=-1)
    return jnp.concatenate([top, bot], axis=-2)


def _s5_weights(fwd, bwd):
    hi = lax.Precision.HIGH
    exact = lax.Precision.HIGHEST
    t = CHUNK_T
    pw_f, bb_f, c_f = _s5_discretise(*fwd)
    pw_b, bb_b, c_b = _s5_discretise(*bwd)
    pairs = (N_LANE_BLOCKS, GROUPS_PER_BLOCK // 2, 2)

    x = jnp.arange(t * SSM_GROUP)
    t_of_x = x // SSM_GROUP
    tile_c = (jnp.arange(SSM_GROUP)[:, None] == x[None, :] % SSM_GROUP).astype(f32)

    def c_over_x(c):
        return [jnp.einsum('gcp,cx->gpx', part, tile_c, precision=exact) for part in c]

    def c_times_power(c_x, pw, power_of_x):
        rep = (jnp.arange(t + 1)[:, None] == power_of_x[None, :]).astype(f32)
        pw_x = [jnp.einsum('jgp,jx->gpx', part, rep, precision=exact) for part in pw]
        return _cmul(c_x[0], c_x[1], pw_x[0], pw_x[1])

    cx_f, cx_b = c_over_x(c_f), c_over_x(c_b)

    def lag_kernel(c_x, pw, bb, power_of_x):
        m_re, m_im = c_times_power(c_x, pw, power_of_x)
        m = jnp.concatenate([m_re, m_im], axis=1)
        b = jnp.concatenate([jnp.swapaxes(bb[0], 1, 2), -jnp.swapaxes(bb[1], 1, 2)], axis=-1)
        return jnp.einsum('gdp,gpx->gdx', b, m, precision=hi)

    kf = lag_kernel(cx_f, pw_f, bb_f, t_of_x)
    kb = lag_kernel(cx_b, pw_b, bb_b, t - 1 - t_of_x)
    keep = (t - 1) * SSM_GROUP
    k_lag = jnp.concatenate([kb[..., :keep], kb[..., keep:] + kf[..., :SSM_GROUP], kf[..., SSM_GROUP:]],
                            axis=-1)
    w_intra = jnp.stack([k_lag[..., (t - 1 - ti) * SSM_GROUP:(2 * t - 1 - ti) * SSM_GROUP]
                         for ti in range(t)], axis=1)
    w_intra = _pair_blockdiag(w_intra.astype(bf16).reshape(*pairs, t * SSM_GROUP, t * SSM_GROUP))

    def state_in(pw_t, bb):
        pw_g = [jnp.transpose(part, (1, 0, 2))[:, :, None, :] for part in pw_t]
        bb_g = [jnp.swapaxes(part, 1, 2)[:, None] for part in bb]
        return [_pair_blockdiag(part.astype(bf16).reshape(*pairs, t * SSM_GROUP, SSM_STATE))
                for part in _cmul(pw_g[0], pw_g[1], bb_g[0], bb_g[1])]

    w_in = jnp.concatenate(state_in([part[:t][::-1] for part in pw_f], bb_f)
                           + state_in([part[:t] for part in pw_b], bb_b), axis=-1)

    def state_out(c_x, pw, power_of_x):
        m_re, m_im = c_times_power(c_x, pw, power_of_x)
        return [_pair_blockdiag(part.astype(bf16).reshape(*pairs, SSM_STATE, t * SSM_GROUP))
                for part in (m_re, -m_im)]

    w_out = jnp.concatenate(state_out(cx_f, pw_f, t_of_x + 1) + state_out(cx_b, pw_b, t - t_of_x), axis=-2)

    def decay(a):
        return a.reshape(N_LANE_BLOCKS, 1, STATE_W)

    dec = jnp.concatenate([decay(pw_f[0][t]), decay(pw_f[1][t]), decay(pw_b[0][t]), decay(pw_b[1][t])],
                          axis=-1)
    return (w_in, w_intra, w_out), dec.astype(f32)


def _gelu_tanh(x):
    return 0.5 * x * (1.0 + jnp.tanh(math.sqrt(2.0 / math.pi) * (x + 0.044715 * (x * x * x))))


PAIRS_PER_BLOCK = GROUPS_PER_BLOCK // 2
PAIR_W = 2 * CHUNK_T * SSM_GROUP


def _s5_kernel(u_ref, win_ref, wintra_ref, wout_ref, dec_ref, d_ref, y_ref,
               cat_ref, catp_ref, st_ref, perm_ref):
    sw = STATE_W

    @pl.when(pl.program_id(0) == 0)
    def _():
        def strip(t, _):
            row = lax.broadcasted_iota(jnp.int32, (LANES, CAT_W), 0)
            col = lax.broadcasted_iota(jnp.int32, (LANES, CAT_W), 1)
            target = (row // SSM_GROUP) * (CHUNK_T * SSM_GROUP) + t * SSM_GROUP + row % SSM_GROUP
            perm_ref[pl.ds(pl.multiple_of(t * LANES, LANES), LANES), :] = (
                jnp.where(col == target, 1.0, 0.0).astype(bf16))
            return 0

        lax.fori_loop(0, CHUNK_T, strip, 0)

    for t in range(CHUNK_T):
        cat_ref[:, t * LANES:(t + 1) * LANES] = u_ref[pl.ds(t, N_CHUNKS, stride=CHUNK_T), :].astype(bf16)
    catp_ref[...] = jnp.dot(cat_ref[...], perm_ref[...], preferred_element_type=f32).astype(bf16)

    for p in range(PAIRS_PER_BLOCK):
        z = jnp.dot(catp_ref[:, p * PAIR_W:(p + 1) * PAIR_W], win_ref[0, p], preferred_element_type=f32)
        for part in range(4):
            st_ref[:, part * sw + p * LANES:part * sw + (p + 1) * LANES] = z[:, part * LANES:(part + 1) * LANES]

    afr = dec_ref[0, :, 0 * sw:1 * sw]
    afi = dec_ref[0, :, 1 * sw:2 * sw]
    abr = dec_ref[0, :, 2 * sw:3 * sw]
    abi = dec_ref[0, :, 3 * sw:4 * sw]

    def step(i, carry):
        sfr, sfi, sbr, sbi = carry
        r = N_CHUNKS - 1 - i
        zf = st_ref[pl.ds(i, 1), 0:2 * sw]
        zb = st_ref[pl.ds(r, 1), 2 * sw:4 * sw]
        st_ref[pl.ds(i, 1), 0:2 * sw] = jnp.concatenate([sfr, sfi], axis=-1)
        st_ref[pl.ds(r, 1), 2 * sw:4 * sw] = jnp.concatenate([sbr, sbi], axis=-1)
        nfr = afr * sfr - afi * sfi + zf[:, :sw]
        nfi = afi * sfr + afr * sfi + zf[:, sw:]
        nbr = abr * sbr - abi * sbi + zb[:, :sw]
        nbi = abi * sbr + abr * sbi + zb[:, sw:]
        return nfr, nfi, nbr, nbi

    zero = jnp.zeros((1, sw), f32)
    lax.fori_loop(0, N_CHUNKS, step, (zero, zero, zero, zero))

    for p in range(PAIRS_PER_BLOCK):
        s_in = jnp.concatenate(
            [st_ref[:, part * sw + p * LANES:part * sw + (p + 1) * LANES] for part in range(4)], axis=-1)
        yp = jnp.dot(catp_ref[:, p * PAIR_W:(p + 1) * PAIR_W], wintra_ref[0, p], preferred_element_type=f32)
        yp = yp + jnp.dot(s_in.astype(bf16), wout_ref[0, p], preferred_element_type=f32)
        cat_ref[:, p * PAIR_W:(p + 1) * PAIR_W] = yp.astype(bf16)

    st_ref[...] = lax.dot_general(cat_ref[...], perm_ref[...], (((1,), (1,)), ((), ())),
                                  preferred_element_type=f32)
    d = d_ref[...]
    for t in range(CHUNK_T):
        rows = pl.ds(t, N_CHUNKS, stride=CHUNK_T)
        v = st_ref[:, t * LANES:(t + 1) * LANES] + d * u_ref[rows, :]
        y_ref[rows, :] = _gelu_tanh(v)


def _s5(u, weights, dec, ssm_d):
    w_spec = pl.BlockSpec((1, PAIRS_PER_BLOCK, PAIR_W, PAIR_W), lambda j: (j, 0, 0, 0))
    return pl.pallas_call(
        _s5_kernel,
        name="s5_scan",
        grid=(N_LANE_BLOCKS,),
        in_specs=[
            pl.BlockSpec((SEQ, LANES), lambda j: (0, j)),
            w_spec, w_spec, w_spec,
            pl.BlockSpec((1, 1, 4 * STATE_W), lambda j: (j, 0, 0)),
            pl.BlockSpec((1, LANES), lambda j: (0, j)),
        ],
        out_specs=pl.BlockSpec((SEQ, LANES), lambda j: (0, j)),
        out_shape=jax.ShapeDtypeStruct((SEQ, SSM_WIDTH), f32),
        scratch_shapes=[
            pltpu.VMEM((N_CHUNKS, CAT_W), bf16),
            pltpu.VMEM((N_CHUNKS, CAT_W), bf16),
            pltpu.VMEM((N_CHUNKS, 4 * STATE_W), f32),
            pltpu.VMEM((CAT_W, CAT_W), bf16),
        ],
        compiler_params=_cparams(("arbitrary",)),
    )(u, *weights, dec, ssm_d.reshape(1, SSM_WIDTH))


GLU_TM = 1024


def _glu_kernel(y_ref, w_ref, b_ref, g_ref, o_ref):
    y = y_ref[...]
    z = jnp.dot(y.astype(bf16), w_ref[...], preferred_element_type=f32) + b_ref[...]
    o = y * (1.0 / (1.0 + jnp.exp(-z)))
    o_ref[...] = _rms(o, g_ref[...]).astype(bf16)


def _glu(y, w_glu_bf, b_glu, g_ssm_out):
    return pl.pallas_call(
        _glu_kernel,
        name="glu_norm",
        grid=(SEQ // GLU_TM,),
        in_specs=[
            pl.BlockSpec((GLU_TM, SSM_WIDTH), lambda i: (i, 0)),
            pl.BlockSpec((SSM_WIDTH, SSM_WIDTH), lambda i: (0, 0)),
            pl.BlockSpec((1, SSM_WIDTH), lambda i: (0, 0)),
            pl.BlockSpec((1, SSM_WIDTH), lambda i: (0, 0)),
        ],
        out_specs=pl.BlockSpec((GLU_TM, SSM_WIDTH), lambda i: (i, 0)),
        out_shape=jax.ShapeDtypeStruct((SEQ, SSM_WIDTH), bf16),
        compiler_params=_cparams(("arbitrary",)),
    )(y, w_glu_bf, b_glu.reshape(1, SSM_WIDTH), g_ssm_out.reshape(1, SSM_WIDTH))


NA_ROWS_PER_STEP = 16
NA_WIN = NA_KH * GRID_W
HEADS_PER_BLOCK = LANES // NA_HEAD_DIM


def _na_bias(rpb):
    c = jnp.arange(GRID_W)
    col_start = jnp.clip(c - NA_KW // 2, 0, GRID_W - NA_KW)
    valid = (c[None, :] >= col_start[:, None]) & (c[None, :] < col_start[:, None] + NA_KW)
    dc = jnp.clip(c[None, :] - c[:, None], -(NA_KW - 1), NA_KW - 1) + (NA_KW - 1)
    sel = (dc[None] == jnp.arange(2 * NA_KW - 1)[:, None, None]).astype(f32)
    tab = jnp.einsum('hrc,cqk->hqrk', rpb.astype(f32), sel, precision=lax.Precision.HIGHEST)
    tab = jnp.where(valid[None, :, None, :], tab, NEG_BIG)
    tab = tab.reshape(NA_HEADS, GRID_W, (2 * NA_KH - 1) * GRID_W)
    return jnp.stack([tab[..., (NA_KH - 1 - v) * GRID_W:(2 * NA_KH - 1 - v) * GRID_W]
                      for v in range(NA_KH)], axis=1)


def _natten_kernel(q_ref, k_ref, v_ref, b_ref, o_ref, s_ref, p_ref):
    rb = pl.program_id(1)
    lane = lax.broadcasted_iota(jnp.int32, (GRID_W, LANES), 1)
    head0 = lane < NA_HEAD_DIM
    scale = NA_HEAD_DIM ** -0.5

    starts, variants = [], []
    for i in range(NA_ROWS_PER_STEP):
        r = rb * NA_ROWS_PER_STEP + i
        rs = jnp.clip(r - NA_KH // 2, 0, GRID_ROWS - NA_KH)
        starts.append(pl.multiple_of(rs * GRID_W, GRID_W))
        variants.append(r - rs)

    for i in range(NA_ROWS_PER_STEP):
        q = q_ref[i * GRID_W:(i + 1) * GRID_W, :] * scale
        kw = k_ref[pl.ds(starts[i], NA_WIN), :]
        for h in range(HEADS_PER_BLOCK):
            qh = jnp.where(head0 if h == 0 else ~head0, q, jnp.zeros_like(q))
            s = lax.dot_general(qh, kw, (((1,), (1,)), ((), ())), preferred_element_type=f32)
            s_ref[i * HEADS_PER_BLOCK + h] = s + b_ref[h, variants[i]]

    n_tiles = NA_ROWS_PER_STEP * HEADS_PER_BLOCK
    maxes = [jnp.max(s_ref[t], axis=-1, keepdims=True) for t in range(n_tiles)]
    inv_sums = []
    for t in range(n_tiles):
        p = jnp.exp(s_ref[t] - maxes[t])
        inv_sums.append(1.0 / jnp.sum(p, axis=-1, keepdims=True))
        p_ref[t] = p.astype(bf16)

    for i in range(NA_ROWS_PER_STEP):
        vw = v_ref[pl.ds(starts[i], NA_WIN), :]
        outs = []
        for h in range(HEADS_PER_BLOCK):
            t = i * HEADS_PER_BLOCK + h
            outs.append(jnp.dot(p_ref[t], vw, preferred_element_type=f32) * inv_sums[t])
        o_ref[i * GRID_W:(i + 1) * GRID_W, :] = jnp.where(head0, outs[0], outs[1]).astype(bf16)


def _natten(qkv, bias):
    tm = NA_ROWS_PER_STEP * GRID_W
    n_hb = NA_WIDTH // LANES
    return pl.pallas_call(
        _natten_kernel,
        name="natten",
        grid=(n_hb, GRID_ROWS // NA_ROWS_PER_STEP),
        in_specs=[
            pl.BlockSpec((tm, LANES), lambda h, r: (r, h)),
            pl.BlockSpec((SEQ, LANES), lambda h, r: (0, n_hb + h)),
            pl.BlockSpec((SEQ, LANES), lambda h, r: (0, 2 * n_hb + h)),
            pl.BlockSpec((HEADS_PER_BLOCK, NA_KH, GRID_W, NA_WIN), lambda h, r: (h, 0, 0, 0)),
        ],
        out_specs=pl.BlockSpec((tm, LANES), lambda h, r: (r, h)),
        out_shape=jax.ShapeDtypeStruct((SEQ, NA_WIDTH), bf16),
        scratch_shapes=[
            pltpu.VMEM((NA_ROWS_PER_STEP * HEADS_PER_BLOCK, GRID_W, NA_WIN), f32),
            pltpu.VMEM((NA_ROWS_PER_STEP * HEADS_PER_BLOCK, GRID_W, NA_WIN), bf16),
        ],
        compiler_params=_cparams(("arbitrary", "arbitrary")),
    )(qkv, qkv, qkv, bias)


OUT_TM = 512


def _out_proj_kernel(ssm_ref, na_ref, x_ref, gna_ref, w_ref, gmoe_ref, wr_ref, br_ref,
                     x1_ref, hn_ref, lg_ref):
    na = _rms(na_ref[...].astype(f32), gna_ref[...]).astype(bf16)
    y = jnp.dot(ssm_ref[...], w_ref[0:SSM_WIDTH, :], preferred_element_type=f32)
    y = y + jnp.dot(na, w_ref[SSM_WIDTH:, :], preferred_element_type=f32)
    x1 = x_ref[...] + y
    x1_ref[...] = x1
    hn = _rms(x1, gmoe_ref[...]).astype(bf16)
    lg_ref[...] = jnp.dot(hn, wr_ref[...], preferred_element_type=f32) + br_ref[...]
    bits = lax.bitcast_convert_type(hn.astype(f32), jnp.uint32)
    packed = (bits[:, HALF_D:] & jnp.uint32(0xFFFF0000)) | (bits[:, :HALF_D] >> 16)
    for j in range(ROW_TILE):
        hn_ref[pl.ds(j, OUT_TM, stride=ROW_TILE), :] = packed[:, j * LANES:(j + 1) * LANES]


def _out_proj(ssm_n, y_na, x, g_na_out, w_out_bf, g_moe, w_router_pad, b_router_pad):
    row = lambda i: (i, 0)
    fixed = lambda i: (0, 0)
    return pl.pallas_call(
        _out_proj_kernel,
        name="out_proj",
        grid=(SEQ // OUT_TM,),
        in_specs=[
            pl.BlockSpec((OUT_TM, SSM_WIDTH), row),
            pl.BlockSpec((OUT_TM, NA_WIDTH), row),
            pl.BlockSpec((OUT_TM, D_MODEL), row),
            pl.BlockSpec((1, NA_WIDTH), fixed),
            pl.BlockSpec((D_MODEL, D_MODEL), fixed),
            pl.BlockSpec((1, D_MODEL), fixed),
            pl.BlockSpec((D_MODEL, LANES), fixed),
            pl.BlockSpec((1, LANES), fixed),
        ],
        out_specs=[
            pl.BlockSpec((OUT_TM, D_MODEL), row),
            pl.BlockSpec((OUT_TM * ROW_TILE, LANES), row),
            pl.BlockSpec((OUT_TM, LANES), row),
        ],
        out_shape=[
            jax.ShapeDtypeStruct((SEQ, D_MODEL), f32),
            jax.ShapeDtypeStruct((SEQ * ROW_TILE, LANES), jnp.uint32),
            jax.ShapeDtypeStruct((SEQ, LANES), f32),
        ],
        compiler_params=_cparams(("arbitrary",)),
    )(ssm_n, y_na, x, g_na_out.reshape(1, NA_WIDTH), w_out_bf, g_moe.reshape(1, D_MODEL),
      w_router_pad, b_router_pad)


ROUTE_TM = 1024


def _route_kernel(lg_ref, tri_ref, dest_ref, gate_ref, cnt_ref, carry_ref, meta_ref):
    phase = pl.program_id(0)
    i = pl.program_id(1)
    rows = pl.ds(pl.multiple_of(i * ROUTE_TM, ROUTE_TM), ROUTE_TM)
    lane = lax.broadcasted_iota(jnp.int32, (ROUTE_TM, LANES), 1)

    @pl.when(phase == 0)
    def _():
        @pl.when(i == 0)
        def _():
            carry_ref[...] = jnp.zeros_like(carry_ref)

        lane_f = lane.astype(f32)
        work = lg_ref[...]
        vals, hits = [], []
        for _ in range(TOP_K):
            m = jnp.max(work, axis=-1, keepdims=True)
            idx = jnp.min(jnp.where(work == m, lane_f, float(LANES)), axis=-1, keepdims=True)
            hit = lane_f == idx
            vals.append(m)
            hits.append((idx, hit))
            work = jnp.where(hit, -jnp.inf, work)

        exps = [jnp.exp(v - vals[0]) for v in vals]
        denom = exps[0] + exps[1] + exps[2] + exps[3]

        onehot = jnp.zeros((ROUTE_TM, LANES), f32)
        for _, hit in hits:
            onehot = onehot + hit.astype(f32)
        before = jnp.dot(tri_ref[...], onehot.astype(bf16), preferred_element_type=f32) + carry_ref[...]

        meta = jnp.zeros((ROUTE_TM, LANES), jnp.int32)
        gate = jnp.zeros((ROUTE_TM, LANES), f32)
        for k, (idx, hit) in enumerate(hits):
            rank = jnp.sum(jnp.where(hit, before, 0.0), axis=-1, keepdims=True).astype(jnp.int32)
            meta = jnp.where(lane == k, idx.astype(jnp.int32), meta)
            meta = jnp.where(lane == TOP_K + k, rank, meta)
            gate = jnp.where(lane == k, exps[k] / denom, gate)
        meta_ref[rows, :] = meta
        gate_ref[...] = gate
        carry_ref[...] += jnp.sum(onehot, axis=0, keepdims=True)
        cnt_ref[...] = carry_ref[...]

    @pl.when(phase == 1)
    def _():
        n_chunks = jnp.floor((carry_ref[...] + float(ROW_CHUNK - 1)) * (1.0 / ROW_CHUNK))
        src = lax.broadcasted_iota(jnp.int32, (LANES, LANES), 0)
        dst = lax.broadcasted_iota(jnp.int32, (LANES, LANES), 1)
        earlier = jnp.where(src < dst, 1.0, 0.0).astype(bf16)
        first_chunk = jnp.dot(jnp.broadcast_to(n_chunks, (SUBLANES, LANES)).astype(bf16), earlier,
                              preferred_element_type=f32)[0:1]
        first_row = first_chunk * float(ROW_CHUNK)
        meta = meta_ref[rows, :]
        dest = jnp.zeros((ROUTE_TM, LANES), jnp.int32)
        for k in range(TOP_K):
            hit = lane == meta[:, k:k + 1]
            base = jnp.sum(jnp.where(hit, first_row, 0.0), axis=-1, keepdims=True).astype(jnp.int32)
            dest = jnp.where(lane == k, base + meta[:, TOP_K + k:TOP_K + k + 1], dest)
        dest_ref[...] = dest


def _route(logits):
    tri = (jnp.arange(ROUTE_TM)[:, None] > jnp.arange(ROUTE_TM)[None, :]).astype(bf16)
    n_tiles = SEQ // ROUTE_TM
    phase0 = lambda p, i: (jnp.where(p == 0, i, n_tiles - 1), 0)
    phase1 = lambda p, i: (i * p, 0)
    return pl.pallas_call(
        _route_kernel,
        name="route",
        grid=(2, n_tiles),
        in_specs=[
            pl.BlockSpec((ROUTE_TM, LANES), phase0),
            pl.BlockSpec((ROUTE_TM, ROUTE_TM), lambda p, i: (0, 0)),
        ],
        out_specs=[
            pl.BlockSpec((ROUTE_TM, LANES), phase1),
            pl.BlockSpec((ROUTE_TM, LANES), phase0),
            pl.BlockSpec((1, LANES), lambda p, i: (0, 0)),
        ],
        out_shape=[
            jax.ShapeDtypeStruct((SEQ, LANES), jnp.int32),
            jax.ShapeDtypeStruct((SEQ, LANES), f32),
            jax.ShapeDtypeStruct((1, LANES), f32),
        ],
        scratch_shapes=[pltpu.VMEM((1, LANES), f32), pltpu.VMEM((SEQ, LANES), jnp.int32)],
        compiler_params=_cparams(("arbitrary", "arbitrary")),
    )(logits, tri)


def _routing_tables(counts):
    cnt = counts[0, :N_EXPERTS].astype(jnp.int32)
    n_chunks = (cnt + ROW_CHUNK - 1) // ROW_CHUNK
    chunk_base = jnp.cumsum(n_chunks) - n_chunks
    total_chunks = jnp.sum(n_chunks)
    last_chunk = jnp.concatenate([jnp.where(cnt > 0, chunk_base + n_chunks - 1, -1),
                                  total_chunks[None]]).astype(jnp.int32)

    n_sb = (n_chunks + CHUNKS_PER_SB - 1) // CHUNKS_PER_SB
    sb_end = jnp.cumsum(n_sb)
    sb_start = sb_end - n_sb
    n_used = sb_end[-1]
    s = jnp.arange(MAX_SB)
    s_eff = jnp.minimum(s, n_used - 1)
    e = jnp.minimum(jnp.searchsorted(sb_end, s_eff, side='right'), N_EXPERTS - 1)
    kk = s_eff - sb_start[e]
    per_sb = n_chunks[e] // jnp.maximum(n_sb[e], 1)
    extra = n_chunks[e] - per_sb * n_sb[e]
    sb_chunk0 = chunk_base[e] + kk * per_sb + jnp.minimum(kk, extra)
    sb_n = jnp.where(s < n_used, per_sb + (kk < extra), 0)
    used = jnp.stack([n_used, total_chunks]).astype(jnp.int32)
    return (last_chunk, e.astype(jnp.int32), sb_chunk0.astype(jnp.int32), sb_n.astype(jnp.int32), used)


DISP_TM = 512


def _packed_rows(first_row, n_rows=1):
    return pl.ds(pl.multiple_of(first_row * ROW_TILE, ROW_TILE), n_rows * ROW_TILE)


def _row_copy(src, src_row, dst, dst_row, sem):
    return pltpu.make_async_copy(src.at[_packed_rows(src_row), :], dst.at[_packed_rows(dst_row), :], sem)


def _dispatch_kernel(last_ref, dest_ref, hn_ref, xs_ref, zero_ref, sem_ref):
    i = pl.program_id(0)

    @pl.when(i == 0)
    def _():
        zero_ref[...] = jnp.zeros_like(zero_ref)

        def chunk_copy(c):
            return pltpu.make_async_copy(zero_ref, xs_ref.at[_packed_rows(c * ROW_CHUNK, ROW_CHUNK), :],
                                         sem_ref.at[1])

        def start(e, _):
            @pl.when(last_ref[e] >= 0)
            def _():
                chunk_copy(last_ref[e]).start()
            return 0

        def wait(e, _):
            @pl.when(last_ref[e] >= 0)
            def _():
                chunk_copy(last_ref[e]).wait()
            return 0

        def start_tail(c, _):
            chunk_copy(c).start()
            return 0

        def wait_tail(c, _):
            chunk_copy(c).wait()
            return 0

        lax.fori_loop(0, N_EXPERTS, start, 0)
        lax.fori_loop(last_ref[N_EXPERTS], MAX_CHUNKS, start_tail, 0)
        lax.fori_loop(0, N_EXPERTS, wait, 0)
        lax.fori_loop(last_ref[N_EXPERTS], MAX_CHUNKS, wait_tail, 0)

    def issue(t, _):
        for k in range(TOP_K):
            _row_copy(hn_ref, t, xs_ref, dest_ref[t * TOP_K + k], sem_ref.at[0]).start(priority=k % 2)
        return 0

    lax.fori_loop(0, DISP_TM, issue, 0, unroll=8)
    for k in range(TOP_K):
        pltpu.make_async_copy(hn_ref, xs_ref.at[_packed_rows(0, DISP_TM), :], sem_ref.at[0]).wait()


def _dispatch(last_chunk, dest, hn):
    return pl.pallas_call(
        _dispatch_kernel,
        name="dispatch",
        grid_spec=pltpu.PrefetchScalarGridSpec(
            num_scalar_prefetch=1,
            grid=(SEQ // DISP_TM,),
            in_specs=[
                pl.BlockSpec((DISP_TM * TOP_K,), lambda i, last: (i,), memory_space=pltpu.SMEM),
                pl.BlockSpec((DISP_TM * ROW_TILE, LANES), lambda i, last: (i, 0)),
            ],
            out_specs=pl.BlockSpec(memory_space=pl.ANY),
            scratch_shapes=[
                pltpu.VMEM((ROW_CHUNK * ROW_TILE, LANES), jnp.uint32),
                pltpu.SemaphoreType.DMA((2,)),
            ],
        ),
        out_shape=jax.ShapeDtypeStruct((MAX_ROWS * ROW_TILE, LANES), jnp.uint32),
        compiler_params=_cparams(("arbitrary",)),
    )(last_chunk, dest, hn)


def _experts_kernel(e_ref, c0_ref, n_ref, used_ref,
                    xs_ref, wg_ref, wu_ref, wd_ref, bg_ref, bu_ref, bd_ref, ys_ref,
                    xin_ref, xbf_ref, act_ref, acc_ref, pend_ref, sem_ref):
    s = pl.program_id(0)
    f = pl.program_id(1)
    n = n_ref[s]
    c0 = c0_ref[s]

    def rows(c, k=1):
        return pl.ds(pl.multiple_of(c * ROW_CHUNK, ROW_CHUNK), k * ROW_CHUNK)

    def cover(body):
        n4 = n // 4

        def quad(i, _):
            body(i * 4, 4)
            return 0

        lax.fori_loop(0, n4, quad, 0)

        @pl.when((n & 2) != 0)
        def _():
            body(n4 * 4, 2)

        @pl.when((n & 1) != 0)
        def _():
            body(n4 * 4 + (n & 2), 1)

    def drain():
        def wait_one(i, _):
            pltpu.make_async_copy(acc_ref.at[0:ROW_CHUNK, :], ys_ref.at[0:ROW_CHUNK, :], sem_ref.at[1]).wait()
            return 0

        lax.fori_loop(0, pend_ref[0], wait_one, 0)
        pend_ref[0] = 0

    @pl.when(jnp.logical_and(s == 0, f == 0))
    def _():
        pend_ref[0] = 0

    def chunk_copy(src_chunk, c):
        return pltpu.make_async_copy(xs_ref.at[_packed_rows(src_chunk * ROW_CHUNK, ROW_CHUNK), :],
                                     xin_ref.at[_packed_rows(c * ROW_CHUNK, ROW_CHUNK), :], sem_ref.at[0])

    def fetch(first_chunk, count):
        def start(c, _):
            chunk_copy(first_chunk + c, c).start()
            return 0

        lax.fori_loop(0, count, start, 0)

    @pl.when(jnp.logical_and(s == 0, f == 0))
    def _():
        fetch(c0, n)

    @pl.when(jnp.logical_and(n > 0, f == 0))
    def _():
        def finish(c, _):
            chunk_copy(c0 + c, c).wait()
            return 0

        def unpack(c, _):
            for j in range(ROW_TILE):
                w = xin_ref[pl.ds(c * (ROW_CHUNK * ROW_TILE) + j, ROW_CHUNK, stride=ROW_TILE), :]
                low = lax.bitcast_convert_type(w << 16, f32)
                high = lax.bitcast_convert_type(w & jnp.uint32(0xFFFF0000), f32)
                xbf_ref[rows(c), j * LANES:(j + 1) * LANES] = low.astype(bf16)
                xbf_ref[rows(c), HALF_D + j * LANES:HALF_D + (j + 1) * LANES] = high.astype(bf16)
            return 0

        lax.fori_loop(0, n, finish, 0)
        lax.fori_loop(0, n, unpack, 0)

    @pl.when(jnp.logical_and(s + 1 < MAX_SB, f == 1))
    def _():
        nxt = jnp.minimum(s + 1, MAX_SB - 1)
        fetch(c0_ref[nxt], n_ref[nxt])

    @pl.when(n > 0)
    def _():
        bg = bg_ref[0]
        bu = bu_ref[0]

        def up_body(c, k):
            x = xbf_ref[rows(c, k), :]
            g = jnp.dot(x, wg_ref[0].astype(bf16), preferred_element_type=f32) + bg
            u = jnp.dot(x, wu_ref[0].astype(bf16), preferred_element_type=f32) + bu
            g = jnp.minimum(g, SWIGLU_LIMIT)
            u = jnp.clip(u, -SWIGLU_LIMIT, SWIGLU_LIMIT)
            a = (u + 1.0) * (g * (1.0 / (1.0 + jnp.exp(-SWIGLU_ALPHA * g))))
            act_ref[rows(c, k), :] = a.astype(bf16)

        cover(up_body)

        @pl.when(f == 0)
        def _():
            drain()
            bias = jnp.broadcast_to(bd_ref[0], (ROW_CHUNK, D_MODEL))

            def init(c, _):
                acc_ref[rows(c), :] = bias
                return 0

            lax.fori_loop(0, n, init, 0)

        def down_body(c, k):
            acc_ref[rows(c, k), :] += jnp.dot(act_ref[rows(c, k), :], wd_ref[0].astype(bf16),
                                              preferred_element_type=f32)

        cover(down_body)

        @pl.when(f == N_FF_TILES - 1)
        def _():
            def write(c, _):
                pltpu.make_async_copy(acc_ref.at[rows(c), :], ys_ref.at[rows(c0 + c), :], sem_ref.at[1]).start()
                return 0

            lax.fori_loop(0, n, write, 0)
            pend_ref[0] = n

    @pl.when(jnp.logical_and(s == pl.num_programs(0) - 1, f == N_FF_TILES - 1))
    def _():
        drain()
        acc_ref[0:ROW_CHUNK, :] = jnp.zeros((ROW_CHUNK, D_MODEL), f32)

        def tail_copy(c):
            return pltpu.make_async_copy(acc_ref.at[0:ROW_CHUNK, :], ys_ref.at[rows(c), :], sem_ref.at[0])

        def start(c, _):
            tail_copy(c).start()
            return 0

        def finish(c, _):
            tail_copy(c).wait()
            return 0

        lax.fori_loop(used_ref[1], MAX_CHUNKS, start, 0)
        lax.fori_loop(used_ref[1], MAX_CHUNKS, finish, 0)


def _experts(sb_e, sb_c0, sb_n, n_used, xs, w_gate, b_gate, w_up, b_up, w_down, b_down):
    def up_map(s, f, e, c0, n, used):
        return (e[s], 0, f)

    def down_map(s, f, e, c0, n, used):
        return (e[s], f, 0)

    def bias_map(s, f, e, c0, n, used):
        return (e[s], 0, 0)

    return pl.pallas_call(
        _experts_kernel,
        name="experts",
        grid_spec=pltpu.PrefetchScalarGridSpec(
            num_scalar_prefetch=4,
            grid=(n_used[0], N_FF_TILES),
            in_specs=[
                pl.BlockSpec(memory_space=pl.ANY),
                pl.BlockSpec((1, D_MODEL, FF_TILE), up_map),
                pl.BlockSpec((1, D_MODEL, FF_TILE), up_map),
                pl.BlockSpec((1, FF_TILE, D_MODEL), down_map),
                pl.BlockSpec((1, 1, FF_TILE), up_map),
                pl.BlockSpec((1, 1, FF_TILE), up_map),
                pl.BlockSpec((1, 1, D_MODEL), bias_map),
            ],
            out_specs=pl.BlockSpec(memory_space=pl.ANY),
            scratch_shapes=[
                pltpu.VMEM((SB_ROWS * ROW_TILE, LANES), jnp.uint32),
                pltpu.VMEM((SB_ROWS, D_MODEL), bf16),
                pltpu.VMEM((SB_ROWS, FF_TILE), bf16),
                pltpu.VMEM((SB_ROWS, D_MODEL), f32),
                pltpu.SMEM((1,), jnp.int32),
                pltpu.SemaphoreType.DMA((2,)),
            ],
        ),
        out_shape=jax.ShapeDtypeStruct((MAX_ROWS, D_MODEL), f32),
        compiler_params=_cparams(("arbitrary", "arbitrary"), EXPERTS_VMEM_LIMIT),
    )(sb_e, sb_c0, sb_n, n_used, xs, w_gate, w_up, w_down,
      b_gate.reshape(N_EXPERTS, 1, D_FF), b_up.reshape(N_EXPERTS, 1, D_FF),
      b_down.reshape(N_EXPERTS, 1, D_MODEL))


COMB_TM = 256


def _combine_kernel(dest_ref, next_ref, ys_ref, x1_ref, gate_ref, gf_ref, o_ref, buf_ref, sem_ref):
    i = pl.program_id(0)
    slot = i % 2

    def gather(rows_ref, into):
        def issue(g, _):
            for u in range(SUBLANES):
                for k in range(TOP_K):
                    row = rows_ref[(g * SUBLANES + u) * TOP_K + k]
                    pltpu.make_async_copy(ys_ref.at[pl.ds(row, 1), :],
                                          buf_ref.at[into, k, g, pl.ds(u, 1), :],
                                          sem_ref.at[into]).start(priority=k % 2)
            return 0

        lax.fori_loop(0, COMB_TM // SUBLANES, issue, 0)

    @pl.when(i == 0)
    def _():
        gather(dest_ref, 0)

    has_next = i + 1 < pl.num_programs(0)
    for into in range(2):
        @pl.when(jnp.logical_and(has_next, slot == 1 - into))
        def _():
            gather(next_ref, into)

    for k in range(TOP_K):
        pltpu.make_async_copy(buf_ref.at[slot, k], buf_ref.at[slot, k], sem_ref.at[slot]).wait()

    gate = gate_ref[...]
    acc = x1_ref[...]
    for k in range(TOP_K):
        acc = acc + gate[:, k:k + 1] * buf_ref[slot, k].reshape(COMB_TM, D_MODEL)
    o_ref[...] = _rms(acc, gf_ref[...])


def _combine(dest, ys, x1, gates, g_final):
    n_tiles = SEQ // COMB_TM
    return pl.pallas_call(
        _combine_kernel,
        name="combine",
        grid=(SEQ // COMB_TM,),
        in_specs=[
            pl.BlockSpec((COMB_TM * TOP_K,), lambda i: (i,), memory_space=pltpu.SMEM),
            pl.BlockSpec((COMB_TM * TOP_K,), lambda i: (jnp.minimum(i + 1, n_tiles - 1),),
                         memory_space=pltpu.SMEM),
            pl.BlockSpec(memory_space=pl.ANY),
            pl.BlockSpec((COMB_TM, D_MODEL), lambda i: (i, 0)),
            pl.BlockSpec((COMB_TM, LANES), lambda i: (i, 0)),
            pl.BlockSpec((1, D_MODEL), lambda i: (0, 0)),
        ],
        out_specs=pl.BlockSpec((COMB_TM, D_MODEL), lambda i: (i, 0)),
        out_shape=jax.ShapeDtypeStruct((SEQ, D_MODEL), f32),
        scratch_shapes=[
            pltpu.VMEM((2, TOP_K, COMB_TM // SUBLANES, SUBLANES, D_MODEL), f32),
            pltpu.SemaphoreType.DMA((2,)),
        ],
        compiler_params=_cparams(("arbitrary",)),
    )(dest, dest, ys, x1, gates, g_final.reshape(1, D_MODEL))


def kernel(x, g_mix, w_in, lam_re_fwd, lam_im_fwd, log_dt_fwd, b_re_fwd, b_im_fwd, c_re_fwd, c_im_fwd, lam_re_bwd, lam_im_bwd, log_dt_bwd, b_re_bwd, b_im_bwd, c_re_bwd, c_im_bwd, ssm_d, w_glu, b_glu, na_rpb, g_ssm_out, g_na_out, w_out, g_moe, w_router, b_router, w_gate, b_gate, w_up, b_up, w_down, b_down, g_final):
    x2 = x.reshape(SEQ, D_MODEL)

    u, qkv = _in_proj(x2, g_mix[0], w_in[0].astype(bf16))

    s5_w, s5_dec = _s5_weights(
        (lam_re_fwd[0], lam_im_fwd[0], log_dt_fwd[0], b_re_fwd[0], b_im_fwd[0], c_re_fwd[0], c_im_fwd[0]),
        (lam_re_bwd[0], lam_im_bwd[0], log_dt_bwd[0], b_re_bwd[0], b_im_bwd[0], c_re_bwd[0], c_im_bwd[0]))
    y = _s5(u, s5_w, s5_dec, ssm_d[0])
    ssm_n = _glu(y, w_glu[0].astype(bf16), b_glu[0], g_ssm_out[0])

    y_na = _natten(qkv, _na_bias(na_rpb[0]))

    w_router_pad = jnp.zeros((D_MODEL, LANES), bf16).at[:, :N_EXPERTS].set(w_router[0].astype(bf16))
    b_router_pad = jnp.full((1, LANES), NEG_BIG, f32).at[0, :N_EXPERTS].set(b_router[0].astype(f32))
    x1, hn, logits = _out_proj(ssm_n, y_na, x2, g_na_out[0], w_out[0].astype(bf16), g_moe[0],
                               w_router_pad, b_router_pad)

    dest_lanes, gates, counts = _route(logits)
    dest = dest_lanes[:, :TOP_K].reshape(-1)
    last_chunk, sb_e, sb_c0, sb_n, n_used = _routing_tables(counts)

    xs = _dispatch(last_chunk, dest, hn)
    ys = _experts(sb_e, sb_c0, sb_n, n_used, xs, w_gate[0], b_gate[0], w_up[0], b_up[0],
                  w_down[0], b_down[0])
    out = _combine(dest, ys, x1, gates, g_final)
    return out.reshape(x.shape)
```

```python
import math

import jax
import jax.numpy as jnp
from jax import lax
from jax.experimental import pallas as pl
from jax.experimental.pallas import tpu as pltpu

f32 = jnp.float32
bf16 = jnp.bfloat16

D_MODEL = 2048
SEQ = 8192
SSM_WIDTH = 1024
NA_WIDTH = 1024
SSM_GROUP = 16
SSM_STATE = 64
NA_HEAD_DIM = 64
NA_HEADS = 16
GRID_W = 64
GRID_ROWS = SEQ // GRID_W
NA_KH = 8
NA_KW = 16
N_EXPERTS = 32
TOP_K = 4
D_FF = 2048
SWIGLU_LIMIT = 7.0
SWIGLU_ALPHA = 1.702
RMS_EPS = 1e-5

LANES = 128
HALF_D = D_MODEL // 2
SUBLANES = 8
ROW_TILE = HALF_D // LANES
assert ROW_TILE == SUBLANES
NEG_BIG = -1e30

CHUNK_T = 16
N_CHUNKS = SEQ // CHUNK_T
GROUPS_PER_BLOCK = LANES // SSM_GROUP
N_LANE_BLOCKS = SSM_WIDTH // LANES
CAT_W = CHUNK_T * LANES
STATE_W = GROUPS_PER_BLOCK * SSM_STATE

ROW_CHUNK = 128
CHUNKS_PER_SB = 12
SB_ROWS = ROW_CHUNK * CHUNKS_PER_SB
MAX_CHUNKS = SEQ * TOP_K // ROW_CHUNK + N_EXPERTS
MAX_ROWS = MAX_CHUNKS * ROW_CHUNK
MAX_SB = MAX_CHUNKS // CHUNKS_PER_SB + N_EXPERTS
FF_TILE = 512
N_FF_TILES = D_FF // FF_TILE

VMEM_LIMIT = 56 * 1024 * 1024
EXPERTS_VMEM_LIMIT = 60 * 1024 * 1024


def _cparams(semantics, vmem=VMEM_LIMIT):
    return pltpu.CompilerParams(dimension_semantics=semantics, vmem_limit_bytes=vmem)


def _rms(x, g):
    return x * lax.rsqrt(jnp.mean(x * x, axis=-1, keepdims=True) + RMS_EPS) * g


IN_TM = 1024
IN_TN = 1024


def _in_proj_kernel(x_ref, g_ref, w_ref, u_ref, qkv_ref, h_ref):
    j = pl.program_id(1)

    @pl.when(j == 0)
    def _():
        h_ref[...] = _rms(x_ref[...], g_ref[...]).astype(bf16)

    acc = jnp.dot(h_ref[...], w_ref[...], preferred_element_type=f32)

    @pl.when(j == 0)
    def _():
        u_ref[...] = acc

    @pl.when(j > 0)
    def _():
        qkv_ref[...] = acc.astype(bf16)


def _in_proj(x, g_mix, w_in_bf):
    n_out = w_in_bf.shape[1]
    return pl.pallas_call(
        _in_proj_kernel,
        name="in_proj",
        grid=(SEQ // IN_TM, n_out // IN_TN),
        in_specs=[
            pl.BlockSpec((IN_TM, D_MODEL), lambda i, j: (i, 0)),
            pl.BlockSpec((1, D_MODEL), lambda i, j: (0, 0)),
            pl.BlockSpec((D_MODEL, IN_TN), lambda i, j: (0, j)),
        ],
        out_specs=[
            pl.BlockSpec((IN_TM, IN_TN), lambda i, j: (i, 0)),
            pl.BlockSpec((IN_TM, IN_TN), lambda i, j: (i, jnp.maximum(j - 1, 0))),
        ],
        out_shape=[
            jax.ShapeDtypeStruct((SEQ, SSM_WIDTH), f32),
            jax.ShapeDtypeStruct((SEQ, 3 * NA_WIDTH), bf16),
        ],
        scratch_shapes=[pltpu.VMEM((IN_TM, D_MODEL), bf16)],
        compiler_params=_cparams(("arbitrary", "arbitrary")),
    )(x, g_mix.reshape(1, D_MODEL), w_in_bf)


def _cmul(ar, ai, br, bi):
    return ar * br - ai * bi, ar * bi + ai * br


def _s5_discretise(lam_re, lam_im, log_dt, b_re, b_im, c_re, c_im):
    a = jnp.minimum(lam_re.astype(f32), -1e-4)
    w = lam_im.astype(f32)
    dt = jnp.exp(log_dt.astype(f32))[:, None]
    steps = jnp.arange(CHUNK_T + 1, dtype=f32)[:, None, None]
    mag = jnp.exp((a * dt)[None] * steps)
    ang = (w * dt)[None] * steps
    pw = (mag * jnp.cos(ang), mag * jnp.sin(ang))
    xr, xi = pw[0][1] - 1.0, pw[1][1]
    den = a * a + w * w
    qr, qi = (xr * a + xi * w) / den, (xi * a - xr * w) / den
    bb = _cmul(qr[..., None], qi[..., None], b_re.astype(f32), b_im.astype(f32))
    return pw, bb, (c_re.astype(f32), c_im.astype(f32))


def _pair_blockdiag(m):
    z = jnp.zeros_like(m[..., 0, :, :])
    top = jnp.concatenate([m[..., 0, :, :], z], axis=-1)
    bot = jnp.concatenate([z, m[..., 1, :, :]], axis=-1)
    return jnp.concatenate([top, bot], axis=-2)


def _s5_weights(fwd, bwd):
    hi = lax.Precision.HIGH
    exact = lax.Precision.HIGHEST
    t = CHUNK_T
    pw_f, bb_f, c_f = _s5_discretise(*fwd)
    pw_b, bb_b, c_b = _s5_discretise(*bwd)
    pairs = (N_LANE_BLOCKS, GROUPS_PER_BLOCK // 2, 2)

    x = jnp.arange(t * SSM_GROUP)
    t_of_x = x // SSM_GROUP
    tile_c = (jnp.arange(SSM_GROUP)[:, None] == x[None, :] % SSM_GROUP).astype(f32)

    def c_over_x(c):
        return [jnp.einsum('gcp,cx->gpx', part, tile_c, precision=exact) for part in c]

    def c_times_power(c_x, pw, power_of_x):
        rep = (jnp.arange(t + 1)[:, None] == power_of_x[None, :]).astype(f32)
        pw_x = [jnp.einsum('jgp,jx->gpx', part, rep, precision=exact) for part in pw]
        return _cmul(c_x[0], c_x[1], pw_x[0], pw_x[1])

    cx_f, cx_b = c_over_x(c_f), c_over_x(c_b)

    def lag_kernel(c_x, pw, bb, power_of_x):
        m_re, m_im = c_times_power(c_x, pw, power_of_x)
        m = jnp.concatenate([m_re, m_im], axis=1)
        b = jnp.concatenate([jnp.swapaxes(bb[0], 1, 2), -jnp.swapaxes(bb[1], 1, 2)], axis=-1)
        return jnp.einsum('gdp,gpx->gdx', b, m, precision=hi)

    kf = lag_kernel(cx_f, pw_f, bb_f, t_of_x)
    kb = lag_kernel(cx_b, pw_b, bb_b, t - 1 - t_of_x)
    keep = (t - 1) * SSM_GROUP
    k_lag = jnp.concatenate([kb[..., :keep], kb[..., keep:] + kf[..., :SSM_GROUP], kf[..., SSM_GROUP:]],
                            axis=-1)
    w_intra = jnp.stack([k_lag[..., (t - 1 - ti) * SSM_GROUP:(2 * t - 1 - ti) * SSM_GROUP]
                         for ti in range(t)], axis=1)
    w_intra = _pair_blockdiag(w_intra.astype(bf16).reshape(*pairs, t * SSM_GROUP, t * SSM_GROUP))

    def state_in(pw_t, bb):
        pw_g = [jnp.transpose(part, (1, 0, 2))[:, :, None, :] for part in pw_t]
        bb_g = [jnp.swapaxes(part, 1, 2)[:, None] for part in bb]
        return [_pair_blockdiag(part.astype(bf16).reshape(*pairs, t * SSM_GROUP, SSM_STATE))
                for part in _cmul(pw_g[0], pw_g[1], bb_g[0], bb_g[1])]

    w_in = jnp.concatenate(state_in([part[:t][::-1] for part in pw_f], bb_f)
                           + state_in([part[:t] for part in pw_b], bb_b), axis=-1)

    def state_out(c_x, pw, power_of_x):
        m_re, m_im = c_times_power(c_x, pw, power_of_x)
        return [_pair_blockdiag(part.astype(bf16).reshape(*pairs, SSM_STATE, t * SSM_GROUP))
                for part in (m_re, -m_im)]

    w_out = jnp.concatenate(state_out(cx_f, pw_f, t_of_x + 1) + state_out(cx_b, pw_b, t - t_of_x), axis=-2)

    def decay(a):
        return a.reshape(N_LANE_BLOCKS, 1, STATE_W)

    dec = jnp.concatenate([decay(pw_f[0][t]), decay(pw_f[1][t]), decay(pw_b[0][t]), decay(pw_b[1][t])],
                          axis=-1)
    return (w_in, w_intra, w_out), dec.astype(f32)


def _gelu_tanh(x):
    return 0.5 * x * (1.0 + jnp.tanh(math.sqrt(2.0 / math.pi) * (x + 0.044715 * (x * x * x))))


PAIRS_PER_BLOCK = GROUPS_PER_BLOCK // 2
PAIR_W = 2 * CHUNK_T * SSM_GROUP


def _s5_kernel(u_ref, win_ref, wintra_ref, wout_ref, dec_ref, d_ref, y_ref,
               cat_ref, catp_ref, st_ref, perm_ref):
    sw = STATE_W

    @pl.when(pl.program_id(0) == 0)
    def _():
        def strip(t, _):
            row = lax.broadcasted_iota(jnp.int32, (LANES, CAT_W), 0)
            col = lax.broadcasted_iota(jnp.int32, (LANES, CAT_W), 1)
            target = (row // SSM_GROUP) * (CHUNK_T * SSM_GROUP) + t * SSM_GROUP + row % SSM_GROUP
            perm_ref[pl.ds(pl.multiple_of(t * LANES, LANES), LANES), :] = (
                jnp.where(col == target, 1.0, 0.0).astype(bf16))
            return 0

        lax.fori_loop(0, CHUNK_T, strip, 0)

    for t in range(CHUNK_T):
        cat_ref[:, t * LANES:(t + 1) * LANES] = u_ref[pl.ds(t, N_CHUNKS, stride=CHUNK_T), :].astype(bf16)
    catp_ref[...] = jnp.dot(cat_ref[...], perm_ref[...], preferred_element_type=f32).astype(bf16)

    for p in range(PAIRS_PER_BLOCK):
        z = jnp.dot(catp_ref[:, p * PAIR_W:(p + 1) * PAIR_W], win_ref[0, p], preferred_element_type=f32)
        for part in range(4):
            st_ref[:, part * sw + p * LANES:part * sw + (p + 1) * LANES] = z[:, part * LANES:(part + 1) * LANES]

    afr = dec_ref[0, :, 0 * sw:1 * sw]
    afi = dec_ref[0, :, 1 * sw:2 * sw]
    abr = dec_ref[0, :, 2 * sw:3 * sw]
    abi = dec_ref[0, :, 3 * sw:4 * sw]

    def step(i, carry):
        sfr, sfi, sbr, sbi = carry
        r = N_CHUNKS - 1 - i
        zf = st_ref[pl.ds(i, 1), 0:2 * sw]
        zb = st_ref[pl.ds(r, 1), 2 * sw:4 * sw]
        st_ref[pl.ds(i, 1), 0:2 * sw] = jnp.concatenate([sfr, sfi], axis=-1)
        st_ref[pl.ds(r, 1), 2 * sw:4 * sw] = jnp.concatenate([sbr, sbi], axis=-1)
        nfr = afr * sfr - afi * sfi + zf[:, :sw]
        nfi = afi * sfr + afr * sfi + zf[:, sw:]
        nbr = abr * sbr - abi * sbi + zb[:, :sw]
        nbi = abi * sbr + abr * sbi + zb[:, sw:]
        return nfr, nfi, nbr, nbi

    zero = jnp.zeros((1, sw), f32)
    lax.fori_loop(0, N_CHUNKS, step, (zero, zero, zero, zero))

    for p in range(PAIRS_PER_BLOCK):
        s_in = jnp.concatenate(
            [st_ref[:, part * sw + p * LANES:part * sw + (p + 1) * LANES] for part in range(4)], axis=-1)
        yp = jnp.dot(catp_ref[:, p * PAIR_W:(p + 1) * PAIR_W], wintra_ref[0, p], preferred_element_type=f32)
        yp = yp + jnp.dot(s_in.astype(bf16), wout_ref[0, p], preferred_element_type=f32)
        cat_ref[:, p * PAIR_W:(p + 1) * PAIR_W] = yp.astype(bf16)

    st_ref[...] = lax.dot_general(cat_ref[...], perm_ref[...], (((1,), (1,)), ((), ())),
                                  preferred_element_type=f32)
    d = d_ref[...]
    for t in range(CHUNK_T):
        rows = pl.ds(t, N_CHUNKS, stride=CHUNK_T)
        v = st_ref[:, t * LANES:(t + 1) * LANES] + d * u_ref[rows, :]
        y_ref[rows, :] = _gelu_tanh(v)


def _s5(u, weights, dec, ssm_d):
    w_spec = pl.BlockSpec((1, PAIRS_PER_BLOCK, PAIR_W, PAIR_W), lambda j: (j, 0, 0, 0))
    return pl.pallas_call(
        _s5_kernel,
        name="s5_scan",
        grid=(N_LANE_BLOCKS,),
        in_specs=[
            pl.BlockSpec((SEQ, LANES), lambda j: (0, j)),
            w_spec, w_spec, w_spec,
            pl.BlockSpec((1, 1, 4 * STATE_W), lambda j: (j, 0, 0)),
            pl.BlockSpec((1, LANES), lambda j: (0, j)),
        ],
        out_specs=pl.BlockSpec((SEQ, LANES), lambda j: (0, j)),
        out_shape=jax.ShapeDtypeStruct((SEQ, SSM_WIDTH), f32),
        scratch_shapes=[
            pltpu.VMEM((N_CHUNKS, CAT_W), bf16),
            pltpu.VMEM((N_CHUNKS, CAT_W), bf16),
            pltpu.VMEM((N_CHUNKS, 4 * STATE_W), f32),
            pltpu.VMEM((CAT_W, CAT_W), bf16),
        ],
        compiler_params=_cparams(("arbitrary",)),
    )(u, *weights, dec, ssm_d.reshape(1, SSM_WIDTH))


GLU_TM = 1024


def _glu_kernel(y_ref, w_ref, b_ref, g_ref, o_ref):
    y = y_ref[...]
    z = jnp.dot(y.astype(bf16), w_ref[...], preferred_element_type=f32) + b_ref[...]
    o = y * (1.0 / (1.0 + jnp.exp(-z)))
    o_ref[...] = _rms(o, g_ref[...]).astype(bf16)


def _glu(y, w_glu_bf, b_glu, g_ssm_out):
    return pl.pallas_call(
        _glu_kernel,
        name="glu_norm",
        grid=(SEQ // GLU_TM,),
        in_specs=[
            pl.BlockSpec((GLU_TM, SSM_WIDTH), lambda i: (i, 0)),
            pl.BlockSpec((SSM_WIDTH, SSM_WIDTH), lambda i: (0, 0)),
            pl.BlockSpec((1, SSM_WIDTH), lambda i: (0, 0)),
            pl.BlockSpec((1, SSM_WIDTH), lambda i: (0, 0)),
        ],
        out_specs=pl.BlockSpec((GLU_TM, SSM_WIDTH), lambda i: (i, 0)),
        out_shape=jax.ShapeDtypeStruct((SEQ, SSM_WIDTH), bf16),
        compiler_params=_cparams(("arbitrary",)),
    )(y, w_glu_bf, b_glu.reshape(1, SSM_WIDTH), g_ssm_out.reshape(1, SSM_WIDTH))


NA_ROWS_PER_STEP = 16
NA_WIN = NA_KH * GRID_W
HEADS_PER_BLOCK = LANES // NA_HEAD_DIM


def _na_bias(rpb):
    c = jnp.arange(GRID_W)
    col_start = jnp.clip(c - NA_KW // 2, 0, GRID_W - NA_KW)
    valid = (c[None, :] >= col_start[:, None]) & (c[None, :] < col_start[:, None] + NA_KW)
    dc = jnp.clip(c[None, :] - c[:, None], -(NA_KW - 1), NA_KW - 1) + (NA_KW - 1)
    sel = (dc[None] == jnp.arange(2 * NA_KW - 1)[:, None, None]).astype(f32)
    tab = jnp.einsum('hrc,cqk->hqrk', rpb.astype(f32), sel, precision=lax.Precision.HIGHEST)
    tab = jnp.where(valid[None, :, None, :], tab, NEG_BIG)
    tab = tab.reshape(NA_HEADS, GRID_W, (2 * NA_KH - 1) * GRID_W)
    return jnp.stack([tab[..., (NA_KH - 1 - v) * GRID_W:(2 * NA_KH - 1 - v) * GRID_W]
                      for v in range(NA_KH)], axis=1)


def _natten_kernel(q_ref, k_ref, v_ref, b_ref, o_ref, s_ref, p_ref):
    rb = pl.program_id(1)
    lane = lax.broadcasted_iota(jnp.int32, (GRID_W, LANES), 1)
    head0 = lane < NA_HEAD_DIM
    scale = NA_HEAD_DIM ** -0.5

    starts, variants = [], []
    for i in range(NA_ROWS_PER_STEP):
        r = rb * NA_ROWS_PER_STEP + i
        rs = jnp.clip(r - NA_KH // 2, 0, GRID_ROWS - NA_KH)
        starts.append(pl.multiple_of(rs * GRID_W, GRID_W))
        variants.append(r - rs)

    for i in range(NA_ROWS_PER_STEP):
        q = q_ref[i * GRID_W:(i + 1) * GRID_W, :] * scale
        kw = k_ref[pl.ds(starts[i], NA_WIN), :]
        for h in range(HEADS_PER_BLOCK):
            qh = jnp.where(head0 if h == 0 else ~head0, q, jnp.zeros_like(q))
            s = lax.dot_general(qh, kw, (((1,), (1,)), ((), ())), preferred_element_type=f32)
            s_ref[i * HEADS_PER_BLOCK + h] = s + b_ref[h, variants[i]]

    n_tiles = NA_ROWS_PER_STEP * HEADS_PER_BLOCK
    maxes = [jnp.max(s_ref[t], axis=-1, keepdims=True) for t in range(n_tiles)]
    inv_sums = []
    for t in range(n_tiles):
        p = jnp.exp(s_ref[t] - maxes[t])
        inv_sums.append(1.0 / jnp.sum(p, axis=-1, keepdims=True))
        p_ref[t] = p.astype(bf16)

    for i in range(NA_ROWS_PER_STEP):
        vw = v_ref[pl.ds(starts[i], NA_WIN), :]
        outs = []
        for h in range(HEADS_PER_BLOCK):
            t = i * HEADS_PER_BLOCK + h
            outs.append(jnp.dot(p_ref[t], vw, preferred_element_type=f32) * inv_sums[t])
        o_ref[i * GRID_W:(i + 1) * GRID_W, :] = jnp.where(head0, outs[0], outs[1]).astype(bf16)


def _natten(qkv, bias):
    tm = NA_ROWS_PER_STEP * GRID_W
    n_hb = NA_WIDTH // LANES
    return pl.pallas_call(
        _natten_kernel,
        name="natten",
        grid=(n_hb, GRID_ROWS // NA_ROWS_PER_STEP),
        in_specs=[
            pl.BlockSpec((tm, LANES), lambda h, r: (r, h)),
            pl.BlockSpec((SEQ, LANES), lambda h, r: (0, n_hb + h)),
            pl.BlockSpec((SEQ, LANES), lambda h, r: (0, 2 * n_hb + h)),
            pl.BlockSpec((HEADS_PER_BLOCK, NA_KH, GRID_W, NA_WIN), lambda h, r: (h, 0, 0, 0)),
        ],
        out_specs=pl.BlockSpec((tm, LANES), lambda h, r: (r, h)),
        out_shape=jax.ShapeDtypeStruct((SEQ, NA_WIDTH), bf16),
        scratch_shapes=[
            pltpu.VMEM((NA_ROWS_PER_STEP * HEADS_PER_BLOCK, GRID_W, NA_WIN), f32),
            pltpu.VMEM((NA_ROWS_PER_STEP * HEADS_PER_BLOCK, GRID_W, NA_WIN), bf16),
        ],
        compiler_params=_cparams(("arbitrary", "arbitrary")),
    )(qkv, qkv, qkv, bias)


OUT_TM = 512


def _out_proj_kernel(ssm_ref, na_ref, x_ref, gna_ref, w_ref, gmoe_ref, wr_ref, br_ref,
                     x1_ref, hn_ref, lg_ref):
    na = _rms(na_ref[...].astype(f32), gna_ref[...]).astype(bf16)
    y = jnp.dot(ssm_ref[...], w_ref[0:SSM_WIDTH, :], preferred_element_type=f32)
    y = y + jnp.dot(na, w_ref[SSM_WIDTH:, :], preferred_element_type=f32)
    x1 = x_ref[...] + y
    x1_ref[...] = x1
    hn = _rms(x1, gmoe_ref[...]).astype(bf16)
    lg_ref[...] = jnp.dot(hn, wr_ref[...], preferred_element_type=f32) + br_ref[...]
    bits = lax.bitcast_convert_type(hn.astype(f32), jnp.uint32)
    packed = (bits[:, HALF_D:] & jnp.uint32(0xFFFF0000)) | (bits[:, :HALF_D] >> 16)
    for j in range(ROW_TILE):
        hn_ref[pl.ds(j, OUT_TM, stride=ROW_TILE), :] = packed[:, j * LANES:(j + 1) * LANES]


def _out_proj(ssm_n, y_na, x, g_na_out, w_out_bf, g_moe, w_router_pad, b_router_pad):
    row = lambda i: (i, 0)
    fixed = lambda i: (0, 0)
    return pl.pallas_call(
        _out_proj_kernel,
        name="out_proj",
        grid=(SEQ // OUT_TM,),
        in_specs=[
            pl.BlockSpec((OUT_TM, SSM_WIDTH), row),
            pl.BlockSpec((OUT_TM, NA_WIDTH), row),
            pl.BlockSpec((OUT_TM, D_MODEL), row),
            pl.BlockSpec((1, NA_WIDTH), fixed),
            pl.BlockSpec((D_MODEL, D_MODEL), fixed),
            pl.BlockSpec((1, D_MODEL), fixed),
            pl.BlockSpec((D_MODEL, LANES), fixed),
            pl.BlockSpec((1, LANES), fixed),
        ],
        out_specs=[
            pl.BlockSpec((OUT_TM, D_MODEL), row),
            pl.BlockSpec((OUT_TM * ROW_TILE, LANES), row),
            pl.BlockSpec((OUT_TM, LANES), row),
        ],
        out_shape=[
            jax.ShapeDtypeStruct((SEQ, D_MODEL), f32),
            jax.ShapeDtypeStruct((SEQ * ROW_TILE, LANES), jnp.uint32),
            jax.ShapeDtypeStruct((SEQ, LANES), f32),
        ],
        compiler_params=_cparams(("arbitrary",)),
    )(ssm_n, y_na, x, g_na_out.reshape(1, NA_WIDTH), w_out_bf, g_moe.reshape(1, D_MODEL),
      w_router_pad, b_router_pad)


ROUTE_TM = 1024


def _route_kernel(lg_ref, tri_ref, dest_ref, gate_ref, cnt_ref, carry_ref, meta_ref):
    phase = pl.program_id(0)
    i = pl.program_id(1)
    rows = pl.ds(pl.multiple_of(i * ROUTE_TM, ROUTE_TM), ROUTE_TM)
    lane = lax.broadcasted_iota(jnp.int32, (ROUTE_TM, LANES), 1)

    @pl.when(phase == 0)
    def _():
        @pl.when(i == 0)
        def _():
            carry_ref[...] = jnp.zeros_like(carry_ref)

        lane_f = lane.astype(f32)
        work = lg_ref[...]
        vals, hits = [], []
        for _ in range(TOP_K):
            m = jnp.max(work, axis=-1, keepdims=True)
            idx = jnp.min(jnp.where(work == m, lane_f, float(LANES)), axis=-1, keepdims=True)
            hit = lane_f == idx
            vals.append(m)
            hits.append((idx, hit))
            work = jnp.where(hit, -jnp.inf, work)

        exps = [jnp.exp(v - vals[0]) for v in vals]
        denom = exps[0] + exps[1] + exps[2] + exps[3]

        onehot = jnp.zeros((ROUTE_TM, LANES), f32)
        for _, hit in hits:
            onehot = onehot + hit.astype(f32)
        before = jnp.dot(tri_ref[...], onehot.astype(bf16), preferred_element_type=f32) + carry_ref[...]

        meta = jnp.zeros((ROUTE_TM, LANES), jnp.int32)
        gate = jnp.zeros((ROUTE_TM, LANES), f32)
        for k, (idx, hit) in enumerate(hits):
            rank = jnp.sum(jnp.where(hit, before, 0.0), axis=-1, keepdims=True).astype(jnp.int32)
            meta = jnp.where(lane == k, idx.astype(jnp.int32), meta)
            meta = jnp.where(lane == TOP_K + k, rank, meta)
            gate = jnp.where(lane == k, exps[k] / denom, gate)
        meta_ref[rows, :] = meta
        gate_ref[...] = gate
        carry_ref[...] += jnp.sum(onehot, axis=0, keepdims=True)
        cnt_ref[...] = carry_ref[...]

    @pl.when(phase == 1)
    def _():
        n_chunks = jnp.floor((carry_ref[...] + float(ROW_CHUNK - 1)) * (1.0 / ROW_CHUNK))
        src = lax.broadcasted_iota(jnp.int32, (LANES, LANES), 0)
        dst = lax.broadcasted_iota(jnp.int32, (LANES, LANES), 1)
        earlier = jnp.where(src < dst, 1.0, 0.0).astype(bf16)
        first_chunk = jnp.dot(jnp.broadcast_to(n_chunks, (SUBLANES, LANES)).astype(bf16), earlier,
                              preferred_element_type=f32)[0:1]
        first_row = first_chunk * float(ROW_CHUNK)
        meta = meta_ref[rows, :]
        dest = jnp.zeros((ROUTE_TM, LANES), jnp.int32)
        for k in range(TOP_K):
            hit = lane == meta[:, k:k + 1]
            base = jnp.sum(jnp.where(hit, first_row, 0.0), axis=-1, keepdims=True).astype(jnp.int32)
            dest = jnp.where(lane == k, base + meta[:, TOP_K + k:TOP_K + k + 1], dest)
        dest_ref[...] = dest


def _route(logits):
    tri = (jnp.arange(ROUTE_TM)[:, None] > jnp.arange(ROUTE_TM)[None, :]).astype(bf16)
    n_tiles = SEQ // ROUTE_TM
    phase0 = lambda p, i: (jnp.where(p == 0, i, n_tiles - 1), 0)
    phase1 = lambda p, i: (i * p, 0)
    return pl.pallas_call(
        _route_kernel,
        name="route",
        grid=(2, n_tiles),
        in_specs=[
            pl.BlockSpec((ROUTE_TM, LANES), phase0),
            pl.BlockSpec((ROUTE_TM, ROUTE_TM), lambda p, i: (0, 0)),
        ],
        out_specs=[
            pl.BlockSpec((ROUTE_TM, LANES), phase1),
            pl.BlockSpec((ROUTE_TM, LANES), phase0),
            pl.BlockSpec((1, LANES), lambda p, i: (0, 0)),
        ],
        out_shape=[
            jax.ShapeDtypeStruct((SEQ, LANES), jnp.int32),
            jax.ShapeDtypeStruct((SEQ, LANES), f32),
            jax.ShapeDtypeStruct((1, LANES), f32),
        ],
        scratch_shapes=[pltpu.VMEM((1, LANES), f32), pltpu.VMEM((SEQ, LANES), jnp.int32)],
        compiler_params=_cparams(("arbitrary", "arbitrary")),
    )(logits, tri)


def _routing_tables(counts):
    cnt = counts[0, :N_EXPERTS].astype(jnp.int32)
    n_chunks = (cnt + ROW_CHUNK - 1) // ROW_CHUNK
    chunk_base = jnp.cumsum(n_chunks) - n_chunks
    total_chunks = jnp.sum(n_chunks)
    last_chunk = jnp.concatenate([jnp.where(cnt > 0, chunk_base + n_chunks - 1, -1),
                                  total_chunks[None]]).astype(jnp.int32)

    n_sb = (n_chunks + CHUNKS_PER_SB - 1) // CHUNKS_PER_SB
    sb_end = jnp.cumsum(n_sb)
    sb_start = sb_end - n_sb
    n_used = sb_end[-1]
    s = jnp.arange(MAX_SB)
    s_eff = jnp.minimum(s, n_used - 1)
    e = jnp.minimum(jnp.searchsorted(sb_end, s_eff, side='right'), N_EXPERTS - 1)
    kk = s_eff - sb_start[e]
    per_sb = n_chunks[e] // jnp.maximum(n_sb[e], 1)
    extra = n_chunks[e] - per_sb * n_sb[e]
    sb_chunk0 = chunk_base[e] + kk * per_sb + jnp.minimum(kk, extra)
    sb_n = jnp.where(s < n_used, per_sb + (kk < extra), 0)
    used = jnp.stack([n_used, total_chunks]).astype(jnp.int32)
    return (last_chunk, e.astype(jnp.int32), sb_chunk0.astype(jnp.int32), sb_n.astype(jnp.int32), used)


DISP_TM = 512


def _packed_rows(first_row, n_rows=1):
    return pl.ds(pl.multiple_of(first_row * ROW_TILE, ROW_TILE), n_rows * ROW_TILE)


def _row_copy(src, src_row, dst, dst_row, sem):
    return pltpu.make_async_copy(src.at[_packed_rows(src_row), :], dst.at[_packed_rows(dst_row), :], sem)


def _dispatch_kernel(last_ref, dest_ref, hn_ref, xs_ref, zero_ref, sem_ref):
    i = pl.program_id(0)

    @pl.when(i == 0)
    def _():
        zero_ref[...] = jnp.zeros_like(zero_ref)

        def chunk_copy(c):
            return pltpu.make_async_copy(zero_ref, xs_ref.at[_packed_rows(c * ROW_CHUNK, ROW_CHUNK), :],
                                         sem_ref.at[1])

        def start(e, _):
            @pl.when(last_ref[e] >= 0)
            def _():
                chunk_copy(last_ref[e]).start()
            return 0

        def wait(e, _):
            @pl.when(last_ref[e] >= 0)
            def _():
                chunk_copy(last_ref[e]).wait()
            return 0

        def start_tail(c, _):
            chunk_copy(c).start()
            return 0

        def wait_tail(c, _):
            chunk_copy(c).wait()
            return 0

        lax.fori_loop(0, N_EXPERTS, start, 0)
        lax.fori_loop(last_ref[N_EXPERTS], MAX_CHUNKS, start_tail, 0)
        lax.fori_loop(0, N_EXPERTS, wait, 0)
        lax.fori_loop(last_ref[N_EXPERTS], MAX_CHUNKS, wait_tail, 0)

    def issue(t, _):
        for k in range(TOP_K):
            _row_copy(hn_ref, t, xs_ref, dest_ref[t * TOP_K + k], sem_ref.at[0]).start(priority=k % 2)
        return 0

    lax.fori_loop(0, DISP_TM, issue, 0, unroll=8)
    for k in range(TOP_K):
        pltpu.make_async_copy(hn_ref, xs_ref.at[_packed_rows(0, DISP_TM), :], sem_ref.at[0]).wait()


def _dispatch(last_chunk, dest, hn):
    return pl.pallas_call(
        _dispatch_kernel,
        name="dispatch",
        grid_spec=pltpu.PrefetchScalarGridSpec(
            num_scalar_prefetch=1,
            grid=(SEQ // DISP_TM,),
            in_specs=[
                pl.BlockSpec((DISP_TM * TOP_K,), lambda i, last: (i,), memory_space=pltpu.SMEM),
                pl.BlockSpec((DISP_TM * ROW_TILE, LANES), lambda i, last: (i, 0)),
            ],
            out_specs=pl.BlockSpec(memory_space=pl.ANY),
            scratch_shapes=[
                pltpu.VMEM((ROW_CHUNK * ROW_TILE, LANES), jnp.uint32),
                pltpu.SemaphoreType.DMA((2,)),
            ],
        ),
        out_shape=jax.ShapeDtypeStruct((MAX_ROWS * ROW_TILE, LANES), jnp.uint32),
        compiler_params=_cparams(("arbitrary",)),
    )(last_chunk, dest, hn)


def _experts_kernel(e_ref, c0_ref, n_ref, used_ref,
                    xs_ref, wg_ref, wu_ref, wd_ref, bg_ref, bu_ref, bd_ref, ys_ref,
                    xin_ref, xbf_ref, act_ref, acc_ref, pend_ref, sem_ref):
    s = pl.program_id(0)
    f = pl.program_id(1)
    n = n_ref[s]
    c0 = c0_ref[s]

    def rows(c, k=1):
        return pl.ds(pl.multiple_of(c * ROW_CHUNK, ROW_CHUNK), k * ROW_CHUNK)

    def cover(body):
        n4 = n // 4

        def quad(i, _):
            body(i * 4, 4)
            return 0

        lax.fori_loop(0, n4, quad, 0)

        @pl.when((n & 2) != 0)
        def _():
            body(n4 * 4, 2)

        @pl.when((n & 1) != 0)
        def _():
            body(n4 * 4 + (n & 2), 1)

    def drain():
        def wait_one(i, _):
            pltpu.make_async_copy(acc_ref.at[0:ROW_CHUNK, :], ys_ref.at[0:ROW_CHUNK, :], sem_ref.at[1]).wait()
            return 0

        lax.fori_loop(0, pend_ref[0], wait_one, 0)
        pend_ref[0] = 0

    @pl.when(jnp.logical_and(s == 0, f == 0))
    def _():
        pend_ref[0] = 0

    def chunk_copy(src_chunk, c):
        return pltpu.make_async_copy(xs_ref.at[_packed_rows(src_chunk * ROW_CHUNK, ROW_CHUNK), :],
                                     xin_ref.at[_packed_rows(c * ROW_CHUNK, ROW_CHUNK), :], sem_ref.at[0])

    def fetch(first_chunk, count):
        def start(c, _):
            chunk_copy(first_chunk + c, c).start()
            return 0

        lax.fori_loop(0, count, start, 0)

    @pl.when(jnp.logical_and(s == 0, f == 0))
    def _():
        fetch(c0, n)

    @pl.when(jnp.logical_and(n > 0, f == 0))
    def _():
        def finish(c, _):
            chunk_copy(c0 + c, c).wait()
            return 0

        def unpack(c, _):
            for j in range(ROW_TILE):
                w = xin_ref[pl.ds(c * (ROW_CHUNK * ROW_TILE) + j, ROW_CHUNK, stride=ROW_TILE), :]
                low = lax.bitcast_convert_type(w << 16, f32)
                high = lax.bitcast_convert_type(w & jnp.uint32(0xFFFF0000), f32)
                xbf_ref[rows(c), j * LANES:(j + 1) * LANES] = low.astype(bf16)
                xbf_ref[rows(c), HALF_D + j * LANES:HALF_D + (j + 1) * LANES] = high.astype(bf16)
            return 0

        lax.fori_loop(0, n, finish, 0)
        lax.fori_loop(0, n, unpack, 0)

    @pl.when(jnp.logical_and(s + 1 < MAX_SB, f == 1))
    def _():
        nxt = jnp.minimum(s + 1, MAX_SB - 1)
        fetch(c0_ref[nxt], n_ref[nxt])

    @pl.when(n > 0)
    def _():
        bg = bg_ref[0]
        bu = bu_ref[0]

        def up_body(c, k):
            x = xbf_ref[rows(c, k), :]
            g = jnp.dot(x, wg_ref[0].astype(bf16), preferred_element_type=f32) + bg
            u = jnp.dot(x, wu_ref[0].astype(bf16), preferred_element_type=f32) + bu
            g = jnp.minimum(g, SWIGLU_LIMIT)
            u = jnp.clip(u, -SWIGLU_LIMIT, SWIGLU_LIMIT)
            a = (u + 1.0) * (g * (1.0 / (1.0 + jnp.exp(-SWIGLU_ALPHA * g))))
            act_ref[rows(c, k), :] = a.astype(bf16)

        cover(up_body)

        @pl.when(f == 0)
        def _():
            drain()
            bias = jnp.broadcast_to(bd_ref[0], (ROW_CHUNK, D_MODEL))

            def init(c, _):
                acc_ref[rows(c), :] = bias
                return 0

            lax.fori_loop(0, n, init, 0)

        def down_body(c, k):
            acc_ref[rows(c, k), :] += jnp.dot(act_ref[rows(c, k), :], wd_ref[0].astype(bf16),
                                              preferred_element_type=f32)

        cover(down_body)

        @pl.when(f == N_FF_TILES - 1)
        def _():
            def write(c, _):
                pltpu.make_async_copy(acc_ref.at[rows(c), :], ys_ref.at[rows(c0 + c), :], sem_ref.at[1]).start()
                return 0

            lax.fori_loop(0, n, write, 0)
            pend_ref[0] = n

    @pl.when(jnp.logical_and(s == pl.num_programs(0) - 1, f == N_FF_TILES - 1))
    def _():
        drain()
        acc_ref[0:ROW_CHUNK, :] = jnp.zeros((ROW_CHUNK, D_MODEL), f32)

        def tail_copy(c):
            return pltpu.make_async_copy(acc_ref.at[0:ROW_CHUNK, :], ys_ref.at[rows(c), :], sem_ref.at[0])

        def start(c, _):
            tail_copy(c).start()
            return 0

        def finish(c, _):
            tail_copy(c).wait()
            return 0

        lax.fori_loop(used_ref[1], MAX_CHUNKS, start, 0)
        lax.fori_loop(used_ref[1], MAX_CHUNKS, finish, 0)


def _experts(sb_e, sb_c0, sb_n, n_used, xs, w_gate, b_gate, w_up, b_up, w_down, b_down):
    def up_map(s, f, e, c0, n, used):
        return (e[s], 0, f)

    def down_map(s, f, e, c0, n, used):
        return (e[s], f, 0)

    def bias_map(s, f, e, c0, n, used):
        return (e[s], 0, 0)

    return pl.pallas_call(
        _experts_kernel,
        name="experts",
        grid_spec=pltpu.PrefetchScalarGridSpec(
            num_scalar_prefetch=4,
            grid=(n_used[0], N_FF_TILES),
            in_specs=[
                pl.BlockSpec(memory_space=pl.ANY),
                pl.BlockSpec((1, D_MODEL, FF_TILE), up_map),
                pl.BlockSpec((1, D_MODEL, FF_TILE), up_map),
                pl.BlockSpec((1, FF_TILE, D_MODEL), down_map),
                pl.BlockSpec((1, 1, FF_TILE), up_map),
                pl.BlockSpec((1, 1, FF_TILE), up_map),
                pl.BlockSpec((1, 1, D_MODEL), bias_map),
            ],
            out_specs=pl.BlockSpec(memory_space=pl.ANY),
            scratch_shapes=[
                pltpu.VMEM((SB_ROWS * ROW_TILE, LANES), jnp.uint32),
                pltpu.VMEM((SB_ROWS, D_MODEL), bf16),
                pltpu.VMEM((SB_ROWS, FF_TILE), bf16),
                pltpu.VMEM((SB_ROWS, D_MODEL), f32),
                pltpu.SMEM((1,), jnp.int32),
                pltpu.SemaphoreType.DMA((2,)),
            ],
        ),
        out_shape=jax.ShapeDtypeStruct((MAX_ROWS, D_MODEL), f32),
        compiler_params=_cparams(("arbitrary", "arbitrary"), EXPERTS_VMEM_LIMIT),
    )(sb_e, sb_c0, sb_n, n_used, xs, w_gate, w_up, w_down,
      b_gate.reshape(N_EXPERTS, 1, D_FF), b_up.reshape(N_EXPERTS, 1, D_FF),
      b_down.reshape(N_EXPERTS, 1, D_MODEL))


COMB_TM = 256


def _combine_kernel(dest_ref, next_ref, ys_ref, x1_ref, gate_ref, gf_ref, o_ref, buf_ref, sem_ref):
    i = pl.program_id(0)
    slot = i % 2

    def gather(rows_ref, into):
        def issue(g, _):
            for u in range(SUBLANES):
                for k in range(TOP_K):
                    row = rows_ref[(g * SUBLANES + u) * TOP_K + k]
                    pltpu.make_async_copy(ys_ref.at[pl.ds(row, 1), :],
                                          buf_ref.at[into, k, g, pl.ds(u, 1), :],
                                          sem_ref.at[into]).start(priority=k % 2)
            return 0

        lax.fori_loop(0, COMB_TM // SUBLANES, issue, 0)

    @pl.when(i == 0)
    def _():
        gather(dest_ref, 0)

    has_next = i + 1 < pl.num_programs(0)
    for into in range(2):
        @pl.when(jnp.logical_and(has_next, slot == 1 - into))
        def _():
            gather(next_ref, into)

    for k in range(TOP_K):
        pltpu.make_async_copy(buf_ref.at[slot, k], buf_ref.at[slot, k], sem_ref.at[slot]).wait()

    gate = gate_ref[...]
    acc = x1_ref[...]
    for k in range(TOP_K):
        acc = acc + gate[:, k:k + 1] * buf_ref[slot, k].reshape(COMB_TM, D_MODEL)
    o_ref[...] = _rms(acc, gf_ref[...])


def _combine(dest, ys, x1, gates, g_final):
    n_tiles = SEQ // COMB_TM
    return pl.pallas_call(
        _combine_kernel,
        name="combine",
        grid=(SEQ // COMB_TM,),
        in_specs=[
            pl.BlockSpec((COMB_TM * TOP_K,), lambda i: (i,), memory_space=pltpu.SMEM),
            pl.BlockSpec((COMB_TM * TOP_K,), lambda i: (jnp.minimum(i + 1, n_tiles - 1),),
                         memory_space=pltpu.SMEM),
            pl.BlockSpec(memory_space=pl.ANY),
            pl.BlockSpec((COMB_TM, D_MODEL), lambda i: (i, 0)),
            pl.BlockSpec((COMB_TM, LANES), lambda i: (i, 0)),
            pl.BlockSpec((1, D_MODEL), lambda i: (0, 0)),
        ],
        out_specs=pl.BlockSpec((COMB_TM, D_MODEL), lambda i: (i, 0)),
        out_shape=jax.ShapeDtypeStruct((SEQ, D_MODEL), f32),
        scratch_shapes=[
            pltpu.VMEM((2, TOP_K, COMB_TM // SUBLANES, SUBLANES, D_MODEL), f32),
            pltpu.SemaphoreType.DMA((2,)),
        ],
        compiler_params=_cparams(("arbitrary",)),
    )(dest, dest, ys, x1, gates, g_final.reshape(1, D_MODEL))


def kernel(x, g_mix, w_in, lam_re_fwd, lam_im_fwd, log_dt_fwd, b_re_fwd, b_im_fwd, c_re_fwd, c_im_fwd, lam_re_bwd, lam_im_bwd, log_dt_bwd, b_re_bwd, b_im_bwd, c_re_bwd, c_im_bwd, ssm_d, w_glu, b_glu, na_rpb, g_ssm_out, g_na_out, w_out, g_moe, w_router, b_router, w_gate, b_gate, w_up, b_up, w_down, b_down, g_final):
    x2 = x.reshape(SEQ, D_MODEL)

    u, qkv = _in_proj(x2, g_mix[0], w_in[0].astype(bf16))

    s5_w, s5_dec = _s5_weights(
        (lam_re_fwd[0], lam_im_fwd[0], log_dt_fwd[0], b_re_fwd[0], b_im_fwd[0], c_re_fwd[0], c_im_fwd[0]),
        (lam_re_bwd[0], lam_im_bwd[0], log_dt_bwd[0], b_re_bwd[0], b_im_bwd[0], c_re_bwd[0], c_im_bwd[0]))
    y = _s5(u, s5_w, s5_dec, ssm_d[0])
    ssm_n = _glu(y, w_glu[0].astype(bf16), b_glu[0], g_ssm_out[0])

    y_na = _natten(qkv, _na_bias(na_rpb[0]))

    w_router_pad = jnp.zeros((D_MODEL, LANES), bf16).at[:, :N_EXPERTS].set(w_router[0].astype(bf16))
    b_router_pad = jnp.full((1, LANES), NEG_BIG, f32).at[0, :N_EXPERTS].set(b_router[0].astype(f32))
    x1, hn, logits = _out_proj(ssm_n, y_na, x2, g_na_out[0], w_out[0].astype(bf16), g_moe[0],
                               w_router_pad, b_router_pad)

    dest_lanes, gates, counts = _route(logits)
    dest = dest_lanes[:, :TOP_K].reshape(-1)
    last_chunk, sb_e, sb_c0, sb_n, n_used = _routing_tables(counts)

    xs = _dispatch(last_chunk, dest, hn)
    ys = _experts(sb_e, sb_c0, sb_n, n_used, xs, w_gate[0], b_gate[0], w_up[0], b_up[0],
                  w_down[0], b_down[0])
    out = _combine(dest, ys, x1, gates, g_final)
    return out.reshape(x.shape)
```

```python
import math

import jax
import jax.numpy as jnp
from jax import lax
from jax.experimental import pallas as pl
from jax.experimental.pallas import tpu as pltpu

f32 = jnp.float32
bf16 = jnp.bfloat16

D_MODEL = 2048
SEQ = 8192
SSM_WIDTH = 1024
NA_WIDTH = 1024
SSM_GROUP = 16
SSM_STATE = 64
NA_HEAD_DIM = 64
NA_HEADS = 16
GRID_W = 64
GRID_ROWS = SEQ // GRID_W
NA_KH = 8
NA_KW = 16
N_EXPERTS = 32
TOP_K = 4
D_FF = 2048
SWIGLU_LIMIT = 7.0
SWIGLU_ALPHA = 1.702
RMS_EPS = 1e-5

LANES = 128
HALF_D = D_MODEL // 2
SUBLANES = 8
ROW_TILE = HALF_D // LANES
assert ROW_TILE == SUBLANES
NEG_BIG = -1e30

CHUNK_T = 16
N_CHUNKS = SEQ // CHUNK_T
GROUPS_PER_BLOCK = LANES // SSM_GROUP
N_LANE_BLOCKS = SSM_WIDTH // LANES
CAT_W = CHUNK_T * LANES
STATE_W = GROUPS_PER_BLOCK * SSM_STATE

ROW_CHUNK = 128
CHUNKS_PER_SB = 12
SB_ROWS = ROW_CHUNK * CHUNKS_PER_SB
MAX_CHUNKS = SEQ * TOP_K // ROW_CHUNK + N_EXPERTS
MAX_ROWS = MAX_CHUNKS * ROW_CHUNK
MAX_SB = MAX_CHUNKS // CHUNKS_PER_SB + N_EXPERTS
FF_TILE = 512
N_FF_TILES = D_FF // FF_TILE

VMEM_LIMIT = 56 * 1024 * 1024
EXPERTS_VMEM_LIMIT = 60 * 1024 * 1024


def _cparams(semantics, vmem=VMEM_LIMIT):
    return pltpu.CompilerParams(dimension_semantics=semantics, vmem_limit_bytes=vmem)


def _rms(x, g):
    return x * lax.rsqrt(jnp.mean(x * x, axis=-1, keepdims=True) + RMS_EPS) * g


IN_TM = 1024
IN_TN = 1024


def _in_proj_kernel(x_ref, g_ref, w_ref, u_ref, qkv_ref, h_ref):
    j = pl.program_id(1)

    @pl.when(j == 0)
    def _():
        h_ref[...] = _rms(x_ref[...], g_ref[...]).astype(bf16)

    acc = jnp.dot(h_ref[...], w_ref[...], preferred_element_type=f32)

    @pl.when(j == 0)
    def _():
        u_ref[...] = acc

    @pl.when(j > 0)
    def _():
        qkv_ref[...] = acc.astype(bf16)


def _in_proj(x, g_mix, w_in_bf):
    n_out = w_in_bf.shape[1]
    return pl.pallas_call(
        _in_proj_kernel,
        name="in_proj",
        grid=(SEQ // IN_TM, n_out // IN_TN),
        in_specs=[
            pl.BlockSpec((IN_TM, D_MODEL), lambda i, j: (i, 0)),
            pl.BlockSpec((1, D_MODEL), lambda i, j: (0, 0)),
            pl.BlockSpec((D_MODEL, IN_TN), lambda i, j: (0, j)),
        ],
        out_specs=[
            pl.BlockSpec((IN_TM, IN_TN), lambda i, j: (i, 0)),
            pl.BlockSpec((IN_TM, IN_TN), lambda i, j: (i, jnp.maximum(j - 1, 0))),
        ],
        out_shape=[
            jax.ShapeDtypeStruct((SEQ, SSM_WIDTH), f32),
            jax.ShapeDtypeStruct((SEQ, 3 * NA_WIDTH), bf16),
        ],
        scratch_shapes=[pltpu.VMEM((IN_TM, D_MODEL), bf16)],
        compiler_params=_cparams(("arbitrary", "arbitrary")),
    )(x, g_mix.reshape(1, D_MODEL), w_in_bf)


def _cmul(ar, ai, br, bi):
    return ar * br - ai * bi, ar * bi + ai * br


def _s5_discretise(lam_re, lam_im, log_dt, b_re, b_im, c_re, c_im):
    a = jnp.minimum(lam_re.astype(f32), -1e-4)
    w = lam_im.astype(f32)
    dt = jnp.exp(log_dt.astype(f32))[:, None]
    steps = jnp.arange(CHUNK_T + 1, dtype=f32)[:, None, None]
    mag = jnp.exp((a * dt)[None] * steps)
    ang = (w * dt)[None] * steps
    pw = (mag * jnp.cos(ang), mag * jnp.sin(ang))
    xr, xi = pw[0][1] - 1.0, pw[1][1]
    den = a * a + w * w
    qr, qi = (xr * a + xi * w) / den, (xi * a - xr * w) / den
    bb = _cmul(qr[..., None], qi[..., None], b_re.astype(f32), b_im.astype(f32))
    return pw, bb, (c_re.astype(f32), c_im.astype(f32))


def _pair_blockdiag(m):
    z = jnp.zeros_like(m[..., 0, :, :])
    top = jnp.concatenate([m[..., 0, :, :], z], axis=-1)
    bot = jnp.concatenate([z, m[..., 1, :, :]], axis=-1)
    return jnp.concatenate([top, bot], axis=-2)


def _s5_weights(fwd, bwd):
    hi = lax.Precision.HIGH
    exact = lax.Precision.HIGHEST
    t = CHUNK_T
    pw_f, bb_f, c_f = _s5_discretise(*fwd)
    pw_b, bb_b, c_b = _s5_discretise(*bwd)
    pairs = (N_LANE_BLOCKS, GROUPS_PER_BLOCK // 2, 2)

    x = jnp.arange(t * SSM_GROUP)
    t_of_x = x // SSM_GROUP
    tile_c = (jnp.arange(SSM_GROUP)[:, None] == x[None, :] % SSM_GROUP).astype(f32)

    def c_over_x(c):
        return [jnp.einsum('gcp,cx->gpx', part, tile_c, precision=exact) for part in c]

    def c_times_power(c_x, pw, power_of_x):
        rep = (jnp.arange(t + 1)[:, None] == power_of_x[None, :]).astype(f32)
        pw_x = [jnp.einsum('jgp,jx->gpx', part, rep, precision=exact) for part in pw]
        return _cmul(c_x[0], c_x[1], pw_x[0], pw_x[1])

    cx_f, cx_b = c_over_x(c_f), c_over_x(c_b)

    def lag_kernel(c_x, pw, bb, power_of_x):
        m_re, m_im = c_times_power(c_x, pw, power_of_x)
        m = jnp.concatenate([m_re, m_im], axis=1)
        b = jnp.concatenate([jnp.swapaxes(bb[0], 1, 2), -jnp.swapaxes(bb[1], 1, 2)], axis=-1)
        return jnp.einsum('gdp,gpx->gdx', b, m, precision=hi)

    kf = lag_kernel(cx_f, pw_f, bb_f, t_of_x)
    kb = lag_kernel(cx_b, pw_b, bb_b, t - 1 - t_of_x)
    keep = (t - 1) * SSM_GROUP
    k_lag = jnp.concatenate([kb[..., :keep], kb[..., keep:] + kf[..., :SSM_GROUP], kf[..., SSM_GROUP:]],
                            axis=-1)
    w_intra = jnp.stack([k_lag[..., (t - 1 - ti) * SSM_GROUP:(2 * t - 1 - ti) * SSM_GROUP]
                         for ti in range(t)], axis=1)
    w_intra = _pair_blockdiag(w_intra.astype(bf16).reshape(*pairs, t * SSM_GROUP, t * SSM_GROUP))

    def state_in(pw_t, bb):
        pw_g = [jnp.transpose(part, (1, 0, 2))[:, :, None, :] for part in pw_t]
        bb_g = [jnp.swapaxes(part, 1, 2)[:, None] for part in bb]
        return [_pair_blockdiag(part.astype(bf16).reshape(*pairs, t * SSM_GROUP, SSM_STATE))
                for part in _cmul(pw_g[0], pw_g[1], bb_g[0], bb_g[1])]

    w_in = jnp.concatenate(state_in([part[:t][::-1] for part in pw_f], bb_f)
                           + state_in([part[:t] for part in pw_b], bb_b), axis=-1)

    def state_out(c_x, pw, power_of_x):
        m_re, m_im = c_times_power(c_x, pw, power_of_x)
        return [_pair_blockdiag(part.astype(bf16).reshape(*pairs, SSM_STATE, t * SSM_GROUP))
                for part in (m_re, -m_im)]

    w_out = jnp.concatenate(state_out(cx_f, pw_f, t_of_x + 1) + state_out(cx_b, pw_b, t - t_of_x), axis=-2)

    def decay(a):
        return a.reshape(N_LANE_BLOCKS, 1, STATE_W)

    dec = jnp.concatenate([decay(pw_f[0][t]), decay(pw_f[1][t]), decay(pw_b[0][t]), decay(pw_b[1][t])],
                          axis=-1)
    return (w_in, w_intra, w_out), dec.astype(f32)


def _gelu_tanh(x):
    return 0.5 * x * (1.0 + jnp.tanh(math.sqrt(2.0 / math.pi) * (x + 0.044715 * (x * x * x))))


PAIRS_PER_BLOCK = GROUPS_PER_BLOCK // 2
PAIR_W = 2 * CHUNK_T * SSM_GROUP


def _s5_kernel(u_ref, win_ref, wintra_ref, wout_ref, dec_ref, d_ref, y_ref,
               cat_ref, catp_ref, st_ref, perm_ref):
    sw = STATE_W

    @pl.when(pl.program_id(0) == 0)
    def _():
        def strip(t, _):
            row = lax.broadcasted_iota(jnp.int32, (LANES, CAT_W), 0)
            col = lax.broadcasted_iota(jnp.int32, (LANES, CAT_W), 1)
            target = (row // SSM_GROUP) * (CHUNK_T * SSM_GROUP) + t * SSM_GROUP + row % SSM_GROUP
            perm_ref[pl.ds(pl.multiple_of(t * LANES, LANES), LANES), :] = (
                jnp.where(col == target, 1.0, 0.0).astype(bf16))
            return 0

        lax.fori_loop(0, CHUNK_T, strip, 0)

    for t in range(CHUNK_T):
        cat_ref[:, t * LANES:(t + 1) * LANES] = u_ref[pl.ds(t, N_CHUNKS, stride=CHUNK_T), :].astype(bf16)
    catp_ref[...] = jnp.dot(cat_ref[...], perm_ref[...], preferred_element_type=f32).astype(bf16)

    for p in range(PAIRS_PER_BLOCK):
        z = jnp.dot(catp_ref[:, p * PAIR_W:(p + 1) * PAIR_W], win_ref[0, p], preferred_element_type=f32)
        for part in range(4):
            st_ref[:, part * sw + p * LANES:part * sw + (p + 1) * LANES] = z[:, part * LANES:(part + 1) * LANES]

    afr = dec_ref[0, :, 0 * sw:1 * sw]
    afi = dec_ref[0, :, 1 * sw:2 * sw]
    abr = dec_ref[0, :, 2 * sw:3 * sw]
    abi = dec_ref[0, :, 3 * sw:4 * sw]

    def step(i, carry):
        sfr, sfi, sbr, sbi = carry
        r = N_CHUNKS - 1 - i
        zf = st_ref[pl.ds(i, 1), 0:2 * sw]
        zb = st_ref[pl.ds(r, 1), 2 * sw:4 * sw]
        st_ref[pl.ds(i, 1), 0:2 * sw] = jnp.concatenate([sfr, sfi], axis=-1)
        st_ref[pl.ds(r, 1), 2 * sw:4 * sw] = jnp.concatenate([sbr, sbi], axis=-1)
        nfr = afr * sfr - afi * sfi + zf[:, :sw]
        nfi = afi * sfr + afr * sfi + zf[:, sw:]
        nbr = abr * sbr - abi * sbi + zb[:, :sw]
        nbi = abi * sbr + abr * sbi + zb[:, sw:]
        return nfr, nfi, nbr, nbi

    zero = jnp.zeros((1, sw), f32)
    lax.fori_loop(0, N_CHUNKS, step, (zero, zero, zero, zero))

    for p in range(PAIRS_PER_BLOCK):
        s_in = jnp.concatenate(
            [st_ref[:, part * sw + p * LANES:part * sw + (p + 1) * LANES] for part in range(4)], axis=-1)
        yp = jnp.dot(catp_ref[:, p * PAIR_W:(p + 1) * PAIR_W], wintra_ref[0, p], preferred_element_type=f32)
        yp = yp + jnp.dot(s_in.astype(bf16), wout_ref[0, p], preferred_element_type=f32)
        cat_ref[:, p * PAIR_W:(p + 1) * PAIR_W] = yp.astype(bf16)

    st_ref[...] = lax.dot_general(cat_ref[...], perm_ref[...], (((1,), (1,)), ((), ())),
                                  preferred_element_type=f32)
    d = d_ref[...]
    for t in range(CHUNK_T):
        rows = pl.ds(t, N_CHUNKS, stride=CHUNK_T)
        v = st_ref[:, t * LANES:(t + 1) * LANES] + d * u_ref[rows, :]
        y_ref[rows, :] = _gelu_tanh(v)


def _s5(u, weights, dec, ssm_d):
    w_spec = pl.BlockSpec((1, PAIRS_PER_BLOCK, PAIR_W, PAIR_W), lambda j: (j, 0, 0, 0))
    return pl.pallas_call(
        _s5_kernel,
        name="s5_scan",
        grid=(N_LANE_BLOCKS,),
        in_specs=[
            pl.BlockSpec((SEQ, LANES), lambda j: (0, j)),
            w_spec, w_spec, w_spec,
            pl.BlockSpec((1, 1, 4 * STATE_W), lambda j: (j, 0, 0)),
            pl.BlockSpec((1, LANES), lambda j: (0, j)),
        ],
        out_specs=pl.BlockSpec((SEQ, LANES), lambda j: (0, j)),
        out_shape=jax.ShapeDtypeStruct((SEQ, SSM_WIDTH), f32),
        scratch_shapes=[
            pltpu.VMEM((N_CHUNKS, CAT_W), bf16),
            pltpu.VMEM((N_CHUNKS, CAT_W), bf16),
            pltpu.VMEM((N_CHUNKS, 4 * STATE_W), f32),
            pltpu.VMEM((CAT_W, CAT_W), bf16),
        ],
        compiler_params=_cparams(("arbitrary",)),
    )(u, *weights, dec, ssm_d.reshape(1, SSM_WIDTH))


GLU_TM = 1024


def _glu_kernel(y_ref, w_ref, b_ref, g_ref, o_ref):
    y = y_ref[...]
    z = jnp.dot(y.astype(bf16), w_ref[...], preferred_element_type=f32) + b_ref[...]
    o = y * (1.0 / (1.0 + jnp.exp(-z)))
    o_ref[...] = _rms(o, g_ref[...]).astype(bf16)


def _glu(y, w_glu_bf, b_glu, g_ssm_out):
    return pl.pallas_call(
        _glu_kernel,
        name="glu_norm",
        grid=(SEQ // GLU_TM,),
        in_specs=[
            pl.BlockSpec((GLU_TM, SSM_WIDTH), lambda i: (i, 0)),
            pl.BlockSpec((SSM_WIDTH, SSM_WIDTH), lambda i: (0, 0)),
            pl.BlockSpec((1, SSM_WIDTH), lambda i: (0, 0)),
            pl.BlockSpec((1, SSM_WIDTH), lambda i: (0, 0)),
        ],
        out_specs=pl.BlockSpec((GLU_TM, SSM_WIDTH), lambda i: (i, 0)),
        out_shape=jax.ShapeDtypeStruct((SEQ, SSM_WIDTH), bf16),
        compiler_params=_cparams(("arbitrary",)),
    )(y, w_glu_bf, b_glu.reshape(1, SSM_WIDTH), g_ssm_out.reshape(1, SSM_WIDTH))


NA_ROWS_PER_STEP = 16
NA_WIN = NA_KH * GRID_W
HEADS_PER_BLOCK = LANES // NA_HEAD_DIM


def _na_bias(rpb):
    c = jnp.arange(GRID_W)
    col_start = jnp.clip(c - NA_KW // 2, 0, GRID_W - NA_KW)
    valid = (c[None, :] >= col_start[:, None]) & (c[None, :] < col_start[:, None] + NA_KW)
    dc = jnp.clip(c[None, :] - c[:, None], -(NA_KW - 1), NA_KW - 1) + (NA_KW - 1)
    sel = (dc[None] == jnp.arange(2 * NA_KW - 1)[:, None, None]).astype(f32)
    tab = jnp.einsum('hrc,cqk->hqrk', rpb.astype(f32), sel, precision=lax.Precision.HIGHEST)
    tab = jnp.where(valid[None, :, None, :], tab, NEG_BIG)
    tab = tab.reshape(NA_HEADS, GRID_W, (2 * NA_KH - 1) * GRID_W)
    return jnp.stack([tab[..., (NA_KH - 1 - v) * GRID_W:(2 * NA_KH - 1 - v) * GRID_W]
                      for v in range(NA_KH)], axis=1)


def _natten_kernel(q_ref, k_ref, v_ref, b_ref, o_ref, s_ref, p_ref):
    rb = pl.program_id(1)
    lane = lax.broadcasted_iota(jnp.int32, (GRID_W, LANES), 1)
    head0 = lane < NA_HEAD_DIM
    scale = NA_HEAD_DIM ** -0.5

    starts, variants = [], []
    for i in range(NA_ROWS_PER_STEP):
        r = rb * NA_ROWS_PER_STEP + i
        rs = jnp.clip(r - NA_KH // 2, 0, GRID_ROWS - NA_KH)
        starts.append(pl.multiple_of(rs * GRID_W, GRID_W))
        variants.append(r - rs)

    for i in range(NA_ROWS_PER_STEP):
        q = q_ref[i * GRID_W:(i + 1) * GRID_W, :] * scale
        kw = k_ref[pl.ds(starts[i], NA_WIN), :]
        for h in range(HEADS_PER_BLOCK):
            qh = jnp.where(head0 if h == 0 else ~head0, q, jnp.zeros_like(q))
            s = lax.dot_general(qh, kw, (((1,), (1,)), ((), ())), preferred_element_type=f32)
            s_ref[i * HEADS_PER_BLOCK + h] = s + b_ref[h, variants[i]]

    n_tiles = NA_ROWS_PER_STEP * HEADS_PER_BLOCK
    maxes = [jnp.max(s_ref[t], axis=-1, keepdims=True) for t in range(n_tiles)]
    inv_sums = []
    for t in range(n_tiles):
        p = jnp.exp(s_ref[t] - maxes[t])
        inv_sums.append(1.0 / jnp.sum(p, axis=-1, keepdims=True))
        p_ref[t] = p.astype(bf16)

    for i in range(NA_ROWS_PER_STEP):
        vw = v_ref[pl.ds(starts[i], NA_WIN), :]
        outs = []
        for h in range(HEADS_PER_BLOCK):
            t = i * HEADS_PER_BLOCK + h
            outs.append(jnp.dot(p_ref[t], vw, preferred_element_type=f32) * inv_sums[t])
        o_ref[i * GRID_W:(i + 1) * GRID_W, :] = jnp.where(head0, outs[0], outs[1]).astype(bf16)


def _natten(qkv, bias):
    tm = NA_ROWS_PER_STEP * GRID_W
    n_hb = NA_WIDTH // LANES
    return pl.pallas_call(
        _natten_kernel,
        name="natten",
        grid=(n_hb, GRID_ROWS // NA_ROWS_PER_STEP),
        in_specs=[
            pl.BlockSpec((tm, LANES), lambda h, r: (r, h)),
            pl.BlockSpec((SEQ, LANES), lambda h, r: (0, n_hb + h)),
            pl.BlockSpec((SEQ, LANES), lambda h, r: (0, 2 * n_hb + h)),
            pl.BlockSpec((HEADS_PER_BLOCK, NA_KH, GRID_W, NA_WIN), lambda h, r: (h, 0, 0, 0)),
        ],
        out_specs=pl.BlockSpec((tm, LANES), lambda h, r: (r, h)),
        out_shape=jax.ShapeDtypeStruct((SEQ, NA_WIDTH), bf16),
        scratch_shapes=[
            pltpu.VMEM((NA_ROWS_PER_STEP * HEADS_PER_BLOCK, GRID_W, NA_WIN), f32),
            pltpu.VMEM((NA_ROWS_PER_STEP * HEADS_PER_BLOCK, GRID_W, NA_WIN), bf16),
        ],
        compiler_params=_cparams(("arbitrary", "arbitrary")),
    )(qkv, qkv, qkv, bias)


OUT_TM = 512


def _out_proj_kernel(ssm_ref, na_ref, x_ref, gna_ref, w_ref, gmoe_ref, wr_ref, br_ref,
                     x1_ref, hn_ref, lg_ref):
    na = _rms(na_ref[...].astype(f32), gna_ref[...]).astype(bf16)
    y = jnp.dot(ssm_ref[...], w_ref[0:SSM_WIDTH, :], preferred_element_type=f32)
    y = y + jnp.dot(na, w_ref[SSM_WIDTH:, :], preferred_element_type=f32)
    x1 = x_ref[...] + y
    x1_ref[...] = x1
    hn = _rms(x1, gmoe_ref[...]).astype(bf16)
    lg_ref[...] = jnp.dot(hn, wr_ref[...], preferred_element_type=f32) + br_ref[...]
    bits = lax.bitcast_convert_type(hn.astype(f32), jnp.uint32)
    packed = (bits[:, HALF_D:] & jnp.uint32(0xFFFF0000)) | (bits[:, :HALF_D] >> 16)
    for j in range(ROW_TILE):
        hn_ref[pl.ds(j, OUT_TM, stride=ROW_TILE), :] = packed[:, j * LANES:(j + 1) * LANES]


def _out_proj(ssm_n, y_na, x, g_na_out, w_out_bf, g_moe, w_router_pad, b_router_pad):
    row = lambda i: (i, 0)
    fixed = lambda i: (0, 0)
    return pl.pallas_call(
        _out_proj_kernel,
        name="out_proj",
        grid=(SEQ // OUT_TM,),
        in_specs=[
            pl.BlockSpec((OUT_TM, SSM_WIDTH), row),
            pl.BlockSpec((OUT_TM, NA_WIDTH), row),
            pl.BlockSpec((OUT_TM, D_MODEL), row),
            pl.BlockSpec((1, NA_WIDTH), fixed),
            pl.BlockSpec((D_MODEL, D_MODEL), fixed),
            pl.BlockSpec((1, D_MODEL), fixed),
            pl.BlockSpec((D_MODEL, LANES), fixed),
            pl.BlockSpec((1, LANES), fixed),
        ],
        out_specs=[
            pl.BlockSpec((OUT_TM, D_MODEL), row),
            pl.BlockSpec((OUT_TM * ROW_TILE, LANES), row),
            pl.BlockSpec((OUT_TM, LANES), row),
        ],
        out_shape=[
            jax.ShapeDtypeStruct((SEQ, D_MODEL), f32),
            jax.ShapeDtypeStruct((SEQ * ROW_TILE, LANES), jnp.uint32),
            jax.ShapeDtypeStruct((SEQ, LANES), f32),
        ],
        compiler_params=_cparams(("arbitrary",)),
    )(ssm_n, y_na, x, g_na_out.reshape(1, NA_WIDTH), w_out_bf, g_moe.reshape(1, D_MODEL),
      w_router_pad, b_router_pad)


ROUTE_TM = 1024


def _route_kernel(lg_ref, tri_ref, dest_ref, gate_ref, cnt_ref, carry_ref, meta_ref):
    phase = pl.program_id(0)
    i = pl.program_id(1)
    rows = pl.ds(pl.multiple_of(i * ROUTE_TM, ROUTE_TM), ROUTE_TM)
    lane = lax.broadcasted_iota(jnp.int32, (ROUTE_TM, LANES), 1)

    @pl.when(phase == 0)
    def _():
        @pl.when(i == 0)
        def _():
            carry_ref[...] = jnp.zeros_like(carry_ref)

        lane_f = lane.astype(f32)
        work = lg_ref[...]
        vals, hits = [], []
        for _ in range(TOP_K):
            m = jnp.max(work, axis=-1, keepdims=True)
            idx = jnp.min(jnp.where(work == m, lane_f, float(LANES)), axis=-1, keepdims=True)
            hit = lane_f == idx
            vals.append(m)
            hits.append((idx, hit))
            work = jnp.where(hit, -jnp.inf, work)

        exps = [jnp.exp(v - vals[0]) for v in vals]
        denom = exps[0] + exps[1] + exps[2] + exps[3]

        onehot = jnp.zeros((ROUTE_TM, LANES), f32)
        for _, hit in hits:
            onehot = onehot + hit.astype(f32)
        before = jnp.dot(tri_ref[...], onehot.astype(bf16), preferred_element_type=f32) + carry_ref[...]

        meta = jnp.zeros((ROUTE_TM, LANES), jnp.int32)
        gate = jnp.zeros((ROUTE_TM, LANES), f32)
        for k, (idx, hit) in enumerate(hits):
            rank = jnp.sum(jnp.where(hit, before, 0.0), axis=-1, keepdims=True).astype(jnp.int32)
            meta = jnp.where(lane == k, idx.astype(jnp.int32), meta)
            meta = jnp.where(lane == TOP_K + k, rank, meta)
            gate = jnp.where(lane == k, exps[k] / denom, gate)
        meta_ref[rows, :] = meta
        gate_ref[...] = gate
        carry_ref[...] += jnp.sum(onehot, axis=0, keepdims=True)
        cnt_ref[...] = carry_ref[...]

    @pl.when(phase == 1)
    def _():
        n_chunks = jnp.floor((carry_ref[...] + float(ROW_CHUNK - 1)) * (1.0 / ROW_CHUNK))
        src = lax.broadcasted_iota(jnp.int32, (LANES, LANES), 0)
        dst = lax.broadcasted_iota(jnp.int32, (LANES, LANES), 1)
        earlier = jnp.where(src < dst, 1.0, 0.0).astype(bf16)
        first_chunk = jnp.dot(jnp.broadcast_to(n_chunks, (SUBLANES, LANES)).astype(bf16), earlier,
                              preferred_element_type=f32)[0:1]
        first_row = first_chunk * float(ROW_CHUNK)
        meta = meta_ref[rows, :]
        dest = jnp.zeros((ROUTE_TM, LANES), jnp.int32)
        for k in range(TOP_K):
            hit = lane == meta[:, k:k + 1]
            base = jnp.sum(jnp.where(hit, first_row, 0.0), axis=-1, keepdims=True).astype(jnp.int32)
            dest = jnp.where(lane == k, base + meta[:, TOP_K + k:TOP_K + k + 1], dest)
        dest_ref[...] = dest


def _route(logits):
    tri = (jnp.arange(ROUTE_TM)[:, None] > jnp.arange(ROUTE_TM)[None, :]).astype(bf16)
    n_tiles = SEQ // ROUTE_TM
    phase0 = lambda p, i: (jnp.where(p == 0, i, n_tiles - 1), 0)
    phase1 = lambda p, i: (i * p, 0)
    return pl.pallas_call(
        _route_kernel,
        name="route",
        grid=(2, n_tiles),
        in_specs=[
            pl.BlockSpec((ROUTE_TM, LANES), phase0),
            pl.BlockSpec((ROUTE_TM, ROUTE_TM), lambda p, i: (0, 0)),
        ],
        out_specs=[
            pl.BlockSpec((ROUTE_TM, LANES), phase1),
            pl.BlockSpec((ROUTE_TM, LANES), phase0),
            pl.BlockSpec((1, LANES), lambda p, i: (0, 0)),
        ],
        out_shape=[
            jax.ShapeDtypeStruct((SEQ, LANES), jnp.int32),
            jax.ShapeDtypeStruct((SEQ, LANES), f32),
            jax.ShapeDtypeStruct((1, LANES), f32),
        ],
        scratch_shapes=[pltpu.VMEM((1, LANES), f32), pltpu.VMEM((SEQ, LANES), jnp.int32)],
        compiler_params=_cparams(("arbitrary", "arbitrary")),
    )(logits, tri)


def _routing_tables(counts):
    cnt = counts[0, :N_EXPERTS].astype(jnp.int32)
    n_chunks = (cnt + ROW_CHUNK - 1) // ROW_CHUNK
    chunk_base = jnp.cumsum(n_chunks) - n_chunks
    total_chunks = jnp.sum(n_chunks)
    last_chunk = jnp.concatenate([jnp.where(cnt > 0, chunk_base + n_chunks - 1, -1),
                                  total_chunks[None]]).astype(jnp.int32)

    n_sb = (n_chunks + CHUNKS_PER_SB - 1) // CHUNKS_PER_SB
    sb_end = jnp.cumsum(n_sb)
    sb_start = sb_end - n_sb
    n_used = sb_end[-1]
    s = jnp.arange(MAX_SB)
    s_eff = jnp.minimum(s, n_used - 1)
    e = jnp.minimum(jnp.searchsorted(sb_end, s_eff, side='right'), N_EXPERTS - 1)
    kk = s_eff - sb_start[e]
    per_sb = n_chunks[e] // jnp.maximum(n_sb[e], 1)
    extra = n_chunks[e] - per_sb * n_sb[e]
    sb_chunk0 = chunk_base[e] + kk * per_sb + jnp.minimum(kk, extra)
    sb_n = jnp.where(s < n_used, per_sb + (kk < extra), 0)
    used = jnp.stack([n_used, total_chunks]).astype(jnp.int32)
    return (last_chunk, e.astype(jnp.int32), sb_chunk0.astype(jnp.int32), sb_n.astype(jnp.int32), used)


DISP_TM = 512


def _packed_rows(first_row, n_rows=1):
    return pl.ds(pl.multiple_of(first_row * ROW_TILE, ROW_TILE), n_rows * ROW_TILE)


def _row_copy(src, src_row, dst, dst_row, sem):
    return pltpu.make_async_copy(src.at[_packed_rows(src_row), :], dst.at[_packed_rows(dst_row), :], sem)


def _dispatch_kernel(last_ref, dest_ref, hn_ref, xs_ref, zero_ref, sem_ref):
    i = pl.program_id(0)

    @pl.when(i == 0)
    def _():
        zero_ref[...] = jnp.zeros_like(zero_ref)

        def chunk_copy(c):
            return pltpu.make_async_copy(zero_ref, xs_ref.at[_packed_rows(c * ROW_CHUNK, ROW_CHUNK), :],
                                         sem_ref.at[1])

        def start(e, _):
            @pl.when(last_ref[e] >= 0)
            def _():
                chunk_copy(last_ref[e]).start()
            return 0

        def wait(e, _):
            @pl.when(last_ref[e] >= 0)
            def _():
                chunk_copy(last_ref[e]).wait()
            return 0

        def start_tail(c, _):
            chunk_copy(c).start()
            return 0

        def wait_tail(c, _):
            chunk_copy(c).wait()
            return 0

        lax.fori_loop(0, N_EXPERTS, start, 0)
        lax.fori_loop(last_ref[N_EXPERTS], MAX_CHUNKS, start_tail, 0)
        lax.fori_loop(0, N_EXPERTS, wait, 0)
        lax.fori_loop(last_ref[N_EXPERTS], MAX_CHUNKS, wait_tail, 0)

    def issue(t, _):
        for k in range(TOP_K):
            _row_copy(hn_ref, t, xs_ref, dest_ref[t * TOP_K + k], sem_ref.at[0]).start(priority=k % 2)
        return 0

    lax.fori_loop(0, DISP_TM, issue, 0, unroll=8)
    for k in range(TOP_K):
        pltpu.make_async_copy(hn_ref, xs_ref.at[_packed_rows(0, DISP_TM), :], sem_ref.at[0]).wait()


def _dispatch(last_chunk, dest, hn):
    return pl.pallas_call(
        _dispatch_kernel,
        name="dispatch",
        grid_spec=pltpu.PrefetchScalarGridSpec(
            num_scalar_prefetch=1,
            grid=(SEQ // DISP_TM,),
            in_specs=[
                pl.BlockSpec((DISP_TM * TOP_K,), lambda i, last: (i,), memory_space=pltpu.SMEM),
                pl.BlockSpec((DISP_TM * ROW_TILE, LANES), lambda i, last: (i, 0)),
            ],
            out_specs=pl.BlockSpec(memory_space=pl.ANY),
            scratch_shapes=[
                pltpu.VMEM((ROW_CHUNK * ROW_TILE, LANES), jnp.uint32),
                pltpu.SemaphoreType.DMA((2,)),
            ],
        ),
        out_shape=jax.ShapeDtypeStruct((MAX_ROWS * ROW_TILE, LANES), jnp.uint32),
        compiler_params=_cparams(("arbitrary",)),
    )(last_chunk, dest, hn)


def _experts_kernel(e_ref, c0_ref, n_ref, used_ref,
                    xs_ref, wg_ref, wu_ref, wd_ref, bg_ref, bu_ref, bd_ref, ys_ref,
                    xin_ref, xbf_ref, act_ref, acc_ref, pend_ref, sem_ref):
    s = pl.program_id(0)
    f = pl.program_id(1)
    n = n_ref[s]
    c0 = c0_ref[s]

    def rows(c, k=1):
        return pl.ds(pl.multiple_of(c * ROW_CHUNK, ROW_CHUNK), k * ROW_CHUNK)

    def cover(body, largest=4):
        def big(i, _):
            body(i * largest, largest)
            return 0

        lax.fori_loop(0, n // largest, big, 0)
        done = (n // largest) * largest
        size = largest // 2
        while size >= 1:
            @pl.when((n & size) != 0)
            def _(done=done, size=size):
                body(done, size)

            done = done + (n & size)
            size //= 2

    def drain():
        def wait_one(i, _):
            pltpu.make_async_copy(acc_ref.at[0:ROW_CHUNK, :], ys_ref.at[0:ROW_CHUNK, :], sem_ref.at[1]).wait()
            return 0

        lax.fori_loop(0, pend_ref[0], wait_one, 0)
        pend_ref[0] = 0

    @pl.when(jnp.logical_and(s == 0, f == 0))
    def _():
        pend_ref[0] = 0

    def chunk_copy(src_chunk, c):
        return pltpu.make_async_copy(xs_ref.at[_packed_rows(src_chunk * ROW_CHUNK, ROW_CHUNK), :],
                                     xin_ref.at[_packed_rows(c * ROW_CHUNK, ROW_CHUNK), :], sem_ref.at[0])

    def fetch(first_chunk, count):
        def start(c, _):
            chunk_copy(first_chunk + c, c).start()
            return 0

        lax.fori_loop(0, count, start, 0)

    @pl.when(jnp.logical_and(s == 0, f == 0))
    def _():
        fetch(c0, n)

    @pl.when(jnp.logical_and(n > 0, f == 0))
    def _():
        def finish(c, _):
            chunk_copy(c0 + c, c).wait()
            return 0

        def unpack(c, _):
            for j in range(ROW_TILE):
                w = xin_ref[pl.ds(c * (ROW_CHUNK * ROW_TILE) + j, ROW_CHUNK, stride=ROW_TILE), :]
                low = lax.bitcast_convert_type(w << 16, f32)
                high = lax.bitcast_convert_type(w & jnp.uint32(0xFFFF0000), f32)
                xbf_ref[rows(c), j * LANES:(j + 1) * LANES] = low.astype(bf16)
                xbf_ref[rows(c), HALF_D + j * LANES:HALF_D + (j + 1) * LANES] = high.astype(bf16)
            return 0

        lax.fori_loop(0, n, finish, 0)
        lax.fori_loop(0, n, unpack, 0)

    @pl.when(jnp.logical_and(s + 1 < MAX_SB, f == 1))
    def _():
        nxt = jnp.minimum(s + 1, MAX_SB - 1)
        fetch(c0_ref[nxt], n_ref[nxt])

    @pl.when(n > 0)
    def _():
        bg = bg_ref[0]
        bu = bu_ref[0]

        def up_body(c, k):
            x = xbf_ref[rows(c, k), :]
            g = jnp.dot(x, wg_ref[0].astype(bf16), preferred_element_type=f32) + bg
            u = jnp.dot(x, wu_ref[0].astype(bf16), preferred_element_type=f32) + bu
            g = jnp.minimum(g, SWIGLU_LIMIT)
            u = jnp.clip(u, -SWIGLU_LIMIT, SWIGLU_LIMIT)
            a = (u + 1.0) * (g * (1.0 / (1.0 + jnp.exp(-SWIGLU_ALPHA * g))))
            act_ref[rows(c, k), :] = a.astype(bf16)

        cover(up_body, largest=8)

        @pl.when(f == 0)
        def _():
            drain()
            bias = jnp.broadcast_to(bd_ref[0], (ROW_CHUNK, D_MODEL))

            def init(c, _):
                acc_ref[rows(c), :] = bias
                return 0

            lax.fori_loop(0, n, init, 0)

        def down_body(c, k):
            acc_ref[rows(c, k), :] += jnp.dot(act_ref[rows(c, k), :], wd_ref[0].astype(bf16),
                                              preferred_element_type=f32)

        cover(down_body)

        @pl.when(f == N_FF_TILES - 1)
        def _():
            def write(c, _):
                pltpu.make_async_copy(acc_ref.at[rows(c), :], ys_ref.at[rows(c0 + c), :], sem_ref.at[1]).start()
                return 0

            lax.fori_loop(0, n, write, 0)
            pend_ref[0] = n

    @pl.when(jnp.logical_and(s == pl.num_programs(0) - 1, f == N_FF_TILES - 1))
    def _():
        drain()
        acc_ref[0:ROW_CHUNK, :] = jnp.zeros((ROW_CHUNK, D_MODEL), f32)

        def tail_copy(c):
            return pltpu.make_async_copy(acc_ref.at[0:ROW_CHUNK, :], ys_ref.at[rows(c), :], sem_ref.at[0])

        def start(c, _):
            tail_copy(c).start()
            return 0

        def finish(c, _):
            tail_copy(c).wait()
            return 0

        lax.fori_loop(used_ref[1], MAX_CHUNKS, start, 0)
        lax.fori_loop(used_ref[1], MAX_CHUNKS, finish, 0)


def _experts(sb_e, sb_c0, sb_n, n_used, xs, w_gate, b_gate, w_up, b_up, w_down, b_down):
    def up_map(s, f, e, c0, n, used):
        return (e[s], 0, f)

    def down_map(s, f, e, c0, n, used):
        return (e[s], f, 0)

    def bias_map(s, f, e, c0, n, used):
        return (e[s], 0, 0)

    return pl.pallas_call(
        _experts_kernel,
        name="experts",
        grid_spec=pltpu.PrefetchScalarGridSpec(
            num_scalar_prefetch=4,
            grid=(n_used[0], N_FF_TILES),
            in_specs=[
                pl.BlockSpec(memory_space=pl.ANY),
                pl.BlockSpec((1, D_MODEL, FF_TILE), up_map),
                pl.BlockSpec((1, D_MODEL, FF_TILE), up_map),
                pl.BlockSpec((1, FF_TILE, D_MODEL), down_map),
                pl.BlockSpec((1, 1, FF_TILE), up_map),
                pl.BlockSpec((1, 1, FF_TILE), up_map),
                pl.BlockSpec((1, 1, D_MODEL), bias_map),
            ],
            out_specs=pl.BlockSpec(memory_space=pl.ANY),
            scratch_shapes=[
                pltpu.VMEM((SB_ROWS * ROW_TILE, LANES), jnp.uint32),
                pltpu.VMEM((SB_ROWS, D_MODEL), bf16),
                pltpu.VMEM((SB_ROWS, FF_TILE), bf16),
                pltpu.VMEM((SB_ROWS, D_MODEL), f32),
                pltpu.SMEM((1,), jnp.int32),
                pltpu.SemaphoreType.DMA((2,)),
            ],
        ),
        out_shape=jax.ShapeDtypeStruct((MAX_ROWS, D_MODEL), f32),
        compiler_params=_cparams(("arbitrary", "arbitrary"), EXPERTS_VMEM_LIMIT),
    )(sb_e, sb_c0, sb_n, n_used, xs, w_gate, w_up, w_down,
      b_gate.reshape(N_EXPERTS, 1, D_FF), b_up.reshape(N_EXPERTS, 1, D_FF),
      b_down.reshape(N_EXPERTS, 1, D_MODEL))


COMB_TM = 256


def _combine_kernel(dest_ref, next_ref, ys_ref, x1_ref, gate_ref, gf_ref, o_ref, buf_ref, sem_ref):
    i = pl.program_id(0)
    slot = i % 2

    def gather(rows_ref, into):
        def issue(g, _):
            for u in range(SUBLANES):
                for k in range(TOP_K):
                    row = rows_ref[(g * SUBLANES + u) * TOP_K + k]
                    pltpu.make_async_copy(ys_ref.at[pl.ds(row, 1), :],
                                          buf_ref.at[into, k, g, pl.ds(u, 1), :],
                                          sem_ref.at[into]).start(priority=k % 2)
            return 0

        lax.fori_loop(0, COMB_TM // SUBLANES, issue, 0)

    @pl.when(i == 0)
    def _():
        gather(dest_ref, 0)

    has_next = i + 1 < pl.num_programs(0)
    for into in range(2):
        @pl.when(jnp.logical_and(has_next, slot == 1 - into))
        def _():
            gather(next_ref, into)

    for k in range(TOP_K):
        pltpu.make_async_copy(buf_ref.at[slot, k], buf_ref.at[slot, k], sem_ref.at[slot]).wait()

    gate = gate_ref[...]
    acc = x1_ref[...]
    for k in range(TOP_K):
        acc = acc + gate[:, k:k + 1] * buf_ref[slot, k].reshape(COMB_TM, D_MODEL)
    o_ref[...] = _rms(acc, gf_ref[...])


def _combine(dest, ys, x1, gates, g_final):
    n_tiles = SEQ // COMB_TM
    return pl.pallas_call(
        _combine_kernel,
        name="combine",
        grid=(SEQ // COMB_TM,),
        in_specs=[
            pl.BlockSpec((COMB_TM * TOP_K,), lambda i: (i,), memory_space=pltpu.SMEM),
            pl.BlockSpec((COMB_TM * TOP_K,), lambda i: (jnp.minimum(i + 1, n_tiles - 1),),
                         memory_space=pltpu.SMEM),
            pl.BlockSpec(memory_space=pl.ANY),
            pl.BlockSpec((COMB_TM, D_MODEL), lambda i: (i, 0)),
            pl.BlockSpec((COMB_TM, LANES), lambda i: (i, 0)),
            pl.BlockSpec((1, D_MODEL), lambda i: (0, 0)),
        ],
        out_specs=pl.BlockSpec((COMB_TM, D_MODEL), lambda i: (i, 0)),
        out_shape=jax.ShapeDtypeStruct((SEQ, D_MODEL), f32),
        scratch_shapes=[
            pltpu.VMEM((2, TOP_K, COMB_TM // SUBLANES, SUBLANES, D_MODEL), f32),
            pltpu.SemaphoreType.DMA((2,)),
        ],
        compiler_params=_cparams(("arbitrary",)),
    )(dest, dest, ys, x1, gates, g_final.reshape(1, D_MODEL))


def kernel(x, g_mix, w_in, lam_re_fwd, lam_im_fwd, log_dt_fwd, b_re_fwd, b_im_fwd, c_re_fwd, c_im_fwd, lam_re_bwd, lam_im_bwd, log_dt_bwd, b_re_bwd, b_im_bwd, c_re_bwd, c_im_bwd, ssm_d, w_glu, b_glu, na_rpb, g_ssm_out, g_na_out, w_out, g_moe, w_router, b_router, w_gate, b_gate, w_up, b_up, w_down, b_down, g_final):
    x2 = x.reshape(SEQ, D_MODEL)

    u, qkv = _in_proj(x2, g_mix[0], w_in[0].astype(bf16))

    s5_w, s5_dec = _s5_weights(
        (lam_re_fwd[0], lam_im_fwd[0], log_dt_fwd[0], b_re_fwd[0], b_im_fwd[0], c_re_fwd[0], c_im_fwd[0]),
        (lam_re_bwd[0], lam_im_bwd[0], log_dt_bwd[0], b_re_bwd[0], b_im_bwd[0], c_re_bwd[0], c_im_bwd[0]))
    y = _s5(u, s5_w, s5_dec, ssm_d[0])
    ssm_n = _glu(y, w_glu[0].astype(bf16), b_glu[0], g_ssm_out[0])

    y_na = _natten(qkv, _na_bias(na_rpb[0]))

    w_router_pad = jnp.zeros((D_MODEL, LANES), bf16).at[:, :N_EXPERTS].set(w_router[0].astype(bf16))
    b_router_pad = jnp.full((1, LANES), NEG_BIG, f32).at[0, :N_EXPERTS].set(b_router[0].astype(f32))
    x1, hn, logits = _out_proj(ssm_n, y_na, x2, g_na_out[0], w_out[0].astype(bf16), g_moe[0],
                               w_router_pad, b_router_pad)

    dest_lanes, gates, counts = _route(logits)
    dest = dest_lanes[:, :TOP_K].reshape(-1)
    last_chunk, sb_e, sb_c0, sb_n, n_used = _routing_tables(counts)

    xs = _dispatch(last_chunk, dest, hn)
    ys = _experts(sb_e, sb_c0, sb_n, n_used, xs, w_gate[0], b_gate[0], w_up[0], b_up[0],
                  w_down[0], b_down[0])
    out = _combine(dest, ys, x1, gates, g_final)
    return out.reshape(x.shape)
```
